```python
import math
import jax, jax.numpy as jnp
from jax import lax
import numpy as np

D_MODEL = 4096
BATCH = 1
SEQ = 8192
DEPTH = 1

CHUNK = 64
RMS_EPS = 1e-6

D_RWKV = D_MODEL // 2
RWKV_HEAD = 64
RWKV_HEADS = D_RWKV // RWKV_HEAD
DECAY_LORA = max(32, int(round(1.8 * D_MODEL ** 0.5 / 32)) * 32)
AAA_LORA = max(32, int(round(1.8 * D_MODEL ** 0.5 / 32)) * 32)
GATE_LORA = max(32, int(round(0.6 * D_MODEL ** 0.8 / 32)) * 32)
GN_EPS = 64e-5

D_S5 = D_MODEL // 4
S5_GROUP_CH = 16
S5_GROUPS = D_S5 // S5_GROUP_CH
S5_STATE = 64
DT_MIN = 0.001
DT_MAX = 0.1

D_FF = -(-8 * D_MODEL // (3 * 256)) * 256

RWKV_COLS = 3 * D_RWKV + DECAY_LORA + AAA_LORA + GATE_LORA
D_IN = RWKV_COLS + D_S5 + 2 * D_MODEL

kernel_name = "hybrid_rwkv7_s5_sandwich_adaln"


def rms_norm(x, g):
    xf = x.astype(jnp.float32)
    y = xf * lax.rsqrt(jnp.mean(xf * xf, axis=-1, keepdims=True) + RMS_EPS)
    return (y * g.astype(jnp.float32)).astype(x.dtype)


def token_shift_mix(p, mu):
    prev = jnp.pad(p, ((0, 0), (1, 0), (0, 0)))[:, :-1]
    return p + (prev - p) * mu


def rwkv7_recurrence(r, w, k, v, kk, a):
    bsz, seq, nh, n = r.shape
    n_chunks = seq // CHUNK

    def to_chunks(t):
        return t.reshape(bsz, n_chunks, CHUNK, nh, n).transpose(1, 2, 0, 3, 4)

    def step(S, inp):
        r_t, w_t, k_t, v_t, kk_t, a_t = inp
        sa = jnp.einsum('bhvk,bhk->bhv', S, -kk_t)
        S = (S * w_t[:, :, None, :]
             + sa[..., None] * (kk_t * a_t)[:, :, None, :]
             + v_t[..., None] * k_t[:, :, None, :])
        o_t = jnp.einsum('bhvk,bhk->bhv', S, r_t)
        return S, o_t

    def chunk_step(S, chunk_inp):
        return lax.scan(step, S, chunk_inp)

    S0 = jnp.zeros((bsz, nh, n, n), jnp.float32)
    _, o = lax.scan(chunk_step, S0, tuple(to_chunks(t) for t in (r, w, k, v, kk, a)))
    return o.transpose(2, 0, 1, 3, 4).reshape(bsz, seq, nh, n)


def rwkv7_time_mix(xr, xk, xv, xw, xa, xg, w0, w2, a0, a2, g2, k_k, k_a, r_k, ln_w, ln_b):
    dtype = xr.dtype
    bsz, seq, _ = xr.shape
    f32 = jnp.float32
    r = xr.astype(f32)
    k = xk.astype(f32)
    v = xv.astype(f32)
    w_log = -jax.nn.softplus(-(w0.astype(f32) + jnp.tanh(xw.astype(f32)) @ w2.astype(f32))) - 0.5
    decay = jnp.exp(-jnp.exp(w_log))
    a = jax.nn.sigmoid(a0.astype(f32) + xa.astype(f32) @ a2.astype(f32))
    g = jax.nn.sigmoid(xg.astype(f32)) @ g2.astype(f32)
    kk = k * k_k.astype(f32)
    k = k * (1.0 + (a - 1.0) * k_a.astype(f32))

    hs = lambda t: t.reshape(bsz, seq, RWKV_HEADS, RWKV_HEAD)
    r, k, v, kk, a, decay = hs(r), hs(k), hs(v), hs(kk), hs(a), hs(decay)
    kk = kk * lax.rsqrt(jnp.maximum(jnp.sum(kk * kk, axis=-1, keepdims=True), 1e-24))

    o = rwkv7_recurrence(r, decay, k, v, kk, a)
    mean = jnp.mean(o, axis=-1, keepdims=True)
    var = jnp.mean(jnp.square(o - mean), axis=-1, keepdims=True)
    o = (o - mean) * lax.rsqrt(var + GN_EPS)
    o = o.reshape(bsz, seq, D_RWKV) * ln_w.astype(f32) + ln_b.astype(f32)
    bonus = jnp.sum(r * k * r_k.astype(f32), axis=-1, keepdims=True) * v
    o = (o + bonus.reshape(bsz, seq, D_RWKV)) * g
    return o.astype(dtype)


def _complex_linear_combine(e1, e2):
    a1r, a1i, b1r, b1i = e1
    a2r, a2i, b2r, b2i = e2
    ar = a2r * a1r - a2i * a1i
    ai = a2r * a1i + a2i * a1r
    br = a2r * b1r - a2i * b1i + b2r
    bi = a2r * b1i + a2i * b1r + b2i
    return ar, ai, br, bi


def s5_mixer(u, lam_re, lam_im, log_dt, b_re, b_im, c_re, c_im, d, w_glu, b_glu):
    dtype = u.dtype
    f32 = jnp.float32
    bsz, seq, _ = u.shape
    uf = u.astype(f32)
    ug = uf.reshape(bsz, seq, S5_GROUPS, S5_GROUP_CH)
    lr = lam_re.astype(f32)
    li = lam_im.astype(f32)
    dt = jnp.exp(log_dt.astype(f32))[:, None]
    mag = jnp.exp(lr * dt)
    lb_re = mag * jnp.cos(li * dt)
    lb_im = mag * jnp.sin(li * dt)
    den = lr * lr + li * li
    z_re = ((lb_re - 1.0) * lr + lb_im * li) / den
    z_im = (lb_im * lr - (lb_re - 1.0) * li) / den
    br = b_re.astype(f32)
    bi = b_im.astype(f32)
    bb_re = z_re[..., None] * br - z_im[..., None] * bi
    bb_im = z_re[..., None] * bi + z_im[..., None] * br
    bu_re = jnp.einsum('blgc,gpc->blgp', ug, bb_re)
    bu_im = jnp.einsum('blgc,gpc->blgp', ug, bb_im)
    a_re = jnp.broadcast_to(lb_re, bu_re.shape)
    a_im = jnp.broadcast_to(lb_im, bu_im.shape)
    _, _, s_re, s_im = lax.associative_scan(
        _complex_linear_combine, (a_re, a_im, bu_re, bu_im), axis=1)
    y = (jnp.einsum('blgp,gcp->blgc', s_re, c_re.astype(f32))
         - jnp.einsum('blgp,gcp->blgc', s_im, c_im.astype(f32)))
    y = y.reshape(bsz, seq, D_S5) + d.astype(f32) * uf
    y = jax.nn.gelu(y)
    val, gate = jnp.split(y @ w_glu.astype(f32) + b_glu.astype(f32), 2, axis=-1)
    return (val * jax.nn.sigmoid(gate)).astype(dtype)


def hybrid_mixer(xm, w_in, rwkv_mu, rwkv_w0, rwkv_w2, rwkv_a0, rwkv_a2, rwkv_g2, rwkv_k_k,
                 rwkv_k_a, rwkv_r_k, rwkv_ln_w, rwkv_ln_b, s5_lam_re, s5_lam_im, s5_log_dt,
                 s5_b_re, s5_b_im, s5_c_re, s5_c_im, s5_d, s5_w_glu, s5_b_glu,
                 w_up_rwkv, w_up_s5, w_out):
    proj = xm @ w_in
    p_rwkv, u_s5, gate_rwkv, gate_s5 = jnp.split(
        proj, [RWKV_COLS, RWKV_COLS + D_S5, RWKV_COLS + D_S5 + D_MODEL], axis=-1)
    p_rwkv = token_shift_mix(p_rwkv, rwkv_mu)
    o1 = 3 * D_RWKV + DECAY_LORA
    xr, xk, xv, xw, xa, xg = jnp.split(
        p_rwkv, [D_RWKV, 2 * D_RWKV, 3 * D_RWKV, o1, o1 + AAA_LORA], axis=-1)
    o_a = rwkv7_time_mix(xr, xk, xv, xw, xa, xg, rwkv_w0, rwkv_w2, rwkv_a0, rwkv_a2, rwkv_g2,
                         rwkv_k_k, rwkv_k_a, rwkv_r_k, rwkv_ln_w, rwkv_ln_b)
    o_b = s5_mixer(u_s5, s5_lam_re, s5_lam_im, s5_log_dt, s5_b_re, s5_b_im, s5_c_re, s5_c_im,
                   s5_d, s5_w_glu, s5_b_glu)
    merged = (jax.nn.sigmoid(gate_rwkv) * (o_a @ w_up_rwkv)
              + jax.nn.sigmoid(gate_s5) * (o_b @ w_up_s5))
    return merged @ w_out


def swiglu(x, w_gate, w_up, w_down):
    return (jax.nn.silu(x @ w_gate) * (x @ w_up)) @ w_down


def setup_inputs(seed: int = 0) -> dict:
    key = jax.random.key(seed)
    ks = jax.random.split(key, 40)
    f32 = jnp.float32
    nrm = lambda k, shape, s: jax.random.normal(k, shape, f32) * s
    L = DEPTH
    return {
        "x": nrm(ks[0], (BATCH, SEQ, D_MODEL), 1.0),
        "c": nrm(ks[1], (BATCH, D_MODEL), 1.0),
        "w_ada": nrm(ks[2], (L, D_MODEL, 6 * D_MODEL), 0.5 * D_MODEL ** -0.5),
        "b_ada": nrm(ks[3], (L, 6 * D_MODEL), 0.01),
        "norm_pre_mix": 1.0 + nrm(ks[4], (L, D_MODEL), 0.02),
        "norm_post_mix": 1.0 + nrm(ks[5], (L, D_MODEL), 0.02),
        "norm_pre_ffn": 1.0 + nrm(ks[6], (L, D_MODEL), 0.02),
        "norm_post_ffn": 1.0 + nrm(ks[7], (L, D_MODEL), 0.02),
        "w_in": nrm(ks[8], (L, D_MODEL, D_IN), D_MODEL ** -0.5),
        "rwkv_mu": jax.random.uniform(ks[9], (L, RWKV_COLS), f32, 0.0, 1.0),
        "rwkv_w0": jax.random.uniform(ks[10], (L, D_RWKV), f32, -6.5, -1.5),
        "rwkv_w2": nrm(ks[11], (L, DECAY_LORA, D_RWKV), 0.1 * DECAY_LORA ** -0.5),
        "rwkv_a0": nrm(ks[12], (L, D_RWKV), 0.1),
        "rwkv_a2": nrm(ks[13], (L, AAA_LORA, D_RWKV), 0.1 * AAA_LORA ** -0.5),
        "rwkv_g2": nrm(ks[14], (L, GATE_LORA, D_RWKV), GATE_LORA ** -0.5),
        "rwkv_k_k": 0.85 + nrm(ks[15], (L, D_RWKV), 0.02),
        "rwkv_k_a": 1.0 + nrm(ks[16], (L, D_RWKV), 0.02),
        "rwkv_r_k": nrm(ks[17], (L, RWKV_HEADS, RWKV_HEAD), 0.1),
        "rwkv_ln_w": 1.0 + nrm(ks[18], (L, D_RWKV), 0.02),
        "rwkv_ln_b": nrm(ks[19], (L, D_RWKV), 0.01),
        "s5_lam_re": -0.5 + nrm(ks[20], (L, S5_GROUPS, S5_STATE), 0.01),
        "s5_lam_im": (math.pi * jnp.arange(S5_STATE, dtype=f32))[None, None, :]
                     + nrm(ks[21], (L, S5_GROUPS, S5_STATE), 0.01),
        "s5_log_dt": jax.random.uniform(ks[22], (L, S5_GROUPS), f32,
                                        math.log(DT_MIN), math.log(DT_MAX)),
        "s5_b_re": nrm(ks[23], (L, S5_GROUPS, S5_STATE, S5_GROUP_CH), (2 * S5_GROUP_CH) ** -0.5),
        "s5_b_im": nrm(ks[24], (L, S5_GROUPS, S5_STATE, S5_GROUP_CH), (2 * S5_GROUP_CH) ** -0.5),
        "s5_c_re": nrm(ks[25], (L, S5_GROUPS, S5_GROUP_CH, S5_STATE), S5_STATE ** -0.5),
        "s5_c_im": nrm(ks[26], (L, S5_GROUPS, S5_GROUP_CH, S5_STATE), S5_STATE ** -0.5),
        "s5_d": nrm(ks[27], (L, D_S5), 1.0),
        "s5_w_glu": nrm(ks[28], (L, D_S5, 2 * D_S5), D_S5 ** -0.5),
        "s5_b_glu": nrm(ks[29], (L, 2 * D_S5), 0.01),
        "w_up_rwkv": nrm(ks[30], (L, D_RWKV, D_MODEL), D_RWKV ** -0.5),
        "w_up_s5": nrm(ks[31], (L, D_S5, D_MODEL), D_S5 ** -0.5),
        "w_out": nrm(ks[32], (L, D_MODEL, D_MODEL), D_MODEL ** -0.5),
        "ffn_w_gate": nrm(ks[33], (L, D_MODEL, D_FF), D_MODEL ** -0.5),
        "ffn_w_up": nrm(ks[34], (L, D_MODEL, D_FF), D_MODEL ** -0.5),
        "ffn_w_down": nrm(ks[35], (L, D_FF, D_MODEL), D_FF ** -0.5),
    }


def reference(x, c, w_ada, b_ada, norm_pre_mix, norm_post_mix, norm_pre_ffn, norm_post_ffn,
              w_in, rwkv_mu, rwkv_w0, rwkv_w2, rwkv_a0, rwkv_a2, rwkv_g2, rwkv_k_k, rwkv_k_a,
              rwkv_r_k, rwkv_ln_w, rwkv_ln_b, s5_lam_re, s5_lam_im, s5_log_dt, s5_b_re, s5_b_im,
              s5_c_re, s5_c_im, s5_d, s5_w_glu, s5_b_glu, w_up_rwkv, w_up_s5, w_out,
              ffn_w_gate, ffn_w_up, ffn_w_down):
    h = x
    cs = jax.nn.silu(c)
    for l in range(DEPTH):
        mod = (cs @ w_ada[l] + b_ada[l])[:, None, :]
        sh1, sc1, g1, sh2, sc2, g2 = jnp.split(mod, 6, axis=-1)
        xm = rms_norm(h, norm_pre_mix[l]) * (1.0 + sc1) + sh1
        mix = hybrid_mixer(xm, w_in[l], rwkv_mu[l], rwkv_w0[l], rwkv_w2[l], rwkv_a0[l],
                           rwkv_a2[l], rwkv_g2[l], rwkv_k_k[l], rwkv_k_a[l], rwkv_r_k[l],
                           rwkv_ln_w[l], rwkv_ln_b[l], s5_lam_re[l], s5_lam_im[l],
                           s5_log_dt[l], s5_b_re[l], s5_b_im[l], s5_c_re[l], s5_c_im[l],
                           s5_d[l], s5_w_glu[l], s5_b_glu[l], w_up_rwkv[l], w_up_s5[l],
                           w_out[l])
        h = h + g1 * rms_norm(mix, norm_post_mix[l])
        xf = rms_norm(h, norm_pre_ffn[l]) * (1.0 + sc2) + sh2
        ff = swiglu(xf, ffn_w_gate[l], ffn_w_up[l], ffn_w_down[l])
        h = h + g2 * rms_norm(ff, norm_post_ffn[l])
    return h
```

```python
import functools
import math

import jax
import jax.numpy as jnp
from jax import lax
from jax.experimental import pallas as pl
from jax.experimental.pallas import tpu as pltpu

F32 = jnp.float32
BF16 = jnp.bfloat16

D_MODEL = 4096
RMS_EPS = 1e-6
D_RWKV = 2048
RWKV_HEAD = 64
LORA_W = 128
LORA_A = 128
LORA_G = 480
LORA_G_PAD = 768
GN_EPS = 64e-5
D_S5 = 1024
S5_GROUPS = 64
S5_GROUP_CH = 16
S5_STATE = 64
S5_BLOCKS = 4
D_FF = 11008
D_FF_PAD = 11264

LANES = 128
PAIRS = D_RWKV // LANES
RWKV_T = 64
RWKV_CHUNKS_PER_STEP = 4
S5_T = 64

COL_LORA = 3 * D_RWKV
COL_U = COL_LORA + 1024
RWKV_COLS = 3 * D_RWKV + LORA_W + LORA_A + LORA_G

VMEM_LIMIT = 56 * 1024 * 1024


def _cparams(n_axes, vmem=VMEM_LIMIT):
    return pltpu.CompilerParams(dimension_semantics=("arbitrary",) * n_axes, vmem_limit_bytes=vmem)


def _dot(a, b):
    return jnp.dot(a.astype(BF16), b.astype(BF16), preferred_element_type=F32)


def _dot_nt(a, b):
    return lax.dot_general(a.astype(BF16), b.astype(BF16), (((1,), (1,)), ((), ())),
                           preferred_element_type=F32)


def _dot_tn(a, b):
    return lax.dot_general(a.astype(BF16), b.astype(BF16), (((0,), (0,)), ((), ())),
                           preferred_element_type=F32)


def _split3(x):
    hi = x.astype(BF16)
    r1 = x - hi.astype(F32)
    mid = r1.astype(BF16)
    lo = (r1 - mid.astype(F32)).astype(BF16)
    return hi, mid, lo


def _dot_f32_lhs(x, sel):
    hi, mid, lo = _split3(x)
    d = lambda p: jnp.dot(p, sel, preferred_element_type=F32)
    return d(hi) + d(mid) + d(lo)


def _dot_f32_rhs(sel, x):
    hi, mid, lo = _split3(x)
    d = lambda p: jnp.dot(sel, p, preferred_element_type=F32)
    return d(hi) + d(mid) + d(lo)


def _rms(x):
    return x * lax.rsqrt(jnp.mean(x * x, axis=-1, keepdims=True) + RMS_EPS)


def _sigmoid(x):
    return 1.0 / (1.0 + jnp.exp(-x))


def _token_shift(p, halo, mu, first):
    last = jnp.where(first, 0.0, halo[7:8, :])
    rolled = pltpu.roll(p, 1, axis=0)
    row = lax.broadcasted_iota(jnp.int32, p.shape, 0)
    prev = jnp.where(row == 0, last, rolled)
    return p + (prev - p) * mu


def _ada_kernel(c_ref, w_ref, b_ref, o_ref):
    c = c_ref[...]
    cs = c * _sigmoid(c)
    tn = o_ref.shape[1]
    rows = 512
    acc = jnp.zeros((8, tn), F32)
    for k0 in range(0, D_MODEL, rows):
        blk = w_ref[k0:k0 + rows, :] * cs[k0:k0 + rows, :]
        acc = acc + jnp.sum(blk.reshape(rows // 8, 8, tn), axis=0)
    o_ref[...] = jnp.sum(acc, axis=0, keepdims=True) + b_ref[...]


def _ada(c_col, w_ada, b_ada):
    n = w_ada.shape[1]
    tn = 512
    return pl.pallas_call(
        _ada_kernel,
        grid=(n // tn,),
        in_specs=[pl.BlockSpec((D_MODEL, 1), lambda j: (0, 0)),
                  pl.BlockSpec((D_MODEL, tn), lambda j: (0, j)),
                  pl.BlockSpec((1, tn), lambda j: (0, j))],
        out_specs=pl.BlockSpec((1, tn), lambda j: (0, j)),
        out_shape=jax.ShapeDtypeStruct((1, n), F32),
        compiler_params=_cparams(1),
        name="ada",
    )(c_col, w_ada, b_ada)


def _prenorm_kernel(x_ref, g_ref, sh_ref, sc_ref, o_ref):
    y = _rms(x_ref[...]) * g_ref[...]
    o_ref[...] = (y * (1.0 + sc_ref[...]) + sh_ref[...]).astype(o_ref.dtype)


def _prenorm(x, g, mod, sh_idx, sc_idx, tm):
    L = x.shape[0]
    return pl.pallas_call(
        _prenorm_kernel,
        grid=(L // tm,),
        in_specs=[pl.BlockSpec((tm, D_MODEL), lambda i: (i, 0)),
                  pl.BlockSpec((1, D_MODEL), lambda i: (0, 0)),
                  pl.BlockSpec((1, D_MODEL), lambda i: (0, sh_idx)),
                  pl.BlockSpec((1, D_MODEL), lambda i: (0, sc_idx))],
        out_specs=pl.BlockSpec((tm, D_MODEL), lambda i: (i, 0)),
        out_shape=jax.ShapeDtypeStruct((L, D_MODEL), BF16),
        compiler_params=_cparams(1),
        name="prenorm",
    )(x, g, mod, mod)


def _mid_kernel(x_ref, mix_ref, gpost_ref, gate_ref, gpre_ref, sh_ref, sc_ref, h_ref, xf_ref):
    h = x_ref[...] + gate_ref[...] * (_rms(mix_ref[...]) * gpost_ref[...])
    h_ref[...] = h
    y = _rms(h) * gpre_ref[...]
    xf_ref[...] = (y * (1.0 + sc_ref[...]) + sh_ref[...]).astype(xf_ref.dtype)


def _mid(x, mix, gpost, gpre, mod, tm):
    L = x.shape[0]
    row = lambda i: (i, 0)
    vec = lambda k: pl.BlockSpec((1, D_MODEL), lambda i: (0, k))
    return pl.pallas_call(
        _mid_kernel,
        grid=(L // tm,),
        in_specs=[pl.BlockSpec((tm, D_MODEL), row), pl.BlockSpec((tm, D_MODEL), row),
                  vec(0), vec(2), vec(0), vec(3), vec(4)],
        out_specs=[pl.BlockSpec((tm, D_MODEL), row), pl.BlockSpec((tm, D_MODEL), row)],
        out_shape=[jax.ShapeDtypeStruct((L, D_MODEL), F32), jax.ShapeDtypeStruct((L, D_MODEL), BF16)],
        compiler_params=_cparams(1),
        name="mid",
    )(x, mix, gpost, mod, gpre, mod, mod)


def _final_kernel(h_ref, ff_ref, gpost_ref, gate_ref, o_ref):
    o_ref[...] = h_ref[...] + gate_ref[...] * (_rms(ff_ref[...]) * gpost_ref[...])


def _final(h, ff, gpost, mod, tm):
    L = h.shape[0]
    row = lambda i: (i, 0)
    return pl.pallas_call(
        _final_kernel,
        grid=(L // tm,),
        in_specs=[pl.BlockSpec((tm, D_MODEL), row), pl.BlockSpec((tm, D_MODEL), row),
                  pl.BlockSpec((1, D_MODEL), lambda i: (0, 0)),
                  pl.BlockSpec((1, D_MODEL), lambda i: (0, 5))],
        out_specs=pl.BlockSpec((tm, D_MODEL), row),
        out_shape=jax.ShapeDtypeStruct((L, D_MODEL), F32),
        compiler_params=_cparams(1),
        name="final",
    )(h, ff, gpost, mod)


def _mm_kernel(a_ref, w_ref, o_ref):
    o_ref[...] = jnp.dot(a_ref[...], w_ref[...], preferred_element_type=F32).astype(o_ref.dtype)


def _matmul(a, w, out_dtype, tm, tn, name):
    m, k = a.shape
    n = w.shape[1]
    return pl.pallas_call(
        _mm_kernel,
        grid=(m // tm, n // tn),
        in_specs=[pl.BlockSpec((tm, k), lambda i, j: (i, 0)),
                  pl.BlockSpec((k, tn), lambda i, j: (0, j))],
        out_specs=pl.BlockSpec((tm, tn), lambda i, j: (i, j)),
        out_shape=jax.ShapeDtypeStruct((m, n), out_dtype),
        compiler_params=_cparams(2),
        name=name,
    )(a, w)


def _mm_acc_kernel(a_ref, w_ref, o_ref, acc_ref):
    kk = pl.program_id(2)
    part = jnp.dot(a_ref[...], w_ref[...], preferred_element_type=F32)

    @pl.when(kk == 0)
    def _():
        acc_ref[...] = part

    @pl.when(kk > 0)
    def _():
        acc_ref[...] = acc_ref[...] + part

    @pl.when(kk == pl.num_programs(2) - 1)
    def _():
        o_ref[...] = acc_ref[...].astype(o_ref.dtype)


def _matmul_ksplit(a, w, out_dtype, tm, tn, tk, name):
    m, k = a.shape
    n = w.shape[1]
    return pl.pallas_call(
        _mm_acc_kernel,
        grid=(m // tm, n // tn, k // tk),
        in_specs=[pl.BlockSpec((tm, tk), lambda i, j, kk: (i, kk)),
                  pl.BlockSpec((tk, tn), lambda i, j, kk: (kk, j))],
        out_specs=pl.BlockSpec((tm, tn), lambda i, j, kk: (i, j)),
        out_shape=jax.ShapeDtypeStruct((m, n), out_dtype),
        scratch_shapes=[pltpu.VMEM((tm, tn), F32)],
        compiler_params=_cparams(3),
        name=name,
    )(a, w)


def _merge_kernel(oa_ref, ob_ref, wa_ref, wb_ref, ga_ref, gb_ref, o_ref):
    ya = jnp.dot(oa_ref[...], wa_ref[...], preferred_element_type=F32)
    yb = jnp.dot(ob_ref[...], wb_ref[...], preferred_element_type=F32)
    m = _sigmoid(ga_ref[...].astype(F32)) * ya + _sigmoid(gb_ref[...].astype(F32)) * yb
    o_ref[...] = m.astype(o_ref.dtype)


def _merge(o_a, o_b, w_up_a, w_up_b, gates, tm, tn):
    L = o_a.shape[0]
    nb = D_MODEL // tn
    return pl.pallas_call(
        _merge_kernel,
        grid=(L // tm, nb),
        in_specs=[pl.BlockSpec((tm, D_RWKV), lambda i, j: (i, 0)),
                  pl.BlockSpec((tm, D_S5), lambda i, j: (i, 0)),
                  pl.BlockSpec((D_RWKV, tn), lambda i, j: (0, j)),
                  pl.BlockSpec((D_S5, tn), lambda i, j: (0, j)),
                  pl.BlockSpec((tm, tn), lambda i, j: (i, j)),
                  pl.BlockSpec((tm, tn), lambda i, j: (i, j + nb))],
        out_specs=pl.BlockSpec((tm, tn), lambda i, j: (i, j)),
        out_shape=jax.ShapeDtypeStruct((L, D_MODEL), BF16),
        compiler_params=_cparams(2),
        name="merge",
    )(o_a, o_b, w_up_a, w_up_b, gates, gates)


def _ffn_up_kernel(x_ref, wg_ref, wu_ref, o_ref):
    x = x_ref[...]
    a = jnp.dot(x, wg_ref[...], preferred_element_type=F32)
    b = jnp.dot(x, wu_ref[...], preferred_element_type=F32)
    o_ref[...] = ((a * _sigmoid(a)) * b).astype(o_ref.dtype)


def _ffn_up(xf, w_gate, w_up, tm, tn):
    L = xf.shape[0]
    n = w_gate.shape[1]
    return pl.pallas_call(
        _ffn_up_kernel,
        grid=(L // tm, n // tn),
        in_specs=[pl.BlockSpec((tm, D_MODEL), lambda i, j: (i, 0)),
                  pl.BlockSpec((D_MODEL, tn), lambda i, j: (0, j)),
                  pl.BlockSpec((D_MODEL, tn), lambda i, j: (0, j))],
        out_specs=pl.BlockSpec((tm, tn), lambda i, j: (i, j)),
        out_shape=jax.ShapeDtypeStruct((L, n), BF16),
        compiler_params=_cparams(2),
        name="ffn_up",
    )(xf, w_gate, w_up)


def _lora_kernel(p_ref, halo_ref, mu_ref, w0_ref, a0_ref, w2_ref, a2_ref, g2_ref,
                 lw_ref, a_ref, g_ref):
    ps = _token_shift(p_ref[...], halo_ref[...], mu_ref[...], pl.program_id(0) == 0)
    xw = ps[:, :LORA_W]
    xa = ps[:, LORA_W:LORA_W + LORA_A]
    xg = ps[:, LORA_W + LORA_A:]
    z = -(w0_ref[...] + _dot(jnp.tanh(xw), w2_ref[...]))
    softplus = jnp.maximum(z, 0.0) + jnp.log(1.0 + jnp.exp(-jnp.abs(z)))
    lw_ref[...] = -jnp.exp(-softplus - 0.5)
    a_ref[...] = _sigmoid(a0_ref[...] + _dot(xa, a2_ref[...]))
    g_ref[...] = _dot(_sigmoid(xg), g2_ref[...])


def _lora(proj, mu, w0, a0, w2, a2, g2, tm):
    L = proj.shape[0]
    cb = COL_LORA // 1024
    full = lambda shape: pl.BlockSpec(shape, lambda i: (0, 0))
    out = pl.BlockSpec((tm, D_RWKV), lambda i: (i, 0))
    return pl.pallas_call(
        _lora_kernel,
        grid=(L // tm,),
        in_specs=[pl.BlockSpec((tm, 1024), lambda i: (i, cb)),
                  pl.BlockSpec((8, 1024), lambda i: (jnp.maximum(i * (tm // 8) - 1, 0), cb)),
                  pl.BlockSpec((1, 1024), lambda i: (0, cb)),
                  full((1, D_RWKV)), full((1, D_RWKV)),
                  full((LORA_W, D_RWKV)), full((LORA_A, D_RWKV)), full((LORA_G_PAD, D_RWKV))],
        out_specs=[out, out, out],
        out_shape=[jax.ShapeDtypeStruct((L, D_RWKV), F32)] * 3,
        compiler_params=_cparams(1),
        name="lora",
    )(proj, proj, mu, w0, a0, w2, a2, g2)


def _rwkv_chunk(r, k, v, am, bm, lw, h, cst):
    T = r.shape[0]
    S = 2 * T
    lane_lo, tri_incl, strict, incl, eye = cst

    def stack(x):
        return jnp.concatenate([jnp.where(lane_lo, x, 0.0), jnp.where(lane_lo, 0.0, x)], axis=0)

    lw_s = stack(lw)
    c = _dot_f32_rhs(tri_incl, lw_s)
    c_end = c[T - 1:T, :] + c[S - 1:S, :]
    e_pos = jnp.exp(c)
    e_neg = jnp.exp(-c)
    e_prev = jnp.exp(c - lw_s)
    e_end = jnp.exp(c_end - c)
    r_s, k_s, v_s, a_s, b_s = stack(r), stack(k), stack(v), stack(am), stack(bm)
    rh = r_s * e_pos
    ah = a_s * e_prev
    bh = b_s * e_neg
    kh = k_s * e_neg
    bc = b_s * e_end
    kc = k_s * e_end

    big = _dot_nt(jnp.concatenate([ah, rh], axis=0), jnp.concatenate([bh, kh], axis=0))
    n_ab = jnp.where(strict, big[:S, :S], 0.0)
    m_ak = jnp.where(strict, big[:S, S:], 0.0)
    m_rb = jnp.where(incl, big[S:, :S], 0.0)
    m_rk = jnp.where(incl, big[S:, S:], 0.0)
    avrv = _dot(jnp.concatenate([m_ak, m_rk], axis=0), v_s)

    x = jnp.concatenate([ah, avrv[:S]], axis=1)
    nj = n_ab
    steps = T.bit_length() - 1
    for j in range(steps):
        x = x + _dot(nj, x)
        if j + 1 < steps:
            nj = _dot(nj, nj)

    y = _dot(m_rb, x)
    q = rh + y[:, :LANES]
    o0 = avrv[S:] + y[:, LANES:]
    rhs = jnp.concatenate([x, jnp.concatenate([jnp.zeros_like(v_s), v_s], axis=1)], axis=0)
    gd = _dot_tn(jnp.concatenate([bc, kc], axis=0), rhs)
    z = _dot(jnp.concatenate([gd[:, :LANES], q], axis=0), h)
    o_s = z[LANES:] + o0
    o = o_s[:T] + o_s[T:]
    decay_col = jnp.sum(jnp.where(eye, jnp.exp(c_end), 0.0), axis=1, keepdims=True)
    h_new = decay_col * h + z[:LANES] + gd[:, LANES:]
    return o, h_new


def _rwkv_kernel(r_ref, k_ref, v_ref, rh_ref, kh_ref, vh_ref, mur_ref, muk_ref, muv_ref,
                 lw_ref, a_ref, g_ref, kk_ref, ka_ref, rk_ref, lnw_ref, lnb_ref,
                 o_ref, h_ref):
    i = pl.program_id(1)
    first = i == 0

    @pl.when(first)
    def _():
        h_ref[...] = jnp.zeros_like(h_ref)

    rows = r_ref.shape[0]
    T = RWKV_T
    S = 2 * T
    ii = lax.broadcasted_iota(jnp.int32, (S, S), 0)
    jj = lax.broadcasted_iota(jnp.int32, (S, S), 1)
    same = (ii // T) == (jj // T)
    strict = same & (jj < ii)
    incl = same & (jj <= ii)
    eye = ii == jj
    tri_incl = jnp.where(incl, 1.0, 0.0).astype(BF16)
    hi_ = lax.broadcasted_iota(jnp.int32, (LANES, LANES), 0) // RWKV_HEAD
    hj_ = lax.broadcasted_iota(jnp.int32, (LANES, LANES), 1) // RWKV_HEAD
    head_sum = jnp.where(hi_ == hj_, 1.0, 0.0).astype(BF16)
    lane_lo = lax.broadcasted_iota(jnp.int32, (T, LANES), 1) < RWKV_HEAD
    cst = (lane_lo, tri_incl, strict, incl, eye)

    r = _token_shift(r_ref[...], rh_ref[...], mur_ref[...], first)
    k = _token_shift(k_ref[...], kh_ref[...], muk_ref[...], first)
    v = _token_shift(v_ref[...], vh_ref[...], muv_ref[...], first)
    a = a_ref[...]
    kk = k * kk_ref[...]
    k = k * (1.0 + (a - 1.0) * ka_ref[...])
    ss = _dot_f32_lhs(kk * kk, head_sum)
    kk = kk * lax.rsqrt(jnp.maximum(ss, 1e-24))
    am = -kk
    bm = kk * a
    lw = lw_ref[...]

    h = h_ref[...]
    outs = []
    for ci in range(rows // T):
        sl = slice(ci * T, (ci + 1) * T)
        o, h = _rwkv_chunk(r[sl], k[sl], v[sl], am[sl], bm[sl], lw[sl], h, cst)
        outs.append(o)
    h_ref[...] = h
    o = jnp.concatenate(outs, axis=0) if len(outs) > 1 else outs[0]

    inv = 1.0 / RWKV_HEAD
    mean = _dot_f32_lhs(o, head_sum) * inv
    d = o - mean
    var = _dot_f32_lhs(d * d, head_sum) * inv
    y = d * lax.rsqrt(var + GN_EPS) * lnw_ref[...] + lnb_ref[...]
    bonus = _dot_f32_lhs(r * k * rk_ref[...], head_sum) * v
    o_ref[...] = ((y + bonus) * g_ref[...]).astype(o_ref.dtype)


def _rwkv(proj, mu, lw, a, g, k_k, k_a, r_k, ln_w, ln_b, rows):
    L = proj.shape[0]
    nb = D_RWKV // LANES

    def col(off):
        return pl.BlockSpec((rows, LANES), lambda p, i: (i, p + off))

    def halo(off):
        return pl.BlockSpec((8, LANES), lambda p, i: (jnp.maximum(i * (rows // 8) - 1, 0), p + off))

    def vec(off):
        return pl.BlockSpec((1, LANES), lambda p, i: (0, p + off))

    return pl.pallas_call(
        _rwkv_kernel,
        grid=(PAIRS, L // rows),
        in_specs=[col(0), col(nb), col(2 * nb), halo(0), halo(nb), halo(2 * nb),
                  vec(0), vec(nb), vec(2 * nb),
                  col(0), col(0), col(0),
                  vec(0), vec(0), vec(0), vec(0), vec(0)],
        out_specs=col(0),
        out_shape=jax.ShapeDtypeStruct((L, D_RWKV), BF16),
        scratch_shapes=[pltpu.VMEM((LANES, LANES), F32)],
        compiler_params=_cparams(2),
        name="rwkv",
    )(proj, proj, proj, proj, proj, proj, mu, mu, mu, lw, a, g, k_k, k_a, r_k, ln_w, ln_b)


def _s5_table_kernel(lr_ref, li_ref, ldt_ref, pos_re, pos_im, neg_re, neg_im, z_re, z_im):
    lr = lr_ref[...]
    li = li_ref[...]
    dt = jnp.exp(ldt_ref[...])
    t = lax.broadcasted_iota(jnp.int32, pos_re.shape, 0).astype(F32)
    mag = jnp.exp(t * (lr * dt))
    ang = t * (li * dt)
    cs, sn = jnp.cos(ang), jnp.sin(ang)
    pos_re[...] = mag * cs
    pos_im[...] = mag * sn
    inv = 1.0 / mag
    neg_re[...] = inv * cs
    neg_im[...] = -(inv * sn)
    m1 = jnp.exp(lr * dt)
    lb_re = m1 * jnp.cos(li * dt)
    lb_im = m1 * jnp.sin(li * dt)
    den = lr * lr + li * li
    z_re[...] = ((lb_re - 1.0) * lr + lb_im * li) / den
    z_im[...] = (lb_im * lr - (lb_re - 1.0) * li) / den


def _s5_tables(lr, li, ldt, rows):
    n = lr.shape[1]
    full = pl.BlockSpec((1, n), lambda: (0, 0))
    tab = pl.BlockSpec((rows, n), lambda: (0, 0))
    return pl.pallas_call(
        _s5_table_kernel,
        in_specs=[full, full, full],
        out_specs=[tab, tab, tab, tab, full, full],
        out_shape=[jax.ShapeDtypeStruct((rows, n), F32)] * 4 + [jax.ShapeDtypeStruct((1, n), F32)] * 2,
        name="s5_tables",
    )(lr, li, ldt)


def _s5_kernel(u_ref, pos_re_ref, pos_im_ref, neg_re_ref, neg_im_ref, z_re_ref, z_im_ref,
               b_re_ref, b_im_ref, c_re_ref, c_im_ref, d_ref, wglu_ref, bglu_ref,
               o_ref, bb_ref, st_ref):
    T = u_ref.shape[0]
    NS = S5_GROUPS * S5_STATE
    BS = NS // S5_BLOCKS
    BC = D_S5 // S5_BLOCKS

    @pl.when(pl.program_id(0) == 0)
    def _():
        st_ref[...] = jnp.zeros_like(st_ref)
        for j in range(S5_BLOCKS):
            zr = z_re_ref[:, j * BS:(j + 1) * BS]
            zi = z_im_ref[:, j * BS:(j + 1) * BS]
            br = b_re_ref[j]
            bi = b_im_ref[j]
            bb_ref[j, :, :BS] = (zr * br - zi * bi).astype(BF16)
            bb_ref[j, :, BS:] = (zr * bi + zi * br).astype(BF16)

    u = u_ref[...]
    ub = u.astype(BF16)
    ii = lax.broadcasted_iota(jnp.int32, (T, T), 0)
    jj = lax.broadcasted_iota(jnp.int32, (T, T), 1)
    tri = jnp.where(jj <= ii, 1.0, 0.0).astype(BF16)
    s_re = st_ref[0:1, :]
    s_im = st_ref[1:2, :]
    lb_re = pos_re_ref[1:2, :]
    lb_im = pos_im_ref[1:2, :]
    c_in_re = lb_re * s_re - lb_im * s_im
    c_in_im = lb_re * s_im + lb_im * s_re

    ys = []
    for j in range(S5_BLOCKS):
        sl = slice(j * BS, (j + 1) * BS)
        bu = jnp.dot(ub[:, j * BC:(j + 1) * BC], bb_ref[j], preferred_element_type=F32)
        bu_re, bu_im = bu[:, :BS], bu[:, BS:]
        nr, ni = neg_re_ref[:, sl], neg_im_ref[:, sl]
        zz = jnp.concatenate([nr * bu_re - ni * bu_im, nr * bu_im + ni * bu_re], axis=1)
        acc = jnp.dot(tri, zz.astype(BF16), preferred_element_type=F32)
        a_re = acc[:, :BS] + c_in_re[:, sl]
        a_im = acc[:, BS:] + c_in_im[:, sl]
        pr, pi = pos_re_ref[:, sl], pos_im_ref[:, sl]
        x_re = pr * a_re - pi * a_im
        x_im = pr * a_im + pi * a_re
        st_ref[0:1, sl] = x_re[T - 1:T, :]
        st_ref[1:2, sl] = x_im[T - 1:T, :]
        ys.append(_dot(x_re, c_re_ref[j]) - _dot(x_im, c_im_ref[j]))
    y = jnp.concatenate(ys, axis=1) + d_ref[...] * u
    y = 0.5 * y * (1.0 + jnp.tanh(math.sqrt(2.0 / math.pi) * (y + 0.044715 * (y * y * y))))
    zg = _dot(y, wglu_ref[...]) + bglu_ref[...]
    o_ref[...] = (zg[:, :D_S5] * _sigmoid(zg[:, D_S5:])).astype(o_ref.dtype)


def _s5(proj, tables, b_re, b_im, c_re, c_im, d, w_glu, b_glu, rows):
    L = proj.shape[0]
    NS = S5_GROUPS * S5_STATE
    BS = NS // S5_BLOCKS
    BC = D_S5 // S5_BLOCKS
    pos_re, pos_im, neg_re, neg_im, z_re, z_im = tables
    c2 = lambda shape: pl.BlockSpec(shape, lambda i: (0, 0))
    c3 = lambda shape: pl.BlockSpec(shape, lambda i: (0, 0, 0))
    return pl.pallas_call(
        _s5_kernel,
        grid=(L // rows,),
        in_specs=[pl.BlockSpec((rows, D_S5), lambda i: (i, COL_U // D_S5)),
                  c2((rows, NS)), c2((rows, NS)), c2((rows, NS)), c2((rows, NS)),
                  c2((1, NS)), c2((1, NS)),
                  c3((S5_BLOCKS, BC, BS)), c3((S5_BLOCKS, BC, BS)),
                  c3((S5_BLOCKS, BS, BC)), c3((S5_BLOCKS, BS, BC)),
                  c2((1, D_S5)), c2((D_S5, 2 * D_S5)), c2((1, 2 * D_S5))],
        out_specs=pl.BlockSpec((rows, D_S5), lambda i: (i, 0)),
        out_shape=jax.ShapeDtypeStruct((L, D_S5), BF16),
        scratch_shapes=[pltpu.VMEM((S5_BLOCKS, BC, 2 * BS), BF16), pltpu.VMEM((8, NS), F32)],
        compiler_params=_cparams(1),
        name="s5",
    )(proj, pos_re, pos_im, neg_re, neg_im, z_re, z_im, b_re, b_im, c_re, c_im, d, w_glu, b_glu)


def _block_diag_groups(w):
    g, r, c = w.shape
    gb = g // S5_BLOCKS
    w = w.reshape(S5_BLOCKS, gb, r, c)
    eye = jnp.eye(gb, dtype=w.dtype)
    out = w[:, :, :, None, :] * eye[None, :, None, :, None]
    return out.reshape(S5_BLOCKS, gb * r, gb * c)


def kernel(x, c, w_ada, b_ada, norm_pre_mix, norm_post_mix, norm_pre_ffn, norm_post_ffn, w_in, rwkv_mu, rwkv_w0, rwkv_w2, rwkv_a0, rwkv_a2, rwkv_g2, rwkv_k_k, rwkv_k_a, rwkv_r_k, rwkv_ln_w, rwkv_ln_b, s5_lam_re, s5_lam_im, s5_log_dt, s5_b_re, s5_b_im, s5_c_re, s5_c_im, s5_d, s5_w_glu, s5_b_glu, w_up_rwkv, w_up_s5, w_out, ffn_w_gate, ffn_w_up, ffn_w_down):
    bsz, L, _ = x.shape
    assert bsz == 1 and w_ada.shape[0] == 1
    h = x.reshape(L, D_MODEL)
    tm = min(1024, L)
    tr = min(256, L)
    row = lambda v: v.reshape(1, -1)

    mod = _ada(c.reshape(D_MODEL, 1), w_ada[0], row(b_ada[0]))

    wi = w_in[0]
    gpad = LORA_G_PAD - LORA_G
    w_a = jnp.concatenate([wi[:, :RWKV_COLS].astype(BF16), jnp.zeros((D_MODEL, gpad), BF16),
                           wi[:, RWKV_COLS:RWKV_COLS + D_S5].astype(BF16)], axis=1)
    w_g = wi[:, RWKV_COLS + D_S5:].astype(BF16)
    mu = jnp.concatenate([rwkv_mu[0], jnp.zeros((gpad,), F32)]).reshape(1, -1)
    g2 = jnp.concatenate([rwkv_g2[0], jnp.zeros((gpad, D_RWKV), F32)], axis=0).astype(BF16)
    fpad = D_FF_PAD - D_FF
    wf_gate = jnp.pad(ffn_w_gate[0].astype(BF16), ((0, 0), (0, fpad)))
    wf_up = jnp.pad(ffn_w_up[0].astype(BF16), ((0, 0), (0, fpad)))
    wf_down = jnp.pad(ffn_w_down[0].astype(BF16), ((0, fpad), (0, 0)))

    xm = _prenorm(h, row(norm_pre_mix[0]), mod, 0, 1, tr)
    proj = _matmul(xm, w_a, F32, tm, 1024, "proj_a")
    gates = _matmul(xm, w_g, BF16, tm, 1024, "proj_g")

    lw, a, g = _lora(proj, mu, row(rwkv_w0[0]), row(rwkv_a0[0]), rwkv_w2[0].astype(BF16),
                     rwkv_a2[0].astype(BF16), g2, tr)
    o_a = _rwkv(proj, mu, lw, a, g, row(rwkv_k_k[0]), row(rwkv_k_a[0]), row(rwkv_r_k[0]),
                row(rwkv_ln_w[0]), row(rwkv_ln_b[0]), min(RWKV_T * RWKV_CHUNKS_PER_STEP, L))

    rep = lambda v: jnp.repeat(v, S5_STATE).reshape(1, -1)
    tables = _s5_tables(row(s5_lam_re[0]), row(s5_lam_im[0]), rep(s5_log_dt[0]), S5_T)
    bt = lambda w: _block_diag_groups(jnp.swapaxes(w, 1, 2))
    o_b = _s5(proj, tables, bt(s5_b_re[0]), bt(s5_b_im[0]),
              _block_diag_groups(jnp.swapaxes(s5_c_re[0], 1, 2)).astype(BF16),
              _block_diag_groups(jnp.swapaxes(s5_c_im[0], 1, 2)).astype(BF16),
              row(s5_d[0]), s5_w_glu[0].astype(BF16), row(s5_b_glu[0]), S5_T)

    merged = _merge(o_a, o_b, w_up_rwkv[0].astype(BF16), w_up_s5[0].astype(BF16), gates, tm, 1024)
    mix = _matmul(merged, w_out[0].astype(BF16), F32, tm, 1024, "w_out")

    h1, xf = _mid(h, mix, row(norm_post_mix[0]), row(norm_pre_ffn[0]), mod, tr)
    act = _ffn_up(xf, wf_gate, wf_up, tm, 512)
    ff = _matmul_ksplit(act, wf_down, F32, tm, 1024, D_FF_PAD // 4, "ffn_down")
    out = _final(h1, ff, row(norm_post_ffn[0]), mod, tr)
    return out.reshape(bsz, L, D_MODEL)
```

```python
import functools
import math

import jax
import jax.numpy as jnp
from jax import lax
from jax.experimental import pallas as pl
from jax.experimental.pallas import tpu as pltpu

F32 = jnp.float32
BF16 = jnp.bfloat16

D_MODEL = 4096
RMS_EPS = 1e-6
D_RWKV = 2048
RWKV_HEAD = 64
LORA_W = 128
LORA_A = 128
LORA_G = 480
LORA_G_PAD = 768
GN_EPS = 64e-5
D_S5 = 1024
S5_GROUPS = 64
S5_GROUP_CH = 16
S5_STATE = 64
S5_BLOCKS = 4
D_FF = 11008
D_FF_PAD = 11264

LANES = 128
PAIRS = D_RWKV // LANES
RWKV_T = 64
RWKV_CHUNKS_PER_STEP = 8
S5_T = 64

COL_LORA = 3 * D_RWKV
COL_U = COL_LORA + 1024
RWKV_COLS = 3 * D_RWKV + LORA_W + LORA_A + LORA_G

VMEM_LIMIT = 56 * 1024 * 1024


def _cparams(n_axes, vmem=VMEM_LIMIT):
    return pltpu.CompilerParams(dimension_semantics=("arbitrary",) * n_axes, vmem_limit_bytes=vmem)


def _dot(a, b):
    return jnp.dot(a.astype(BF16), b.astype(BF16), preferred_element_type=F32)


def _dot_nt(a, b):
    return lax.dot_general(a.astype(BF16), b.astype(BF16), (((1,), (1,)), ((), ())),
                           preferred_element_type=F32)


def _dot_tn(a, b):
    return lax.dot_general(a.astype(BF16), b.astype(BF16), (((0,), (0,)), ((), ())),
                           preferred_element_type=F32)


def _split3(x):
    hi = x.astype(BF16)
    r1 = x - hi.astype(F32)
    mid = r1.astype(BF16)
    lo = (r1 - mid.astype(F32)).astype(BF16)
    return hi, mid, lo


def _dot_f32_lhs(x, sel):
    hi, mid, lo = _split3(x)
    d = lambda p: jnp.dot(p, sel, preferred_element_type=F32)
    return d(hi) + d(mid) + d(lo)


def _dot_f32_rhs(sel, x):
    hi, mid, lo = _split3(x)
    d = lambda p: jnp.dot(sel, p, preferred_element_type=F32)
    return d(hi) + d(mid) + d(lo)


def _rms(x):
    return x * lax.rsqrt(jnp.mean(x * x, axis=-1, keepdims=True) + RMS_EPS)


def _sigmoid(x):
    return 1.0 / (1.0 + jnp.exp(-x))


def _token_shift(p, halo, mu, first):
    last = jnp.where(first, 0.0, halo[7:8, :])
    rolled = pltpu.roll(p, 1, axis=0)
    row = lax.broadcasted_iota(jnp.int32, p.shape, 0)
    prev = jnp.where(row == 0, last, rolled)
    return p + (prev - p) * mu


def _ada_kernel(c_ref, w_ref, b_ref, o_ref):
    c = c_ref[...]
    cs = c * _sigmoid(c)
    tn = o_ref.shape[1]
    rows = 512
    acc = jnp.zeros((8, tn), F32)
    for k0 in range(0, D_MODEL, rows):
        blk = w_ref[k0:k0 + rows, :] * cs[k0:k0 + rows, :]
        acc = acc + jnp.sum(blk.reshape(rows // 8, 8, tn), axis=0)
    o_ref[...] = jnp.sum(acc, axis=0, keepdims=True) + b_ref[...]


def _ada(c_col, w_ada, b_ada):
    n = w_ada.shape[1]
    tn = 512
    return pl.pallas_call(
        _ada_kernel,
        grid=(n // tn,),
        in_specs=[pl.BlockSpec((D_MODEL, 1), lambda j: (0, 0)),
                  pl.BlockSpec((D_MODEL, tn), lambda j: (0, j)),
                  pl.BlockSpec((1, tn), lambda j: (0, j))],
        out_specs=pl.BlockSpec((1, tn), lambda j: (0, j)),
        out_shape=jax.ShapeDtypeStruct((1, n), F32),
        compiler_params=_cparams(1),
        name="ada",
    )(c_col, w_ada, b_ada)


def _prenorm_kernel(x_ref, g_ref, sh_ref, sc_ref, o_ref):
    y = _rms(x_ref[...]) * g_ref[...]
    o_ref[...] = (y * (1.0 + sc_ref[...]) + sh_ref[...]).astype(o_ref.dtype)


def _prenorm(x, g, mod, sh_idx, sc_idx, tm):
    L = x.shape[0]
    return pl.pallas_call(
        _prenorm_kernel,
        grid=(L // tm,),
        in_specs=[pl.BlockSpec((tm, D_MODEL), lambda i: (i, 0)),
                  pl.BlockSpec((1, D_MODEL), lambda i: (0, 0)),
                  pl.BlockSpec((1, D_MODEL), lambda i: (0, sh_idx)),
                  pl.BlockSpec((1, D_MODEL), lambda i: (0, sc_idx))],
        out_specs=pl.BlockSpec((tm, D_MODEL), lambda i: (i, 0)),
        out_shape=jax.ShapeDtypeStruct((L, D_MODEL), BF16),
        compiler_params=_cparams(1),
        name="prenorm",
    )(x, g, mod, mod)


def _mid_kernel(x_ref, mix_ref, gpost_ref, gate_ref, gpre_ref, sh_ref, sc_ref, h_ref, xf_ref):
    h = x_ref[...] + gate_ref[...] * (_rms(mix_ref[...]) * gpost_ref[...])
    h_ref[...] = h
    y = _rms(h) * gpre_ref[...]
    xf_ref[...] = (y * (1.0 + sc_ref[...]) + sh_ref[...]).astype(xf_ref.dtype)


def _mid(x, mix, gpost, gpre, mod, tm):
    L = x.shape[0]
    row = lambda i: (i, 0)
    vec = lambda k: pl.BlockSpec((1, D_MODEL), lambda i: (0, k))
    return pl.pallas_call(
        _mid_kernel,
        grid=(L // tm,),
        in_specs=[pl.BlockSpec((tm, D_MODEL), row), pl.BlockSpec((tm, D_MODEL), row),
                  vec(0), vec(2), vec(0), vec(3), vec(4)],
        out_specs=[pl.BlockSpec((tm, D_MODEL), row), pl.BlockSpec((tm, D_MODEL), row)],
        out_shape=[jax.ShapeDtypeStruct((L, D_MODEL), F32), jax.ShapeDtypeStruct((L, D_MODEL), BF16)],
        compiler_params=_cparams(1),
        name="mid",
    )(x, mix, gpost, mod, gpre, mod, mod)


def _final_kernel(h_ref, ff_ref, gpost_ref, gate_ref, o_ref):
    o_ref[...] = h_ref[...] + gate_ref[...] * (_rms(ff_ref[...]) * gpost_ref[...])


def _final(h, ff, gpost, mod, tm):
    L = h.shape[0]
    row = lambda i: (i, 0)
    return pl.pallas_call(
        _final_kernel,
        grid=(L // tm,),
        in_specs=[pl.BlockSpec((tm, D_MODEL), row), pl.BlockSpec((tm, D_MODEL), row),
                  pl.BlockSpec((1, D_MODEL), lambda i: (0, 0)),
                  pl.BlockSpec((1, D_MODEL), lambda i: (0, 5))],
        out_specs=pl.BlockSpec((tm, D_MODEL), row),
        out_shape=jax.ShapeDtypeStruct((L, D_MODEL), F32),
        compiler_params=_cparams(1),
        name="final",
    )(h, ff, gpost, mod)


def _mm_kernel(a_ref, w_ref, o_ref):
    o_ref[...] = jnp.dot(a_ref[...], w_ref[...], preferred_element_type=F32).astype(o_ref.dtype)


def _matmul(a, w, out_dtype, tm, tn, name):
    m, k = a.shape
    n = w.shape[1]
    return pl.pallas_call(
        _mm_kernel,
        grid=(m // tm, n // tn),
        in_specs=[pl.BlockSpec((tm, k), lambda i, j: (i, 0)),
                  pl.BlockSpec((k, tn), lambda i, j: (0, j))],
        out_specs=pl.BlockSpec((tm, tn), lambda i, j: (i, j)),
        out_shape=jax.ShapeDtypeStruct((m, n), out_dtype),
        compiler_params=_cparams(2),
        name=name,
    )(a, w)


def _mm_acc_kernel(a_ref, w_ref, o_ref, acc_ref):
    kk = pl.program_id(2)
    part = jnp.dot(a_ref[...], w_ref[...], preferred_element_type=F32)

    @pl.when(kk == 0)
    def _():
        acc_ref[...] = part

    @pl.when(kk > 0)
    def _():
        acc_ref[...] = acc_ref[...] + part

    @pl.when(kk == pl.num_programs(2) - 1)
    def _():
        o_ref[...] = acc_ref[...].astype(o_ref.dtype)


def _matmul_ksplit(a, w, out_dtype, tm, tn, tk, name):
    m, k = a.shape
    n = w.shape[1]
    return pl.pallas_call(
        _mm_acc_kernel,
        grid=(m // tm, n // tn, k // tk),
        in_specs=[pl.BlockSpec((tm, tk), lambda i, j, kk: (i, kk)),
                  pl.BlockSpec((tk, tn), lambda i, j, kk: (kk, j))],
        out_specs=pl.BlockSpec((tm, tn), lambda i, j, kk: (i, j)),
        out_shape=jax.ShapeDtypeStruct((m, n), out_dtype),
        scratch_shapes=[pltpu.VMEM((tm, tn), F32)],
        compiler_params=_cparams(3),
        name=name,
    )(a, w)


def _merge_kernel(oa_ref, ob_ref, wa_ref, wb_ref, ga_ref, gb_ref, o_ref):
    ya = jnp.dot(oa_ref[...], wa_ref[...], preferred_element_type=F32)
    yb = jnp.dot(ob_ref[...], wb_ref[...], preferred_element_type=F32)
    m = _sigmoid(ga_ref[...].astype(F32)) * ya + _sigmoid(gb_ref[...].astype(F32)) * yb
    o_ref[...] = m.astype(o_ref.dtype)


def _merge(o_a, o_b, w_up_a, w_up_b, gates, tm, tn):
    L = o_a.shape[0]
    nb = D_MODEL // tn
    return pl.pallas_call(
        _merge_kernel,
        grid=(L // tm, nb),
        in_specs=[pl.BlockSpec((tm, D_RWKV), lambda i, j: (i, 0)),
                  pl.BlockSpec((tm, D_S5), lambda i, j: (i, 0)),
                  pl.BlockSpec((D_RWKV, tn), lambda i, j: (0, j)),
                  pl.BlockSpec((D_S5, tn), lambda i, j: (0, j)),
                  pl.BlockSpec((tm, tn), lambda i, j: (i, j)),
                  pl.BlockSpec((tm, tn), lambda i, j: (i, j + nb))],
        out_specs=pl.BlockSpec((tm, tn), lambda i, j: (i, j)),
        out_shape=jax.ShapeDtypeStruct((L, D_MODEL), BF16),
        compiler_params=_cparams(2),
        name="merge",
    )(o_a, o_b, w_up_a, w_up_b, gates, gates)


def _ffn_up_kernel(x_ref, wg_ref, wu_ref, o_ref):
    x = x_ref[...]
    a = jnp.dot(x, wg_ref[...], preferred_element_type=F32)
    b = jnp.dot(x, wu_ref[...], preferred_element_type=F32)
    o_ref[...] = ((a * _sigmoid(a)) * b).astype(o_ref.dtype)


def _ffn_up(xf, w_gate, w_up, tm, tn):
    L = xf.shape[0]
    n = w_gate.shape[1]
    return pl.pallas_call(
        _ffn_up_kernel,
        grid=(L // tm, n // tn),
        in_specs=[pl.BlockSpec((tm, D_MODEL), lambda i, j: (i, 0)),
                  pl.BlockSpec((D_MODEL, tn), lambda i, j: (0, j)),
                  pl.BlockSpec((D_MODEL, tn), lambda i, j: (0, j))],
        out_specs=pl.BlockSpec((tm, tn), lambda i, j: (i, j)),
        out_shape=jax.ShapeDtypeStruct((L, n), BF16),
        compiler_params=_cparams(2),
        name="ffn_up",
    )(xf, w_gate, w_up)


def _lora_kernel(p_ref, halo_ref, mu_ref, w0_ref, a0_ref, w2_ref, a2_ref, g2_ref,
                 lw_ref, a_ref, g_ref):
    ps = _token_shift(p_ref[...], halo_ref[...], mu_ref[...], pl.program_id(0) == 0)
    xw = ps[:, :LORA_W]
    xa = ps[:, LORA_W:LORA_W + LORA_A]
    xg = ps[:, LORA_W + LORA_A:]
    z = -(w0_ref[...] + _dot(jnp.tanh(xw), w2_ref[...]))
    softplus = jnp.maximum(z, 0.0) + jnp.log(1.0 + jnp.exp(-jnp.abs(z)))
    lw_ref[...] = -jnp.exp(-softplus - 0.5)
    a_ref[...] = _sigmoid(a0_ref[...] + _dot(xa, a2_ref[...]))
    g_ref[...] = _dot(_sigmoid(xg), g2_ref[...])


def _lora(proj, mu, w0, a0, w2, a2, g2, tm):
    L = proj.shape[0]
    cb = COL_LORA // 1024
    full = lambda shape: pl.BlockSpec(shape, lambda i: (0, 0))
    out = pl.BlockSpec((tm, D_RWKV), lambda i: (i, 0))
    return pl.pallas_call(
        _lora_kernel,
        grid=(L // tm,),
        in_specs=[pl.BlockSpec((tm, 1024), lambda i: (i, cb)),
                  pl.BlockSpec((8, 1024), lambda i: (jnp.maximum(i * (tm // 8) - 1, 0), cb)),
                  pl.BlockSpec((1, 1024), lambda i: (0, cb)),
                  full((1, D_RWKV)), full((1, D_RWKV)),
                  full((LORA_W, D_RWKV)), full((LORA_A, D_RWKV)), full((LORA_G_PAD, D_RWKV))],
        out_specs=[out, out, out],
        out_shape=[jax.ShapeDtypeStruct((L, D_RWKV), F32)] * 3,
        compiler_params=_cparams(1),
        name="lora",
    )(proj, proj, mu, w0, a0, w2, a2, g2)


def _rwkv_block(r, k, v, am, bm, lw, h, cst, T):
    S = 2 * T
    n = r.shape[0] // T
    lane_lo, tri_incl, strict, incl, eye = cst
    cr = range(n)

    def stack(x, c):
        xc = x[c * T:(c + 1) * T]
        return jnp.concatenate([jnp.where(lane_lo, xc, 0.0), jnp.where(lane_lo, 0.0, xc)], axis=0)

    lw_s = [stack(lw, c) for c in cr]
    c_all = _dot_f32_rhs(tri_incl, jnp.concatenate(lw_s, axis=1))
    cs = [c_all[:, c * LANES:(c + 1) * LANES] for c in cr]
    c_end = [cs[c][T - 1:T, :] + cs[c][S - 1:S, :] for c in cr]
    r_s = [stack(r, c) for c in cr]
    k_s = [stack(k, c) for c in cr]
    v_s = [stack(v, c) for c in cr]
    a_s = [stack(am, c) for c in cr]
    b_s = [stack(bm, c) for c in cr]
    e_neg = [jnp.exp(-cs[c]) for c in cr]
    e_end = [jnp.exp(c_end[c] - cs[c]) for c in cr]
    rh = [r_s[c] * jnp.exp(cs[c]) for c in cr]
    ah = [a_s[c] * jnp.exp(cs[c] - lw_s[c]) for c in cr]
    bh = [b_s[c] * e_neg[c] for c in cr]
    kh = [k_s[c] * e_neg[c] for c in cr]
    bc = [b_s[c] * e_end[c] for c in cr]
    kc = [k_s[c] * e_end[c] for c in cr]

    big = [_dot_nt(jnp.concatenate([ah[c], rh[c]], axis=0), jnp.concatenate([bh[c], kh[c]], axis=0))
           for c in cr]
    nj = [jnp.where(strict, big[c][:S, :S], 0.0) for c in cr]
    m_rb = [jnp.where(incl, big[c][S:, :S], 0.0) for c in cr]
    m_kk = [jnp.concatenate([jnp.where(strict, big[c][:S, S:], 0.0),
                             jnp.where(incl, big[c][S:, S:], 0.0)], axis=0) for c in cr]
    avrv = [_dot(m_kk[c], v_s[c]) for c in cr]

    x = [jnp.concatenate([ah[c], avrv[c][:S]], axis=1) for c in cr]
    steps = T.bit_length() - 1
    for j in range(steps):
        x = [x[c] + _dot(nj[c], x[c]) for c in cr]
        if j + 1 < steps:
            nj = [_dot(nj[c], nj[c]) for c in cr]

    y = [_dot(m_rb[c], x[c]) for c in cr]
    q = [rh[c] + y[c][:, :LANES] for c in cr]
    o0 = [avrv[c][S:] + y[c][:, LANES:] for c in cr]
    gd = [_dot_tn(jnp.concatenate([bc[c], kc[c]], axis=0),
                  jnp.concatenate([x[c], jnp.concatenate([jnp.zeros_like(v_s[c]), v_s[c]], axis=1)],
                                  axis=0)) for c in cr]
    decay_col = [jnp.sum(jnp.where(eye, jnp.exp(c_end[c]), 0.0), axis=1, keepdims=True) for c in cr]

    outs = []
    for c in cr:
        z = _dot(jnp.concatenate([gd[c][:, :LANES], q[c]], axis=0), h)
        o_s = z[LANES:] + o0[c]
        outs.append(o_s[:T] + o_s[T:])
        h = decay_col[c] * h + z[:LANES] + gd[c][:, LANES:]
    o = jnp.concatenate(outs, axis=0) if n > 1 else outs[0]
    return o, h


def _rwkv_kernel(r_ref, k_ref, v_ref, rh_ref, kh_ref, vh_ref, mur_ref, muk_ref, muv_ref,
                 lw_ref, a_ref, g_ref, kk_ref, ka_ref, rk_ref, lnw_ref, lnb_ref,
                 o_ref, h_ref):
    i = pl.program_id(1)
    first = i == 0

    @pl.when(first)
    def _():
        h_ref[...] = jnp.zeros_like(h_ref)

    rows = r_ref.shape[0]
    T = RWKV_T
    S = 2 * T
    ii = lax.broadcasted_iota(jnp.int32, (S, S), 0)
    jj = lax.broadcasted_iota(jnp.int32, (S, S), 1)
    same = (ii // T) == (jj // T)
    strict = same & (jj < ii)
    incl = same & (jj <= ii)
    eye = ii == jj
    tri_incl = jnp.where(incl, 1.0, 0.0).astype(BF16)
    hi_ = lax.broadcasted_iota(jnp.int32, (LANES, LANES), 0) // RWKV_HEAD
    hj_ = lax.broadcasted_iota(jnp.int32, (LANES, LANES), 1) // RWKV_HEAD
    head_sum = jnp.where(hi_ == hj_, 1.0, 0.0).astype(BF16)
    lane_lo = lax.broadcasted_iota(jnp.int32, (T, LANES), 1) < RWKV_HEAD
    cst = (lane_lo, tri_incl, strict, incl, eye)

    r = _token_shift(r_ref[...], rh_ref[...], mur_ref[...], first)
    k = _token_shift(k_ref[...], kh_ref[...], muk_ref[...], first)
    v = _token_shift(v_ref[...], vh_ref[...], muv_ref[...], first)
    a = a_ref[...]
    kk = k * kk_ref[...]
    k = k * (1.0 + (a - 1.0) * ka_ref[...])
    ss = _dot_f32_lhs(kk * kk, head_sum)
    kk = kk * lax.rsqrt(jnp.maximum(ss, 1e-24))
    am = -kk
    bm = kk * a
    lw = lw_ref[...]

    o, h = _rwkv_block(r, k, v, am, bm, lw, h_ref[...], cst, T)
    h_ref[...] = h

    inv = 1.0 / RWKV_HEAD
    mean = _dot_f32_lhs(o, head_sum) * inv
    d = o - mean
    var = _dot_f32_lhs(d * d, head_sum) * inv
    y = d * lax.rsqrt(var + GN_EPS) * lnw_ref[...] + lnb_ref[...]
    bonus = _dot_f32_lhs(r * k * rk_ref[...], head_sum) * v
    o_ref[...] = ((y + bonus) * g_ref[...]).astype(o_ref.dtype)


def _rwkv(proj, mu, lw, a, g, k_k, k_a, r_k, ln_w, ln_b, rows):
    L = proj.shape[0]
    nb = D_RWKV // LANES

    def col(off):
        return pl.BlockSpec((rows, LANES), lambda p, i: (i, p + off))

    def halo(off):
        return pl.BlockSpec((8, LANES), lambda p, i: (jnp.maximum(i * (rows // 8) - 1, 0), p + off))

    def vec(off):
        return pl.BlockSpec((1, LANES), lambda p, i: (0, p + off))

    return pl.pallas_call(
        _rwkv_kernel,
        grid=(PAIRS, L // rows),
        in_specs=[col(0), col(nb), col(2 * nb), halo(0), halo(nb), halo(2 * nb),
                  vec(0), vec(nb), vec(2 * nb),
                  col(0), col(0), col(0),
                  vec(0), vec(0), vec(0), vec(0), vec(0)],
        out_specs=col(0),
        out_shape=jax.ShapeDtypeStruct((L, D_RWKV), BF16),
        scratch_shapes=[pltpu.VMEM((LANES, LANES), F32)],
        compiler_params=_cparams(2),
        name="rwkv",
    )(proj, proj, proj, proj, proj, proj, mu, mu, mu, lw, a, g, k_k, k_a, r_k, ln_w, ln_b)


def _s5_table_kernel(lr_ref, li_ref, ldt_ref, pos_re, pos_im, neg_re, neg_im, z_re, z_im):
    lr = lr_ref[...]
    li = li_ref[...]
    dt = jnp.exp(ldt_ref[...])
    t = lax.broadcasted_iota(jnp.int32, pos_re.shape, 0).astype(F32)
    mag = jnp.exp(t * (lr * dt))
    ang = t * (li * dt)
    cs, sn = jnp.cos(ang), jnp.sin(ang)
    pos_re[...] = mag * cs
    pos_im[...] = mag * sn
    inv = 1.0 / mag
    neg_re[...] = inv * cs
    neg_im[...] = -(inv * sn)
    m1 = jnp.exp(lr * dt)
    lb_re = m1 * jnp.cos(li * dt)
    lb_im = m1 * jnp.sin(li * dt)
    den = lr * lr + li * li
    z_re[...] = ((lb_re - 1.0) * lr + lb_im * li) / den
    z_im[...] = (lb_im * lr - (lb_re - 1.0) * li) / den


def _s5_tables(lr, li, ldt, rows):
    n = lr.shape[1]
    full = pl.BlockSpec((1, n), lambda: (0, 0))
    tab = pl.BlockSpec((rows, n), lambda: (0, 0))
    return pl.pallas_call(
        _s5_table_kernel,
        in_specs=[full, full, full],
        out_specs=[tab, tab, tab, tab, full, full],
        out_shape=[jax.ShapeDtypeStruct((rows, n), F32)] * 4 + [jax.ShapeDtypeStruct((1, n), F32)] * 2,
        name="s5_tables",
    )(lr, li, ldt)


def _s5_kernel(u_ref, pos_re_ref, pos_im_ref, neg_re_ref, neg_im_ref, z_re_ref, z_im_ref,
               b_re_ref, b_im_ref, c_re_ref, c_im_ref, d_ref, wglu_ref, bglu_ref,
               o_ref, bb_ref, st_ref):
    T = u_ref.shape[0]
    NS = S5_GROUPS * S5_STATE
    BS = NS // S5_BLOCKS
    BC = D_S5 // S5_BLOCKS

    @pl.when(pl.program_id(0) == 0)
    def _():
        st_ref[...] = jnp.zeros_like(st_ref)
        for j in range(S5_BLOCKS):
            zr = z_re_ref[:, j * BS:(j + 1) * BS]
            zi = z_im_ref[:, j * BS:(j + 1) * BS]
            br = b_re_ref[j]
            bi = b_im_ref[j]
            bb_ref[j, :, :BS] = (zr * br - zi * bi).astype(BF16)
            bb_ref[j, :, BS:] = (zr * bi + zi * br).astype(BF16)

    u = u_ref[...]
    ub = u.astype(BF16)
    ii = lax.broadcasted_iota(jnp.int32, (T, T), 0)
    jj = lax.broadcasted_iota(jnp.int32, (T, T), 1)
    tri = jnp.where(jj <= ii, 1.0, 0.0).astype(BF16)
    s_re = st_ref[0:1, :]
    s_im = st_ref[1:2, :]
    lb_re = pos_re_ref[1:2, :]
    lb_im = pos_im_ref[1:2, :]
    c_in_re = lb_re * s_re - lb_im * s_im
    c_in_im = lb_re * s_im + lb_im * s_re

    ys = []
    for j in range(S5_BLOCKS):
        sl = slice(j * BS, (j + 1) * BS)
        bu = jnp.dot(ub[:, j * BC:(j + 1) * BC], bb_ref[j], preferred_element_type=F32)
        bu_re, bu_im = bu[:, :BS], bu[:, BS:]
        nr, ni = neg_re_ref[:, sl], neg_im_ref[:, sl]
        zz = jnp.concatenate([nr * bu_re - ni * bu_im, nr * bu_im + ni * bu_re], axis=1)
        acc = jnp.dot(tri, zz.astype(BF16), preferred_element_type=F32)
        a_re = acc[:, :BS] + c_in_re[:, sl]
        a_im = acc[:, BS:] + c_in_im[:, sl]
        pr, pi = pos_re_ref[:, sl], pos_im_ref[:, sl]
        x_re = pr * a_re - pi * a_im
        x_im = pr * a_im + pi * a_re
        st_ref[0:1, sl] = x_re[T - 1:T, :]
        st_ref[1:2, sl] = x_im[T - 1:T, :]
        ys.append(_dot(x_re, c_re_ref[j]) - _dot(x_im, c_im_ref[j]))
    y = jnp.concatenate(ys, axis=1) + d_ref[...] * u
    y = 0.5 * y * (1.0 + jnp.tanh(math.sqrt(2.0 / math.pi) * (y + 0.044715 * (y * y * y))))
    zg = _dot(y, wglu_ref[...]) + bglu_ref[...]
    o_ref[...] = (zg[:, :D_S5] * _sigmoid(zg[:, D_S5:])).astype(o_ref.dtype)


def _s5(proj, tables, b_re, b_im, c_re, c_im, d, w_glu, b_glu, rows):
    L = proj.shape[0]
    NS = S5_GROUPS * S5_STATE
    BS = NS // S5_BLOCKS
    BC = D_S5 // S5_BLOCKS
    pos_re, pos_im, neg_re, neg_im, z_re, z_im = tables
    c2 = lambda shape: pl.BlockSpec(shape, lambda i: (0, 0))
    c3 = lambda shape: pl.BlockSpec(shape, lambda i: (0, 0, 0))
    return pl.pallas_call(
        _s5_kernel,
        grid=(L // rows,),
        in_specs=[pl.BlockSpec((rows, D_S5), lambda i: (i, COL_U // D_S5)),
                  c2((rows, NS)), c2((rows, NS)), c2((rows, NS)), c2((rows, NS)),
                  c2((1, NS)), c2((1, NS)),
                  c3((S5_BLOCKS, BC, BS)), c3((S5_BLOCKS, BC, BS)),
                  c3((S5_BLOCKS, BS, BC)), c3((S5_BLOCKS, BS, BC)),
                  c2((1, D_S5)), c2((D_S5, 2 * D_S5)), c2((1, 2 * D_S5))],
        out_specs=pl.BlockSpec((rows, D_S5), lambda i: (i, 0)),
        out_shape=jax.ShapeDtypeStruct((L, D_S5), BF16),
        scratch_shapes=[pltpu.VMEM((S5_BLOCKS, BC, 2 * BS), BF16), pltpu.VMEM((8, NS), F32)],
        compiler_params=_cparams(1),
        name="s5",
    )(proj, pos_re, pos_im, neg_re, neg_im, z_re, z_im, b_re, b_im, c_re, c_im, d, w_glu, b_glu)


def _block_diag_groups(w):
    g, r, c = w.shape
    gb = g // S5_BLOCKS
    w = w.reshape(S5_BLOCKS, gb, r, c)
    eye = jnp.eye(gb, dtype=w.dtype)
    out = w[:, :, :, None, :] * eye[None, :, None, :, None]
    return out.reshape(S5_BLOCKS, gb * r, gb * c)


def kernel(x, c, w_ada, b_ada, norm_pre_mix, norm_post_mix, norm_pre_ffn, norm_post_ffn, w_in, rwkv_mu, rwkv_w0, rwkv_w2, rwkv_a0, rwkv_a2, rwkv_g2, rwkv_k_k, rwkv_k_a, rwkv_r_k, rwkv_ln_w, rwkv_ln_b, s5_lam_re, s5_lam_im, s5_log_dt, s5_b_re, s5_b_im, s5_c_re, s5_c_im, s5_d, s5_w_glu, s5_b_glu, w_up_rwkv, w_up_s5, w_out, ffn_w_gate, ffn_w_up, ffn_w_down):
    bsz, L, _ = x.shape
    assert bsz == 1 and w_ada.shape[0] == 1
    h = x.reshape(L, D_MODEL)
    tm = min(1024, L)
    tr = min(256, L)
    row = lambda v: v.reshape(1, -1)

    mod = _ada(c.reshape(D_MODEL, 1), w_ada[0], row(b_ada[0]))

    wi = w_in[0]
    gpad = LORA_G_PAD - LORA_G
    w_a = jnp.concatenate([wi[:, :RWKV_COLS].astype(BF16), jnp.zeros((D_MODEL, gpad), BF16),
                           wi[:, RWKV_COLS:RWKV_COLS + D_S5].astype(BF16)], axis=1)
    w_g = wi[:, RWKV_COLS + D_S5:].astype(BF16)
    mu = jnp.concatenate([rwkv_mu[0], jnp.zeros((gpad,), F32)]).reshape(1, -1)
    g2 = jnp.concatenate([rwkv_g2[0], jnp.zeros((gpad, D_RWKV), F32)], axis=0).astype(BF16)
    fpad = D_FF_PAD - D_FF
    wf_gate = jnp.pad(ffn_w_gate[0].astype(BF16), ((0, 0), (0, fpad)))
    wf_up = jnp.pad(ffn_w_up[0].astype(BF16), ((0, 0), (0, fpad)))
    wf_down = jnp.pad(ffn_w_down[0].astype(BF16), ((0, fpad), (0, 0)))

    xm = _prenorm(h, row(norm_pre_mix[0]), mod, 0, 1, tr)
    proj = _matmul(xm, w_a, F32, tm, 1024, "proj_a")
    gates = _matmul(xm, w_g, BF16, tm, 1024, "proj_g")

    lw, a, g = _lora(proj, mu, row(rwkv_w0[0]), row(rwkv_a0[0]), rwkv_w2[0].astype(BF16),
                     rwkv_a2[0].astype(BF16), g2, tr)
    o_a = _rwkv(proj, mu, lw, a, g, row(rwkv_k_k[0]), row(rwkv_k_a[0]), row(rwkv_r_k[0]),
                row(rwkv_ln_w[0]), row(rwkv_ln_b[0]), min(RWKV_T * RWKV_CHUNKS_PER_STEP, L))

    rep = lambda v: jnp.repeat(v, S5_STATE).reshape(1, -1)
    tables = _s5_tables(row(s5_lam_re[0]), row(s5_lam_im[0]), rep(s5_log_dt[0]), S5_T)
    bt = lambda w: _block_diag_groups(jnp.swapaxes(w, 1, 2))
    o_b = _s5(proj, tables, bt(s5_b_re[0]), bt(s5_b_im[0]),
              _block_diag_groups(jnp.swapaxes(s5_c_re[0], 1, 2)).astype(BF16),
              _block_diag_groups(jnp.swapaxes(s5_c_im[0], 1, 2)).astype(BF16),
              row(s5_d[0]), s5_w_glu[0].astype(BF16), row(s5_b_glu[0]), S5_T)

    merged = _merge(o_a, o_b, w_up_rwkv[0].astype(BF16), w_up_s5[0].astype(BF16), gates, tm, 1024)
    mix = _matmul(merged, w_out[0].astype(BF16), F32, tm, 1024, "w_out")

    h1, xf = _mid(h, mix, row(norm_post_mix[0]), row(norm_pre_ffn[0]), mod, tr)
    act = _ffn_up(xf, wf_gate, wf_up, tm, 512)
    ff = _matmul_ksplit(act, wf_down, F32, tm, 1024, D_FF_PAD // 4, "ffn_down")
    out = _final(h1, ff, row(norm_post_ffn[0]), mod, tr)
    return out.reshape(bsz, L, D_MODEL)
```

```python
import functools
import math

import jax
import jax.numpy as jnp
from jax import lax
from jax.experimental import pallas as pl
from jax.experimental.pallas import tpu as pltpu

F32 = jnp.float32
BF16 = jnp.bfloat16

D_MODEL = 4096
RMS_EPS = 1e-6
D_RWKV = 2048
RWKV_HEAD = 64
LORA_W = 128
LORA_A = 128
LORA_G = 480
LORA_G_PAD = 768
GN_EPS = 64e-5
D_S5 = 1024
S5_GROUPS = 64
S5_GROUP_CH = 16
S5_STATE = 64
S5_BLOCKS = 4
D_FF = 11008

LANES = 128
PAIRS = D_RWKV // LANES
RWKV_T = 64
RWKV_CHUNKS_PER_STEP = 8
S5_T = 64

COL_LORA = 0
COL_U = 1024
RWKV_COLS = 3 * D_RWKV + LORA_W + LORA_A + LORA_G

VMEM_LIMIT = 56 * 1024 * 1024


def _cparams(n_axes, vmem=VMEM_LIMIT):
    return pltpu.CompilerParams(dimension_semantics=("arbitrary",) * n_axes, vmem_limit_bytes=vmem)


def _dot(a, b):
    return jnp.dot(a.astype(BF16), b.astype(BF16), preferred_element_type=F32)


def _dot_nt(a, b):
    return lax.dot_general(a.astype(BF16), b.astype(BF16), (((1,), (1,)), ((), ())),
                           preferred_element_type=F32)


def _dot_tn(a, b):
    return lax.dot_general(a.astype(BF16), b.astype(BF16), (((0,), (0,)), ((), ())),
                           preferred_element_type=F32)


def _split3(x):
    hi = x.astype(BF16)
    r1 = x - hi.astype(F32)
    mid = r1.astype(BF16)
    lo = (r1 - mid.astype(F32)).astype(BF16)
    return hi, mid, lo


def _dot_f32_lhs(x, sel):
    hi, mid, lo = _split3(x)
    d = lambda p: jnp.dot(p, sel, preferred_element_type=F32)
    return d(hi) + d(mid) + d(lo)


def _dot_f32_rhs(sel, x):
    hi, mid, lo = _split3(x)
    d = lambda p: jnp.dot(sel, p, preferred_element_type=F32)
    return d(hi) + d(mid) + d(lo)


def _rms(x):
    return x * lax.rsqrt(jnp.mean(x * x, axis=-1, keepdims=True) + RMS_EPS)


def _sigmoid(x):
    return 1.0 / (1.0 + jnp.exp(-x))


def _token_shift(p, halo, mu, first):
    last = jnp.where(first, 0.0, halo[7:8, :])
    rolled = pltpu.roll(p, 1, axis=0)
    row = lax.broadcasted_iota(jnp.int32, p.shape, 0)
    prev = jnp.where(row == 0, last, rolled)
    return p + (prev - p) * mu


def _ada_kernel(c_ref, w_ref, b_ref, o_ref):
    c = c_ref[...]
    cs = c * _sigmoid(c)
    tn = o_ref.shape[1]
    rows = 512
    acc = jnp.zeros((8, tn), F32)
    for k0 in range(0, D_MODEL, rows):
        blk = w_ref[k0:k0 + rows, :] * cs[k0:k0 + rows, :]
        acc = acc + jnp.sum(blk.reshape(rows // 8, 8, tn), axis=0)
    o_ref[...] = jnp.sum(acc, axis=0, keepdims=True) + b_ref[...]


def _ada(c_col, w_ada, b_ada):
    n = w_ada.shape[1]
    tn = 512
    return pl.pallas_call(
        _ada_kernel,
        grid=(n // tn,),
        in_specs=[pl.BlockSpec((D_MODEL, 1), lambda j: (0, 0)),
                  pl.BlockSpec((D_MODEL, tn), lambda j: (0, j)),
                  pl.BlockSpec((1, tn), lambda j: (0, j))],
        out_specs=pl.BlockSpec((1, tn), lambda j: (0, j)),
        out_shape=jax.ShapeDtypeStruct((1, n), F32),
        compiler_params=_cparams(1),
        name="ada",
    )(c_col, w_ada, b_ada)


def _prenorm_kernel(x_ref, g_ref, sh_ref, sc_ref, o_ref):
    y = _rms(x_ref[...]) * g_ref[...]
    o_ref[...] = (y * (1.0 + sc_ref[...]) + sh_ref[...]).astype(o_ref.dtype)


def _prenorm(x, g, mod, sh_idx, sc_idx, tm):
    L = x.shape[0]
    return pl.pallas_call(
        _prenorm_kernel,
        grid=(L // tm,),
        in_specs=[pl.BlockSpec((tm, D_MODEL), lambda i: (i, 0)),
                  pl.BlockSpec((1, D_MODEL), lambda i: (0, 0)),
                  pl.BlockSpec((1, D_MODEL), lambda i: (0, sh_idx)),
                  pl.BlockSpec((1, D_MODEL), lambda i: (0, sc_idx))],
        out_specs=pl.BlockSpec((tm, D_MODEL), lambda i: (i, 0)),
        out_shape=jax.ShapeDtypeStruct((L, D_MODEL), BF16),
        compiler_params=_cparams(1),
        name="prenorm",
    )(x, g, mod, mod)


def _mid_kernel(x_ref, mix_ref, gpost_ref, gate_ref, gpre_ref, sh_ref, sc_ref, h_ref, xf_ref):
    h = x_ref[...] + gate_ref[...] * (_rms(mix_ref[...]) * gpost_ref[...])
    h_ref[...] = h
    y = _rms(h) * gpre_ref[...]
    xf_ref[...] = (y * (1.0 + sc_ref[...]) + sh_ref[...]).astype(xf_ref.dtype)


def _mid(x, mix, gpost, gpre, mod, tm):
    L = x.shape[0]
    row = lambda i: (i, 0)
    vec = lambda k: pl.BlockSpec((1, D_MODEL), lambda i: (0, k))
    return pl.pallas_call(
        _mid_kernel,
        grid=(L // tm,),
        in_specs=[pl.BlockSpec((tm, D_MODEL), row), pl.BlockSpec((tm, D_MODEL), row),
                  vec(0), vec(2), vec(0), vec(3), vec(4)],
        out_specs=[pl.BlockSpec((tm, D_MODEL), row), pl.BlockSpec((tm, D_MODEL), row)],
        out_shape=[jax.ShapeDtypeStruct((L, D_MODEL), F32), jax.ShapeDtypeStruct((L, D_MODEL), BF16)],
        compiler_params=_cparams(1),
        name="mid",
    )(x, mix, gpost, mod, gpre, mod, mod)


def _final_kernel(h_ref, ff_ref, gpost_ref, gate_ref, o_ref):
    o_ref[...] = h_ref[...] + gate_ref[...] * (_rms(ff_ref[...]) * gpost_ref[...])


def _final(h, ff, gpost, mod, tm):
    L = h.shape[0]
    row = lambda i: (i, 0)
    return pl.pallas_call(
        _final_kernel,
        grid=(L // tm,),
        in_specs=[pl.BlockSpec((tm, D_MODEL), row), pl.BlockSpec((tm, D_MODEL), row),
                  pl.BlockSpec((1, D_MODEL), lambda i: (0, 0)),
                  pl.BlockSpec((1, D_MODEL), lambda i: (0, 5))],
        out_specs=pl.BlockSpec((tm, D_MODEL), row),
        out_shape=jax.ShapeDtypeStruct((L, D_MODEL), F32),
        compiler_params=_cparams(1),
        name="final",
    )(h, ff, gpost, mod)


def _mm_kernel(a_ref, w_ref, o_ref):
    o_ref[...] = jnp.dot(a_ref[...], w_ref[...].astype(BF16),
                         preferred_element_type=F32).astype(o_ref.dtype)


def _matmul(a, w, out_dtype, tm, tn, name, n=None):
    m, k = a.shape
    n = w.shape[1] if n is None else n
    return pl.pallas_call(
        _mm_kernel,
        grid=(m // tm, n // tn),
        in_specs=[pl.BlockSpec((tm, k), lambda i, j: (i, 0)),
                  pl.BlockSpec((k, tn), lambda i, j: (0, j))],
        out_specs=pl.BlockSpec((tm, tn), lambda i, j: (i, j)),
        out_shape=jax.ShapeDtypeStruct((m, n), out_dtype),
        compiler_params=_cparams(2),
        name=name,
    )(a, w)


def _mm_acc_kernel(a_ref, w_ref, o_ref, acc_ref):
    kk = pl.program_id(2)
    part = jnp.dot(a_ref[...], w_ref[...], preferred_element_type=F32)

    @pl.when(kk == 0)
    def _():
        acc_ref[...] = part

    @pl.when(kk > 0)
    def _():
        acc_ref[...] = acc_ref[...] + part

    @pl.when(kk == pl.num_programs(2) - 1)
    def _():
        o_ref[...] = acc_ref[...].astype(o_ref.dtype)


def _matmul_ksplit(a, w, out_dtype, tm, tn, tk, name):
    m, k = a.shape
    n = w.shape[1]
    return pl.pallas_call(
        _mm_acc_kernel,
        grid=(m // tm, n // tn, k // tk),
        in_specs=[pl.BlockSpec((tm, tk), lambda i, j, kk: (i, kk)),
                  pl.BlockSpec((tk, tn), lambda i, j, kk: (kk, j))],
        out_specs=pl.BlockSpec((tm, tn), lambda i, j, kk: (i, j)),
        out_shape=jax.ShapeDtypeStruct((m, n), out_dtype),
        scratch_shapes=[pltpu.VMEM((tm, tn), F32)],
        compiler_params=_cparams(3),
        name=name,
    )(a, w)


def _merge_kernel(oa_ref, ob_ref, wa_ref, wb_ref, ga_ref, gb_ref, o_ref):
    ya = jnp.dot(oa_ref[...], wa_ref[...].astype(BF16), preferred_element_type=F32)
    yb = jnp.dot(ob_ref[...], wb_ref[...].astype(BF16), preferred_element_type=F32)
    m = _sigmoid(ga_ref[...].astype(F32)) * ya + _sigmoid(gb_ref[...].astype(F32)) * yb
    o_ref[...] = m.astype(o_ref.dtype)


def _merge(o_a, o_b, w_up_a, w_up_b, gates, tm, tn):
    L = o_a.shape[0]
    nb = D_MODEL // tn
    return pl.pallas_call(
        _merge_kernel,
        grid=(L // tm, nb),
        in_specs=[pl.BlockSpec((tm, D_RWKV), lambda i, j: (i, 0)),
                  pl.BlockSpec((tm, D_S5), lambda i, j: (i, 0)),
                  pl.BlockSpec((D_RWKV, tn), lambda i, j: (0, j)),
                  pl.BlockSpec((D_S5, tn), lambda i, j: (0, j)),
                  pl.BlockSpec((tm, tn), lambda i, j: (i, j)),
                  pl.BlockSpec((tm, tn), lambda i, j: (i, j + nb))],
        out_specs=pl.BlockSpec((tm, tn), lambda i, j: (i, j)),
        out_shape=jax.ShapeDtypeStruct((L, D_MODEL), BF16),
        compiler_params=_cparams(2),
        name="merge",
    )(o_a, o_b, w_up_a, w_up_b, gates, gates)


def _ffn_up_kernel(x_ref, wg_ref, wu_ref, o_ref):
    x = x_ref[...]
    a = jnp.dot(x, wg_ref[...].astype(BF16), preferred_element_type=F32)
    b = jnp.dot(x, wu_ref[...].astype(BF16), preferred_element_type=F32)
    o_ref[...] = ((a * _sigmoid(a)) * b).astype(o_ref.dtype)


def _ffn_up(xf, w_gate, w_up, tm, tn):
    L = xf.shape[0]
    n = w_gate.shape[1]
    return pl.pallas_call(
        _ffn_up_kernel,
        grid=(L // tm, n // tn),
        in_specs=[pl.BlockSpec((tm, D_MODEL), lambda i, j: (i, 0)),
                  pl.BlockSpec((D_MODEL, tn), lambda i, j: (0, j)),
                  pl.BlockSpec((D_MODEL, tn), lambda i, j: (0, j))],
        out_specs=pl.BlockSpec((tm, tn), lambda i, j: (i, j)),
        out_shape=jax.ShapeDtypeStruct((L, n), BF16),
        compiler_params=_cparams(2),
        name="ffn_up",
    )(xf, w_gate, w_up)


def _lora_kernel(p_ref, halo_ref, mu_ref, w0_ref, a0_ref, w2_ref, a2_ref, g2_ref,
                 lw_ref, a_ref, g_ref):
    ps = _token_shift(p_ref[...], halo_ref[...], mu_ref[...], pl.program_id(0) == 0)
    xw = ps[:, :LORA_W]
    xa = ps[:, LORA_W:LORA_W + LORA_A]
    xg = ps[:, LORA_W + LORA_A:]
    z = -(w0_ref[...] + _dot(jnp.tanh(xw), w2_ref[...]))
    softplus = jnp.maximum(z, 0.0) + jnp.log(1.0 + jnp.exp(-jnp.abs(z)))
    lw_ref[...] = -jnp.exp(-softplus - 0.5)
    a_ref[...] = _sigmoid(a0_ref[...] + _dot(xa, a2_ref[...]))
    g_ref[...] = _dot(_sigmoid(xg), g2_ref[...])


def _lora(proj, mu, w0, a0, w2, a2, g2, tm):
    L = proj.shape[0]
    cb = COL_LORA // 1024
    full = lambda shape: pl.BlockSpec(shape, lambda i: (0, 0))
    out = pl.BlockSpec((tm, D_RWKV), lambda i: (i, 0))
    return pl.pallas_call(
        _lora_kernel,
        grid=(L // tm,),
        in_specs=[pl.BlockSpec((tm, 1024), lambda i: (i, cb)),
                  pl.BlockSpec((8, 1024), lambda i: (jnp.maximum(i * (tm // 8) - 1, 0), cb)),
                  pl.BlockSpec((1, 1024), lambda i: (0, cb)),
                  full((1, D_RWKV)), full((1, D_RWKV)),
                  full((LORA_W, D_RWKV)), full((LORA_A, D_RWKV)), full((LORA_G_PAD, D_RWKV))],
        out_specs=[out, out, out],
        out_shape=[jax.ShapeDtypeStruct((L, D_RWKV), F32)] * 3,
        compiler_params=_cparams(1),
        name="lora",
    )(proj, proj, mu, w0, a0, w2, a2, g2)


def _rwkv_block(r, k, v, am, bm, lw, h, cst, T):
    S = 2 * T
    n = r.shape[0] // T
    lane_lo, tri_incl, strict, incl, eye = cst
    cr = range(n)

    def stack(x, c):
        xc = x[c * T:(c + 1) * T]
        return jnp.concatenate([jnp.where(lane_lo, xc, 0.0), jnp.where(lane_lo, 0.0, xc)], axis=0)

    lw_s = [stack(lw, c) for c in cr]
    c_all = _dot_f32_rhs(tri_incl, jnp.concatenate(lw_s, axis=1))
    cs = [c_all[:, c * LANES:(c + 1) * LANES] for c in cr]
    c_end = [cs[c][T - 1:T, :] + cs[c][S - 1:S, :] for c in cr]
    r_s = [stack(r, c) for c in cr]
    k_s = [stack(k, c) for c in cr]
    v_s = [stack(v, c) for c in cr]
    a_s = [stack(am, c) for c in cr]
    b_s = [stack(bm, c) for c in cr]
    e_neg = [jnp.exp(-cs[c]) for c in cr]
    e_end = [jnp.exp(c_end[c] - cs[c]) for c in cr]
    rh = [r_s[c] * jnp.exp(cs[c]) for c in cr]
    ah = [a_s[c] * jnp.exp(cs[c] - lw_s[c]) for c in cr]
    bh = [b_s[c] * e_neg[c] for c in cr]
    kh = [k_s[c] * e_neg[c] for c in cr]
    bc = [b_s[c] * e_end[c] for c in cr]
    kc = [k_s[c] * e_end[c] for c in cr]

    big = [_dot_nt(jnp.concatenate([ah[c], rh[c]], axis=0), jnp.concatenate([bh[c], kh[c]], axis=0))
           for c in cr]
    nj = [jnp.where(strict, big[c][:S, :S], 0.0) for c in cr]
    m_rb = [jnp.where(incl, big[c][S:, :S], 0.0) for c in cr]
    m_kk = [jnp.concatenate([jnp.where(strict, big[c][:S, S:], 0.0),
                             jnp.where(incl, big[c][S:, S:], 0.0)], axis=0) for c in cr]
    avrv = [_dot(m_kk[c], v_s[c]) for c in cr]

    x = [jnp.concatenate([ah[c], avrv[c][:S]], axis=1) for c in cr]
    steps = T.bit_length() - 1
    for j in range(steps):
        x = [x[c] + _dot(nj[c], x[c]) for c in cr]
        if j + 1 < steps:
            nj = [_dot(nj[c], nj[c]) for c in cr]

    y = [_dot(m_rb[c], x[c]) for c in cr]
    q = [rh[c] + y[c][:, :LANES] for c in cr]
    o0 = [avrv[c][S:] + y[c][:, LANES:] for c in cr]
    gd = [_dot_tn(jnp.concatenate([bc[c], kc[c]], axis=0),
                  jnp.concatenate([x[c], jnp.concatenate([jnp.zeros_like(v_s[c]), v_s[c]], axis=1)],
                                  axis=0)) for c in cr]
    decay_col = [jnp.sum(jnp.where(eye, jnp.exp(c_end[c]), 0.0), axis=1, keepdims=True) for c in cr]

    outs = []
    for c in cr:
        z = _dot(jnp.concatenate([gd[c][:, :LANES], q[c]], axis=0), h)
        o_s = z[LANES:] + o0[c]
        outs.append(o_s[:T] + o_s[T:])
        h = decay_col[c] * h + z[:LANES] + gd[c][:, LANES:]
    o = jnp.concatenate(outs, axis=0) if n > 1 else outs[0]
    return o, h


def _rwkv_kernel(r_ref, k_ref, v_ref, rh_ref, kh_ref, vh_ref, mur_ref, muk_ref, muv_ref,
                 lw_ref, a_ref, g_ref, kk_ref, ka_ref, rk_ref, lnw_ref, lnb_ref,
                 o_ref, h_ref):
    i = pl.program_id(1)
    first = i == 0

    @pl.when(first)
    def _():
        h_ref[...] = jnp.zeros_like(h_ref)

    rows = r_ref.shape[0]
    T = RWKV_T
    S = 2 * T
    ii = lax.broadcasted_iota(jnp.int32, (S, S), 0)
    jj = lax.broadcasted_iota(jnp.int32, (S, S), 1)
    same = (ii // T) == (jj // T)
    strict = same & (jj < ii)
    incl = same & (jj <= ii)
    eye = ii == jj
    tri_incl = jnp.where(incl, 1.0, 0.0).astype(BF16)
    hi_ = lax.broadcasted_iota(jnp.int32, (LANES, LANES), 0) // RWKV_HEAD
    hj_ = lax.broadcasted_iota(jnp.int32, (LANES, LANES), 1) // RWKV_HEAD
    head_sum = jnp.where(hi_ == hj_, 1.0, 0.0).astype(BF16)
    lane_lo = lax.broadcasted_iota(jnp.int32, (T, LANES), 1) < RWKV_HEAD
    cst = (lane_lo, tri_incl, strict, incl, eye)

    r = _token_shift(r_ref[...], rh_ref[...], mur_ref[...], first)
    k = _token_shift(k_ref[...], kh_ref[...], muk_ref[...], first)
    v = _token_shift(v_ref[...], vh_ref[...], muv_ref[...], first)
    a = a_ref[...]
    kk = k * kk_ref[...]
    k = k * (1.0 + (a - 1.0) * ka_ref[...])
    ss = _dot_f32_lhs(kk * kk, head_sum)
    kk = kk * lax.rsqrt(jnp.maximum(ss, 1e-24))
    am = -kk
    bm = kk * a
    lw = lw_ref[...]

    o, h = _rwkv_block(r, k, v, am, bm, lw, h_ref[...], cst, T)
    h_ref[...] = h

    inv = 1.0 / RWKV_HEAD
    mean = _dot_f32_lhs(o, head_sum) * inv
    d = o - mean
    var = _dot_f32_lhs(d * d, head_sum) * inv
    y = d * lax.rsqrt(var + GN_EPS) * lnw_ref[...] + lnb_ref[...]
    bonus = _dot_f32_lhs(r * k * rk_ref[...], head_sum) * v
    o_ref[...] = ((y + bonus) * g_ref[...]).astype(o_ref.dtype)


def _rwkv(proj, mu, lw, a, g, k_k, k_a, r_k, ln_w, ln_b, rows):
    L = proj.shape[0]
    nb = D_RWKV // LANES

    def col(off):
        return pl.BlockSpec((rows, LANES), lambda p, i: (i, p + off))

    def halo(off):
        return pl.BlockSpec((8, LANES), lambda p, i: (jnp.maximum(i * (rows // 8) - 1, 0), p + off))

    def vec(off):
        return pl.BlockSpec((1, LANES), lambda p, i: (0, p + off))

    return pl.pallas_call(
        _rwkv_kernel,
        grid=(PAIRS, L // rows),
        in_specs=[col(0), col(nb), col(2 * nb), halo(0), halo(nb), halo(2 * nb),
                  vec(0), vec(nb), vec(2 * nb),
                  col(0), col(0), col(0),
                  vec(0), vec(0), vec(0), vec(0), vec(0)],
        out_specs=col(0),
        out_shape=jax.ShapeDtypeStruct((L, D_RWKV), BF16),
        scratch_shapes=[pltpu.VMEM((LANES, LANES), F32)],
        compiler_params=_cparams(2),
        name="rwkv",
    )(proj, proj, proj, proj, proj, proj, mu, mu, mu, lw, a, g, k_k, k_a, r_k, ln_w, ln_b)


def _s5_table_kernel(lr_ref, li_ref, ldt_ref, pos_re, pos_im, neg_re, neg_im, z_re, z_im):
    lr = lr_ref[...]
    li = li_ref[...]
    dt = jnp.exp(ldt_ref[...])
    t = lax.broadcasted_iota(jnp.int32, pos_re.shape, 0).astype(F32)
    mag = jnp.exp(t * (lr * dt))
    ang = t * (li * dt)
    cs, sn = jnp.cos(ang), jnp.sin(ang)
    pos_re[...] = mag * cs
    pos_im[...] = mag * sn
    inv = 1.0 / mag
    neg_re[...] = inv * cs
    neg_im[...] = -(inv * sn)
    m1 = jnp.exp(lr * dt)
    lb_re = m1 * jnp.cos(li * dt)
    lb_im = m1 * jnp.sin(li * dt)
    den = lr * lr + li * li
    z_re[...] = ((lb_re - 1.0) * lr + lb_im * li) / den
    z_im[...] = (lb_im * lr - (lb_re - 1.0) * li) / den


def _s5_tables(lr, li, ldt, rows):
    n = lr.shape[1]
    full = pl.BlockSpec((1, n), lambda: (0, 0))
    tab = pl.BlockSpec((rows, n), lambda: (0, 0))
    return pl.pallas_call(
        _s5_table_kernel,
        in_specs=[full, full, full],
        out_specs=[tab, tab, tab, tab, full, full],
        out_shape=[jax.ShapeDtypeStruct((rows, n), F32)] * 4 + [jax.ShapeDtypeStruct((1, n), F32)] * 2,
        name="s5_tables",
    )(lr, li, ldt)


def _s5_kernel(u_ref, pos_re_ref, pos_im_ref, neg_re_ref, neg_im_ref, z_re_ref, z_im_ref,
               b_re_ref, b_im_ref, c_re_ref, c_im_ref, d_ref, wglu_ref, bglu_ref,
               o_ref, bb_ref, st_ref):
    T = u_ref.shape[0]
    NS = S5_GROUPS * S5_STATE
    BS = NS // S5_BLOCKS
    BC = D_S5 // S5_BLOCKS

    @pl.when(pl.program_id(0) == 0)
    def _():
        st_ref[...] = jnp.zeros_like(st_ref)
        for j in range(S5_BLOCKS):
            zr = z_re_ref[:, j * BS:(j + 1) * BS]
            zi = z_im_ref[:, j * BS:(j + 1) * BS]
            br = b_re_ref[j]
            bi = b_im_ref[j]
            bb_ref[j, :, :BS] = (zr * br - zi * bi).astype(BF16)
            bb_ref[j, :, BS:] = (zr * bi + zi * br).astype(BF16)

    u = u_ref[...]
    ub = u.astype(BF16)
    ii = lax.broadcasted_iota(jnp.int32, (T, T), 0)
    jj = lax.broadcasted_iota(jnp.int32, (T, T), 1)
    tri = jnp.where(jj <= ii, 1.0, 0.0).astype(BF16)
    s_re = st_ref[0:1, :]
    s_im = st_ref[1:2, :]
    lb_re = pos_re_ref[1:2, :]
    lb_im = pos_im_ref[1:2, :]
    c_in_re = lb_re * s_re - lb_im * s_im
    c_in_im = lb_re * s_im + lb_im * s_re

    ys = []
    for j in range(S5_BLOCKS):
        sl = slice(j * BS, (j + 1) * BS)
        bu = jnp.dot(ub[:, j * BC:(j + 1) * BC], bb_ref[j], preferred_element_type=F32)
        bu_re, bu_im = bu[:, :BS], bu[:, BS:]
        nr, ni = neg_re_ref[:, sl], neg_im_ref[:, sl]
        zz = jnp.concatenate([nr * bu_re - ni * bu_im, nr * bu_im + ni * bu_re], axis=1)
        acc = jnp.dot(tri, zz.astype(BF16), preferred_element_type=F32)
        a_re = acc[:, :BS] + c_in_re[:, sl]
        a_im = acc[:, BS:] + c_in_im[:, sl]
        pr, pi = pos_re_ref[:, sl], pos_im_ref[:, sl]
        x_re = pr * a_re - pi * a_im
        x_im = pr * a_im + pi * a_re
        st_ref[0:1, sl] = x_re[T - 1:T, :]
        st_ref[1:2, sl] = x_im[T - 1:T, :]
        ys.append(_dot(x_re, c_re_ref[j]) - _dot(x_im, c_im_ref[j]))
    y = jnp.concatenate(ys, axis=1) + d_ref[...] * u
    y = 0.5 * y * (1.0 + jnp.tanh(math.sqrt(2.0 / math.pi) * (y + 0.044715 * (y * y * y))))
    zg = _dot(y, wglu_ref[...]) + bglu_ref[...]
    o_ref[...] = (zg[:, :D_S5] * _sigmoid(zg[:, D_S5:])).astype(o_ref.dtype)


def _s5(proj, tables, b_re, b_im, c_re, c_im, d, w_glu, b_glu, rows):
    L = proj.shape[0]
    NS = S5_GROUPS * S5_STATE
    BS = NS // S5_BLOCKS
    BC = D_S5 // S5_BLOCKS
    pos_re, pos_im, neg_re, neg_im, z_re, z_im = tables
    c2 = lambda shape: pl.BlockSpec(shape, lambda i: (0, 0))
    c3 = lambda shape: pl.BlockSpec(shape, lambda i: (0, 0, 0))
    return pl.pallas_call(
        _s5_kernel,
        grid=(L // rows,),
        in_specs=[pl.BlockSpec((rows, D_S5), lambda i: (i, COL_U // D_S5)),
                  c2((rows, NS)), c2((rows, NS)), c2((rows, NS)), c2((rows, NS)),
                  c2((1, NS)), c2((1, NS)),
                  c3((S5_BLOCKS, BC, BS)), c3((S5_BLOCKS, BC, BS)),
                  c3((S5_BLOCKS, BS, BC)), c3((S5_BLOCKS, BS, BC)),
                  c2((1, D_S5)), c2((D_S5, 2 * D_S5)), c2((1, 2 * D_S5))],
        out_specs=pl.BlockSpec((rows, D_S5), lambda i: (i, 0)),
        out_shape=jax.ShapeDtypeStruct((L, D_S5), BF16),
        scratch_shapes=[pltpu.VMEM((S5_BLOCKS, BC, 2 * BS), BF16), pltpu.VMEM((8, NS), F32)],
        compiler_params=_cparams(1),
        name="s5",
    )(proj, pos_re, pos_im, neg_re, neg_im, z_re, z_im, b_re, b_im, c_re, c_im, d, w_glu, b_glu)


def _block_diag_groups(w):
    g, r, c = w.shape
    gb = g // S5_BLOCKS
    w = w.reshape(S5_BLOCKS, gb, r, c)
    eye = jnp.eye(gb, dtype=w.dtype)
    out = w[:, :, :, None, :] * eye[None, :, None, :, None]
    return out.reshape(S5_BLOCKS, gb * r, gb * c)


def kernel(x, c, w_ada, b_ada, norm_pre_mix, norm_post_mix, norm_pre_ffn, norm_post_ffn, w_in, rwkv_mu, rwkv_w0, rwkv_w2, rwkv_a0, rwkv_a2, rwkv_g2, rwkv_k_k, rwkv_k_a, rwkv_r_k, rwkv_ln_w, rwkv_ln_b, s5_lam_re, s5_lam_im, s5_log_dt, s5_b_re, s5_b_im, s5_c_re, s5_c_im, s5_d, s5_w_glu, s5_b_glu, w_up_rwkv, w_up_s5, w_out, ffn_w_gate, ffn_w_up, ffn_w_down):
    bsz, L, _ = x.shape
    assert bsz == 1 and w_ada.shape[0] == 1
    h = x.reshape(L, D_MODEL)
    tm = min(1024, L)
    tr = min(256, L)
    row = lambda v: v.reshape(1, -1)

    mod = _ada(c.reshape(D_MODEL, 1), w_ada[0], row(b_ada[0]))

    wi = w_in[0]
    gpad = LORA_G_PAD - LORA_G
    n_rkv = 3 * D_RWKV
    w_lu = jnp.concatenate([wi[:, n_rkv:RWKV_COLS].astype(BF16), jnp.zeros((D_MODEL, gpad), BF16),
                            wi[:, RWKV_COLS:RWKV_COLS + D_S5].astype(BF16)], axis=1)
    w_g = wi[:, RWKV_COLS + D_S5:].astype(BF16)
    mu = row(rwkv_mu[0])
    mu_lora = jnp.concatenate([rwkv_mu[0, n_rkv:], jnp.zeros((gpad,), F32)]).reshape(1, -1)
    g2 = jnp.concatenate([rwkv_g2[0], jnp.zeros((gpad, D_RWKV), F32)], axis=0).astype(BF16)

    xm = _prenorm(h, row(norm_pre_mix[0]), mod, 0, 1, tr)
    proj_rkv = _matmul(xm, wi, F32, tm, 512, "proj_rkv", n=n_rkv)
    proj_lu = _matmul(xm, w_lu, F32, tm, 1024, "proj_lu")
    gates = _matmul(xm, w_g, BF16, tm, 1024, "proj_g")

    lw, a, g = _lora(proj_lu, mu_lora, row(rwkv_w0[0]), row(rwkv_a0[0]), rwkv_w2[0].astype(BF16),
                     rwkv_a2[0].astype(BF16), g2, tr)
    o_a = _rwkv(proj_rkv, mu, lw, a, g, row(rwkv_k_k[0]), row(rwkv_k_a[0]), row(rwkv_r_k[0]),
                row(rwkv_ln_w[0]), row(rwkv_ln_b[0]), min(RWKV_T * RWKV_CHUNKS_PER_STEP, L))

    rep = lambda v: jnp.repeat(v, S5_STATE).reshape(1, -1)
    tables = _s5_tables(row(s5_lam_re[0]), row(s5_lam_im[0]), rep(s5_log_dt[0]), S5_T)
    bt = lambda w: _block_diag_groups(jnp.swapaxes(w, 1, 2))
    o_b = _s5(proj_lu, tables, bt(s5_b_re[0]), bt(s5_b_im[0]),
              _block_diag_groups(jnp.swapaxes(s5_c_re[0], 1, 2)).astype(BF16),
              _block_diag_groups(jnp.swapaxes(s5_c_im[0], 1, 2)).astype(BF16),
              row(s5_d[0]), s5_w_glu[0].astype(BF16), row(s5_b_glu[0]), S5_T)

    merged = _merge(o_a, o_b, w_up_rwkv[0], w_up_s5[0], gates, tm, 512)
    mix = _matmul(merged, w_out[0], F32, tm, 512, "w_out")

    h1, xf = _mid(h, mix, row(norm_post_mix[0]), row(norm_pre_ffn[0]), mod, tr)
    act = _ffn_up(xf, ffn_w_gate[0], ffn_w_up[0], tm, 256)
    ff = _matmul_ksplit(act, ffn_w_down[0].astype(BF16), F32, tm, 512, D_FF // 2, "ffn_down")
    out = _final(h1, ff, row(norm_post_ffn[0]), mod, tr)
    return out.reshape(bsz, L, D_MODEL)
```

```python
import functools
import math

import jax
import jax.numpy as jnp
from jax import lax
from jax.experimental import pallas as pl
from jax.experimental.pallas import tpu as pltpu

F32 = jnp.float32
BF16 = jnp.bfloat16

D_MODEL = 4096
RMS_EPS = 1e-6
D_RWKV = 2048
RWKV_HEAD = 64
LORA_W = 128
LORA_A = 128
LORA_G = 480
LORA_G_PAD = 512
GN_EPS = 64e-5
D_S5 = 1024
S5_GROUPS = 64
S5_GROUP_CH = 16
S5_STATE = 64
S5_BLOCKS = 4
D_FF = 11008

LANES = 128
PAIRS = D_RWKV // LANES
RWKV_T = 64
RWKV_CHUNKS_PER_STEP = 8
S5_T = 64

RWKV_COLS = 3 * D_RWKV + LORA_W + LORA_A + LORA_G

VMEM_LIMIT = 56 * 1024 * 1024


def _cparams(n_axes, vmem=VMEM_LIMIT):
    return pltpu.CompilerParams(dimension_semantics=("arbitrary",) * n_axes, vmem_limit_bytes=vmem)


def _dot(a, b):
    return jnp.dot(a.astype(BF16), b.astype(BF16), preferred_element_type=F32)


def _dot_nt(a, b):
    return lax.dot_general(a.astype(BF16), b.astype(BF16), (((1,), (1,)), ((), ())),
                           preferred_element_type=F32)


def _dot_tn(a, b):
    return lax.dot_general(a.astype(BF16), b.astype(BF16), (((0,), (0,)), ((), ())),
                           preferred_element_type=F32)


def _split3(x):
    hi = x.astype(BF16)
    r1 = x - hi.astype(F32)
    mid = r1.astype(BF16)
    lo = (r1 - mid.astype(F32)).astype(BF16)
    return hi, mid, lo


def _dot_f32_lhs(x, sel):
    hi, mid, lo = _split3(x)
    d = lambda p: jnp.dot(p, sel, preferred_element_type=F32)
    return d(hi) + d(mid) + d(lo)


def _dot_f32_rhs(sel, x):
    hi, mid, lo = _split3(x)
    d = lambda p: jnp.dot(sel, p, preferred_element_type=F32)
    return d(hi) + d(mid) + d(lo)


def _rms(x):
    return x * lax.rsqrt(jnp.mean(x * x, axis=-1, keepdims=True) + RMS_EPS)


def _sigmoid(x):
    return 1.0 / (1.0 + jnp.exp(-x))


def _token_shift(p, halo, mu, first):
    last = jnp.where(first, 0.0, halo[7:8, :])
    rolled = pltpu.roll(p, 1, axis=0)
    row = lax.broadcasted_iota(jnp.int32, p.shape, 0)
    prev = jnp.where(row == 0, last, rolled)
    return p + (prev - p) * mu


def _ada_kernel(c_ref, w_ref, b_ref, o_ref):
    c = c_ref[...]
    cs = c * _sigmoid(c)
    tn = o_ref.shape[1]
    rows = 512
    acc = jnp.zeros((8, tn), F32)
    for k0 in range(0, D_MODEL, rows):
        blk = w_ref[k0:k0 + rows, :] * cs[k0:k0 + rows, :]
        acc = acc + jnp.sum(blk.reshape(rows // 8, 8, tn), axis=0)
    o_ref[...] = jnp.sum(acc, axis=0, keepdims=True) + b_ref[...]


def _ada(c_col, w_ada, b_ada):
    n = w_ada.shape[1]
    tn = 512
    return pl.pallas_call(
        _ada_kernel,
        grid=(n // tn,),
        in_specs=[pl.BlockSpec((D_MODEL, 1), lambda j: (0, 0)),
                  pl.BlockSpec((D_MODEL, tn), lambda j: (0, j)),
                  pl.BlockSpec((1, tn), lambda j: (0, j))],
        out_specs=pl.BlockSpec((1, tn), lambda j: (0, j)),
        out_shape=jax.ShapeDtypeStruct((1, n), F32),
        compiler_params=_cparams(1),
        name="ada",
    )(c_col, w_ada, b_ada)


def _prenorm_kernel(x_ref, g_ref, sh_ref, sc_ref, o_ref):
    y = _rms(x_ref[...]) * g_ref[...]
    o_ref[...] = (y * (1.0 + sc_ref[...]) + sh_ref[...]).astype(o_ref.dtype)


def _prenorm(x, g, mod, sh_idx, sc_idx, tm):
    L = x.shape[0]
    return pl.pallas_call(
        _prenorm_kernel,
        grid=(L // tm,),
        in_specs=[pl.BlockSpec((tm, D_MODEL), lambda i: (i, 0)),
                  pl.BlockSpec((1, D_MODEL), lambda i: (0, 0)),
                  pl.BlockSpec((1, D_MODEL), lambda i: (0, sh_idx)),
                  pl.BlockSpec((1, D_MODEL), lambda i: (0, sc_idx))],
        out_specs=pl.BlockSpec((tm, D_MODEL), lambda i: (i, 0)),
        out_shape=jax.ShapeDtypeStruct((L, D_MODEL), BF16),
        compiler_params=_cparams(1),
        name="prenorm",
    )(x, g, mod, mod)


def _mid_kernel(x_ref, mix_ref, gpost_ref, gate_ref, gpre_ref, sh_ref, sc_ref, h_ref, xf_ref):
    h = x_ref[...] + gate_ref[...] * (_rms(mix_ref[...]) * gpost_ref[...])
    h_ref[...] = h
    y = _rms(h) * gpre_ref[...]
    xf_ref[...] = (y * (1.0 + sc_ref[...]) + sh_ref[...]).astype(xf_ref.dtype)


def _mid(x, mix, gpost, gpre, mod, tm):
    L = x.shape[0]
    row = lambda i: (i, 0)
    vec = lambda k: pl.BlockSpec((1, D_MODEL), lambda i: (0, k))
    return pl.pallas_call(
        _mid_kernel,
        grid=(L // tm,),
        in_specs=[pl.BlockSpec((tm, D_MODEL), row), pl.BlockSpec((tm, D_MODEL), row),
                  vec(0), vec(2), vec(0), vec(3), vec(4)],
        out_specs=[pl.BlockSpec((tm, D_MODEL), row), pl.BlockSpec((tm, D_MODEL), row)],
        out_shape=[jax.ShapeDtypeStruct((L, D_MODEL), F32), jax.ShapeDtypeStruct((L, D_MODEL), BF16)],
        compiler_params=_cparams(1),
        name="mid",
    )(x, mix, gpost, mod, gpre, mod, mod)


def _final_kernel(h_ref, ff_ref, gpost_ref, gate_ref, o_ref):
    o_ref[...] = h_ref[...] + gate_ref[...] * (_rms(ff_ref[...]) * gpost_ref[...])


def _final(h, ff, gpost, mod, tm):
    L = h.shape[0]
    row = lambda i: (i, 0)
    return pl.pallas_call(
        _final_kernel,
        grid=(L // tm,),
        in_specs=[pl.BlockSpec((tm, D_MODEL), row), pl.BlockSpec((tm, D_MODEL), row),
                  pl.BlockSpec((1, D_MODEL), lambda i: (0, 0)),
                  pl.BlockSpec((1, D_MODEL), lambda i: (0, 5))],
        out_specs=pl.BlockSpec((tm, D_MODEL), row),
        out_shape=jax.ShapeDtypeStruct((L, D_MODEL), F32),
        compiler_params=_cparams(1),
        name="final",
    )(h, ff, gpost, mod)


def _mm_kernel(a_ref, w_ref, o_ref):
    o_ref[...] = jnp.dot(a_ref[...], w_ref[...].astype(BF16),
                         preferred_element_type=F32).astype(o_ref.dtype)


def _matmul(a, w, out_dtype, tm, tn, name, n=None):
    m, k = a.shape
    n = w.shape[1] if n is None else n
    return pl.pallas_call(
        _mm_kernel,
        grid=(m // tm, n // tn),
        in_specs=[pl.BlockSpec((tm, k), lambda i, j: (i, 0)),
                  pl.BlockSpec((k, tn), lambda i, j: (0, j))],
        out_specs=pl.BlockSpec((tm, tn), lambda i, j: (i, j)),
        out_shape=jax.ShapeDtypeStruct((m, n), out_dtype),
        compiler_params=_cparams(2),
        name=name,
    )(a, w)


def _mm_acc_kernel(a_ref, w_ref, o_ref, acc_ref):
    kk = pl.program_id(2)
    part = jnp.dot(a_ref[...], w_ref[...], preferred_element_type=F32)

    @pl.when(kk == 0)
    def _():
        acc_ref[...] = part

    @pl.when(kk > 0)
    def _():
        acc_ref[...] = acc_ref[...] + part

    @pl.when(kk == pl.num_programs(2) - 1)
    def _():
        o_ref[...] = acc_ref[...].astype(o_ref.dtype)


def _dot_wt(a, wt):
    return lax.dot_general(a, wt.astype(BF16), (((1,), (1,)), ((), ())), preferred_element_type=F32)


def _mm_wt_kernel(a_ref, wt_ref, o_ref):
    o_ref[...] = _dot_wt(a_ref[...], wt_ref[...]).astype(o_ref.dtype)


def _matmul_wt(a, wt, row0, n, out_dtype, tm, tn, name):
    m, k = a.shape
    return pl.pallas_call(
        _mm_wt_kernel,
        grid=(m // tm, n // tn),
        in_specs=[pl.BlockSpec((tm, k), lambda i, j: (i, 0)),
                  pl.BlockSpec((pl.Element(tn), pl.Element(k)),
                               lambda i, j: (pl.multiple_of(row0 + j * tn, 8), 0))],
        out_specs=pl.BlockSpec((tm, tn), lambda i, j: (i, j)),
        out_shape=jax.ShapeDtypeStruct((m, n), out_dtype),
        compiler_params=_cparams(2),
        name=name,
    )(a, wt)


def _matmul_ksplit(a, w, out_dtype, tm, tn, tk, name):
    m, k = a.shape
    n = w.shape[1]
    return pl.pallas_call(
        _mm_acc_kernel,
        grid=(m // tm, n // tn, k // tk),
        in_specs=[pl.BlockSpec((tm, tk), lambda i, j, kk: (i, kk)),
                  pl.BlockSpec((tk, tn), lambda i, j, kk: (kk, j))],
        out_specs=pl.BlockSpec((tm, tn), lambda i, j, kk: (i, j)),
        out_shape=jax.ShapeDtypeStruct((m, n), out_dtype),
        scratch_shapes=[pltpu.VMEM((tm, tn), F32)],
        compiler_params=_cparams(3),
        name=name,
    )(a, w)


def _merge_kernel(oa_ref, ob_ref, wa_ref, wb_ref, ga_ref, gb_ref, o_ref):
    ya = jnp.dot(oa_ref[...], wa_ref[...].astype(BF16), preferred_element_type=F32)
    yb = jnp.dot(ob_ref[...], wb_ref[...].astype(BF16), preferred_element_type=F32)
    m = _sigmoid(ga_ref[...].astype(F32)) * ya + _sigmoid(gb_ref[...].astype(F32)) * yb
    o_ref[...] = m.astype(o_ref.dtype)


def _merge(o_a, o_b, w_up_a, w_up_b, gates, tm, tn):
    L = o_a.shape[0]
    nb = D_MODEL // tn
    return pl.pallas_call(
        _merge_kernel,
        grid=(L // tm, nb),
        in_specs=[pl.BlockSpec((tm, D_RWKV), lambda i, j: (i, 0)),
                  pl.BlockSpec((tm, D_S5), lambda i, j: (i, 0)),
                  pl.BlockSpec((D_RWKV, tn), lambda i, j: (0, j)),
                  pl.BlockSpec((D_S5, tn), lambda i, j: (0, j)),
                  pl.BlockSpec((tm, tn), lambda i, j: (i, j)),
                  pl.BlockSpec((tm, tn), lambda i, j: (i, j + nb))],
        out_specs=pl.BlockSpec((tm, tn), lambda i, j: (i, j)),
        out_shape=jax.ShapeDtypeStruct((L, D_MODEL), BF16),
        compiler_params=_cparams(2),
        name="merge",
    )(o_a, o_b, w_up_a, w_up_b, gates, gates)


def _ffn_up_kernel(x_ref, wg_ref, wu_ref, o_ref):
    x = x_ref[...]
    a = jnp.dot(x, wg_ref[...].astype(BF16), preferred_element_type=F32)
    b = jnp.dot(x, wu_ref[...].astype(BF16), preferred_element_type=F32)
    o_ref[...] = ((a * _sigmoid(a)) * b).astype(o_ref.dtype)


def _ffn_up(xf, w_gate, w_up, tm, tn):
    L = xf.shape[0]
    n = w_gate.shape[1]
    return pl.pallas_call(
        _ffn_up_kernel,
        grid=(L // tm, n // tn),
        in_specs=[pl.BlockSpec((tm, D_MODEL), lambda i, j: (i, 0)),
                  pl.BlockSpec((D_MODEL, tn), lambda i, j: (0, j)),
                  pl.BlockSpec((D_MODEL, tn), lambda i, j: (0, j))],
        out_specs=pl.BlockSpec((tm, tn), lambda i, j: (i, j)),
        out_shape=jax.ShapeDtypeStruct((L, n), BF16),
        compiler_params=_cparams(2),
        name="ffn_up",
    )(xf, w_gate, w_up)


def _lora_kernel(p_ref, halo_ref, mu_ref, w0_ref, a0_ref, w2_ref, a2_ref, g2_ref,
                 lw_ref, a_ref, g_ref):
    ps = _token_shift(p_ref[...], halo_ref[...], mu_ref[...], pl.program_id(0) == 0)
    xw = ps[:, :LORA_W]
    xa = ps[:, LORA_W:LORA_W + LORA_A]
    xg = ps[:, LORA_W + LORA_A:]
    z = -(w0_ref[...] + _dot(jnp.tanh(xw), w2_ref[...]))
    softplus = jnp.maximum(z, 0.0) + jnp.log(1.0 + jnp.exp(-jnp.abs(z)))
    lw_ref[...] = -jnp.exp(-softplus - 0.5)
    a_ref[...] = _sigmoid(a0_ref[...] + _dot(xa, a2_ref[...]))
    g_ref[...] = _dot(_sigmoid(xg), g2_ref[...])


def _lora(proj, mu, w0, a0, w2, a2, g2, tm):
    L, n = proj.shape
    full = lambda shape: pl.BlockSpec(shape, lambda i: (0, 0))
    out = pl.BlockSpec((tm, D_RWKV), lambda i: (i, 0))
    return pl.pallas_call(
        _lora_kernel,
        grid=(L // tm,),
        in_specs=[pl.BlockSpec((tm, n), lambda i: (i, 0)),
                  pl.BlockSpec((8, n), lambda i: (jnp.maximum(i * (tm // 8) - 1, 0), 0)),
                  full((1, n)),
                  full((1, D_RWKV)), full((1, D_RWKV)),
                  full((LORA_W, D_RWKV)), full((LORA_A, D_RWKV)), full((LORA_G_PAD, D_RWKV))],
        out_specs=[out, out, out],
        out_shape=[jax.ShapeDtypeStruct((L, D_RWKV), F32)] * 3,
        compiler_params=_cparams(1),
        name="lora",
    )(proj, proj, mu, w0, a0, w2, a2, g2)


def _rwkv_block(r, k, v, am, bm, lw, h, cst, T):
    S = 2 * T
    n = r.shape[0] // T
    lane_lo, tri_incl, strict, incl, eye = cst
    cr = range(n)

    def stack(x, c):
        xc = x[c * T:(c + 1) * T]
        return jnp.concatenate([jnp.where(lane_lo, xc, 0.0), jnp.where(lane_lo, 0.0, xc)], axis=0)

    lw_s = [stack(lw, c) for c in cr]
    c_all = _dot_f32_rhs(tri_incl, jnp.concatenate(lw_s, axis=1))
    cs = [c_all[:, c * LANES:(c + 1) * LANES] for c in cr]
    c_end = [cs[c][T - 1:T, :] + cs[c][S - 1:S, :] for c in cr]
    r_s = [stack(r, c) for c in cr]
    k_s = [stack(k, c) for c in cr]
    v_s = [stack(v, c) for c in cr]
    a_s = [stack(am, c) for c in cr]
    b_s = [stack(bm, c) for c in cr]
    e_neg = [jnp.exp(-cs[c]) for c in cr]
    e_end = [jnp.exp(c_end[c] - cs[c]) for c in cr]
    rh = [r_s[c] * jnp.exp(cs[c]) for c in cr]
    ah = [a_s[c] * jnp.exp(cs[c] - lw_s[c]) for c in cr]
    bh = [b_s[c] * e_neg[c] for c in cr]
    kh = [k_s[c] * e_neg[c] for c in cr]
    bc = [b_s[c] * e_end[c] for c in cr]
    kc = [k_s[c] * e_end[c] for c in cr]

    big = [_dot_nt(jnp.concatenate([ah[c], rh[c]], axis=0), jnp.concatenate([bh[c], kh[c]], axis=0))
           for c in cr]
    nj = [jnp.where(strict, big[c][:S, :S], 0.0) for c in cr]
    m_rb = [jnp.where(incl, big[c][S:, :S], 0.0) for c in cr]
    m_kk = [jnp.concatenate([jnp.where(strict, big[c][:S, S:], 0.0),
                             jnp.where(incl, big[c][S:, S:], 0.0)], axis=0) for c in cr]
    avrv = [_dot(m_kk[c], v_s[c]) for c in cr]

    x = [jnp.concatenate([ah[c], avrv[c][:S]], axis=1) for c in cr]
    steps = T.bit_length() - 1
    for j in range(steps):
        x = [x[c] + _dot(nj[c], x[c]) for c in cr]
        if j + 1 < steps:
            nj = [_dot(nj[c], nj[c]) for c in cr]

    y = [_dot(m_rb[c], x[c]) for c in cr]
    q = [rh[c] + y[c][:, :LANES] for c in cr]
    o0 = [avrv[c][S:] + y[c][:, LANES:] for c in cr]
    gd = [_dot_tn(jnp.concatenate([bc[c], kc[c]], axis=0),
                  jnp.concatenate([x[c], jnp.concatenate([jnp.zeros_like(v_s[c]), v_s[c]], axis=1)],
                                  axis=0)) for c in cr]
    decay_col = [jnp.sum(jnp.where(eye, jnp.exp(c_end[c]), 0.0), axis=1, keepdims=True) for c in cr]

    outs = []
    for c in cr:
        z = _dot(jnp.concatenate([gd[c][:, :LANES], q[c]], axis=0), h)
        o_s = z[LANES:] + o0[c]
        outs.append(o_s[:T] + o_s[T:])
        h = decay_col[c] * h + z[:LANES] + gd[c][:, LANES:]
    o = jnp.concatenate(outs, axis=0) if n > 1 else outs[0]
    return o, h


def _rwkv_kernel(r_ref, k_ref, v_ref, rh_ref, kh_ref, vh_ref, mur_ref, muk_ref, muv_ref,
                 lw_ref, a_ref, g_ref, kk_ref, ka_ref, rk_ref, lnw_ref, lnb_ref,
                 o_ref, h_ref):
    i = pl.program_id(1)
    first = i == 0

    @pl.when(first)
    def _():
        h_ref[...] = jnp.zeros_like(h_ref)

    rows = r_ref.shape[0]
    T = RWKV_T
    S = 2 * T
    ii = lax.broadcasted_iota(jnp.int32, (S, S), 0)
    jj = lax.broadcasted_iota(jnp.int32, (S, S), 1)
    same = (ii // T) == (jj // T)
    strict = same & (jj < ii)
    incl = same & (jj <= ii)
    eye = ii == jj
    tri_incl = jnp.where(incl, 1.0, 0.0).astype(BF16)
    hi_ = lax.broadcasted_iota(jnp.int32, (LANES, LANES), 0) // RWKV_HEAD
    hj_ = lax.broadcasted_iota(jnp.int32, (LANES, LANES), 1) // RWKV_HEAD
    head_sum = jnp.where(hi_ == hj_, 1.0, 0.0).astype(BF16)
    lane_lo = lax.broadcasted_iota(jnp.int32, (T, LANES), 1) < RWKV_HEAD
    cst = (lane_lo, tri_incl, strict, incl, eye)

    r = _token_shift(r_ref[...], rh_ref[...], mur_ref[...], first)
    k = _token_shift(k_ref[...], kh_ref[...], muk_ref[...], first)
    v = _token_shift(v_ref[...], vh_ref[...], muv_ref[...], first)
    a = a_ref[...]
    kk = k * kk_ref[...]
    k = k * (1.0 + (a - 1.0) * ka_ref[...])
    ss = _dot_f32_lhs(kk * kk, head_sum)
    kk = kk * lax.rsqrt(jnp.maximum(ss, 1e-24))
    am = -kk
    bm = kk * a
    lw = lw_ref[...]

    o, h = _rwkv_block(r, k, v, am, bm, lw, h_ref[...], cst, T)
    h_ref[...] = h

    inv = 1.0 / RWKV_HEAD
    mean = _dot_f32_lhs(o, head_sum) * inv
    d = o - mean
    var = _dot_f32_lhs(d * d, head_sum) * inv
    y = d * lax.rsqrt(var + GN_EPS) * lnw_ref[...] + lnb_ref[...]
    bonus = _dot_f32_lhs(r * k * rk_ref[...], head_sum) * v
    o_ref[...] = ((y + bonus) * g_ref[...]).astype(o_ref.dtype)


def _rwkv(proj, mu, lw, a, g, k_k, k_a, r_k, ln_w, ln_b, rows):
    L = proj.shape[0]
    nb = D_RWKV // LANES

    def col(off):
        return pl.BlockSpec((rows, LANES), lambda p, i: (i, p + off))

    def halo(off):
        return pl.BlockSpec((8, LANES), lambda p, i: (jnp.maximum(i * (rows // 8) - 1, 0), p + off))

    def vec(off):
        return pl.BlockSpec((1, LANES), lambda p, i: (0, p + off))

    return pl.pallas_call(
        _rwkv_kernel,
        grid=(PAIRS, L // rows),
        in_specs=[col(0), col(nb), col(2 * nb), halo(0), halo(nb), halo(2 * nb),
                  vec(0), vec(nb), vec(2 * nb),
                  col(0), col(0), col(0),
                  vec(0), vec(0), vec(0), vec(0), vec(0)],
        out_specs=col(0),
        out_shape=jax.ShapeDtypeStruct((L, D_RWKV), BF16),
        scratch_shapes=[pltpu.VMEM((LANES, LANES), F32)],
        compiler_params=_cparams(2),
        name="rwkv",
    )(proj, proj, proj, proj, proj, proj, mu, mu, mu, lw, a, g, k_k, k_a, r_k, ln_w, ln_b)


def _s5_table_kernel(lr_ref, li_ref, ldt_ref, pos_re, pos_im, neg_re, neg_im, z_re, z_im):
    lr = lr_ref[...]
    li = li_ref[...]
    dt = jnp.exp(ldt_ref[...])
    t = lax.broadcasted_iota(jnp.int32, pos_re.shape, 0).astype(F32)
    mag = jnp.exp(t * (lr * dt))
    ang = t * (li * dt)
    cs, sn = jnp.cos(ang), jnp.sin(ang)
    pos_re[...] = mag * cs
    pos_im[...] = mag * sn
    inv = 1.0 / mag
    neg_re[...] = inv * cs
    neg_im[...] = -(inv * sn)
    m1 = jnp.exp(lr * dt)
    lb_re = m1 * jnp.cos(li * dt)
    lb_im = m1 * jnp.sin(li * dt)
    den = lr * lr + li * li
    z_re[...] = ((lb_re - 1.0) * lr + lb_im * li) / den
    z_im[...] = (lb_im * lr - (lb_re - 1.0) * li) / den


def _s5_tables(lr, li, ldt, rows):
    n = lr.shape[1]
    full = pl.BlockSpec((1, n), lambda: (0, 0))
    tab = pl.BlockSpec((rows, n), lambda: (0, 0))
    return pl.pallas_call(
        _s5_table_kernel,
        in_specs=[full, full, full],
        out_specs=[tab, tab, tab, tab, full, full],
        out_shape=[jax.ShapeDtypeStruct((rows, n), F32)] * 4 + [jax.ShapeDtypeStruct((1, n), F32)] * 2,
        name="s5_tables",
    )(lr, li, ldt)


def _s5_kernel(u_ref, pos_re_ref, pos_im_ref, neg_re_ref, neg_im_ref, z_re_ref, z_im_ref,
               b_re_ref, b_im_ref, c_re_ref, c_im_ref, d_ref, wglu_ref, bglu_ref,
               o_ref, bb_ref, st_ref):
    T = u_ref.shape[0]
    NS = S5_GROUPS * S5_STATE
    BS = NS // S5_BLOCKS
    BC = D_S5 // S5_BLOCKS

    @pl.when(pl.program_id(0) == 0)
    def _():
        st_ref[...] = jnp.zeros_like(st_ref)
        for j in range(S5_BLOCKS):
            zr = z_re_ref[:, j * BS:(j + 1) * BS]
            zi = z_im_ref[:, j * BS:(j + 1) * BS]
            br = b_re_ref[j]
            bi = b_im_ref[j]
            bb_ref[j, :, :BS] = (zr * br - zi * bi).astype(BF16)
            bb_ref[j, :, BS:] = (zr * bi + zi * br).astype(BF16)

    u = u_ref[...]
    ub = u.astype(BF16)
    ii = lax.broadcasted_iota(jnp.int32, (T, T), 0)
    jj = lax.broadcasted_iota(jnp.int32, (T, T), 1)
    tri = jnp.where(jj <= ii, 1.0, 0.0).astype(BF16)
    s_re = st_ref[0:1, :]
    s_im = st_ref[1:2, :]
    lb_re = pos_re_ref[1:2, :]
    lb_im = pos_im_ref[1:2, :]
    c_in_re = lb_re * s_re - lb_im * s_im
    c_in_im = lb_re * s_im + lb_im * s_re

    ys = []
    for j in range(S5_BLOCKS):
        sl = slice(j * BS, (j + 1) * BS)
        bu = jnp.dot(ub[:, j * BC:(j + 1) * BC], bb_ref[j], preferred_element_type=F32)
        bu_re, bu_im = bu[:, :BS], bu[:, BS:]
        nr, ni = neg_re_ref[:, sl], neg_im_ref[:, sl]
        zz = jnp.concatenate([nr * bu_re - ni * bu_im, nr * bu_im + ni * bu_re], axis=1)
        acc = jnp.dot(tri, zz.astype(BF16), preferred_element_type=F32)
        a_re = acc[:, :BS] + c_in_re[:, sl]
        a_im = acc[:, BS:] + c_in_im[:, sl]
        pr, pi = pos_re_ref[:, sl], pos_im_ref[:, sl]
        x_re = pr * a_re - pi * a_im
        x_im = pr * a_im + pi * a_re
        st_ref[0:1, sl] = x_re[T - 1:T, :]
        st_ref[1:2, sl] = x_im[T - 1:T, :]
        ys.append(_dot(x_re, c_re_ref[j]) - _dot(x_im, c_im_ref[j]))
    y = jnp.concatenate(ys, axis=1) + d_ref[...] * u
    y = 0.5 * y * (1.0 + jnp.tanh(math.sqrt(2.0 / math.pi) * (y + 0.044715 * (y * y * y))))
    zg = _dot(y, wglu_ref[...]) + bglu_ref[...]
    o_ref[...] = (zg[:, :D_S5] * _sigmoid(zg[:, D_S5:])).astype(o_ref.dtype)


def _s5(proj, tables, b_re, b_im, c_re, c_im, d, w_glu, b_glu, rows):
    L = proj.shape[0]
    NS = S5_GROUPS * S5_STATE
    BS = NS // S5_BLOCKS
    BC = D_S5 // S5_BLOCKS
    pos_re, pos_im, neg_re, neg_im, z_re, z_im = tables
    c2 = lambda shape: pl.BlockSpec(shape, lambda i: (0, 0))
    c3 = lambda shape: pl.BlockSpec(shape, lambda i: (0, 0, 0))
    return pl.pallas_call(
        _s5_kernel,
        grid=(L // rows,),
        in_specs=[pl.BlockSpec((rows, D_S5), lambda i: (i, 0)),
                  c2((rows, NS)), c2((rows, NS)), c2((rows, NS)), c2((rows, NS)),
                  c2((1, NS)), c2((1, NS)),
                  c3((S5_BLOCKS, BC, BS)), c3((S5_BLOCKS, BC, BS)),
                  c3((S5_BLOCKS, BS, BC)), c3((S5_BLOCKS, BS, BC)),
                  c2((1, D_S5)), c2((D_S5, 2 * D_S5)), c2((1, 2 * D_S5))],
        out_specs=pl.BlockSpec((rows, D_S5), lambda i: (i, 0)),
        out_shape=jax.ShapeDtypeStruct((L, D_S5), BF16),
        scratch_shapes=[pltpu.VMEM((S5_BLOCKS, BC, 2 * BS), BF16), pltpu.VMEM((8, NS), F32)],
        compiler_params=_cparams(1),
        name="s5",
    )(proj, pos_re, pos_im, neg_re, neg_im, z_re, z_im, b_re, b_im, c_re, c_im, d, w_glu, b_glu)


def _block_diag_groups(w):
    g, r, c = w.shape
    gb = g // S5_BLOCKS
    w = w.reshape(S5_BLOCKS, gb, r, c)
    eye = jnp.eye(gb, dtype=w.dtype)
    out = w[:, :, :, None, :] * eye[None, :, None, :, None]
    return out.reshape(S5_BLOCKS, gb * r, gb * c)


def kernel(x, c, w_ada, b_ada, norm_pre_mix, norm_post_mix, norm_pre_ffn, norm_post_ffn, w_in, rwkv_mu, rwkv_w0, rwkv_w2, rwkv_a0, rwkv_a2, rwkv_g2, rwkv_k_k, rwkv_k_a, rwkv_r_k, rwkv_ln_w, rwkv_ln_b, s5_lam_re, s5_lam_im, s5_log_dt, s5_b_re, s5_b_im, s5_c_re, s5_c_im, s5_d, s5_w_glu, s5_b_glu, w_up_rwkv, w_up_s5, w_out, ffn_w_gate, ffn_w_up, ffn_w_down):
    bsz, L, _ = x.shape
    assert bsz == 1 and w_ada.shape[0] == 1
    h = x.reshape(L, D_MODEL)
    tm = min(1024, L)
    tr = min(256, L)
    row = lambda v: v.reshape(1, -1)

    mod = _ada(c.reshape(D_MODEL, 1), w_ada[0], row(b_ada[0]))

    wit = jnp.swapaxes(w_in[0], 0, 1)
    gpad = LORA_G_PAD - LORA_G
    n_rkv = 3 * D_RWKV
    mu = row(rwkv_mu[0])
    mu_lora = jnp.concatenate([rwkv_mu[0, n_rkv:], jnp.zeros((gpad,), F32)]).reshape(1, -1)
    g2 = jnp.concatenate([rwkv_g2[0], jnp.zeros((gpad, D_RWKV), F32)], axis=0).astype(BF16)

    xm = _prenorm(h, row(norm_pre_mix[0]), mod, 0, 1, tr)
    proj_rkv = _matmul_wt(xm, wit, 0, n_rkv, F32, tm, 512, "proj_rkv")
    proj_l = _matmul_wt(xm, wit, n_rkv, LORA_W + LORA_A + LORA_G_PAD, F32, tm, 256, "proj_lora")
    proj_u = _matmul_wt(xm, wit, RWKV_COLS, D_S5, F32, tm, 512, "proj_u")
    gates = _matmul_wt(xm, wit, RWKV_COLS + D_S5, 2 * D_MODEL, BF16, tm, 512, "proj_g")

    lw, a, g = _lora(proj_l, mu_lora, row(rwkv_w0[0]), row(rwkv_a0[0]), rwkv_w2[0].astype(BF16),
                     rwkv_a2[0].astype(BF16), g2, tr)
    o_a = _rwkv(proj_rkv, mu, lw, a, g, row(rwkv_k_k[0]), row(rwkv_k_a[0]), row(rwkv_r_k[0]),
                row(rwkv_ln_w[0]), row(rwkv_ln_b[0]), min(RWKV_T * RWKV_CHUNKS_PER_STEP, L))

    rep = lambda v: jnp.repeat(v, S5_STATE).reshape(1, -1)
    tables = _s5_tables(row(s5_lam_re[0]), row(s5_lam_im[0]), rep(s5_log_dt[0]), S5_T)
    bt = lambda w: _block_diag_groups(jnp.swapaxes(w, 1, 2))
    o_b = _s5(proj_u, tables, bt(s5_b_re[0]), bt(s5_b_im[0]),
              _block_diag_groups(jnp.swapaxes(s5_c_re[0], 1, 2)).astype(BF16),
              _block_diag_groups(jnp.swapaxes(s5_c_im[0], 1, 2)).astype(BF16),
              row(s5_d[0]), s5_w_glu[0].astype(BF16), row(s5_b_glu[0]), S5_T)

    merged = _merge(o_a, o_b, w_up_rwkv[0], w_up_s5[0], gates, tm, 512)
    mix = _matmul(merged, w_out[0], F32, tm, 512, "w_out")

    h1, xf = _mid(h, mix, row(norm_post_mix[0]), row(norm_pre_ffn[0]), mod, tr)
    act = _ffn_up(xf, ffn_w_gate[0], ffn_w_up[0], tm, 256)
    ff = _matmul_ksplit(act, ffn_w_down[0].astype(BF16), F32, tm, 512, D_FF // 2, "ffn_down")
    out = _final(h1, ff, row(norm_post_ffn[0]), mod, tr)
    return out.reshape(bsz, L, D_MODEL)
```

```python
import functools
import math

import jax
import jax.numpy as jnp
from jax import lax
from jax.experimental import pallas as pl
from jax.experimental.pallas import tpu as pltpu

F32 = jnp.float32
BF16 = jnp.bfloat16

D_MODEL = 4096
RMS_EPS = 1e-6
D_RWKV = 2048
RWKV_HEAD = 64
LORA_W = 128
LORA_A = 128
LORA_G = 480
LORA_G_PAD = 512
GN_EPS = 64e-5
D_S5 = 1024
S5_GROUPS = 64
S5_GROUP_CH = 16
S5_STATE = 64
S5_BLOCKS = 4
D_FF = 11008

LANES = 128
PAIRS = D_RWKV // LANES
RWKV_T = 64
RWKV_CHUNKS_PER_STEP = 8
S5_T = 128
S5_ROWS = 256

RWKV_COLS = 3 * D_RWKV + LORA_W + LORA_A + LORA_G

VMEM_LIMIT = 56 * 1024 * 1024


def _cparams(n_axes, vmem=VMEM_LIMIT):
    return pltpu.CompilerParams(dimension_semantics=("arbitrary",) * n_axes, vmem_limit_bytes=vmem)


def _dot(a, b):
    return jnp.dot(a.astype(BF16), b.astype(BF16), preferred_element_type=F32)


def _dot_nt(a, b):
    return lax.dot_general(a.astype(BF16), b.astype(BF16), (((1,), (1,)), ((), ())),
                           preferred_element_type=F32)


def _dot_tn(a, b):
    return lax.dot_general(a.astype(BF16), b.astype(BF16), (((0,), (0,)), ((), ())),
                           preferred_element_type=F32)


def _split3(x):
    hi = x.astype(BF16)
    r1 = x - hi.astype(F32)
    mid = r1.astype(BF16)
    lo = (r1 - mid.astype(F32)).astype(BF16)
    return hi, mid, lo


def _dot_f32_lhs(x, sel):
    hi, mid, lo = _split3(x)
    d = lambda p: jnp.dot(p, sel, preferred_element_type=F32)
    return d(hi) + d(mid) + d(lo)


def _dot_f32_rhs(sel, x):
    hi, mid, lo = _split3(x)
    d = lambda p: jnp.dot(sel, p, preferred_element_type=F32)
    return d(hi) + d(mid) + d(lo)


def _rms(x):
    return x * lax.rsqrt(jnp.mean(x * x, axis=-1, keepdims=True) + RMS_EPS)


def _sigmoid(x):
    return 1.0 / (1.0 + jnp.exp(-x))


def _token_shift(p, halo, mu, first):
    last = jnp.where(first, 0.0, halo[7:8, :])
    rolled = pltpu.roll(p, 1, axis=0)
    row = lax.broadcasted_iota(jnp.int32, p.shape, 0)
    prev = jnp.where(row == 0, last, rolled)
    return p + (prev - p) * mu


def _ada_kernel(c_ref, w_ref, b_ref, o_ref):
    c = c_ref[...]
    cs = c * _sigmoid(c)
    tn = o_ref.shape[1]
    rows = 512
    acc = jnp.zeros((8, tn), F32)
    for k0 in range(0, D_MODEL, rows):
        blk = w_ref[k0:k0 + rows, :] * cs[k0:k0 + rows, :]
        acc = acc + jnp.sum(blk.reshape(rows // 8, 8, tn), axis=0)
    o_ref[...] = jnp.sum(acc, axis=0, keepdims=True) + b_ref[...]


def _ada(c_col, w_ada, b_ada):
    n = w_ada.shape[1]
    tn = 512
    return pl.pallas_call(
        _ada_kernel,
        grid=(n // tn,),
        in_specs=[pl.BlockSpec((D_MODEL, 1), lambda j: (0, 0)),
                  pl.BlockSpec((D_MODEL, tn), lambda j: (0, j)),
                  pl.BlockSpec((1, tn), lambda j: (0, j))],
        out_specs=pl.BlockSpec((1, tn), lambda j: (0, j)),
        out_shape=jax.ShapeDtypeStruct((1, n), F32),
        compiler_params=_cparams(1),
        name="ada",
    )(c_col, w_ada, b_ada)


def _prenorm_kernel(x_ref, g_ref, sh_ref, sc_ref, o_ref):
    y = _rms(x_ref[...]) * g_ref[...]
    o_ref[...] = (y * (1.0 + sc_ref[...]) + sh_ref[...]).astype(o_ref.dtype)


def _prenorm(x, g, mod, sh_idx, sc_idx, tm):
    L = x.shape[0]
    return pl.pallas_call(
        _prenorm_kernel,
        grid=(L // tm,),
        in_specs=[pl.BlockSpec((tm, D_MODEL), lambda i: (i, 0)),
                  pl.BlockSpec((1, D_MODEL), lambda i: (0, 0)),
                  pl.BlockSpec((1, D_MODEL), lambda i: (0, sh_idx)),
                  pl.BlockSpec((1, D_MODEL), lambda i: (0, sc_idx))],
        out_specs=pl.BlockSpec((tm, D_MODEL), lambda i: (i, 0)),
        out_shape=jax.ShapeDtypeStruct((L, D_MODEL), BF16),
        compiler_params=_cparams(1),
        name="prenorm",
    )(x, g, mod, mod)


def _mid_kernel(x_ref, mix_ref, gpost_ref, gate_ref, gpre_ref, sh_ref, sc_ref, h_ref, xf_ref):
    h = x_ref[...] + gate_ref[...] * (_rms(mix_ref[...]) * gpost_ref[...])
    h_ref[...] = h
    y = _rms(h) * gpre_ref[...]
    xf_ref[...] = (y * (1.0 + sc_ref[...]) + sh_ref[...]).astype(xf_ref.dtype)


def _mid(x, mix, gpost, gpre, mod, tm):
    L = x.shape[0]
    row = lambda i: (i, 0)
    vec = lambda k: pl.BlockSpec((1, D_MODEL), lambda i: (0, k))
    return pl.pallas_call(
        _mid_kernel,
        grid=(L // tm,),
        in_specs=[pl.BlockSpec((tm, D_MODEL), row), pl.BlockSpec((tm, D_MODEL), row),
                  vec(0), vec(2), vec(0), vec(3), vec(4)],
        out_specs=[pl.BlockSpec((tm, D_MODEL), row), pl.BlockSpec((tm, D_MODEL), row)],
        out_shape=[jax.ShapeDtypeStruct((L, D_MODEL), F32), jax.ShapeDtypeStruct((L, D_MODEL), BF16)],
        compiler_params=_cparams(1),
        name="mid",
    )(x, mix, gpost, mod, gpre, mod, mod)


def _final_kernel(h_ref, ff_ref, gpost_ref, gate_ref, o_ref):
    o_ref[...] = h_ref[...] + gate_ref[...] * (_rms(ff_ref[...]) * gpost_ref[...])


def _final(h, ff, gpost, mod, tm):
    L = h.shape[0]
    row = lambda i: (i, 0)
    return pl.pallas_call(
        _final_kernel,
        grid=(L // tm,),
        in_specs=[pl.BlockSpec((tm, D_MODEL), row), pl.BlockSpec((tm, D_MODEL), row),
                  pl.BlockSpec((1, D_MODEL), lambda i: (0, 0)),
                  pl.BlockSpec((1, D_MODEL), lambda i: (0, 5))],
        out_specs=pl.BlockSpec((tm, D_MODEL), row),
        out_shape=jax.ShapeDtypeStruct((L, D_MODEL), F32),
        compiler_params=_cparams(1),
        name="final",
    )(h, ff, gpost, mod)


def _mm_kernel(a_ref, w_ref, o_ref):
    o_ref[...] = jnp.dot(a_ref[...], w_ref[...].astype(BF16),
                         preferred_element_type=F32).astype(o_ref.dtype)


def _matmul(a, w, out_dtype, tm, tn, name, n=None):
    m, k = a.shape
    n = w.shape[1] if n is None else n
    return pl.pallas_call(
        _mm_kernel,
        grid=(m // tm, n // tn),
        in_specs=[pl.BlockSpec((tm, k), lambda i, j: (i, 0)),
                  pl.BlockSpec((k, tn), lambda i, j: (0, j))],
        out_specs=pl.BlockSpec((tm, tn), lambda i, j: (i, j)),
        out_shape=jax.ShapeDtypeStruct((m, n), out_dtype),
        compiler_params=_cparams(2),
        name=name,
    )(a, w)


def _mm_acc_kernel(a_ref, w_ref, o_ref, acc_ref):
    kk = pl.program_id(2)
    part = jnp.dot(a_ref[...], w_ref[...], preferred_element_type=F32)

    @pl.when(kk == 0)
    def _():
        acc_ref[...] = part

    @pl.when(kk > 0)
    def _():
        acc_ref[...] = acc_ref[...] + part

    @pl.when(kk == pl.num_programs(2) - 1)
    def _():
        o_ref[...] = acc_ref[...].astype(o_ref.dtype)


def _dot_wt(a, wt):
    return lax.dot_general(a, wt.astype(BF16), (((1,), (1,)), ((), ())), preferred_element_type=F32)


def _mm_wt_kernel(a_ref, wt_ref, o_ref):
    o_ref[...] = _dot_wt(a_ref[...], wt_ref[...]).astype(o_ref.dtype)


def _matmul_wt(a, wt, row0, n, out_dtype, tm, tn, name):
    m, k = a.shape
    return pl.pallas_call(
        _mm_wt_kernel,
        grid=(m // tm, n // tn),
        in_specs=[pl.BlockSpec((tm, k), lambda i, j: (i, 0)),
                  pl.BlockSpec((pl.Element(tn), pl.Element(k)),
                               lambda i, j: (pl.multiple_of(row0 + j * tn, 8), 0))],
        out_specs=pl.BlockSpec((tm, tn), lambda i, j: (i, j)),
        out_shape=jax.ShapeDtypeStruct((m, n), out_dtype),
        compiler_params=_cparams(2),
        name=name,
    )(a, wt)


def _matmul_ksplit(a, w, out_dtype, tm, tn, tk, name):
    m, k = a.shape
    n = w.shape[1]
    return pl.pallas_call(
        _mm_acc_kernel,
        grid=(m // tm, n // tn, k // tk),
        in_specs=[pl.BlockSpec((tm, tk), lambda i, j, kk: (i, kk)),
                  pl.BlockSpec((tk, tn), lambda i, j, kk: (kk, j))],
        out_specs=pl.BlockSpec((tm, tn), lambda i, j, kk: (i, j)),
        out_shape=jax.ShapeDtypeStruct((m, n), out_dtype),
        scratch_shapes=[pltpu.VMEM((tm, tn), F32)],
        compiler_params=_cparams(3),
        name=name,
    )(a, w)


def _merge_kernel(oa_ref, ob_ref, wa_ref, wb_ref, ga_ref, gb_ref, o_ref):
    ya = jnp.dot(oa_ref[...], wa_ref[...].astype(BF16), preferred_element_type=F32)
    yb = jnp.dot(ob_ref[...], wb_ref[...].astype(BF16), preferred_element_type=F32)
    m = _sigmoid(ga_ref[...].astype(F32)) * ya + _sigmoid(gb_ref[...].astype(F32)) * yb
    o_ref[...] = m.astype(o_ref.dtype)


def _merge(o_a, o_b, w_up_a, w_up_b, gates, tm, tn):
    L = o_a.shape[0]
    nb = D_MODEL // tn
    return pl.pallas_call(
        _merge_kernel,
        grid=(L // tm, nb),
        in_specs=[pl.BlockSpec((tm, D_RWKV), lambda i, j: (i, 0)),
                  pl.BlockSpec((tm, D_S5), lambda i, j: (i, 0)),
                  pl.BlockSpec((D_RWKV, tn), lambda i, j: (0, j)),
                  pl.BlockSpec((D_S5, tn), lambda i, j: (0, j)),
                  pl.BlockSpec((tm, tn), lambda i, j: (i, j)),
                  pl.BlockSpec((tm, tn), lambda i, j: (i, j + nb))],
        out_specs=pl.BlockSpec((tm, tn), lambda i, j: (i, j)),
        out_shape=jax.ShapeDtypeStruct((L, D_MODEL), BF16),
        compiler_params=_cparams(2),
        name="merge",
    )(o_a, o_b, w_up_a, w_up_b, gates, gates)


def _ffn_up_kernel(x_ref, wg_ref, wu_ref, o_ref):
    x = x_ref[...]
    a = jnp.dot(x, wg_ref[...].astype(BF16), preferred_element_type=F32)
    b = jnp.dot(x, wu_ref[...].astype(BF16), preferred_element_type=F32)
    o_ref[...] = ((a * _sigmoid(a)) * b).astype(o_ref.dtype)


def _ffn_up(xf, w_gate, w_up, tm, tn):
    L = xf.shape[0]
    n = w_gate.shape[1]
    return pl.pallas_call(
        _ffn_up_kernel,
        grid=(L // tm, n // tn),
        in_specs=[pl.BlockSpec((tm, D_MODEL), lambda i, j: (i, 0)),
                  pl.BlockSpec((D_MODEL, tn), lambda i, j: (0, j)),
                  pl.BlockSpec((D_MODEL, tn), lambda i, j: (0, j))],
        out_specs=pl.BlockSpec((tm, tn), lambda i, j: (i, j)),
        out_shape=jax.ShapeDtypeStruct((L, n), BF16),
        compiler_params=_cparams(2),
        name="ffn_up",
    )(xf, w_gate, w_up)


def _lora_kernel(p_ref, halo_ref, mu_ref, w0_ref, a0_ref, w2_ref, a2_ref, g2_ref,
                 lw_ref, a_ref, g_ref):
    ps = _token_shift(p_ref[...], halo_ref[...], mu_ref[...], pl.program_id(0) == 0)
    xw = ps[:, :LORA_W]
    xa = ps[:, LORA_W:LORA_W + LORA_A]
    xg = ps[:, LORA_W + LORA_A:]
    z = -(w0_ref[...] + _dot(jnp.tanh(xw), w2_ref[...]))
    softplus = jnp.maximum(z, 0.0) + jnp.log(1.0 + jnp.exp(-jnp.abs(z)))
    lw_ref[...] = -jnp.exp(-softplus - 0.5)
    a_ref[...] = _sigmoid(a0_ref[...] + _dot(xa, a2_ref[...]))
    g_ref[...] = _dot(_sigmoid(xg), g2_ref[...])


def _lora(proj, mu, w0, a0, w2, a2, g2, tm):
    L, n = proj.shape
    full = lambda shape: pl.BlockSpec(shape, lambda i: (0, 0))
    out = pl.BlockSpec((tm, D_RWKV), lambda i: (i, 0))
    return pl.pallas_call(
        _lora_kernel,
        grid=(L // tm,),
        in_specs=[pl.BlockSpec((tm, n), lambda i: (i, 0)),
                  pl.BlockSpec((8, n), lambda i: (jnp.maximum(i * (tm // 8) - 1, 0), 0)),
                  full((1, n)),
                  full((1, D_RWKV)), full((1, D_RWKV)),
                  full((LORA_W, D_RWKV)), full((LORA_A, D_RWKV)), full((LORA_G_PAD, D_RWKV))],
        out_specs=[out, out, out],
        out_shape=[jax.ShapeDtypeStruct((L, D_RWKV), F32)] * 3,
        compiler_params=_cparams(1),
        name="lora",
    )(proj, proj, mu, w0, a0, w2, a2, g2)


def _rwkv_parallel(r, k, v, am, bm, lw, cst, T):
    S = 2 * T
    n = r.shape[0] // T
    lane_lo, tri_incl, strict, incl, eye = cst
    cr = range(n)

    def stack(x, c):
        xc = x[c * T:(c + 1) * T]
        return jnp.concatenate([jnp.where(lane_lo, xc, 0.0), jnp.where(lane_lo, 0.0, xc)], axis=0)

    lw_s = [stack(lw, c) for c in cr]
    c_all = _dot_f32_rhs(tri_incl, jnp.concatenate(lw_s, axis=1))
    yield
    cs = [c_all[:, c * LANES:(c + 1) * LANES] for c in cr]
    c_end = [cs[c][T - 1:T, :] + cs[c][S - 1:S, :] for c in cr]
    r_s = [stack(r, c) for c in cr]
    k_s = [stack(k, c) for c in cr]
    v_s = [stack(v, c) for c in cr]
    a_s = [stack(am, c) for c in cr]
    b_s = [stack(bm, c) for c in cr]
    e_neg = [jnp.exp(-cs[c]) for c in cr]
    e_end = [jnp.exp(c_end[c] - cs[c]) for c in cr]
    rh = [r_s[c] * jnp.exp(cs[c]) for c in cr]
    ah = [a_s[c] * jnp.exp(cs[c] - lw_s[c]) for c in cr]
    bh = [b_s[c] * e_neg[c] for c in cr]
    kh = [k_s[c] * e_neg[c] for c in cr]
    bc = [b_s[c] * e_end[c] for c in cr]
    kc = [k_s[c] * e_end[c] for c in cr]

    big = [_dot_nt(jnp.concatenate([ah[c], rh[c]], axis=0), jnp.concatenate([bh[c], kh[c]], axis=0))
           for c in cr]
    yield
    nj = [jnp.where(strict, big[c][:S, :S], 0.0) for c in cr]
    m_rb = [jnp.where(incl, big[c][S:, :S], 0.0) for c in cr]
    m_kk = [jnp.concatenate([jnp.where(strict, big[c][:S, S:], 0.0),
                             jnp.where(incl, big[c][S:, S:], 0.0)], axis=0) for c in cr]
    avrv = [_dot(m_kk[c], v_s[c]) for c in cr]
    yield

    x = [jnp.concatenate([ah[c], avrv[c][:S]], axis=1) for c in cr]
    steps = T.bit_length() - 1
    for j in range(steps):
        x = [x[c] + _dot(nj[c], x[c]) for c in cr]
        if j + 1 < steps:
            nj = [_dot(nj[c], nj[c]) for c in cr]
        yield

    y = [_dot(m_rb[c], x[c]) for c in cr]
    q = [rh[c] + y[c][:, :LANES] for c in cr]
    o0 = [avrv[c][S:] + y[c][:, LANES:] for c in cr]
    gd = [_dot_tn(jnp.concatenate([bc[c], kc[c]], axis=0),
                  jnp.concatenate([x[c], jnp.concatenate([jnp.zeros_like(v_s[c]), v_s[c]], axis=1)],
                                  axis=0)) for c in cr]
    decay = [jnp.broadcast_to(jnp.sum(jnp.where(eye, jnp.exp(c_end[c]), 0.0), axis=1, keepdims=True),
                              (LANES, LANES)) for c in cr]
    return gd, q, o0, decay


def _rwkv_kernel(r_ref, k_ref, v_ref, rh_ref, kh_ref, vh_ref, mur_ref, muk_ref, muv_ref,
                 lw_ref, a_ref, kk_ref, ka_ref, rk_ref, g_ref, lnw_ref, lnb_ref,
                 o_ref, h_ref, gd_ref, q_ref, o0_ref, dec_ref, bonus_ref, *, nblk):
    s = pl.program_id(0)
    last = pl.num_programs(0) - 2
    first_cur = lax.rem(jnp.minimum(s, last), nblk) == 0
    first_prv = lax.rem(jnp.maximum(s - 1, 0), nblk) == 0

    @pl.when(s == 0)
    def _():
        h_ref[...] = jnp.zeros_like(h_ref)
        gd_ref[...] = jnp.zeros_like(gd_ref)
        q_ref[...] = jnp.zeros_like(q_ref)
        o0_ref[...] = jnp.zeros_like(o0_ref)
        dec_ref[...] = jnp.zeros_like(dec_ref)
        bonus_ref[...] = jnp.zeros_like(bonus_ref)

    rows = r_ref.shape[0]
    T = RWKV_T
    S = 2 * T
    n = rows // T
    ii =lax.broadcasted_iota(jnp.int32, (S, S), 0)
    jj = lax.broadcasted_iota(jnp.int32, (S, S), 1)
    same = (ii // T) == (jj // T)
    strict = same & (jj < ii)
    incl = same & (jj <= ii)
    eye = ii == jj
    tri_incl = jnp.where(incl, 1.0, 0.0).astype(BF16)
    hi_ = lax.broadcasted_iota(jnp.int32, (LANES, LANES), 0) // RWKV_HEAD
    hj_ = lax.broadcasted_iota(jnp.int32, (LANES, LANES), 1) // RWKV_HEAD
    head_sum = jnp.where(hi_ == hj_, 1.0, 0.0).astype(BF16)
    lane_lo = lax.broadcasted_iota(jnp.int32, (T, LANES), 1) < RWKV_HEAD
    cst = (lane_lo, tri_incl, strict, incl, eye)

    state = [jnp.where(first_prv, 0.0, h_ref[...])]
    outs = []

    def serial_step(c):
        h = state[0]
        gd = gd_ref[c]
        z = _dot(jnp.concatenate([gd[:, :LANES], q_ref[c]], axis=0), h)
        o_s = z[LANES:] + o0_ref[c]
        outs.append(o_s[:T] + o_s[T:])
        state[0] = dec_ref[c] * h + z[:LANES] + gd[:, LANES:]

    def serial_finish():
        h_ref[...] = state[0]
        o = jnp.concatenate(outs, axis=0) if n > 1 else outs[0]
        inv = 1.0 / RWKV_HEAD
        mean = _dot_f32_lhs(o, head_sum) * inv
        d = o - mean
        var = _dot_f32_lhs(d * d, head_sum) * inv
        y = d * lax.rsqrt(var + GN_EPS) * lnw_ref[...] + lnb_ref[...]
        o_ref[...] = ((y + bonus_ref[...]) * g_ref[...]).astype(o_ref.dtype)

    r = _token_shift(r_ref[...], rh_ref[...], mur_ref[...], first_cur)
    k = _token_shift(k_ref[...], kh_ref[...], muk_ref[...], first_cur)
    v = _token_shift(v_ref[...], vh_ref[...], muv_ref[...], first_cur)
    a = a_ref[...]
    kk = k * kk_ref[...]
    k = k * (1.0 + (a - 1.0) * ka_ref[...])
    serial_step(0)
    ss = _dot_f32_lhs(kk * kk, head_sum)
    kk = kk * lax.rsqrt(jnp.maximum(ss, 1e-24))
    bonus = _dot_f32_lhs(r * k * rk_ref[...], head_sum) * v

    gen = _rwkv_parallel(r, k, v, -kk, kk * a, lw_ref[...], cst, T)
    done = 1
    while True:
        try:
            next(gen)
        except StopIteration as stop:
            gd, q, o0, decay = stop.value
            break
        if done < n:
            serial_step(done)
        elif done == n:
            serial_finish()
        done += 1
    assert done > n, "fewer parallel stages than chunks per block"

    for c in range(n):
        gd_ref[c] = gd[c]
        q_ref[c] = q[c]
        o0_ref[c] = o0[c]
        dec_ref[c] = decay[c]
    bonus_ref[...] = bonus


def _rwkv(proj, mu, lw, a, g, k_k, k_a, r_k, ln_w, ln_b, rows):
    L = proj.shape[0]
    nb = D_RWKV // LANES
    nblk = L // rows
    nsteps = PAIRS * nblk
    n = rows // RWKV_T

    def cur(s):
        b = jnp.minimum(s, nsteps - 1)
        return lax.div(b, nblk), lax.rem(b, nblk)

    def prv(s):
        b = jnp.maximum(s - 1, 0)
        return lax.div(b, nblk), lax.rem(b, nblk)

    def col(off):
        return pl.BlockSpec((rows, LANES), lambda s: (cur(s)[1], cur(s)[0] + off))

    def halo(off):
        return pl.BlockSpec((8, LANES),
                            lambda s: (jnp.maximum(cur(s)[1] * (rows // 8) - 1, 0), cur(s)[0] + off))

    def vec(off):
        return pl.BlockSpec((1, LANES), lambda s: (0, cur(s)[0] + off))

    def lag_col():
        return pl.BlockSpec((rows, LANES), lambda s: (prv(s)[1], prv(s)[0]))

    def lag_vec():
        return pl.BlockSpec((1, LANES), lambda s: (0, prv(s)[0]))

    return pl.pallas_call(
        functools.partial(_rwkv_kernel, nblk=nblk),
        grid=(nsteps + 1,),
        in_specs=[col(0), col(nb), col(2 * nb), halo(0), halo(nb), halo(2 * nb),
                  vec(0), vec(nb), vec(2 * nb),
                  col(0), col(0), vec(0), vec(0), vec(0),
                  lag_col(), lag_vec(), lag_vec()],
        out_specs=lag_col(),
        out_shape=jax.ShapeDtypeStruct((L, D_RWKV), BF16),
        scratch_shapes=[pltpu.VMEM((LANES, LANES), F32),
                        pltpu.VMEM((n, LANES, 2 * LANES), F32),
                        pltpu.VMEM((n, 2 * RWKV_T, LANES), F32),
                        pltpu.VMEM((n, 2 * RWKV_T, LANES), F32),
                        pltpu.VMEM((n, LANES, LANES), F32),
                        pltpu.VMEM((rows, LANES), F32)],
        compiler_params=_cparams(1),
        name="rwkv",
    )(proj, proj, proj, proj, proj, proj, mu, mu, mu, lw, a, k_k, k_a, r_k, g, ln_w, ln_b)


def _s5_table_kernel(lr_ref, li_ref, ldt_ref, pos_re, pos_im, neg_re, neg_im, z_re, z_im,
                     lbc_re, lbc_im):
    lr = lr_ref[...]
    li = li_ref[...]
    dt = jnp.exp(ldt_ref[...])
    centre = pos_re.shape[0] // 2
    t = (lax.broadcasted_iota(jnp.int32, pos_re.shape, 0) - centre).astype(F32)
    mag = jnp.exp(t * (lr * dt))
    ang = t * (li * dt)
    cs, sn = jnp.cos(ang), jnp.sin(ang)
    pos_re[...] = mag * cs
    pos_im[...] = mag * sn
    inv = 1.0 / mag
    neg_re[...] = inv * cs
    neg_im[...] = -(inv * sn)
    mc = jnp.exp((centre + 1.0) * (lr * dt))
    lbc_re[...] = mc * jnp.cos((centre + 1.0) * (li * dt))
    lbc_im[...] = mc * jnp.sin((centre + 1.0) * (li * dt))
    m1 = jnp.exp(lr * dt)
    lb_re = m1 * jnp.cos(li * dt)
    lb_im = m1 * jnp.sin(li * dt)
    den = lr * lr + li * li
    z_re[...] = ((lb_re - 1.0) * lr + lb_im * li) / den
    z_im[...] = (lb_im * lr - (lb_re - 1.0) * li) / den


def _s5_tables(lr, li, ldt, rows):
    n = lr.shape[1]
    full = pl.BlockSpec((1, n), lambda: (0, 0))
    tab = pl.BlockSpec((rows, n), lambda: (0, 0))
    return pl.pallas_call(
        _s5_table_kernel,
        in_specs=[full, full, full],
        out_specs=[tab, tab, tab, tab, full, full, full, full],
        out_shape=[jax.ShapeDtypeStruct((rows, n), F32)] * 4 + [jax.ShapeDtypeStruct((1, n), F32)] * 4,
        name="s5_tables",
    )(lr, li, ldt)


def _s5_kernel(u_ref, pos_re_ref, pos_im_ref, neg_re_ref, neg_im_ref, z_re_ref, z_im_ref,
               lbc_re_ref, lbc_im_ref, b_re_ref, b_im_ref, c_re_ref, c_im_ref, d_ref, wglu_ref,
               bglu_ref, o_ref, bb_ref, st_ref):
    R = u_ref.shape[0]
    T = pos_re_ref.shape[0]
    NS = S5_GROUPS * S5_STATE
    BS = NS // S5_BLOCKS
    BC = D_S5 // S5_BLOCKS

    @pl.when(pl.program_id(0) == 0)
    def _():
        st_ref[...] = jnp.zeros_like(st_ref)
        for j in range(S5_BLOCKS):
            zr = z_re_ref[:, j * BS:(j + 1) * BS]
            zi = z_im_ref[:, j * BS:(j + 1) * BS]
            br = b_re_ref[j]
            bi = b_im_ref[j]
            bb_ref[j, :, :BS] = (zr * br - zi * bi).astype(BF16)
            bb_ref[j, :, BS:] = (zr * bi + zi * br).astype(BF16)

    u = u_ref[...]
    ub = u.astype(BF16)
    ii = lax.broadcasted_iota(jnp.int32, (T, T), 0)
    jj = lax.broadcasted_iota(jnp.int32, (T, T), 1)
    tri = jnp.where(jj <= ii, 1.0, 0.0).astype(BF16)

    ys = []
    for j in range(S5_BLOCKS):
        sl = slice(j * BS, (j + 1) * BS)
        bu = jnp.dot(ub[:, j * BC:(j + 1) * BC], bb_ref[j], preferred_element_type=F32)
        nr, ni = neg_re_ref[:, sl], neg_im_ref[:, sl]
        pr, pi = pos_re_ref[:, sl], pos_im_ref[:, sl]
        lr_, li_ = lbc_re_ref[:, sl], lbc_im_ref[:, sl]
        s_re = st_ref[0:1, sl]
        s_im = st_ref[1:2, sl]
        xs_re, xs_im = [], []
        for t in range(R // T):
            bu_re, bu_im = bu[t * T:(t + 1) * T, :BS], bu[t * T:(t + 1) * T, BS:]
            zz = jnp.concatenate([nr * bu_re - ni * bu_im, nr * bu_im + ni * bu_re], axis=1)
            acc = jnp.dot(tri, zz.astype(BF16), preferred_element_type=F32)
            a_re = acc[:, :BS] + (lr_ * s_re - li_ * s_im)
            a_im = acc[:, BS:] + (lr_ * s_im + li_ * s_re)
            x_re = pr * a_re - pi * a_im
            x_im = pr * a_im + pi * a_re
            s_re, s_im = x_re[T - 1:T, :], x_im[T - 1:T, :]
            xs_re.append(x_re)
            xs_im.append(x_im)
        st_ref[0:1, sl] = s_re
        st_ref[1:2, sl] = s_im
        cat = lambda xs: jnp.concatenate(xs, axis=0) if len(xs) > 1 else xs[0]
        ys.append(_dot(cat(xs_re), c_re_ref[j]) - _dot(cat(xs_im), c_im_ref[j]))
    y = jnp.concatenate(ys, axis=1) + d_ref[...] * u
    y = 0.5 * y * (1.0 + jnp.tanh(math.sqrt(2.0 / math.pi) * (y + 0.044715 * (y * y * y))))
    zg = _dot(y, wglu_ref[...]) + bglu_ref[...]
    o_ref[...] = (zg[:, :D_S5] * _sigmoid(zg[:, D_S5:])).astype(o_ref.dtype)


def _s5(proj, tables, b_re, b_im, c_re, c_im, d, w_glu, b_glu, rows):
    L = proj.shape[0]
    NS = S5_GROUPS * S5_STATE
    BS = NS // S5_BLOCKS
    BC = D_S5 // S5_BLOCKS
    t = tables[0].shape[0]
    c2 = lambda shape: pl.BlockSpec(shape, lambda i: (0, 0))
    c3 = lambda shape: pl.BlockSpec(shape, lambda i: (0, 0, 0))
    return pl.pallas_call(
        _s5_kernel,
        grid=(L // rows,),
        in_specs=[pl.BlockSpec((rows, D_S5), lambda i: (i, 0)),
                  c2((t, NS)), c2((t, NS)), c2((t, NS)), c2((t, NS)),
                  c2((1, NS)), c2((1, NS)), c2((1, NS)), c2((1, NS)),
                  c3((S5_BLOCKS, BC, BS)), c3((S5_BLOCKS, BC, BS)),
                  c3((S5_BLOCKS, BS, BC)), c3((S5_BLOCKS, BS, BC)),
                  c2((1, D_S5)), c2((D_S5, 2 * D_S5)), c2((1, 2 * D_S5))],
        out_specs=pl.BlockSpec((rows, D_S5), lambda i: (i, 0)),
        out_shape=jax.ShapeDtypeStruct((L, D_S5), BF16),
        scratch_shapes=[pltpu.VMEM((S5_BLOCKS, BC, 2 * BS), BF16), pltpu.VMEM((8, NS), F32)],
        compiler_params=_cparams(1),
        name="s5",
    )(proj, *tables, b_re, b_im, c_re, c_im, d, w_glu, b_glu)


def _block_diag_groups(w):
    g, r, c = w.shape
    gb = g // S5_BLOCKS
    w = w.reshape(S5_BLOCKS, gb, r, c)
    eye = jnp.eye(gb, dtype=w.dtype)
    out = w[:, :, :, None, :] * eye[None, :, None, :, None]
    return out.reshape(S5_BLOCKS, gb * r, gb * c)


def kernel(x, c, w_ada, b_ada, norm_pre_mix, norm_post_mix, norm_pre_ffn, norm_post_ffn, w_in, rwkv_mu, rwkv_w0, rwkv_w2, rwkv_a0, rwkv_a2, rwkv_g2, rwkv_k_k, rwkv_k_a, rwkv_r_k, rwkv_ln_w, rwkv_ln_b, s5_lam_re, s5_lam_im, s5_log_dt, s5_b_re, s5_b_im, s5_c_re, s5_c_im, s5_d, s5_w_glu, s5_b_glu, w_up_rwkv, w_up_s5, w_out, ffn_w_gate, ffn_w_up, ffn_w_down):
    bsz, L, _ = x.shape
    assert bsz == 1 and w_ada.shape[0] == 1
    h = x.reshape(L, D_MODEL)
    tm = min(1024, L)
    tr = min(256, L)
    row = lambda v: v.reshape(1, -1)

    mod = _ada(c.reshape(D_MODEL, 1), w_ada[0], row(b_ada[0]))

    wit = jnp.swapaxes(w_in[0], 0, 1)
    gpad = LORA_G_PAD - LORA_G
    n_rkv = 3 * D_RWKV
    mu = row(rwkv_mu[0])
    mu_lora = jnp.concatenate([rwkv_mu[0, n_rkv:], jnp.zeros((gpad,), F32)]).reshape(1, -1)
    g2 = jnp.concatenate([rwkv_g2[0], jnp.zeros((gpad, D_RWKV), F32)], axis=0).astype(BF16)

    xm = _prenorm(h, row(norm_pre_mix[0]), mod, 0, 1, tr)
    proj_rkv = _matmul_wt(xm, wit, 0, n_rkv, F32, tm, 512, "proj_rkv")
    proj_l = _matmul_wt(xm, wit, n_rkv, LORA_W + LORA_A + LORA_G_PAD, F32, tm, 256, "proj_lora")
    proj_u = _matmul_wt(xm, wit, RWKV_COLS, D_S5, F32, tm, 512, "proj_u")
    gates = _matmul_wt(xm, wit, RWKV_COLS + D_S5, 2 * D_MODEL, BF16, tm, 512, "proj_g")

    lw, a, g = _lora(proj_l, mu_lora, row(rwkv_w0[0]), row(rwkv_a0[0]), rwkv_w2[0].astype(BF16),
                     rwkv_a2[0].astype(BF16), g2, tr)
    o_a = _rwkv(proj_rkv, mu, lw, a, g, row(rwkv_k_k[0]), row(rwkv_k_a[0]), row(rwkv_r_k[0]),
                row(rwkv_ln_w[0]), row(rwkv_ln_b[0]), min(RWKV_T * RWKV_CHUNKS_PER_STEP, L))

    rep = lambda v: jnp.repeat(v, S5_STATE).reshape(1, -1)
    tables = _s5_tables(row(s5_lam_re[0]), row(s5_lam_im[0]), rep(s5_log_dt[0]), S5_T)
    bt = lambda w: _block_diag_groups(jnp.swapaxes(w, 1, 2))
    o_b = _s5(proj_u, tables, bt(s5_b_re[0]), bt(s5_b_im[0]),
              _block_diag_groups(jnp.swapaxes(s5_c_re[0], 1, 2)).astype(BF16),
              _block_diag_groups(jnp.swapaxes(s5_c_im[0], 1, 2)).astype(BF16),
              row(s5_d[0]), s5_w_glu[0].astype(BF16), row(s5_b_glu[0]), min(S5_ROWS, L))

    merged = _merge(o_a, o_b, w_up_rwkv[0], w_up_s5[0], gates, tm, 512)
    mix = _matmul(merged, w_out[0], F32, tm, 512, "w_out")

    h1, xf = _mid(h, mix, row(norm_post_mix[0]), row(norm_pre_ffn[0]), mod, tr)
    act = _ffn_up(xf, ffn_w_gate[0], ffn_w_up[0], tm, 256)
    ff = _matmul_ksplit(act, ffn_w_down[0].astype(BF16), F32, tm, 512, D_FF // 2, "ffn_down")
    out = _final(h1, ff, row(norm_post_ffn[0]), mod, tr)
    return out.reshape(bsz, L, D_MODEL)
```

```python
import functools
import math

import jax
import jax.numpy as jnp
from jax import lax
from jax.experimental import pallas as pl
from jax.experimental.pallas import tpu as pltpu

F32 = jnp.float32
BF16 = jnp.bfloat16

D_MODEL = 4096
RMS_EPS = 1e-6
D_RWKV = 2048
RWKV_HEAD = 64
LORA_W = 128
LORA_A = 128
LORA_G = 480
LORA_G_PAD = 512
GN_EPS = 64e-5
D_S5 = 1024
S5_GROUPS = 64
S5_GROUP_CH = 16
S5_STATE = 64
S5_BLOCKS = 4
D_FF = 11008

LANES = 128
PAIRS = D_RWKV // LANES
RWKV_T = 64
RWKV_CHUNKS_PER_STEP = 8
S5_T = 128
S5_ROWS = 256

RWKV_COLS = 3 * D_RWKV + LORA_W + LORA_A + LORA_G

VMEM_LIMIT = 56 * 1024 * 1024


def _cparams(n_axes, vmem=VMEM_LIMIT):
    return pltpu.CompilerParams(dimension_semantics=("arbitrary",) * n_axes, vmem_limit_bytes=vmem)


def _dot(a, b):
    return jnp.dot(a.astype(BF16), b.astype(BF16), preferred_element_type=F32)


def _dot_nt(a, b):
    return lax.dot_general(a.astype(BF16), b.astype(BF16), (((1,), (1,)), ((), ())),
                           preferred_element_type=F32)


def _dot_tn(a, b):
    return lax.dot_general(a.astype(BF16), b.astype(BF16), (((0,), (0,)), ((), ())),
                           preferred_element_type=F32)


def _split2(x):
    hi = x.astype(BF16)
    lo = (x - hi.astype(F32)).astype(BF16)
    return hi, lo


def _head_sums(x, sel2):
    half = x.shape[0] // 2
    hi, lo = _split2(jnp.concatenate([x[:half], x[half:]], axis=1))
    res = (jnp.dot(hi, sel2, preferred_element_type=F32) + jnp.dot(lo, sel2, preferred_element_type=F32))
    return jnp.concatenate([res[:, :LANES], res[:, LANES:]], axis=0)


def _dot_f32_rhs(sel, x):
    hi, lo = _split2(x)
    return (jnp.dot(sel, hi, preferred_element_type=F32) + jnp.dot(sel, lo, preferred_element_type=F32))


def _rms(x):
    return x * lax.rsqrt(jnp.mean(x * x, axis=-1, keepdims=True) + RMS_EPS)


def _sigmoid(x):
    return 1.0 / (1.0 + jnp.exp(-x))


def _token_shift(p, halo, mu, first):
    last = jnp.where(first, 0.0, halo[7:8, :])
    rolled = pltpu.roll(p, 1, axis=0)
    row = lax.broadcasted_iota(jnp.int32, p.shape, 0)
    prev = jnp.where(row == 0, last, rolled)
    return p + (prev - p) * mu


def _ada_kernel(c_ref, w_ref, b_ref, o_ref):
    c = c_ref[...]
    cs = c * _sigmoid(c)
    tn = o_ref.shape[1]
    rows = 512
    acc = jnp.zeros((8, tn), F32)
    for k0 in range(0, D_MODEL, rows):
        blk = w_ref[k0:k0 + rows, :] * cs[k0:k0 + rows, :]
        acc = acc + jnp.sum(blk.reshape(rows // 8, 8, tn), axis=0)
    o_ref[...] = jnp.sum(acc, axis=0, keepdims=True) + b_ref[...]


def _ada(c_col, w_ada, b_ada):
    n = w_ada.shape[1]
    tn = 512
    return pl.pallas_call(
        _ada_kernel,
        grid=(n // tn,),
        in_specs=[pl.BlockSpec((D_MODEL, 1), lambda j: (0, 0)),
                  pl.BlockSpec((D_MODEL, tn), lambda j: (0, j)),
                  pl.BlockSpec((1, tn), lambda j: (0, j))],
        out_specs=pl.BlockSpec((1, tn), lambda j: (0, j)),
        out_shape=jax.ShapeDtypeStruct((1, n), F32),
        compiler_params=_cparams(1),
        name="ada",
    )(c_col, w_ada, b_ada)


def _prenorm_kernel(x_ref, g_ref, sh_ref, sc_ref, o_ref):
    y = _rms(x_ref[...]) * g_ref[...]
    o_ref[...] = (y * (1.0 + sc_ref[...]) + sh_ref[...]).astype(o_ref.dtype)


def _prenorm(x, g, mod, sh_idx, sc_idx, tm):
    L = x.shape[0]
    return pl.pallas_call(
        _prenorm_kernel,
        grid=(L // tm,),
        in_specs=[pl.BlockSpec((tm, D_MODEL), lambda i: (i, 0)),
                  pl.BlockSpec((1, D_MODEL), lambda i: (0, 0)),
                  pl.BlockSpec((1, D_MODEL), lambda i: (0, sh_idx)),
                  pl.BlockSpec((1, D_MODEL), lambda i: (0, sc_idx))],
        out_specs=pl.BlockSpec((tm, D_MODEL), lambda i: (i, 0)),
        out_shape=jax.ShapeDtypeStruct((L, D_MODEL), BF16),
        compiler_params=_cparams(1),
        name="prenorm",
    )(x, g, mod, mod)


def _mid_kernel(x_ref, mix_ref, gpost_ref, gate_ref, gpre_ref, sh_ref, sc_ref, h_ref, xf_ref):
    h = x_ref[...] + gate_ref[...] * (_rms(mix_ref[...].astype(F32)) * gpost_ref[...])
    h_ref[...] = h
    y = _rms(h) * gpre_ref[...]
    xf_ref[...] = (y * (1.0 + sc_ref[...]) + sh_ref[...]).astype(xf_ref.dtype)


def _mid(x, mix, gpost, gpre, mod, tm):
    L = x.shape[0]
    row = lambda i: (i, 0)
    vec = lambda k: pl.BlockSpec((1, D_MODEL), lambda i: (0, k))
    return pl.pallas_call(
        _mid_kernel,
        grid=(L // tm,),
        in_specs=[pl.BlockSpec((tm, D_MODEL), row), pl.BlockSpec((tm, D_MODEL), row),
                  vec(0), vec(2), vec(0), vec(3), vec(4)],
        out_specs=[pl.BlockSpec((tm, D_MODEL), row), pl.BlockSpec((tm, D_MODEL), row)],
        out_shape=[jax.ShapeDtypeStruct((L, D_MODEL), F32), jax.ShapeDtypeStruct((L, D_MODEL), BF16)],
        compiler_params=_cparams(1),
        name="mid",
    )(x, mix, gpost, mod, gpre, mod, mod)


def _final_kernel(h_ref, ff_ref, gpost_ref, gate_ref, o_ref):
    o_ref[...] = h_ref[...] + gate_ref[...] * (_rms(ff_ref[...].astype(F32)) * gpost_ref[...])


def _final(h, ff, gpost, mod, tm):
    L = h.shape[0]
    row = lambda i: (i, 0)
    return pl.pallas_call(
        _final_kernel,
        grid=(L // tm,),
        in_specs=[pl.BlockSpec((tm, D_MODEL), row), pl.BlockSpec((tm, D_MODEL), row),
                  pl.BlockSpec((1, D_MODEL), lambda i: (0, 0)),
                  pl.BlockSpec((1, D_MODEL), lambda i: (0, 5))],
        out_specs=pl.BlockSpec((tm, D_MODEL), row),
        out_shape=jax.ShapeDtypeStruct((L, D_MODEL), F32),
        compiler_params=_cparams(1),
        name="final",
    )(h, ff, gpost, mod)


def _mm_kernel(a_ref, w_ref, o_ref):
    o_ref[...] = jnp.dot(a_ref[...], w_ref[...].astype(BF16),
                         preferred_element_type=F32).astype(o_ref.dtype)


def _matmul(a, w, out_dtype, tm, tn, name, n=None):
    m, k = a.shape
    n = w.shape[1] if n is None else n
    return pl.pallas_call(
        _mm_kernel,
        grid=(m // tm, n // tn),
        in_specs=[pl.BlockSpec((tm, k), lambda i, j: (i, 0)),
                  pl.BlockSpec((k, tn), lambda i, j: (0, j))],
        out_specs=pl.BlockSpec((tm, tn), lambda i, j: (i, j)),
        out_shape=jax.ShapeDtypeStruct((m, n), out_dtype),
        compiler_params=_cparams(2),
        name=name,
    )(a, w)


def _mm_acc_kernel(a_ref, w_ref, o_ref, acc_ref):
    kk = pl.program_id(2)
    last = pl.num_programs(2) - 1
    part = jnp.dot(a_ref[...], w_ref[...], preferred_element_type=F32)

    @pl.when(kk == 0)
    def _():
        acc_ref[...] = part

    @pl.when((kk > 0) & (kk < last))
    def _():
        acc_ref[...] = acc_ref[...] + part

    @pl.when(kk == last)
    def _():
        o_ref[...] = (acc_ref[...] + part).astype(o_ref.dtype)


def _dot_wt(a, wt):
    return lax.dot_general(a, wt.astype(BF16), (((1,), (1,)), ((), ())), preferred_element_type=F32)


def _mm_wt_kernel(a_ref, wt_ref, o_ref):
    o_ref[...] = _dot_wt(a_ref[...], wt_ref[...]).astype(o_ref.dtype)


def _matmul_wt(a, wt, row0, n, out_dtype, tm, tn, name):
    m, k = a.shape
    return pl.pallas_call(
        _mm_wt_kernel,
        grid=(m // tm, n // tn),
        in_specs=[pl.BlockSpec((tm, k), lambda i, j: (i, 0)),
                  pl.BlockSpec((pl.Element(tn), pl.Element(k)),
                               lambda i, j: (pl.multiple_of(row0 + j * tn, 8), 0))],
        out_specs=pl.BlockSpec((tm, tn), lambda i, j: (i, j)),
        out_shape=jax.ShapeDtypeStruct((m, n), out_dtype),
        compiler_params=_cparams(2),
        name=name,
    )(a, wt)


def _matmul_ksplit(a, w, out_dtype, tm, tn, tk, name):
    m, k = a.shape
    n = w.shape[1]
    assert k % tk == 0 and k // tk >= 2
    return pl.pallas_call(
        _mm_acc_kernel,
        grid=(m // tm, n // tn, k // tk),
        in_specs=[pl.BlockSpec((tm, tk), lambda i, j, kk: (i, kk)),
                  pl.BlockSpec((tk, tn), lambda i, j, kk: (kk, j))],
        out_specs=pl.BlockSpec((tm, tn), lambda i, j, kk: (i, j)),
        out_shape=jax.ShapeDtypeStruct((m, n), out_dtype),
        scratch_shapes=[pltpu.VMEM((tm, tn), F32)],
        compiler_params=_cparams(3),
        name=name,
    )(a, w)


def _merge_kernel(oa_ref, ob_ref, wa_ref, wb_ref, ga_ref, gb_ref, o_ref):
    ya = jnp.dot(oa_ref[...], wa_ref[...].astype(BF16), preferred_element_type=F32)
    yb = jnp.dot(ob_ref[...], wb_ref[...].astype(BF16), preferred_element_type=F32)
    m = _sigmoid(ga_ref[...].astype(F32)) * ya + _sigmoid(gb_ref[...].astype(F32)) * yb
    o_ref[...] = m.astype(o_ref.dtype)


def _merge(o_a, o_b, w_up_a, w_up_b, gates, tm, tn):
    L = o_a.shape[0]
    nb = D_MODEL // tn
    return pl.pallas_call(
        _merge_kernel,
        grid=(L // tm, nb),
        in_specs=[pl.BlockSpec((tm, D_RWKV), lambda i, j: (i, 0)),
                  pl.BlockSpec((tm, D_S5), lambda i, j: (i, 0)),
                  pl.BlockSpec((D_RWKV, tn), lambda i, j: (0, j)),
                  pl.BlockSpec((D_S5, tn), lambda i, j: (0, j)),
                  pl.BlockSpec((tm, tn), lambda i, j: (i, j)),
                  pl.BlockSpec((tm, tn), lambda i, j: (i, j + nb))],
        out_specs=pl.BlockSpec((tm, tn), lambda i, j: (i, j)),
        out_shape=jax.ShapeDtypeStruct((L, D_MODEL), BF16),
        compiler_params=_cparams(2),
        name="merge",
    )(o_a, o_b, w_up_a, w_up_b, gates, gates)


def _ffn_up_kernel(x_ref, wg_ref, wu_ref, o_ref):
    x = x_ref[...]
    a = jnp.dot(x, wg_ref[...].astype(BF16), preferred_element_type=F32)
    b = jnp.dot(x, wu_ref[...].astype(BF16), preferred_element_type=F32)
    o_ref[...] = ((a * _sigmoid(a)) * b).astype(o_ref.dtype)


def _ffn_up(xf, w_gate, w_up, tm, tn):
    L = xf.shape[0]
    n = w_gate.shape[1]
    return pl.pallas_call(
        _ffn_up_kernel,
        grid=(L // tm, n // tn),
        in_specs=[pl.BlockSpec((tm, D_MODEL), lambda i, j: (i, 0)),
                  pl.BlockSpec((D_MODEL, tn), lambda i, j: (0, j)),
                  pl.BlockSpec((D_MODEL, tn), lambda i, j: (0, j))],
        out_specs=pl.BlockSpec((tm, tn), lambda i, j: (i, j)),
        out_shape=jax.ShapeDtypeStruct((L, n), BF16),
        compiler_params=_cparams(2),
        name="ffn_up",
    )(xf, w_gate, w_up)


def _lora_kernel(p_ref, halo_ref, mu_ref, w0_ref, a0_ref, w2_ref, a2_ref, g2_ref,
                 lw_ref, a_ref, g_ref):
    ps = _token_shift(p_ref[...], halo_ref[...], mu_ref[...], pl.program_id(0) == 0)
    xw = ps[:, :LORA_W]
    xa = ps[:, LORA_W:LORA_W + LORA_A]
    xg = ps[:, LORA_W + LORA_A:]
    z = -(w0_ref[...] + _dot(jnp.tanh(xw), w2_ref[...]))
    softplus = jnp.maximum(z, 0.0) + jnp.log(1.0 + jnp.exp(-jnp.abs(z)))
    lw_ref[...] = -jnp.exp(-softplus - 0.5)
    a_ref[...] = _sigmoid(a0_ref[...] + _dot(xa, a2_ref[...]))
    g_ref[...] = _dot(_sigmoid(xg), g2_ref[...])


def _lora(proj, mu, w0, a0, w2, a2, g2, tm):
    L, n = proj.shape
    full = lambda shape: pl.BlockSpec(shape, lambda i: (0, 0))
    out = pl.BlockSpec((tm, D_RWKV), lambda i: (i, 0))
    return pl.pallas_call(
        _lora_kernel,
        grid=(L // tm,),
        in_specs=[pl.BlockSpec((tm, n), lambda i: (i, 0)),
                  pl.BlockSpec((8, n), lambda i: (jnp.maximum(i * (tm // 8) - 1, 0), 0)),
                  full((1, n)),
                  full((1, D_RWKV)), full((1, D_RWKV)),
                  full((LORA_W, D_RWKV)), full((LORA_A, D_RWKV)), full((LORA_G_PAD, D_RWKV))],
        out_specs=[out, out, out],
        out_shape=[jax.ShapeDtypeStruct((L, D_RWKV), F32)] * 3,
        compiler_params=_cparams(1),
        name="lora",
    )(proj, proj, mu, w0, a0, w2, a2, g2)


def _rwkv_parallel(r, k, v, am, bm, lw, cst, T):
    S = 2 * T
    n = r.shape[0] // T
    lane_lo, tri_incl, strict, incl, eye = cst
    cr = range(n)

    def stack(x, c):
        xc = x[c * T:(c + 1) * T]
        return jnp.concatenate([jnp.where(lane_lo, xc, 0.0), jnp.where(lane_lo, 0.0, xc)], axis=0)

    lw_s = [stack(lw, c) for c in cr]
    c_all = _dot_f32_rhs(tri_incl, jnp.concatenate(lw_s, axis=1))
    yield
    cs = [c_all[:, c * LANES:(c + 1) * LANES] for c in cr]
    c_end = [cs[c][T - 1:T, :] + cs[c][S - 1:S, :] for c in cr]
    r_s = [stack(r, c) for c in cr]
    k_s = [stack(k, c) for c in cr]
    v_s = [stack(v, c) for c in cr]
    a_s = [stack(am, c) for c in cr]
    b_s = [stack(bm, c) for c in cr]
    e_neg = [jnp.exp(-cs[c]) for c in cr]
    e_end = [jnp.exp(c_end[c] - cs[c]) for c in cr]
    rh = [r_s[c] * jnp.exp(cs[c]) for c in cr]
    ah = [a_s[c] * jnp.exp(cs[c] - lw_s[c]) for c in cr]
    bh = [b_s[c] * e_neg[c] for c in cr]
    kh = [k_s[c] * e_neg[c] for c in cr]
    bc = [b_s[c] * e_end[c] for c in cr]
    kc = [k_s[c] * e_end[c] for c in cr]

    big = [_dot_nt(jnp.concatenate([ah[c], rh[c]], axis=0), jnp.concatenate([bh[c], kh[c]], axis=0))
           for c in cr]
    yield
    nj = [jnp.where(strict, big[c][:S, :S], 0.0) for c in cr]
    m_rb = [jnp.where(incl, big[c][S:, :S], 0.0) for c in cr]
    m_kk = [jnp.concatenate([jnp.where(strict, big[c][:S, S:], 0.0),
                             jnp.where(incl, big[c][S:, S:], 0.0)], axis=0) for c in cr]
    avrv = [_dot(m_kk[c], v_s[c]) for c in cr]
    yield

    x = [jnp.concatenate([ah[c], avrv[c][:S]], axis=1) for c in cr]
    steps = T.bit_length() - 1
    for j in range(steps):
        x = [x[c] + _dot(nj[c], x[c]) for c in cr]
        if j + 1 < steps:
            nj = [_dot(nj[c], nj[c]) for c in cr]
        yield

    y = [_dot(m_rb[c], x[c]) for c in cr]
    q = [rh[c] + y[c][:, :LANES] for c in cr]
    o0 = [avrv[c][S:] + y[c][:, LANES:] for c in cr]
    gd = [_dot_tn(jnp.concatenate([bc[c], kc[c]], axis=0),
                  jnp.concatenate([x[c], jnp.concatenate([jnp.zeros_like(v_s[c]), v_s[c]], axis=1)],
                                  axis=0)) for c in cr]
    decay = [jnp.broadcast_to(jnp.sum(jnp.where(eye, jnp.exp(c_end[c]), 0.0), axis=1, keepdims=True),
                              (LANES, LANES)) for c in cr]
    return gd, q, o0, decay


def _rwkv_kernel(r_ref, k_ref, v_ref, rh_ref, kh_ref, vh_ref, mur_ref, muk_ref, muv_ref,
                 lw_ref, a_ref, kk_ref, ka_ref, rk_ref, g_ref, lnw_ref, lnb_ref,
                 o_ref, h_ref, gd_ref, q_ref, o0_ref, dec_ref, bonus_ref, *, nblk):
    s = pl.program_id(0)
    last = pl.num_programs(0) - 2
    first_cur = lax.rem(jnp.minimum(s, last), nblk) == 0
    first_prv = lax.rem(jnp.maximum(s - 1, 0), nblk) == 0

    @pl.when(s == 0)
    def _():
        h_ref[...] = jnp.zeros_like(h_ref)
        gd_ref[...] = jnp.zeros_like(gd_ref)
        q_ref[...] = jnp.zeros_like(q_ref)
        o0_ref[...] = jnp.zeros_like(o0_ref)
        dec_ref[...] = jnp.zeros_like(dec_ref)
        bonus_ref[...] = jnp.zeros_like(bonus_ref)

    rows = r_ref.shape[0]
    T = RWKV_T
    S = 2 * T
    n = rows // T
    ii =lax.broadcasted_iota(jnp.int32, (S, S), 0)
    jj = lax.broadcasted_iota(jnp.int32, (S, S), 1)
    same = (ii // T) == (jj // T)
    strict = same & (jj < ii)
    incl = same & (jj <= ii)
    eye = ii == jj
    tri_incl = jnp.where(incl, 1.0, 0.0).astype(BF16)
    hi_ = lax.broadcasted_iota(jnp.int32, (2 * LANES, 2 * LANES), 0) // RWKV_HEAD
    hj_ = lax.broadcasted_iota(jnp.int32, (2 * LANES, 2 * LANES), 1) // RWKV_HEAD
    head_sum = jnp.where(hi_ == hj_, 1.0, 0.0).astype(BF16)
    lane_lo = lax.broadcasted_iota(jnp.int32, (T, LANES), 1) < RWKV_HEAD
    cst = (lane_lo, tri_incl, strict, incl, eye)

    state = [jnp.where(first_prv, 0.0, h_ref[...])]
    outs = []

    def serial_step(c):
        h = state[0]
        gd = gd_ref[c]
        z = _dot(jnp.concatenate([gd[:, :LANES], q_ref[c]], axis=0), h)
        o_s = z[LANES:] + o0_ref[c]
        outs.append(o_s[:T] + o_s[T:])
        state[0] = dec_ref[c] * h + z[:LANES] + gd[:, LANES:]

    def serial_finish():
        h_ref[...] = state[0]
        o = jnp.concatenate(outs, axis=0) if n > 1 else outs[0]
        inv = 1.0 / RWKV_HEAD
        mean = _head_sums(o, head_sum) * inv
        d = o - mean
        var = _head_sums(d * d, head_sum) * inv
        y = d * lax.rsqrt(var + GN_EPS) * lnw_ref[...] + lnb_ref[...]
        o_ref[...] = ((y + bonus_ref[...]) * g_ref[...]).astype(o_ref.dtype)

    r = _token_shift(r_ref[...], rh_ref[...], mur_ref[...], first_cur)
    k = _token_shift(k_ref[...], kh_ref[...], muk_ref[...], first_cur)
    v = _token_shift(v_ref[...], vh_ref[...], muv_ref[...], first_cur)
    a = a_ref[...]
    kk = k * kk_ref[...]
    k = k * (1.0 + (a - 1.0) * ka_ref[...])
    serial_step(0)
    ss = _head_sums(kk * kk, head_sum)
    kk = kk * lax.rsqrt(jnp.maximum(ss, 1e-24))
    bonus = _head_sums(r * k * rk_ref[...], head_sum) * v

    gen = _rwkv_parallel(r, k, v, -kk, kk * a, lw_ref[...], cst, T)
    done = 1
    while True:
        try:
            next(gen)
        except StopIteration as stop:
            gd, q, o0, decay = stop.value
            break
        if done < n:
            serial_step(done)
        elif done == n:
            serial_finish()
        done += 1
    assert done > n, "fewer parallel stages than chunks per block"

    for c in range(n):
        gd_ref[c] = gd[c]
        q_ref[c] = q[c]
        o0_ref[c] = o0[c]
        dec_ref[c] = decay[c]
    bonus_ref[...] = bonus


def _rwkv(proj, mu, lw, a, g, k_k, k_a, r_k, ln_w, ln_b, rows):
    L = proj.shape[0]
    nb = D_RWKV // LANES
    nblk = L // rows
    nsteps = PAIRS * nblk
    n = rows // RWKV_T

    def cur(s):
        b = jnp.minimum(s, nsteps - 1)
        return lax.div(b, nblk), lax.rem(b, nblk)

    def prv(s):
        b = jnp.maximum(s - 1, 0)
        return lax.div(b, nblk), lax.rem(b, nblk)

    def col(off):
        return pl.BlockSpec((rows, LANES), lambda s: (cur(s)[1], cur(s)[0] + off))

    def halo(off):
        return pl.BlockSpec((8, LANES),
                            lambda s: (jnp.maximum(cur(s)[1] * (rows // 8) - 1, 0), cur(s)[0] + off))

    def vec(off):
        return pl.BlockSpec((1, LANES), lambda s: (0, cur(s)[0] + off))

    def lag_col():
        return pl.BlockSpec((rows, LANES), lambda s: (prv(s)[1], prv(s)[0]))

    def lag_vec():
        return pl.BlockSpec((1, LANES), lambda s: (0, prv(s)[0]))

    return pl.pallas_call(
        functools.partial(_rwkv_kernel, nblk=nblk),
        grid=(nsteps + 1,),
        in_specs=[col(0), col(nb), col(2 * nb), halo(0), halo(nb), halo(2 * nb),
                  vec(0), vec(nb), vec(2 * nb),
                  col(0), col(0), vec(0), vec(0), vec(0),
                  lag_col(), lag_vec(), lag_vec()],
        out_specs=lag_col(),
        out_shape=jax.ShapeDtypeStruct((L, D_RWKV), BF16),
        scratch_shapes=[pltpu.VMEM((LANES, LANES), F32),
                        pltpu.VMEM((n, LANES, 2 * LANES), F32),
                        pltpu.VMEM((n, 2 * RWKV_T, LANES), F32),
                        pltpu.VMEM((n, 2 * RWKV_T, LANES), F32),
                        pltpu.VMEM((n, LANES, LANES), F32),
                        pltpu.VMEM((rows, LANES), F32)],
        compiler_params=_cparams(1),
        name="rwkv",
    )(proj, proj, proj, proj, proj, proj, mu, mu, mu, lw, a, k_k, k_a, r_k, g, ln_w, ln_b)


def _s5_table_kernel(lr_ref, li_ref, ldt_ref, pos_re, pos_im, neg_re, neg_im, z_re, z_im,
                     lbc_re, lbc_im):
    lr = lr_ref[...]
    li = li_ref[...]
    dt = jnp.exp(ldt_ref[...])
    centre = pos_re.shape[0] // 2
    t = (lax.broadcasted_iota(jnp.int32, pos_re.shape, 0) - centre).astype(F32)
    mag = jnp.exp(t * (lr * dt))
    ang = t * (li * dt)
    cs, sn = jnp.cos(ang), jnp.sin(ang)
    pos_re[...] = mag * cs
    pos_im[...] = mag * sn
    inv = 1.0 / mag
    neg_re[...] = inv * cs
    neg_im[...] = -(inv * sn)
    mc = jnp.exp((centre + 1.0) * (lr * dt))
    lbc_re[...] = mc * jnp.cos((centre + 1.0) * (li * dt))
    lbc_im[...] = mc * jnp.sin((centre + 1.0) * (li * dt))
    m1 = jnp.exp(lr * dt)
    lb_re = m1 * jnp.cos(li * dt)
    lb_im = m1 * jnp.sin(li * dt)
    den = lr * lr + li * li
    z_re[...] = ((lb_re - 1.0) * lr + lb_im * li) / den
    z_im[...] = (lb_im * lr - (lb_re - 1.0) * li) / den


def _s5_tables(lr, li, ldt, rows):
    n = lr.shape[1]
    full = pl.BlockSpec((1, n), lambda: (0, 0))
    tab = pl.BlockSpec((rows, n), lambda: (0, 0))
    return pl.pallas_call(
        _s5_table_kernel,
        in_specs=[full, full, full],
        out_specs=[tab, tab, tab, tab, full, full, full, full],
        out_shape=[jax.ShapeDtypeStruct((rows, n), F32)] * 4 + [jax.ShapeDtypeStruct((1, n), F32)] * 4,
        name="s5_tables",
    )(lr, li, ldt)


def _s5_kernel(u_ref, pos_re_ref, pos_im_ref, neg_re_ref, neg_im_ref, z_re_ref, z_im_ref,
               lbc_re_ref, lbc_im_ref, b_re_ref, b_im_ref, c_re_ref, c_im_ref, d_ref, wglu_ref,
               bglu_ref, o_ref, bb_ref, st_ref):
    R = u_ref.shape[0]
    T = pos_re_ref.shape[0]
    NS = S5_GROUPS * S5_STATE
    BS = NS // S5_BLOCKS
    BC = D_S5 // S5_BLOCKS

    @pl.when(pl.program_id(0) == 0)
    def _():
        st_ref[...] = jnp.zeros_like(st_ref)
        for j in range(S5_BLOCKS):
            zr = z_re_ref[:, j * BS:(j + 1) * BS]
            zi = z_im_ref[:, j * BS:(j + 1) * BS]
            br = b_re_ref[j]
            bi = b_im_ref[j]
            bb_ref[j, :, :BS] = (zr * br - zi * bi).astype(BF16)
            bb_ref[j, :, BS:] = (zr * bi + zi * br).astype(BF16)

    u = u_ref[...]
    ub = u.astype(BF16)
    ii = lax.broadcasted_iota(jnp.int32, (T, T), 0)
    jj = lax.broadcasted_iota(jnp.int32, (T, T), 1)
    tri = jnp.where(jj <= ii, 1.0, 0.0).astype(BF16)

    ys = []
    for j in range(S5_BLOCKS):
        sl = slice(j * BS, (j + 1) * BS)
        bu = jnp.dot(ub[:, j * BC:(j + 1) * BC], bb_ref[j], preferred_element_type=F32)
        nr, ni = neg_re_ref[:, sl], neg_im_ref[:, sl]
        pr, pi = pos_re_ref[:, sl], pos_im_ref[:, sl]
        lr_, li_ = lbc_re_ref[:, sl], lbc_im_ref[:, sl]
        s_re = st_ref[0:1, sl]
        s_im = st_ref[1:2, sl]
        xs_re, xs_im = [], []
        for t in range(R // T):
            bu_re, bu_im = bu[t * T:(t + 1) * T, :BS], bu[t * T:(t + 1) * T, BS:]
            zz = jnp.concatenate([nr * bu_re - ni * bu_im, nr * bu_im + ni * bu_re], axis=1)
            acc = jnp.dot(tri, zz.astype(BF16), preferred_element_type=F32)
            a_re = acc[:, :BS] + (lr_ * s_re - li_ * s_im)
            a_im = acc[:, BS:] + (lr_ * s_im + li_ * s_re)
            x_re = pr * a_re - pi * a_im
            x_im = pr * a_im + pi * a_re
            s_re, s_im = x_re[T - 1:T, :], x_im[T - 1:T, :]
            xs_re.append(x_re)
            xs_im.append(x_im)
        st_ref[0:1, sl] = s_re
        st_ref[1:2, sl] = s_im
        cat = lambda xs: jnp.concatenate(xs, axis=0) if len(xs) > 1 else xs[0]
        ys.append(_dot(cat(xs_re), c_re_ref[j]) - _dot(cat(xs_im), c_im_ref[j]))
    y = jnp.concatenate(ys, axis=1) + d_ref[...] * u
    y = 0.5 * y * (1.0 + jnp.tanh(math.sqrt(2.0 / math.pi) * (y + 0.044715 * (y * y * y))))
    zg = _dot(y, wglu_ref[...]) + bglu_ref[...]
    o_ref[...] = (zg[:, :D_S5] * _sigmoid(zg[:, D_S5:])).astype(o_ref.dtype)


def _s5(proj, tables, b_re, b_im, c_re, c_im, d, w_glu, b_glu, rows):
    L = proj.shape[0]
    NS = S5_GROUPS * S5_STATE
    BS = NS // S5_BLOCKS
    BC = D_S5 // S5_BLOCKS
    t = tables[0].shape[0]
    c2 = lambda shape: pl.BlockSpec(shape, lambda i: (0, 0))
    c3 = lambda shape: pl.BlockSpec(shape, lambda i: (0, 0, 0))
    return pl.pallas_call(
        _s5_kernel,
        grid=(L // rows,),
        in_specs=[pl.BlockSpec((rows, D_S5), lambda i: (i, 0)),
                  c2((t, NS)), c2((t, NS)), c2((t, NS)), c2((t, NS)),
                  c2((1, NS)), c2((1, NS)), c2((1, NS)), c2((1, NS)),
                  c3((S5_BLOCKS, BC, BS)), c3((S5_BLOCKS, BC, BS)),
                  c3((S5_BLOCKS, BS, BC)), c3((S5_BLOCKS, BS, BC)),
                  c2((1, D_S5)), c2((D_S5, 2 * D_S5)), c2((1, 2 * D_S5))],
        out_specs=pl.BlockSpec((rows, D_S5), lambda i: (i, 0)),
        out_shape=jax.ShapeDtypeStruct((L, D_S5), BF16),
        scratch_shapes=[pltpu.VMEM((S5_BLOCKS, BC, 2 * BS), BF16), pltpu.VMEM((8, NS), F32)],
        compiler_params=_cparams(1),
        name="s5",
    )(proj, *tables, b_re, b_im, c_re, c_im, d, w_glu, b_glu)


def _block_diag_groups(w):
    g, r, c = w.shape
    gb = g // S5_BLOCKS
    w = w.reshape(S5_BLOCKS, gb, r, c)
    eye = jnp.eye(gb, dtype=w.dtype)
    out = w[:, :, :, None, :] * eye[None, :, None, :, None]
    return out.reshape(S5_BLOCKS, gb * r, gb * c)


def kernel(x, c, w_ada, b_ada, norm_pre_mix, norm_post_mix, norm_pre_ffn, norm_post_ffn, w_in, rwkv_mu, rwkv_w0, rwkv_w2, rwkv_a0, rwkv_a2, rwkv_g2, rwkv_k_k, rwkv_k_a, rwkv_r_k, rwkv_ln_w, rwkv_ln_b, s5_lam_re, s5_lam_im, s5_log_dt, s5_b_re, s5_b_im, s5_c_re, s5_c_im, s5_d, s5_w_glu, s5_b_glu, w_up_rwkv, w_up_s5, w_out, ffn_w_gate, ffn_w_up, ffn_w_down):
    bsz, L, _ = x.shape
    assert bsz == 1 and w_ada.shape[0] == 1
    h = x.reshape(L, D_MODEL)
    tm = min(1024, L)
    tr = min(256, L)
    row = lambda v: v.reshape(1, -1)

    mod = _ada(c.reshape(D_MODEL, 1), w_ada[0], row(b_ada[0]))

    wit = jnp.swapaxes(w_in[0], 0, 1)
    gpad = LORA_G_PAD - LORA_G
    n_rkv = 3 * D_RWKV
    mu = row(rwkv_mu[0])
    mu_lora = jnp.concatenate([rwkv_mu[0, n_rkv:], jnp.zeros((gpad,), F32)]).reshape(1, -1)
    g2 = jnp.concatenate([rwkv_g2[0], jnp.zeros((gpad, D_RWKV), F32)], axis=0).astype(BF16)

    xm = _prenorm(h, row(norm_pre_mix[0]), mod, 0, 1, tr)
    proj_rkv = _matmul_wt(xm, wit, 0, n_rkv, F32, tm, 512, "proj_rkv")
    proj_l = _matmul_wt(xm, wit, n_rkv, LORA_W + LORA_A + LORA_G_PAD, F32, tm, 256, "proj_lora")
    proj_u = _matmul_wt(xm, wit, RWKV_COLS, D_S5, F32, tm, 512, "proj_u")
    gates = _matmul_wt(xm, wit, RWKV_COLS + D_S5, 2 * D_MODEL, BF16, tm, 512, "proj_g")

    lw, a, g = _lora(proj_l, mu_lora, row(rwkv_w0[0]), row(rwkv_a0[0]), rwkv_w2[0].astype(BF16),
                     rwkv_a2[0].astype(BF16), g2, tr)
    o_a = _rwkv(proj_rkv, mu, lw, a, g, row(rwkv_k_k[0]), row(rwkv_k_a[0]), row(rwkv_r_k[0]),
                row(rwkv_ln_w[0]), row(rwkv_ln_b[0]), min(RWKV_T * RWKV_CHUNKS_PER_STEP, L))

    rep = lambda v: jnp.repeat(v, S5_STATE).reshape(1, -1)
    tables = _s5_tables(row(s5_lam_re[0]), row(s5_lam_im[0]), rep(s5_log_dt[0]), S5_T)
    bt = lambda w: _block_diag_groups(jnp.swapaxes(w, 1, 2))
    o_b = _s5(proj_u, tables, bt(s5_b_re[0]), bt(s5_b_im[0]),
              _block_diag_groups(jnp.swapaxes(s5_c_re[0], 1, 2)).astype(BF16),
              _block_diag_groups(jnp.swapaxes(s5_c_im[0], 1, 2)).astype(BF16),
              row(s5_d[0]), s5_w_glu[0].astype(BF16), row(s5_b_glu[0]), min(S5_ROWS, L))

    merged = _merge(o_a, o_b, w_up_rwkv[0], w_up_s5[0], gates, tm, 512)
    mix = _matmul(merged, w_out[0], BF16, tm, 512, "w_out")

    h1, xf = _mid(h, mix, row(norm_post_mix[0]), row(norm_pre_ffn[0]), mod, tr)
    act = _ffn_up(xf, ffn_w_gate[0], ffn_w_up[0], tm, 256)
    ff = _matmul_ksplit(act, ffn_w_down[0].astype(BF16), BF16, tm, 1024, D_FF // 2, "ffn_down")
    out = _final(h1, ff, row(norm_post_ffn[0]), mod, tr)
    return out.reshape(bsz, L, D_MODEL)
```

```python
import functools
import math

import jax
import jax.numpy as jnp
from jax import lax
from jax.experimental import pallas as pl
from jax.experimental.pallas import tpu as pltpu

F32 = jnp.float32
BF16 = jnp.bfloat16

D_MODEL = 4096
RMS_EPS = 1e-6
D_RWKV = 2048
RWKV_HEAD = 64
LORA_W = 128
LORA_A = 128
LORA_G = 480
LORA_G_PAD = 512
GN_EPS = 64e-5
D_S5 = 1024
S5_GROUPS = 64
S5_GROUP_CH = 16
S5_STATE = 64
S5_BLOCKS = 4
D_FF = 11008

LANES = 128
PAIRS = D_RWKV // LANES
RWKV_T = 64
RWKV_CHUNKS_PER_STEP = 8
S5_T = 128
S5_ROWS = 256

RWKV_COLS = 3 * D_RWKV + LORA_W + LORA_A + LORA_G

VMEM_LIMIT = 56 * 1024 * 1024


def _cparams(n_axes, vmem=VMEM_LIMIT):
    return pltpu.CompilerParams(dimension_semantics=("arbitrary",) * n_axes, vmem_limit_bytes=vmem)


def _dot(a, b):
    return jnp.dot(a.astype(BF16), b.astype(BF16), preferred_element_type=F32)


def _dot_nt(a, b):
    return lax.dot_general(a.astype(BF16), b.astype(BF16), (((1,), (1,)), ((), ())),
                           preferred_element_type=F32)


def _dot_tn(a, b):
    return lax.dot_general(a.astype(BF16), b.astype(BF16), (((0,), (0,)), ((), ())),
                           preferred_element_type=F32)


def _split2(x):
    hi = x.astype(BF16)
    lo = (x - hi.astype(F32)).astype(BF16)
    return hi, lo


def _head_sums(x, sel2):
    half = x.shape[0] // 2
    hi, lo = _split2(jnp.concatenate([x[:half], x[half:]], axis=1))
    res = (jnp.dot(hi, sel2, preferred_element_type=F32) + jnp.dot(lo, sel2, preferred_element_type=F32))
    return jnp.concatenate([res[:, :LANES], res[:, LANES:]], axis=0)


def _dot_f32_rhs(sel, x):
    hi, lo = _split2(x)
    return (jnp.dot(sel, hi, preferred_element_type=F32) + jnp.dot(sel, lo, preferred_element_type=F32))


def _rms(x):
    return x * lax.rsqrt(jnp.mean(x * x, axis=-1, keepdims=True) + RMS_EPS)


def _sigmoid(x):
    return 1.0 / (1.0 + jnp.exp(-x))


def _token_shift(p, halo, mu, first):
    last = jnp.where(first, 0.0, halo[7:8, :])
    rolled = pltpu.roll(p, 1, axis=0)
    row = lax.broadcasted_iota(jnp.int32, p.shape, 0)
    prev = jnp.where(row == 0, last, rolled)
    return p + (prev - p) * mu


def _ada_kernel(c_ref, w_ref, b_ref, o_ref):
    c = c_ref[...]
    cs = c * _sigmoid(c)
    tn = o_ref.shape[1]
    rows = 512
    acc = jnp.zeros((8, tn), F32)
    for k0 in range(0, D_MODEL, rows):
        blk = w_ref[k0:k0 + rows, :] * cs[k0:k0 + rows, :]
        acc = acc + jnp.sum(blk.reshape(rows // 8, 8, tn), axis=0)
    o_ref[...] = jnp.sum(acc, axis=0, keepdims=True) + b_ref[...]


def _ada(c_col, w_ada, b_ada):
    n = w_ada.shape[1]
    tn = 512
    return pl.pallas_call(
        _ada_kernel,
        grid=(n // tn,),
        in_specs=[pl.BlockSpec((D_MODEL, 1), lambda j: (0, 0)),
                  pl.BlockSpec((D_MODEL, tn), lambda j: (0, j)),
                  pl.BlockSpec((1, tn), lambda j: (0, j))],
        out_specs=pl.BlockSpec((1, tn), lambda j: (0, j)),
        out_shape=jax.ShapeDtypeStruct((1, n), F32),
        compiler_params=_cparams(1),
        name="ada",
    )(c_col, w_ada, b_ada)


def _prenorm_kernel(x_ref, g_ref, sh_ref, sc_ref, o_ref):
    y = _rms(x_ref[...]) * g_ref[...]
    o_ref[...] = (y * (1.0 + sc_ref[...]) + sh_ref[...]).astype(o_ref.dtype)


def _prenorm(x, g, mod, sh_idx, sc_idx, tm):
    L = x.shape[0]
    return pl.pallas_call(
        _prenorm_kernel,
        grid=(L // tm,),
        in_specs=[pl.BlockSpec((tm, D_MODEL), lambda i: (i, 0)),
                  pl.BlockSpec((1, D_MODEL), lambda i: (0, 0)),
                  pl.BlockSpec((1, D_MODEL), lambda i: (0, sh_idx)),
                  pl.BlockSpec((1, D_MODEL), lambda i: (0, sc_idx))],
        out_specs=pl.BlockSpec((tm, D_MODEL), lambda i: (i, 0)),
        out_shape=jax.ShapeDtypeStruct((L, D_MODEL), BF16),
        compiler_params=_cparams(1),
        name="prenorm",
    )(x, g, mod, mod)


def _mid_kernel(x_ref, mix_ref, gpost_ref, gate_ref, gpre_ref, sh_ref, sc_ref, h_ref, xf_ref):
    h = x_ref[...] + gate_ref[...] * (_rms(mix_ref[...].astype(F32)) * gpost_ref[...])
    h_ref[...] = h
    y = _rms(h) * gpre_ref[...]
    xf_ref[...] = (y * (1.0 + sc_ref[...]) + sh_ref[...]).astype(xf_ref.dtype)


def _mid(x, mix, gpost, gpre, mod, tm):
    L = x.shape[0]
    row = lambda i: (i, 0)
    vec = lambda k: pl.BlockSpec((1, D_MODEL), lambda i: (0, k))
    return pl.pallas_call(
        _mid_kernel,
        grid=(L // tm,),
        in_specs=[pl.BlockSpec((tm, D_MODEL), row), pl.BlockSpec((tm, D_MODEL), row),
                  vec(0), vec(2), vec(0), vec(3), vec(4)],
        out_specs=[pl.BlockSpec((tm, D_MODEL), row), pl.BlockSpec((tm, D_MODEL), row)],
        out_shape=[jax.ShapeDtypeStruct((L, D_MODEL), F32), jax.ShapeDtypeStruct((L, D_MODEL), BF16)],
        compiler_params=_cparams(1),
        name="mid",
    )(x, mix, gpost, mod, gpre, mod, mod)


def _final_kernel(h_ref, ff_ref, gpost_ref, gate_ref, o_ref):
    o_ref[...] = h_ref[...] + gate_ref[...] * (_rms(ff_ref[...].astype(F32)) * gpost_ref[...])


def _final(h, ff, gpost, mod, tm):
    L = h.shape[0]
    row = lambda i: (i, 0)
    return pl.pallas_call(
        _final_kernel,
        grid=(L // tm,),
        in_specs=[pl.BlockSpec((tm, D_MODEL), row), pl.BlockSpec((tm, D_MODEL), row),
                  pl.BlockSpec((1, D_MODEL), lambda i: (0, 0)),
                  pl.BlockSpec((1, D_MODEL), lambda i: (0, 5))],
        out_specs=pl.BlockSpec((tm, D_MODEL), row),
        out_shape=jax.ShapeDtypeStruct((L, D_MODEL), F32),
        compiler_params=_cparams(1),
        name="final",
    )(h, ff, gpost, mod)


def _mm_kernel(a_ref, w_ref, o_ref):
    o_ref[...] = jnp.dot(a_ref[...], w_ref[...].astype(BF16),
                         preferred_element_type=F32).astype(o_ref.dtype)


def _matmul(a, w, out_dtype, tm, tn, name, n=None):
    m, k = a.shape
    n = w.shape[1] if n is None else n
    return pl.pallas_call(
        _mm_kernel,
        grid=(m // tm, n // tn),
        in_specs=[pl.BlockSpec((tm, k), lambda i, j: (i, 0)),
                  pl.BlockSpec((k, tn), lambda i, j: (0, j))],
        out_specs=pl.BlockSpec((tm, tn), lambda i, j: (i, j)),
        out_shape=jax.ShapeDtypeStruct((m, n), out_dtype),
        compiler_params=_cparams(2),
        name=name,
    )(a, w)


def _mm_acc_kernel(a_ref, w_ref, o_ref, acc_ref):
    kk = pl.program_id(2)
    last = pl.num_programs(2) - 1
    part = jnp.dot(a_ref[...], w_ref[...], preferred_element_type=F32)

    @pl.when(kk == 0)
    def _():
        acc_ref[...] = part

    @pl.when((kk > 0) & (kk < last))
    def _():
        acc_ref[...] = acc_ref[...] + part

    @pl.when(kk == last)
    def _():
        o_ref[...] = (acc_ref[...] + part).astype(o_ref.dtype)


def _dot_wt(a, wt):
    return lax.dot_general(a, wt.astype(BF16), (((1,), (1,)), ((), ())), preferred_element_type=F32)


def _mm_wt_kernel(a_ref, wt_ref, o_ref):
    o_ref[...] = _dot_wt(a_ref[...], wt_ref[...]).astype(o_ref.dtype)


def _matmul_wt(a, wt, row0, n, out_dtype, tm, tn, name):
    m, k = a.shape
    return pl.pallas_call(
        _mm_wt_kernel,
        grid=(m // tm, n // tn),
        in_specs=[pl.BlockSpec((tm, k), lambda i, j: (i, 0)),
                  pl.BlockSpec((pl.Element(tn), pl.Element(k)),
                               lambda i, j: (pl.multiple_of(row0 + j * tn, 8), 0))],
        out_specs=pl.BlockSpec((tm, tn), lambda i, j: (i, j)),
        out_shape=jax.ShapeDtypeStruct((m, n), out_dtype),
        compiler_params=_cparams(2),
        name=name,
    )(a, wt)


def _matmul_ksplit(a, w, out_dtype, tm, tn, tk, name):
    m, k = a.shape
    n = w.shape[1]
    assert k % tk == 0 and k // tk >= 2
    return pl.pallas_call(
        _mm_acc_kernel,
        grid=(m // tm, n // tn, k // tk),
        in_specs=[pl.BlockSpec((tm, tk), lambda i, j, kk: (i, kk)),
                  pl.BlockSpec((tk, tn), lambda i, j, kk: (kk, j))],
        out_specs=pl.BlockSpec((tm, tn), lambda i, j, kk: (i, j)),
        out_shape=jax.ShapeDtypeStruct((m, n), out_dtype),
        scratch_shapes=[pltpu.VMEM((tm, tn), F32)],
        compiler_params=_cparams(3),
        name=name,
    )(a, w)


def _merge_kernel(oa_ref, ob_ref, wa_ref, wb_ref, ga_ref, gb_ref, o_ref):
    ya = jnp.dot(oa_ref[...], wa_ref[...].astype(BF16), preferred_element_type=F32)
    yb = jnp.dot(ob_ref[...], wb_ref[...].astype(BF16), preferred_element_type=F32)
    m = _sigmoid(ga_ref[...].astype(F32)) * ya + _sigmoid(gb_ref[...].astype(F32)) * yb
    o_ref[...] = m.astype(o_ref.dtype)


def _merge(o_a, o_b, w_up_a, w_up_b, gates, tm, tn):
    L = o_a.shape[0]
    nb = D_MODEL // tn
    return pl.pallas_call(
        _merge_kernel,
        grid=(L // tm, nb),
        in_specs=[pl.BlockSpec((tm, D_RWKV), lambda i, j: (i, 0)),
                  pl.BlockSpec((tm, D_S5), lambda i, j: (i, 0)),
                  pl.BlockSpec((D_RWKV, tn), lambda i, j: (0, j)),
                  pl.BlockSpec((D_S5, tn), lambda i, j: (0, j)),
                  pl.BlockSpec((tm, tn), lambda i, j: (i, j)),
                  pl.BlockSpec((tm, tn), lambda i, j: (i, j + nb))],
        out_specs=pl.BlockSpec((tm, tn), lambda i, j: (i, j)),
        out_shape=jax.ShapeDtypeStruct((L, D_MODEL), BF16),
        compiler_params=_cparams(2),
        name="merge",
    )(o_a, o_b, w_up_a, w_up_b, gates, gates)


def _ffn_up_kernel(x_ref, wg_ref, wu_ref, o_ref):
    x = x_ref[...]
    a = jnp.dot(x, wg_ref[...].astype(BF16), preferred_element_type=F32)
    b = jnp.dot(x, wu_ref[...].astype(BF16), preferred_element_type=F32)
    o_ref[...] = ((a * _sigmoid(a)) * b).astype(o_ref.dtype)


def _ffn_up(xf, w_gate, w_up, tm, tn):
    L = xf.shape[0]
    n = w_gate.shape[1]
    return pl.pallas_call(
        _ffn_up_kernel,
        grid=(L // tm, n // tn),
        in_specs=[pl.BlockSpec((tm, D_MODEL), lambda i, j: (i, 0), pipeline_mode=pl.Buffered(1)),
                  pl.BlockSpec((D_MODEL, tn), lambda i, j: (0, j)),
                  pl.BlockSpec((D_MODEL, tn), lambda i, j: (0, j))],
        out_specs=pl.BlockSpec((tm, tn), lambda i, j: (i, j)),
        out_shape=jax.ShapeDtypeStruct((L, n), BF16),
        compiler_params=_cparams(2),
        name="ffn_up",
    )(xf, w_gate, w_up)


def _lora_kernel(p_ref, halo_ref, mu_ref, w0_ref, a0_ref, w2_ref, a2_ref, g2_ref,
                 lw_ref, a_ref, g_ref):
    ps = _token_shift(p_ref[...], halo_ref[...], mu_ref[...], pl.program_id(0) == 0)
    xw = ps[:, :LORA_W]
    xa = ps[:, LORA_W:LORA_W + LORA_A]
    xg = ps[:, LORA_W + LORA_A:]
    z = -(w0_ref[...] + _dot(jnp.tanh(xw), w2_ref[...]))
    softplus = jnp.maximum(z, 0.0) + jnp.log(1.0 + jnp.exp(-jnp.abs(z)))
    lw_ref[...] = -jnp.exp(-softplus - 0.5)
    a_ref[...] = _sigmoid(a0_ref[...] + _dot(xa, a2_ref[...]))
    g_ref[...] = _dot(_sigmoid(xg), g2_ref[...])


def _lora(proj, mu, w0, a0, w2, a2, g2, tm):
    L, n = proj.shape
    full = lambda shape: pl.BlockSpec(shape, lambda i: (0, 0))
    out = pl.BlockSpec((tm, D_RWKV), lambda i: (i, 0))
    return pl.pallas_call(
        _lora_kernel,
        grid=(L // tm,),
        in_specs=[pl.BlockSpec((tm, n), lambda i: (i, 0)),
                  pl.BlockSpec((8, n), lambda i: (jnp.maximum(i * (tm // 8) - 1, 0), 0)),
                  full((1, n)),
                  full((1, D_RWKV)), full((1, D_RWKV)),
                  full((LORA_W, D_RWKV)), full((LORA_A, D_RWKV)), full((LORA_G_PAD, D_RWKV))],
        out_specs=[out, out, out],
        out_shape=[jax.ShapeDtypeStruct((L, D_RWKV), F32)] * 3,
        compiler_params=_cparams(1),
        name="lora",
    )(proj, proj, mu, w0, a0, w2, a2, g2)


def _rwkv_parallel(r, k, v, am, bm, lw, cst, T):
    S = 2 * T
    n = r.shape[0] // T
    lane_lo, tri_incl, strict, incl2, eye = cst
    cr = range(n)

    def stack(x, c):
        xc = x[c * T:(c + 1) * T]
        return jnp.concatenate([jnp.where(lane_lo, xc, 0.0), jnp.where(lane_lo, 0.0, xc)], axis=0)

    lw_s = [stack(lw, c) for c in cr]
    c_all = _dot_f32_rhs(tri_incl, jnp.concatenate(lw_s, axis=1))
    yield
    cs = [c_all[:, c * LANES:(c + 1) * LANES] for c in cr]
    c_end = [cs[c][T - 1:T, :] + cs[c][S - 1:S, :] for c in cr]
    r_s = [stack(r, c) for c in cr]
    k_s = [stack(k, c) for c in cr]
    v_s = [stack(v, c) for c in cr]
    a_s = [stack(am, c) for c in cr]
    b_s = [stack(bm, c) for c in cr]
    e_neg = [jnp.exp(-cs[c]) for c in cr]
    e_end = [jnp.exp(c_end[c] - cs[c]) for c in cr]
    rh = [r_s[c] * jnp.exp(cs[c]) for c in cr]
    ah = [a_s[c] * jnp.exp(cs[c] - lw_s[c]) for c in cr]
    bh = [b_s[c] * e_neg[c] for c in cr]
    kh = [k_s[c] * e_neg[c] for c in cr]
    bc = [b_s[c] * e_end[c] for c in cr]
    kc = [k_s[c] * e_end[c] for c in cr]

    big = [_dot_nt(jnp.concatenate([ah[c], rh[c]], axis=0), jnp.concatenate([bh[c], kh[c]], axis=0))
           for c in cr]
    yield
    nj = [jnp.where(strict, big[c][:S, :S], 0.0) for c in cr]
    m_r = [jnp.where(incl2, big[c][S:, :], 0.0) for c in cr]
    av = [_dot(jnp.where(strict, big[c][:S, S:], 0.0), v_s[c]) for c in cr]
    yield

    steps = T.bit_length() - 1
    p = [jnp.where(eye, 1.0, 0.0) + nj[c] for c in cr]
    nj = [_dot(nj[c], nj[c]) for c in cr]
    yield
    for j in range(1, steps):
        if j + 1 < steps:
            res = [_dot(nj[c], jnp.concatenate([p[c], nj[c]], axis=1)) for c in cr]
            p = [p[c] + res[c][:, :S] for c in cr]
            nj = [res[c][:, S:] for c in cr]
        else:
            p = [p[c] + _dot(nj[c], p[c]) for c in cr]
        yield
    x = [_dot(p[c], jnp.concatenate([ah[c], av[c]], axis=1)) for c in cr]
    yield

    wv = [jnp.concatenate([x[c], jnp.concatenate([jnp.zeros_like(v_s[c]), v_s[c]], axis=1)], axis=0)
          for c in cr]
    y = [_dot(m_r[c], wv[c]) for c in cr]
    q = [rh[c] + y[c][:, :LANES] for c in cr]
    o0 = [y[c][:, LANES:] for c in cr]
    gd = [_dot_tn(jnp.concatenate([bc[c], kc[c]], axis=0), wv[c]) for c in cr]
    decay = [jnp.broadcast_to(jnp.sum(jnp.where(eye, jnp.exp(c_end[c]), 0.0), axis=1, keepdims=True),
                              (LANES, LANES)) for c in cr]
    return gd, q, o0, decay


def _rwkv_kernel(r_ref, k_ref, v_ref, rh_ref, kh_ref, vh_ref, mur_ref, muk_ref, muv_ref,
                 lw_ref, a_ref, kk_ref, ka_ref, rk_ref, g_ref, lnw_ref, lnb_ref,
                 o_ref, h_ref, gd_ref, q_ref, o0_ref, dec_ref, bonus_ref, *, nblk):
    s = pl.program_id(0)
    last = pl.num_programs(0) - 2
    first_cur = lax.rem(jnp.minimum(s, last), nblk) == 0
    first_prv = lax.rem(jnp.maximum(s - 1, 0), nblk) == 0

    @pl.when(s == 0)
    def _():
        h_ref[...] = jnp.zeros_like(h_ref)
        gd_ref[...] = jnp.zeros_like(gd_ref)
        q_ref[...] = jnp.zeros_like(q_ref)
        o0_ref[...] = jnp.zeros_like(o0_ref)
        dec_ref[...] = jnp.zeros_like(dec_ref)
        bonus_ref[...] = jnp.zeros_like(bonus_ref)

    rows = r_ref.shape[0]
    T = RWKV_T
    S = 2 * T
    n = rows // T
    ii = lax.broadcasted_iota(jnp.int32, (S, S), 0)
    jj = lax.broadcasted_iota(jnp.int32, (S, S), 1)
    same = (ii // T) == (jj // T)
    strict = same & (jj < ii)
    incl = same & (jj <= ii)
    eye = ii == jj
    tri_incl = jnp.where(incl, 1.0, 0.0).astype(BF16)
    ii2 = lax.broadcasted_iota(jnp.int32, (S, 2 * S), 0)
    jj2 = lax.broadcasted_iota(jnp.int32, (S, 2 * S), 1) % S
    incl2 = ((ii2 // T) == (jj2 // T)) & (jj2 <= ii2)
    hi_ = lax.broadcasted_iota(jnp.int32, (2 * LANES, 2 * LANES), 0) // RWKV_HEAD
    hj_ = lax.broadcasted_iota(jnp.int32, (2 * LANES, 2 * LANES), 1) // RWKV_HEAD
    head_sum = jnp.where(hi_ == hj_, 1.0, 0.0).astype(BF16)
    lane_lo = lax.broadcasted_iota(jnp.int32, (T, LANES), 1) < RWKV_HEAD
    cst = (lane_lo, tri_incl, strict, incl2, eye)

    state = [jnp.where(first_prv, 0.0, h_ref[...])]
    outs = []

    def serial_step(c):
        h = state[0]
        gd = gd_ref[c]
        z = _dot(jnp.concatenate([gd[:, :LANES], q_ref[c]], axis=0), h)
        o_s = z[LANES:] + o0_ref[c]
        outs.append(o_s[:T] + o_s[T:])
        state[0] = dec_ref[c] * h + z[:LANES] + gd[:, LANES:]

    def serial_finish():
        h_ref[...] = state[0]
        o = jnp.concatenate(outs, axis=0) if n > 1 else outs[0]
        inv = 1.0 / RWKV_HEAD
        mean = _head_sums(o, head_sum) * inv
        d = o - mean
        var = _head_sums(d * d, head_sum) * inv
        y = d * lax.rsqrt(var + GN_EPS) * lnw_ref[...] + lnb_ref[...]
        o_ref[...] = ((y + bonus_ref[...]) * g_ref[...]).astype(o_ref.dtype)

    r = _token_shift(r_ref[...], rh_ref[...], mur_ref[...], first_cur)
    k = _token_shift(k_ref[...], kh_ref[...], muk_ref[...], first_cur)
    v = _token_shift(v_ref[...], vh_ref[...], muv_ref[...], first_cur)
    a = a_ref[...]
    kk = k * kk_ref[...]
    k = k * (1.0 + (a - 1.0) * ka_ref[...])
    serial_step(0)
    ss = _head_sums(kk * kk, head_sum)
    kk = kk * lax.rsqrt(jnp.maximum(ss, 1e-24))
    bonus = _head_sums(r * k * rk_ref[...], head_sum) * v

    gen = _rwkv_parallel(r, k, v, -kk, kk * a, lw_ref[...], cst, T)
    done = 1
    while True:
        try:
            next(gen)
        except StopIteration as stop:
            gd, q, o0, decay = stop.value
            break
        if done < n:
            serial_step(done)
        elif done == n:
            serial_finish()
        done += 1
    assert done > n, "fewer parallel stages than chunks per block"

    for c in range(n):
        gd_ref[c] = gd[c]
        q_ref[c] = q[c]
        o0_ref[c] = o0[c]
        dec_ref[c] = decay[c]
    bonus_ref[...] = bonus


def _rwkv(proj, mu, lw, a, g, k_k, k_a, r_k, ln_w, ln_b, rows):
    L = proj.shape[0]
    nb = D_RWKV // LANES
    nblk = L // rows
    nsteps = PAIRS * nblk
    n = rows // RWKV_T

    def cur(s):
        b = jnp.minimum(s, nsteps - 1)
        return lax.div(b, nblk), lax.rem(b, nblk)

    def prv(s):
        b = jnp.maximum(s - 1, 0)
        return lax.div(b, nblk), lax.rem(b, nblk)

    def col(off):
        return pl.BlockSpec((rows, LANES), lambda s: (cur(s)[1], cur(s)[0] + off))

    def halo(off):
        return pl.BlockSpec((8, LANES),
                            lambda s: (jnp.maximum(cur(s)[1] * (rows // 8) - 1, 0), cur(s)[0] + off))

    def vec(off):
        return pl.BlockSpec((1, LANES), lambda s: (0, cur(s)[0] + off))

    def lag_col():
        return pl.BlockSpec((rows, LANES), lambda s: (prv(s)[1], prv(s)[0]))

    def lag_vec():
        return pl.BlockSpec((1, LANES), lambda s: (0, prv(s)[0]))

    return pl.pallas_call(
        functools.partial(_rwkv_kernel, nblk=nblk),
        grid=(nsteps + 1,),
        in_specs=[col(0), col(nb), col(2 * nb), halo(0), halo(nb), halo(2 * nb),
                  vec(0), vec(nb), vec(2 * nb),
                  col(0), col(0), vec(0), vec(0), vec(0),
                  lag_col(), lag_vec(), lag_vec()],
        out_specs=lag_col(),
        out_shape=jax.ShapeDtypeStruct((L, D_RWKV), BF16),
        scratch_shapes=[pltpu.VMEM((LANES, LANES), F32),
                        pltpu.VMEM((n, LANES, 2 * LANES), F32),
                        pltpu.VMEM((n, 2 * RWKV_T, LANES), F32),
                        pltpu.VMEM((n, 2 * RWKV_T, LANES), F32),
                        pltpu.VMEM((n, LANES, LANES), F32),
                        pltpu.VMEM((rows, LANES), F32)],
        compiler_params=_cparams(1),
        name="rwkv",
    )(proj, proj, proj, proj, proj, proj, mu, mu, mu, lw, a, k_k, k_a, r_k, g, ln_w, ln_b)


def _s5_table_kernel(lr_ref, li_ref, ldt_ref, pos_re, pos_im, neg_re, neg_im, z_re, z_im,
                     lbc_re, lbc_im):
    lr = lr_ref[...]
    li = li_ref[...]
    dt = jnp.exp(ldt_ref[...])
    centre = pos_re.shape[0] // 2
    t = (lax.broadcasted_iota(jnp.int32, pos_re.shape, 0) - centre).astype(F32)
    mag = jnp.exp(t * (lr * dt))
    ang = t * (li * dt)
    cs, sn = jnp.cos(ang), jnp.sin(ang)
    pos_re[...] = mag * cs
    pos_im[...] = mag * sn
    inv = 1.0 / mag
    neg_re[...] = inv * cs
    neg_im[...] = -(inv * sn)
    mc = jnp.exp((centre + 1.0) * (lr * dt))
    lbc_re[...] = mc * jnp.cos((centre + 1.0) * (li * dt))
    lbc_im[...] = mc * jnp.sin((centre + 1.0) * (li * dt))
    m1 = jnp.exp(lr * dt)
    lb_re = m1 * jnp.cos(li * dt)
    lb_im = m1 * jnp.sin(li * dt)
    den = lr * lr + li * li
    z_re[...] = ((lb_re - 1.0) * lr + lb_im * li) / den
    z_im[...] = (lb_im * lr - (lb_re - 1.0) * li) / den


def _s5_tables(lr, li, ldt, rows):
    n = lr.shape[1]
    full = pl.BlockSpec((1, n), lambda: (0, 0))
    tab = pl.BlockSpec((rows, n), lambda: (0, 0))
    return pl.pallas_call(
        _s5_table_kernel,
        in_specs=[full, full, full],
        out_specs=[tab, tab, tab, tab, full, full, full, full],
        out_shape=[jax.ShapeDtypeStruct((rows, n), F32)] * 4 + [jax.ShapeDtypeStruct((1, n), F32)] * 4,
        name="s5_tables",
    )(lr, li, ldt)


def _s5_kernel(u_ref, pos_re_ref, pos_im_ref, neg_re_ref, neg_im_ref, z_re_ref, z_im_ref,
               lbc_re_ref, lbc_im_ref, b_re_ref, b_im_ref, c_re_ref, c_im_ref, d_ref, wglu_ref,
               bglu_ref, o_ref, bb_ref, st_ref):
    R = u_ref.shape[0]
    T = pos_re_ref.shape[0]
    NS = S5_GROUPS * S5_STATE
    BS = NS // S5_BLOCKS
    BC = D_S5 // S5_BLOCKS

    @pl.when(pl.program_id(0) == 0)
    def _():
        st_ref[...] = jnp.zeros_like(st_ref)
        for j in range(S5_BLOCKS):
            zr = z_re_ref[:, j * BS:(j + 1) * BS]
            zi = z_im_ref[:, j * BS:(j + 1) * BS]
            br = b_re_ref[j]
            bi = b_im_ref[j]
            bb_ref[j, :, :BS] = (zr * br - zi * bi).astype(BF16)
            bb_ref[j, :, BS:] = (zr * bi + zi * br).astype(BF16)

    u = u_ref[...]
    ub = u.astype(BF16)
    ii = lax.broadcasted_iota(jnp.int32, (T, T), 0)
    jj = lax.broadcasted_iota(jnp.int32, (T, T), 1)
    tri = jnp.where(jj <= ii, 1.0, 0.0).astype(BF16)

    ys = []
    for j in range(S5_BLOCKS):
        sl = slice(j * BS, (j + 1) * BS)
        bu = jnp.dot(ub[:, j * BC:(j + 1) * BC], bb_ref[j], preferred_element_type=F32)
        nr, ni = neg_re_ref[:, sl], neg_im_ref[:, sl]
        pr, pi = pos_re_ref[:, sl], pos_im_ref[:, sl]
        lr_, li_ = lbc_re_ref[:, sl], lbc_im_ref[:, sl]
        s_re = st_ref[0:1, sl]
        s_im = st_ref[1:2, sl]
        xs_re, xs_im = [], []
        for t in range(R // T):
            bu_re, bu_im = bu[t * T:(t + 1) * T, :BS], bu[t * T:(t + 1) * T, BS:]
            zz = jnp.concatenate([nr * bu_re - ni * bu_im, nr * bu_im + ni * bu_re], axis=1)
            acc = jnp.dot(tri, zz.astype(BF16), preferred_element_type=F32)
            a_re = acc[:, :BS] + (lr_ * s_re - li_ * s_im)
            a_im = acc[:, BS:] + (lr_ * s_im + li_ * s_re)
            x_re = pr * a_re - pi * a_im
            x_im = pr * a_im + pi * a_re
            s_re, s_im = x_re[T - 1:T, :], x_im[T - 1:T, :]
            xs_re.append(x_re)
            xs_im.append(x_im)
        st_ref[0:1, sl] = s_re
        st_ref[1:2, sl] = s_im
        cat = lambda xs: jnp.concatenate(xs, axis=0) if len(xs) > 1 else xs[0]
        ys.append(_dot(cat(xs_re), c_re_ref[j]) - _dot(cat(xs_im), c_im_ref[j]))
    y = jnp.concatenate(ys, axis=1) + d_ref[...] * u
    y = 0.5 * y * (1.0 + jnp.tanh(math.sqrt(2.0 / math.pi) * (y + 0.044715 * (y * y * y))))
    zg = _dot(y, wglu_ref[...]) + bglu_ref[...]
    o_ref[...] = (zg[:, :D_S5] * _sigmoid(zg[:, D_S5:])).astype(o_ref.dtype)


def _s5(proj, tables, b_re, b_im, c_re, c_im, d, w_glu, b_glu, rows):
    L = proj.shape[0]
    NS = S5_GROUPS * S5_STATE
    BS = NS // S5_BLOCKS
    BC = D_S5 // S5_BLOCKS
    t = tables[0].shape[0]
    c2 = lambda shape: pl.BlockSpec(shape, lambda i: (0, 0))
    c3 = lambda shape: pl.BlockSpec(shape, lambda i: (0, 0, 0))
    return pl.pallas_call(
        _s5_kernel,
        grid=(L // rows,),
        in_specs=[pl.BlockSpec((rows, D_S5), lambda i: (i, 0)),
                  c2((t, NS)), c2((t, NS)), c2((t, NS)), c2((t, NS)),
                  c2((1, NS)), c2((1, NS)), c2((1, NS)), c2((1, NS)),
                  c3((S5_BLOCKS, BC, BS)), c3((S5_BLOCKS, BC, BS)),
                  c3((S5_BLOCKS, BS, BC)), c3((S5_BLOCKS, BS, BC)),
                  c2((1, D_S5)), c2((D_S5, 2 * D_S5)), c2((1, 2 * D_S5))],
        out_specs=pl.BlockSpec((rows, D_S5), lambda i: (i, 0)),
        out_shape=jax.ShapeDtypeStruct((L, D_S5), BF16),
        scratch_shapes=[pltpu.VMEM((S5_BLOCKS, BC, 2 * BS), BF16), pltpu.VMEM((8, NS), F32)],
        compiler_params=_cparams(1),
        name="s5",
    )(proj, *tables, b_re, b_im, c_re, c_im, d, w_glu, b_glu)


def _block_diag_groups(w):
    g, r, c = w.shape
    gb = g // S5_BLOCKS
    w = w.reshape(S5_BLOCKS, gb, r, c)
    eye = jnp.eye(gb, dtype=w.dtype)
    out = w[:, :, :, None, :] * eye[None, :, None, :, None]
    return out.reshape(S5_BLOCKS, gb * r, gb * c)


def kernel(x, c, w_ada, b_ada, norm_pre_mix, norm_post_mix, norm_pre_ffn, norm_post_ffn, w_in, rwkv_mu, rwkv_w0, rwkv_w2, rwkv_a0, rwkv_a2, rwkv_g2, rwkv_k_k, rwkv_k_a, rwkv_r_k, rwkv_ln_w, rwkv_ln_b, s5_lam_re, s5_lam_im, s5_log_dt, s5_b_re, s5_b_im, s5_c_re, s5_c_im, s5_d, s5_w_glu, s5_b_glu, w_up_rwkv, w_up_s5, w_out, ffn_w_gate, ffn_w_up, ffn_w_down):
    bsz, L, _ = x.shape
    assert bsz == 1 and w_ada.shape[0] == 1
    h = x.reshape(L, D_MODEL)
    tm = min(1024, L)
    tr = min(256, L)
    row = lambda v: v.reshape(1, -1)

    mod = _ada(c.reshape(D_MODEL, 1), w_ada[0], row(b_ada[0]))

    wit = jnp.swapaxes(w_in[0], 0, 1)
    gpad = LORA_G_PAD - LORA_G
    n_rkv = 3 * D_RWKV
    mu = row(rwkv_mu[0])
    mu_lora = jnp.concatenate([rwkv_mu[0, n_rkv:], jnp.zeros((gpad,), F32)]).reshape(1, -1)
    g2 = jnp.concatenate([rwkv_g2[0], jnp.zeros((gpad, D_RWKV), F32)], axis=0).astype(BF16)

    xm = _prenorm(h, row(norm_pre_mix[0]), mod, 0, 1, tr)
    proj_rkv = _matmul_wt(xm, wit, 0, n_rkv, F32, tm, 512, "proj_rkv")
    proj_l = _matmul_wt(xm, wit, n_rkv, LORA_W + LORA_A + LORA_G_PAD, F32, tm, 256, "proj_lora")
    proj_u = _matmul_wt(xm, wit, RWKV_COLS, D_S5, F32, tm, 512, "proj_u")
    gates = _matmul_wt(xm, wit, RWKV_COLS + D_S5, 2 * D_MODEL, BF16, tm, 512, "proj_g")

    lw, a, g = _lora(proj_l, mu_lora, row(rwkv_w0[0]), row(rwkv_a0[0]), rwkv_w2[0].astype(BF16),
                     rwkv_a2[0].astype(BF16), g2, tr)
    o_a = _rwkv(proj_rkv, mu, lw, a, g, row(rwkv_k_k[0]), row(rwkv_k_a[0]), row(rwkv_r_k[0]),
                row(rwkv_ln_w[0]), row(rwkv_ln_b[0]), min(RWKV_T * RWKV_CHUNKS_PER_STEP, L))

    rep = lambda v: jnp.repeat(v, S5_STATE).reshape(1, -1)
    tables = _s5_tables(row(s5_lam_re[0]), row(s5_lam_im[0]), rep(s5_log_dt[0]), S5_T)
    bt = lambda w: _block_diag_groups(jnp.swapaxes(w, 1, 2))
    o_b = _s5(proj_u, tables, bt(s5_b_re[0]), bt(s5_b_im[0]),
              _block_diag_groups(jnp.swapaxes(s5_c_re[0], 1, 2)).astype(BF16),
              _block_diag_groups(jnp.swapaxes(s5_c_im[0], 1, 2)).astype(BF16),
              row(s5_d[0]), s5_w_glu[0].astype(BF16), row(s5_b_glu[0]), min(S5_ROWS, L))

    merged = _merge(o_a, o_b, w_up_rwkv[0], w_up_s5[0], gates, tm, 512)
    mix = _matmul(merged, w_out[0], BF16, tm, 512, "w_out")

    h1, xf = _mid(h, mix, row(norm_post_mix[0]), row(norm_pre_ffn[0]), mod, tr)
    act = _ffn_up(xf, ffn_w_gate[0], ffn_w_up[0], min(2048, L), 256)
    ff = _matmul_ksplit(act, ffn_w_down[0].astype(BF16), BF16, tm, 1024, D_FF // 2, "ffn_down")
    out = _final(h1, ff, row(norm_post_ffn[0]), mod, tr)
    return out.reshape(bsz, L, D_MODEL)
```

```python
import functools
import math

import jax
import jax.numpy as jnp
from jax import lax
from jax.experimental import pallas as pl
from jax.experimental.pallas import tpu as pltpu

F32 = jnp.float32
BF16 = jnp.bfloat16

D_MODEL = 4096
RMS_EPS = 1e-6
D_RWKV = 2048
RWKV_HEAD = 64
LORA_W = 128
LORA_A = 128
LORA_G = 480
LORA_G_PAD = 512
GN_EPS = 64e-5
D_S5 = 1024
S5_GROUPS = 64
S5_GROUP_CH = 16
S5_STATE = 64
S5_BLOCKS = 4
D_FF = 11008

LANES = 128
PAIRS = D_RWKV // LANES
RWKV_T = 64
RWKV_CHUNKS_PER_STEP = 8
S5_T = 128
S5_ROWS = 256

RWKV_COLS = 3 * D_RWKV + LORA_W + LORA_A + LORA_G

VMEM_LIMIT = 56 * 1024 * 1024


def _cparams(n_axes, vmem=VMEM_LIMIT):
    return pltpu.CompilerParams(dimension_semantics=("arbitrary",) * n_axes, vmem_limit_bytes=vmem)


def _dot(a, b):
    return jnp.dot(a.astype(BF16), b.astype(BF16), preferred_element_type=F32)


def _dot_nt(a, b):
    return lax.dot_general(a.astype(BF16), b.astype(BF16), (((1,), (1,)), ((), ())),
                           preferred_element_type=F32)


def _dot_tn(a, b):
    return lax.dot_general(a.astype(BF16), b.astype(BF16), (((0,), (0,)), ((), ())),
                           preferred_element_type=F32)


def _split2(x):
    hi = x.astype(BF16)
    lo = (x - hi.astype(F32)).astype(BF16)
    return hi, lo


def _head_sums(x, sel2):
    half = x.shape[0] // 2
    hi, lo = _split2(jnp.concatenate([x[:half], x[half:]], axis=1))
    res = (jnp.dot(hi, sel2, preferred_element_type=F32) + jnp.dot(lo, sel2, preferred_element_type=F32))
    return jnp.concatenate([res[:, :LANES], res[:, LANES:]], axis=0)


def _dot_f32_rhs(sel, x):
    hi, lo = _split2(x)
    return (jnp.dot(sel, hi, preferred_element_type=F32) + jnp.dot(sel, lo, preferred_element_type=F32))


def _rms(x):
    return x * lax.rsqrt(jnp.mean(x * x, axis=-1, keepdims=True) + RMS_EPS)


def _sigmoid(x):
    return 1.0 / (1.0 + jnp.exp(-x))


def _token_shift(p, halo, mu, first):
    last = jnp.where(first, 0.0, halo[7:8, :])
    rolled = pltpu.roll(p, 1, axis=0)
    row = lax.broadcasted_iota(jnp.int32, p.shape, 0)
    prev = jnp.where(row == 0, last, rolled)
    return p + (prev - p) * mu


def _ada_kernel(c_ref, w_ref, b_ref, o_ref):
    c = c_ref[...]
    cs = c * _sigmoid(c)
    tn = o_ref.shape[1]
    rows = 512
    acc = jnp.zeros((8, tn), F32)
    for k0 in range(0, D_MODEL, rows):
        blk = w_ref[k0:k0 + rows, :] * cs[k0:k0 + rows, :]
        acc = acc + jnp.sum(blk.reshape(rows // 8, 8, tn), axis=0)
    o_ref[...] = jnp.sum(acc, axis=0, keepdims=True) + b_ref[...]


def _ada(c_col, w_ada, b_ada):
    n = w_ada.shape[1]
    tn = 512
    return pl.pallas_call(
        _ada_kernel,
        grid=(n // tn,),
        in_specs=[pl.BlockSpec((D_MODEL, 1), lambda j: (0, 0)),
                  pl.BlockSpec((D_MODEL, tn), lambda j: (0, j)),
                  pl.BlockSpec((1, tn), lambda j: (0, j))],
        out_specs=pl.BlockSpec((1, tn), lambda j: (0, j)),
        out_shape=jax.ShapeDtypeStruct((1, n), F32),
        compiler_params=_cparams(1),
        name="ada",
    )(c_col, w_ada, b_ada)


def _prenorm_kernel(x_ref, g_ref, sh_ref, sc_ref, o_ref):
    y = _rms(x_ref[...]) * g_ref[...]
    o_ref[...] = (y * (1.0 + sc_ref[...]) + sh_ref[...]).astype(o_ref.dtype)


def _prenorm(x, g, mod, sh_idx, sc_idx, tm):
    L = x.shape[0]
    return pl.pallas_call(
        _prenorm_kernel,
        grid=(L // tm,),
        in_specs=[pl.BlockSpec((tm, D_MODEL), lambda i: (i, 0)),
                  pl.BlockSpec((1, D_MODEL), lambda i: (0, 0)),
                  pl.BlockSpec((1, D_MODEL), lambda i: (0, sh_idx)),
                  pl.BlockSpec((1, D_MODEL), lambda i: (0, sc_idx))],
        out_specs=pl.BlockSpec((tm, D_MODEL), lambda i: (i, 0)),
        out_shape=jax.ShapeDtypeStruct((L, D_MODEL), BF16),
        compiler_params=_cparams(1),
        name="prenorm",
    )(x, g, mod, mod)


def _mid_kernel(x_ref, mix_ref, gpost_ref, gate_ref, gpre_ref, sh_ref, sc_ref, h_ref, xf_ref):
    h = x_ref[...] + gate_ref[...] * (_rms(mix_ref[...].astype(F32)) * gpost_ref[...])
    h_ref[...] = h
    y = _rms(h) * gpre_ref[...]
    xf_ref[...] = (y * (1.0 + sc_ref[...]) + sh_ref[...]).astype(xf_ref.dtype)


def _mid(x, mix, gpost, gpre, mod, tm):
    L = x.shape[0]
    row = lambda i: (i, 0)
    vec = lambda k: pl.BlockSpec((1, D_MODEL), lambda i: (0, k))
    return pl.pallas_call(
        _mid_kernel,
        grid=(L // tm,),
        in_specs=[pl.BlockSpec((tm, D_MODEL), row), pl.BlockSpec((tm, D_MODEL), row),
                  vec(0), vec(2), vec(0), vec(3), vec(4)],
        out_specs=[pl.BlockSpec((tm, D_MODEL), row), pl.BlockSpec((tm, D_MODEL), row)],
        out_shape=[jax.ShapeDtypeStruct((L, D_MODEL), F32), jax.ShapeDtypeStruct((L, D_MODEL), BF16)],
        compiler_params=_cparams(1),
        name="mid",
    )(x, mix, gpost, mod, gpre, mod, mod)


def _final_kernel(h_ref, ff_ref, gpost_ref, gate_ref, o_ref):
    o_ref[...] = h_ref[...] + gate_ref[...] * (_rms(ff_ref[...].astype(F32)) * gpost_ref[...])


def _final(h, ff, gpost, mod, tm):
    L = h.shape[0]
    row = lambda i: (i, 0)
    return pl.pallas_call(
        _final_kernel,
        grid=(L // tm,),
        in_specs=[pl.BlockSpec((tm, D_MODEL), row), pl.BlockSpec((tm, D_MODEL), row),
                  pl.BlockSpec((1, D_MODEL), lambda i: (0, 0)),
                  pl.BlockSpec((1, D_MODEL), lambda i: (0, 5))],
        out_specs=pl.BlockSpec((tm, D_MODEL), row),
        out_shape=jax.ShapeDtypeStruct((L, D_MODEL), F32),
        compiler_params=_cparams(1),
        name="final",
    )(h, ff, gpost, mod)


def _mm_kernel(a_ref, w_ref, o_ref):
    o_ref[...] = jnp.dot(a_ref[...], w_ref[...].astype(BF16),
                         preferred_element_type=F32).astype(o_ref.dtype)


def _matmul(a, w, out_dtype, tm, tn, name, n=None):
    m, k = a.shape
    n = w.shape[1] if n is None else n
    return pl.pallas_call(
        _mm_kernel,
        grid=(m // tm, n // tn),
        in_specs=[pl.BlockSpec((tm, k), lambda i, j: (i, 0)),
                  pl.BlockSpec((k, tn), lambda i, j: (0, j))],
        out_specs=pl.BlockSpec((tm, tn), lambda i, j: (i, j)),
        out_shape=jax.ShapeDtypeStruct((m, n), out_dtype),
        compiler_params=_cparams(2),
        name=name,
    )(a, w)


def _mm_acc_kernel(a_ref, w_ref, o_ref, acc_ref):
    kk = pl.program_id(2)
    last = pl.num_programs(2) - 1
    part = jnp.dot(a_ref[...], w_ref[...], preferred_element_type=F32)

    @pl.when(kk == 0)
    def _():
        acc_ref[...] = part

    @pl.when((kk > 0) & (kk < last))
    def _():
        acc_ref[...] = acc_ref[...] + part

    @pl.when(kk == last)
    def _():
        o_ref[...] = (acc_ref[...] + part).astype(o_ref.dtype)


def _dot_wt(a, wt):
    return lax.dot_general(a, wt.astype(BF16), (((1,), (1,)), ((), ())), preferred_element_type=F32)


def _mm_wt_kernel(a_ref, wt_ref, o_ref):
    o_ref[...] = _dot_wt(a_ref[...], wt_ref[...]).astype(o_ref.dtype)


def _matmul_wt(a, wt, row0, n, out_dtype, tm, tn, name):
    m, k = a.shape
    return pl.pallas_call(
        _mm_wt_kernel,
        grid=(m // tm, n // tn),
        in_specs=[pl.BlockSpec((tm, k), lambda i, j: (i, 0)),
                  pl.BlockSpec((pl.Element(tn), pl.Element(k)),
                               lambda i, j: (pl.multiple_of(row0 + j * tn, 8), 0))],
        out_specs=pl.BlockSpec((tm, tn), lambda i, j: (i, j)),
        out_shape=jax.ShapeDtypeStruct((m, n), out_dtype),
        compiler_params=_cparams(2),
        name=name,
    )(a, wt)


def _matmul_ksplit(a, w, out_dtype, tm, tn, tk, name):
    m, k = a.shape
    n = w.shape[1]
    assert k % tk == 0 and k // tk >= 2
    return pl.pallas_call(
        _mm_acc_kernel,
        grid=(m // tm, n // tn, k // tk),
        in_specs=[pl.BlockSpec((tm, tk), lambda i, j, kk: (i, kk)),
                  pl.BlockSpec((tk, tn), lambda i, j, kk: (kk, j))],
        out_specs=pl.BlockSpec((tm, tn), lambda i, j, kk: (i, j)),
        out_shape=jax.ShapeDtypeStruct((m, n), out_dtype),
        scratch_shapes=[pltpu.VMEM((tm, tn), F32)],
        compiler_params=_cparams(3),
        name=name,
    )(a, w)


def _merge_kernel(oa_ref, ob_ref, wa_ref, wb_ref, ga_ref, gb_ref, o_ref):
    ya = jnp.dot(oa_ref[...], wa_ref[...].astype(BF16), preferred_element_type=F32)
    yb = jnp.dot(ob_ref[...], wb_ref[...].astype(BF16), preferred_element_type=F32)
    m = _sigmoid(ga_ref[...].astype(F32)) * ya + _sigmoid(gb_ref[...].astype(F32)) * yb
    o_ref[...] = m.astype(o_ref.dtype)


def _merge(o_a, o_b, w_up_a, w_up_b, gates, tm, tn):
    L = o_a.shape[0]
    nb = D_MODEL // tn
    return pl.pallas_call(
        _merge_kernel,
        grid=(L // tm, nb),
        in_specs=[pl.BlockSpec((tm, D_RWKV), lambda i, j: (i, 0)),
                  pl.BlockSpec((tm, D_S5), lambda i, j: (i, 0)),
                  pl.BlockSpec((D_RWKV, tn), lambda i, j: (0, j)),
                  pl.BlockSpec((D_S5, tn), lambda i, j: (0, j)),
                  pl.BlockSpec((tm, tn), lambda i, j: (i, j)),
                  pl.BlockSpec((tm, tn), lambda i, j: (i, j + nb))],
        out_specs=pl.BlockSpec((tm, tn), lambda i, j: (i, j)),
        out_shape=jax.ShapeDtypeStruct((L, D_MODEL), BF16),
        compiler_params=_cparams(2),
        name="merge",
    )(o_a, o_b, w_up_a, w_up_b, gates, gates)


def _ffn_up_kernel(x_ref, wg_ref, wu_ref, wd_ref, o_ref, wd_bf_ref):
    x = x_ref[...]
    a = jnp.dot(x, wg_ref[...].astype(BF16), preferred_element_type=F32)
    b = jnp.dot(x, wu_ref[...].astype(BF16), preferred_element_type=F32)
    o_ref[...] = ((a * _sigmoid(a)) * b).astype(o_ref.dtype)
    wd_bf_ref[...] = wd_ref[...].astype(BF16)


def _ffn_up(xf, w_gate, w_up, w_down, tm, tn):
    L = xf.shape[0]
    n = w_gate.shape[1]
    ni, nj = L // tm, n // tn
    slab = w_down.shape[0] // (ni * nj)
    assert slab * ni * nj == w_down.shape[0] and slab % 16 == 0
    wd_spec = lambda: pl.BlockSpec((slab, D_MODEL), lambda i, j: (i * nj + j, 0))
    return pl.pallas_call(
        _ffn_up_kernel,
        grid=(ni, nj),
        in_specs=[pl.BlockSpec((tm, D_MODEL), lambda i, j: (i, 0), pipeline_mode=pl.Buffered(1)),
                  pl.BlockSpec((D_MODEL, tn), lambda i, j: (0, j)),
                  pl.BlockSpec((D_MODEL, tn), lambda i, j: (0, j)),
                  wd_spec()],
        out_specs=[pl.BlockSpec((tm, tn), lambda i, j: (i, j)), wd_spec()],
        out_shape=[jax.ShapeDtypeStruct((L, n), BF16), jax.ShapeDtypeStruct(w_down.shape, BF16)],
        compiler_params=_cparams(2),
        name="ffn_up",
    )(xf, w_gate, w_up, w_down)


def _lora_kernel(p_ref, halo_ref, mu_ref, w0_ref, a0_ref, w2_ref, a2_ref, g2_ref,
                 lw_ref, a_ref, g_ref):
    ps = _token_shift(p_ref[...], halo_ref[...], mu_ref[...], pl.program_id(0) == 0)
    xw = ps[:, :LORA_W]
    xa = ps[:, LORA_W:LORA_W + LORA_A]
    xg = ps[:, LORA_W + LORA_A:]
    z = -(w0_ref[...] + _dot(jnp.tanh(xw), w2_ref[...]))
    softplus = jnp.maximum(z, 0.0) + jnp.log(1.0 + jnp.exp(-jnp.abs(z)))
    lw_ref[...] = -jnp.exp(-softplus - 0.5)
    a_ref[...] = _sigmoid(a0_ref[...] + _dot(xa, a2_ref[...]))
    g_ref[...] = _dot(_sigmoid(xg), g2_ref[...])


def _lora(proj, mu, w0, a0, w2, a2, g2, tm):
    L, n = proj.shape
    full = lambda shape: pl.BlockSpec(shape, lambda i: (0, 0))
    out = pl.BlockSpec((tm, D_RWKV), lambda i: (i, 0))
    return pl.pallas_call(
        _lora_kernel,
        grid=(L // tm,),
        in_specs=[pl.BlockSpec((tm, n), lambda i: (i, 0)),
                  pl.BlockSpec((8, n), lambda i: (jnp.maximum(i * (tm // 8) - 1, 0), 0)),
                  full((1, n)),
                  full((1, D_RWKV)), full((1, D_RWKV)),
                  full((LORA_W, D_RWKV)), full((LORA_A, D_RWKV)), full((LORA_G_PAD, D_RWKV))],
        out_specs=[out, out, out],
        out_shape=[jax.ShapeDtypeStruct((L, D_RWKV), F32)] * 3,
        compiler_params=_cparams(1),
        name="lora",
    )(proj, proj, mu, w0, a0, w2, a2, g2)


def _rwkv_parallel(r, k, v, am, bm, lw, cst, T):
    S = 2 * T
    n = r.shape[0] // T
    lane_lo, tri_t, strict, incl2, eye, same_head = cst
    cr = range(n)

    def chunk(x, c):
        return x[c * T:(c + 1) * T]

    def stack_own(xc):
        return jnp.concatenate([jnp.where(lane_lo, xc, 0.0), jnp.where(lane_lo, 0.0, xc)], axis=0)

    def stack_dup(xc):
        return jnp.concatenate([xc, xc], axis=0)

    c_all = _dot_f32_rhs(tri_t, jnp.concatenate([chunk(lw, c) for c in cr], axis=1))
    yield
    cu = [c_all[:, c * LANES:(c + 1) * LANES] for c in cr]
    c_end = [cu[c][T - 1:T, :] for c in cr]
    e_neg = [jnp.exp(-cu[c]) for c in cr]
    e_end = [jnp.exp(c_end[c] - cu[c]) for c in cr]
    rh = [stack_own(chunk(r, c) * jnp.exp(cu[c])) for c in cr]
    ah = [stack_own(chunk(am, c) * jnp.exp(cu[c] - chunk(lw, c))) for c in cr]
    bh = [stack_dup(chunk(bm, c) * e_neg[c]) for c in cr]
    kh = [stack_dup(chunk(k, c) * e_neg[c]) for c in cr]
    bc = [stack_own(chunk(bm, c) * e_end[c]) for c in cr]
    kc = [stack_own(chunk(k, c) * e_end[c]) for c in cr]
    v_s = [stack_dup(chunk(v, c)) for c in cr]

    big = [_dot_nt(jnp.concatenate([ah[c], rh[c]], axis=0), jnp.concatenate([bh[c], kh[c]], axis=0))
           for c in cr]
    yield
    nj = [jnp.where(strict, big[c][:S, :S], 0.0) for c in cr]
    m_r = [jnp.where(incl2, big[c][S:, :], 0.0) for c in cr]
    av = [_dot(jnp.where(strict, big[c][:S, S:], 0.0), v_s[c]) for c in cr]
    yield

    steps = T.bit_length() - 1
    p = [jnp.where(eye, 1.0, 0.0) + nj[c] for c in cr]
    nj = [_dot(nj[c], nj[c]) for c in cr]
    yield
    for j in range(1, steps):
        if j + 1 < steps:
            res = [_dot(nj[c], jnp.concatenate([p[c], nj[c]], axis=1)) for c in cr]
            p = [p[c] + res[c][:, :S] for c in cr]
            nj = [res[c][:, S:] for c in cr]
        else:
            p = [p[c] + _dot(nj[c], p[c]) for c in cr]
        yield
    x = [_dot(p[c], jnp.concatenate([ah[c], av[c]], axis=1)) for c in cr]
    yield

    wv = [jnp.concatenate([x[c], jnp.concatenate([jnp.zeros_like(v_s[c]), v_s[c]], axis=1)], axis=0)
          for c in cr]
    y = [_dot(m_r[c], wv[c]) for c in cr]
    q = [rh[c] + y[c][:, :LANES] for c in cr]
    o0 = [y[c][:, LANES:] for c in cr]
    gd = [_dot_tn(jnp.concatenate([bc[c], kc[c]], axis=0), wv[c]) for c in cr]
    gd = [jnp.concatenate([gd[c][:, :LANES], jnp.where(same_head, gd[c][:, LANES:], 0.0)], axis=1)
          for c in cr]
    decay = [jnp.broadcast_to(jnp.sum(jnp.where(eye, jnp.exp(c_end[c]), 0.0), axis=1, keepdims=True),
                              (LANES, LANES)) for c in cr]
    return gd, q, o0, decay


def _rwkv_kernel(r_ref, k_ref, v_ref, rh_ref, kh_ref, vh_ref, mur_ref, muk_ref, muv_ref,
                 lw_ref, a_ref, kk_ref, ka_ref, rk_ref, g_ref, lnw_ref, lnb_ref,
                 o_ref, h_ref, gd_ref, q_ref, o0_ref, dec_ref, bonus_ref, *, nblk):
    s = pl.program_id(0)
    last = pl.num_programs(0) - 2
    first_cur = lax.rem(jnp.minimum(s, last), nblk) == 0
    first_prv = lax.rem(jnp.maximum(s - 1, 0), nblk) == 0

    @pl.when(s == 0)
    def _():
        h_ref[...] = jnp.zeros_like(h_ref)
        gd_ref[...] = jnp.zeros_like(gd_ref)
        q_ref[...] = jnp.zeros_like(q_ref)
        o0_ref[...] = jnp.zeros_like(o0_ref)
        dec_ref[...] = jnp.zeros_like(dec_ref)
        bonus_ref[...] = jnp.zeros_like(bonus_ref)

    rows = r_ref.shape[0]
    T = RWKV_T
    S = 2 * T
    n = rows // T
    ii = lax.broadcasted_iota(jnp.int32, (S, S), 0)
    jj = lax.broadcasted_iota(jnp.int32, (S, S), 1)
    same = (ii // T) == (jj // T)
    strict = same & (jj < ii)
    incl = same & (jj <= ii)
    eye = ii == jj
    tri_incl = jnp.where(incl, 1.0, 0.0).astype(BF16)
    ii2 = lax.broadcasted_iota(jnp.int32, (S, 2 * S), 0)
    jj2 = lax.broadcasted_iota(jnp.int32, (S, 2 * S), 1) % S
    incl2 = ((ii2 // T) == (jj2 // T)) & (jj2 <= ii2)
    hi_ = lax.broadcasted_iota(jnp.int32, (2 * LANES, 2 * LANES), 0) // RWKV_HEAD
    hj_ = lax.broadcasted_iota(jnp.int32, (2 * LANES, 2 * LANES), 1) // RWKV_HEAD
    head_sum = jnp.where(hi_ == hj_, 1.0, 0.0).astype(BF16)
    lane_lo = lax.broadcasted_iota(jnp.int32, (T, LANES), 1) < RWKV_HEAD
    it = lax.broadcasted_iota(jnp.int32, (T, T), 0)
    jt = lax.broadcasted_iota(jnp.int32, (T, T), 1)
    tri_t = jnp.where(jt <= it, 1.0, 0.0).astype(BF16)
    same_head = ((lax.broadcasted_iota(jnp.int32, (LANES, LANES), 0) // RWKV_HEAD)
                 == (lax.broadcasted_iota(jnp.int32, (LANES, LANES), 1) // RWKV_HEAD))
    cst = (lane_lo, tri_t, strict, incl2, eye, same_head)

    state = [jnp.where(first_prv, 0.0, h_ref[...])]
    outs = []

    def serial_step(c):
        h = state[0]
        gd = gd_ref[c]
        z = _dot(jnp.concatenate([gd[:, :LANES], q_ref[c]], axis=0), h)
        o_s = z[LANES:] + o0_ref[c]
        outs.append(jnp.where(lane_lo, o_s[:T], o_s[T:]))
        state[0] = dec_ref[c] * h + z[:LANES] + gd[:, LANES:]

    def serial_finish():
        h_ref[...] = state[0]
        o = jnp.concatenate(outs, axis=0) if n > 1 else outs[0]
        inv = 1.0 / RWKV_HEAD
        mean = _head_sums(o, head_sum) * inv
        d = o - mean
        var = _head_sums(d * d, head_sum) * inv
        y = d * lax.rsqrt(var + GN_EPS) * lnw_ref[...] + lnb_ref[...]
        o_ref[...] = ((y + bonus_ref[...]) * g_ref[...]).astype(o_ref.dtype)

    r = _token_shift(r_ref[...], rh_ref[...], mur_ref[...], first_cur)
    k = _token_shift(k_ref[...], kh_ref[...], muk_ref[...], first_cur)
    v = _token_shift(v_ref[...], vh_ref[...], muv_ref[...], first_cur)
    a = a_ref[...]
    kk = k * kk_ref[...]
    k = k * (1.0 + (a - 1.0) * ka_ref[...])
    serial_step(0)
    ss = _head_sums(kk * kk, head_sum)
    kk = kk * lax.rsqrt(jnp.maximum(ss, 1e-24))
    bonus = _head_sums(r * k * rk_ref[...], head_sum) * v

    gen = _rwkv_parallel(r, k, v, -kk, kk * a, lw_ref[...], cst, T)
    done = 1
    while True:
        try:
            next(gen)
        except StopIteration as stop:
            gd, q, o0, decay = stop.value
            break
        if done < n:
            serial_step(done)
        elif done == n:
            serial_finish()
        done += 1
    assert done > n, "fewer parallel stages than chunks per block"

    for c in range(n):
        gd_ref[c] = gd[c]
        q_ref[c] = q[c]
        o0_ref[c] = o0[c]
        dec_ref[c] = decay[c]
    bonus_ref[...] = bonus


def _rwkv(proj, mu, lw, a, g, k_k, k_a, r_k, ln_w, ln_b, rows):
    L = proj.shape[0]
    nb = D_RWKV // LANES
    nblk = L // rows
    nsteps = PAIRS * nblk
    n = rows // RWKV_T

    def cur(s):
        b = jnp.minimum(s, nsteps - 1)
        return lax.div(b, nblk), lax.rem(b, nblk)

    def prv(s):
        b = jnp.maximum(s - 1, 0)
        return lax.div(b, nblk), lax.rem(b, nblk)

    def col(off):
        return pl.BlockSpec((rows, LANES), lambda s: (cur(s)[1], cur(s)[0] + off))

    def halo(off):
        return pl.BlockSpec((8, LANES),
                            lambda s: (jnp.maximum(cur(s)[1] * (rows // 8) - 1, 0), cur(s)[0] + off))

    def vec(off):
        return pl.BlockSpec((1, LANES), lambda s: (0, cur(s)[0] + off))

    def lag_col():
        return pl.BlockSpec((rows, LANES), lambda s: (prv(s)[1], prv(s)[0]))

    def lag_vec():
        return pl.BlockSpec((1, LANES), lambda s: (0, prv(s)[0]))

    return pl.pallas_call(
        functools.partial(_rwkv_kernel, nblk=nblk),
        grid=(nsteps + 1,),
        in_specs=[col(0), col(nb), col(2 * nb), halo(0), halo(nb), halo(2 * nb),
                  vec(0), vec(nb), vec(2 * nb),
                  col(0), col(0), vec(0), vec(0), vec(0),
                  lag_col(), lag_vec(), lag_vec()],
        out_specs=lag_col(),
        out_shape=jax.ShapeDtypeStruct((L, D_RWKV), BF16),
        scratch_shapes=[pltpu.VMEM((LANES, LANES), F32),
                        pltpu.VMEM((n, LANES, 2 * LANES), F32),
                        pltpu.VMEM((n, 2 * RWKV_T, LANES), F32),
                        pltpu.VMEM((n, 2 * RWKV_T, LANES), F32),
                        pltpu.VMEM((n, LANES, LANES), F32),
                        pltpu.VMEM((rows, LANES), F32)],
        compiler_params=_cparams(1),
        name="rwkv",
    )(proj, proj, proj, proj, proj, proj, mu, mu, mu, lw, a, k_k, k_a, r_k, g, ln_w, ln_b)


def _s5_table_kernel(lr_ref, li_ref, ldt_ref, pos_re, pos_im, neg_re, neg_im, z_re, z_im,
                     lbc_re, lbc_im):
    lr = lr_ref[...]
    li = li_ref[...]
    dt = jnp.exp(ldt_ref[...])
    centre = pos_re.shape[0] // 2
    t = (lax.broadcasted_iota(jnp.int32, pos_re.shape, 0) - centre).astype(F32)
    mag = jnp.exp(t * (lr * dt))
    ang = t * (li * dt)
    cs, sn = jnp.cos(ang), jnp.sin(ang)
    pos_re[...] = mag * cs
    pos_im[...] = mag * sn
    inv = 1.0 / mag
    neg_re[...] = inv * cs
    neg_im[...] = -(inv * sn)
    mc = jnp.exp((centre + 1.0) * (lr * dt))
    lbc_re[...] = mc * jnp.cos((centre + 1.0) * (li * dt))
    lbc_im[...] = mc * jnp.sin((centre + 1.0) * (li * dt))
    m1 = jnp.exp(lr * dt)
    lb_re = m1 * jnp.cos(li * dt)
    lb_im = m1 * jnp.sin(li * dt)
    den = lr * lr + li * li
    z_re[...] = ((lb_re - 1.0) * lr + lb_im * li) / den
    z_im[...] = (lb_im * lr - (lb_re - 1.0) * li) / den


def _s5_tables(lr, li, ldt, rows):
    n = lr.shape[1]
    full = pl.BlockSpec((1, n), lambda: (0, 0))
    tab = pl.BlockSpec((rows, n), lambda: (0, 0))
    return pl.pallas_call(
        _s5_table_kernel,
        in_specs=[full, full, full],
        out_specs=[tab, tab, tab, tab, full, full, full, full],
        out_shape=[jax.ShapeDtypeStruct((rows, n), F32)] * 4 + [jax.ShapeDtypeStruct((1, n), F32)] * 4,
        name="s5_tables",
    )(lr, li, ldt)


def _s5_kernel(u_ref, pos_re_ref, pos_im_ref, neg_re_ref, neg_im_ref, z_re_ref, z_im_ref,
               lbc_re_ref, lbc_im_ref, b_re_ref, b_im_ref, c_re_ref, c_im_ref, d_ref, wglu_ref,
               bglu_ref, o_ref, bb_ref, st_ref):
    R = u_ref.shape[0]
    T = pos_re_ref.shape[0]
    NS = S5_GROUPS * S5_STATE
    BS = NS // S5_BLOCKS
    BC = D_S5 // S5_BLOCKS

    @pl.when(pl.program_id(0) == 0)
    def _():
        st_ref[...] = jnp.zeros_like(st_ref)
        for j in range(S5_BLOCKS):
            zr = z_re_ref[:, j * BS:(j + 1) * BS]
            zi = z_im_ref[:, j * BS:(j + 1) * BS]
            br = b_re_ref[j]
            bi = b_im_ref[j]
            bb_ref[j, :, :BS] = (zr * br - zi * bi).astype(BF16)
            bb_ref[j, :, BS:] = (zr * bi + zi * br).astype(BF16)

    u = u_ref[...]
    ub = u.astype(BF16)
    ii = lax.broadcasted_iota(jnp.int32, (T, T), 0)
    jj = lax.broadcasted_iota(jnp.int32, (T, T), 1)
    tri = jnp.where(jj <= ii, 1.0, 0.0).astype(BF16)

    ys = []
    for j in range(S5_BLOCKS):
        sl = slice(j * BS, (j + 1) * BS)
        bu = jnp.dot(ub[:, j * BC:(j + 1) * BC], bb_ref[j], preferred_element_type=F32)
        nr, ni = neg_re_ref[:, sl], neg_im_ref[:, sl]
        pr, pi = pos_re_ref[:, sl], pos_im_ref[:, sl]
        lr_, li_ = lbc_re_ref[:, sl], lbc_im_ref[:, sl]
        s_re = st_ref[0:1, sl]
        s_im = st_ref[1:2, sl]
        xs_re, xs_im = [], []
        for t in range(R // T):
            bu_re, bu_im = bu[t * T:(t + 1) * T, :BS], bu[t * T:(t + 1) * T, BS:]
            zz = jnp.concatenate([nr * bu_re - ni * bu_im, nr * bu_im + ni * bu_re], axis=1)
            acc = jnp.dot(tri, zz.astype(BF16), preferred_element_type=F32)
            a_re = acc[:, :BS] + (lr_ * s_re - li_ * s_im)
            a_im = acc[:, BS:] + (lr_ * s_im + li_ * s_re)
            x_re = pr * a_re - pi * a_im
            x_im = pr * a_im + pi * a_re
            s_re, s_im = x_re[T - 1:T, :], x_im[T - 1:T, :]
            xs_re.append(x_re)
            xs_im.append(x_im)
        st_ref[0:1, sl] = s_re
        st_ref[1:2, sl] = s_im
        cat = lambda xs: jnp.concatenate(xs, axis=0) if len(xs) > 1 else xs[0]
        ys.append(_dot(cat(xs_re), c_re_ref[j]) - _dot(cat(xs_im), c_im_ref[j]))
    y = jnp.concatenate(ys, axis=1) + d_ref[...] * u
    y = 0.5 * y * (1.0 + jnp.tanh(math.sqrt(2.0 / math.pi) * (y + 0.044715 * (y * y * y))))
    zg = _dot(y, wglu_ref[...]) + bglu_ref[...]
    o_ref[...] = (zg[:, :D_S5] * _sigmoid(zg[:, D_S5:])).astype(o_ref.dtype)


def _s5(proj, tables, b_re, b_im, c_re, c_im, d, w_glu, b_glu, rows):
    L = proj.shape[0]
    NS = S5_GROUPS * S5_STATE
    BS = NS // S5_BLOCKS
    BC = D_S5 // S5_BLOCKS
    t = tables[0].shape[0]
    c2 = lambda shape: pl.BlockSpec(shape, lambda i: (0, 0))
    c3 = lambda shape: pl.BlockSpec(shape, lambda i: (0, 0, 0))
    return pl.pallas_call(
        _s5_kernel,
        grid=(L // rows,),
        in_specs=[pl.BlockSpec((rows, D_S5), lambda i: (i, 0)),
                  c2((t, NS)), c2((t, NS)), c2((t, NS)), c2((t, NS)),
                  c2((1, NS)), c2((1, NS)), c2((1, NS)), c2((1, NS)),
                  c3((S5_BLOCKS, BC, BS)), c3((S5_BLOCKS, BC, BS)),
                  c3((S5_BLOCKS, BS, BC)), c3((S5_BLOCKS, BS, BC)),
                  c2((1, D_S5)), c2((D_S5, 2 * D_S5)), c2((1, 2 * D_S5))],
        out_specs=pl.BlockSpec((rows, D_S5), lambda i: (i, 0)),
        out_shape=jax.ShapeDtypeStruct((L, D_S5), BF16),
        scratch_shapes=[pltpu.VMEM((S5_BLOCKS, BC, 2 * BS), BF16), pltpu.VMEM((8, NS), F32)],
        compiler_params=_cparams(1),
        name="s5",
    )(proj, *tables, b_re, b_im, c_re, c_im, d, w_glu, b_glu)


def _block_diag_groups(w):
    g, r, c = w.shape
    gb = g // S5_BLOCKS
    w = w.reshape(S5_BLOCKS, gb, r, c)
    eye = jnp.eye(gb, dtype=w.dtype)
    out = w[:, :, :, None, :] * eye[None, :, None, :, None]
    return out.reshape(S5_BLOCKS, gb * r, gb * c)


def kernel(x, c, w_ada, b_ada, norm_pre_mix, norm_post_mix, norm_pre_ffn, norm_post_ffn, w_in, rwkv_mu, rwkv_w0, rwkv_w2, rwkv_a0, rwkv_a2, rwkv_g2, rwkv_k_k, rwkv_k_a, rwkv_r_k, rwkv_ln_w, rwkv_ln_b, s5_lam_re, s5_lam_im, s5_log_dt, s5_b_re, s5_b_im, s5_c_re, s5_c_im, s5_d, s5_w_glu, s5_b_glu, w_up_rwkv, w_up_s5, w_out, ffn_w_gate, ffn_w_up, ffn_w_down):
    bsz, L, _ = x.shape
    assert bsz == 1 and w_ada.shape[0] == 1
    h = x.reshape(L, D_MODEL)
    tm = min(1024, L)
    tr = min(256, L)
    row = lambda v: v.reshape(1, -1)

    mod = _ada(c.reshape(D_MODEL, 1), w_ada[0], row(b_ada[0]))

    wit = jnp.swapaxes(w_in[0], 0, 1)
    gpad = LORA_G_PAD - LORA_G
    n_rkv = 3 * D_RWKV
    mu = row(rwkv_mu[0])
    mu_lora = jnp.concatenate([rwkv_mu[0, n_rkv:], jnp.zeros((gpad,), F32)]).reshape(1, -1)
    g2 = jnp.concatenate([rwkv_g2[0], jnp.zeros((gpad, D_RWKV), F32)], axis=0).astype(BF16)

    xm = _prenorm(h, row(norm_pre_mix[0]), mod, 0, 1, tr)
    proj_rkv = _matmul_wt(xm, wit, 0, n_rkv, F32, tm, 512, "proj_rkv")
    proj_l = _matmul_wt(xm, wit, n_rkv, LORA_W + LORA_A + LORA_G_PAD, F32, tm, 256, "proj_lora")
    proj_u = _matmul_wt(xm, wit, RWKV_COLS, D_S5, F32, tm, 512, "proj_u")
    gates = _matmul_wt(xm, wit, RWKV_COLS + D_S5, 2 * D_MODEL, BF16, tm, 512, "proj_g")

    lw, a, g = _lora(proj_l, mu_lora, row(rwkv_w0[0]), row(rwkv_a0[0]), rwkv_w2[0].astype(BF16),
                     rwkv_a2[0].astype(BF16), g2, tr)
    o_a = _rwkv(proj_rkv, mu, lw, a, g, row(rwkv_k_k[0]), row(rwkv_k_a[0]), row(rwkv_r_k[0]),
                row(rwkv_ln_w[0]), row(rwkv_ln_b[0]), min(RWKV_T * RWKV_CHUNKS_PER_STEP, L))

    rep = lambda v: jnp.repeat(v, S5_STATE).reshape(1, -1)
    tables = _s5_tables(row(s5_lam_re[0]), row(s5_lam_im[0]), rep(s5_log_dt[0]), S5_T)
    bt = lambda w: _block_diag_groups(jnp.swapaxes(w, 1, 2))
    o_b = _s5(proj_u, tables, bt(s5_b_re[0]), bt(s5_b_im[0]),
              _block_diag_groups(jnp.swapaxes(s5_c_re[0], 1, 2)).astype(BF16),
              _block_diag_groups(jnp.swapaxes(s5_c_im[0], 1, 2)).astype(BF16),
              row(s5_d[0]), s5_w_glu[0].astype(BF16), row(s5_b_glu[0]), min(S5_ROWS, L))

    merged = _merge(o_a, o_b, w_up_rwkv[0], w_up_s5[0], gates, tm, 512)
    mix = _matmul(merged, w_out[0], BF16, tm, 512, "w_out")

    h1, xf = _mid(h, mix, row(norm_post_mix[0]), row(norm_pre_ffn[0]), mod, tr)
    act, wd_bf = _ffn_up(xf, ffn_w_gate[0], ffn_w_up[0], ffn_w_down[0], min(2048, L), 256)
    ff = _matmul_ksplit(act, wd_bf, BF16, tm, 1024, D_FF // 2, "ffn_down")
    out = _final(h1, ff, row(norm_post_ffn[0]), mod, tr)
    return out.reshape(bsz, L, D_MODEL)
```

```python
import functools
import math

import jax
import jax.numpy as jnp
from jax import lax
from jax.experimental import pallas as pl
from jax.experimental.pallas import tpu as pltpu

F32 = jnp.float32
BF16 = jnp.bfloat16

D_MODEL = 4096
RMS_EPS = 1e-6
D_RWKV = 2048
RWKV_HEAD = 64
LORA_W = 128
LORA_A = 128
LORA_G = 480
LORA_G_PAD = 512
GN_EPS = 64e-5
D_S5 = 1024
S5_GROUPS = 64
S5_GROUP_CH = 16
S5_STATE = 64
S5_BLOCKS = 4
D_FF = 11008

LANES = 128
PAIRS = D_RWKV // LANES
RWKV_T = 64
RWKV_CHUNKS_PER_STEP = 8
S5_T = 128
S5_ROWS = 256

RWKV_COLS = 3 * D_RWKV + LORA_W + LORA_A + LORA_G

VMEM_LIMIT = 56 * 1024 * 1024


def _cparams(n_axes, vmem=VMEM_LIMIT):
    return pltpu.CompilerParams(dimension_semantics=("arbitrary",) * n_axes, vmem_limit_bytes=vmem)


def _dot(a, b):
    return jnp.dot(a.astype(BF16), b.astype(BF16), preferred_element_type=F32)


def _dot_nt(a, b):
    return lax.dot_general(a.astype(BF16), b.astype(BF16), (((1,), (1,)), ((), ())),
                           preferred_element_type=F32)


def _dot_tn(a, b):
    return lax.dot_general(a.astype(BF16), b.astype(BF16), (((0,), (0,)), ((), ())),
                           preferred_element_type=F32)


def _split2(x):
    hi = x.astype(BF16)
    lo = (x - hi.astype(F32)).astype(BF16)
    return hi, lo


def _head_sums(x, sel2):
    half = x.shape[0] // 2
    hi, lo = _split2(jnp.concatenate([x[:half], x[half:]], axis=1))
    res = (jnp.dot(hi, sel2, preferred_element_type=F32) + jnp.dot(lo, sel2, preferred_element_type=F32))
    return jnp.concatenate([res[:, :LANES], res[:, LANES:]], axis=0)


def _dot_f32_rhs(sel, x):
    hi, lo = _split2(x)
    return (jnp.dot(sel, hi, preferred_element_type=F32) + jnp.dot(sel, lo, preferred_element_type=F32))


def _rms(x):
    return x * lax.rsqrt(jnp.mean(x * x, axis=-1, keepdims=True) + RMS_EPS)


def _sigmoid(x):
    return 1.0 / (1.0 + jnp.exp(-x))


def _token_shift(p, halo, mu, first):
    last = jnp.where(first, 0.0, halo[7:8, :])
    rolled = pltpu.roll(p, 1, axis=0)
    row = lax.broadcasted_iota(jnp.int32, p.shape, 0)
    prev = jnp.where(row == 0, last, rolled)
    return p + (prev - p) * mu


def _col_matvec(cs, w_ref, b_ref):
    tn = w_ref.shape[1]
    rows = 512
    acc = jnp.zeros((8, tn), F32)
    for k0 in range(0, D_MODEL, rows):
        blk = w_ref[k0:k0 + rows, :] * cs[k0:k0 + rows, :]
        acc = acc + jnp.sum(blk.reshape(rows // 8, 8, tn), axis=0)
    return jnp.sum(acc, axis=0, keepdims=True) + b_ref[...]


def _ada_kernel(c_ref, w_ref, b_ref, o_ref, cs_ref):
    c = c_ref[...]
    cs = c * _sigmoid(c)
    cs_ref[...] = cs
    o_ref[...] = _col_matvec(cs, w_ref, b_ref)


def _ada(c_col, w_ada, b_ada, n):
    tn = 512
    return pl.pallas_call(
        _ada_kernel,
        grid=(n // tn,),
        in_specs=[pl.BlockSpec((D_MODEL, 1), lambda j: (0, 0)),
                  pl.BlockSpec((D_MODEL, tn), lambda j: (0, j)),
                  pl.BlockSpec((1, tn), lambda j: (0, j))],
        out_specs=[pl.BlockSpec((1, tn), lambda j: (0, j)),
                   pl.BlockSpec((D_MODEL, 1), lambda j: (0, 0))],
        out_shape=[jax.ShapeDtypeStruct((1, n), F32), jax.ShapeDtypeStruct((D_MODEL, 1), F32)],
        compiler_params=_cparams(1),
        name="ada",
    )(c_col, w_ada, b_ada)


def _prenorm_kernel(x_ref, g_ref, sh_ref, sc_ref, o_ref):
    y = _rms(x_ref[...]) * g_ref[...]
    o_ref[...] = (y * (1.0 + sc_ref[...]) + sh_ref[...]).astype(o_ref.dtype)


def _prenorm(x, g, mod, sh_idx, sc_idx, tm):
    L = x.shape[0]
    return pl.pallas_call(
        _prenorm_kernel,
        grid=(L // tm,),
        in_specs=[pl.BlockSpec((tm, D_MODEL), lambda i: (i, 0)),
                  pl.BlockSpec((1, D_MODEL), lambda i: (0, 0)),
                  pl.BlockSpec((1, D_MODEL), lambda i: (0, sh_idx)),
                  pl.BlockSpec((1, D_MODEL), lambda i: (0, sc_idx))],
        out_specs=pl.BlockSpec((tm, D_MODEL), lambda i: (i, 0)),
        out_shape=jax.ShapeDtypeStruct((L, D_MODEL), BF16),
        compiler_params=_cparams(1),
        name="prenorm",
    )(x, g, mod, mod)


def _mid_kernel(x_ref, mix_ref, gpost_ref, gate_ref, gpre_ref, sh_ref, sc_ref, h_ref, xf_ref):
    h = x_ref[...] + gate_ref[...] * (_rms(mix_ref[...].astype(F32)) * gpost_ref[...])
    h_ref[...] = h
    y = _rms(h) * gpre_ref[...]
    xf_ref[...] = (y * (1.0 + sc_ref[...]) + sh_ref[...]).astype(xf_ref.dtype)


def _mid(x, mix, gpost, gpre, mod1, mod2, tm):
    L = x.shape[0]
    row = lambda i: (i, 0)
    vec = lambda k: pl.BlockSpec((1, D_MODEL), lambda i: (0, k))
    return pl.pallas_call(
        _mid_kernel,
        grid=(L // tm,),
        in_specs=[pl.BlockSpec((tm, D_MODEL), row), pl.BlockSpec((tm, D_MODEL), row),
                  vec(0), vec(2), vec(0), vec(0), vec(1)],
        out_specs=[pl.BlockSpec((tm, D_MODEL), row), pl.BlockSpec((tm, D_MODEL), row)],
        out_shape=[jax.ShapeDtypeStruct((L, D_MODEL), F32), jax.ShapeDtypeStruct((L, D_MODEL), BF16)],
        compiler_params=_cparams(1),
        name="mid",
    )(x, mix, gpost, mod1, gpre, mod2, mod2)


def _final_kernel(h_ref, ff_ref, gpost_ref, gate_ref, o_ref):
    o_ref[...] = h_ref[...] + gate_ref[...] * (_rms(ff_ref[...].astype(F32)) * gpost_ref[...])


def _final(h, ff, gpost, mod, tm):
    L = h.shape[0]
    row = lambda i: (i, 0)
    return pl.pallas_call(
        _final_kernel,
        grid=(L // tm,),
        in_specs=[pl.BlockSpec((tm, D_MODEL), row), pl.BlockSpec((tm, D_MODEL), row),
                  pl.BlockSpec((1, D_MODEL), lambda i: (0, 0)),
                  pl.BlockSpec((1, D_MODEL), lambda i: (0, 2))],
        out_specs=pl.BlockSpec((tm, D_MODEL), row),
        out_shape=jax.ShapeDtypeStruct((L, D_MODEL), F32),
        compiler_params=_cparams(1),
        name="final",
    )(h, ff, gpost, mod)


def _mm_kernel(a_ref, w_ref, o_ref):
    o_ref[...] = jnp.dot(a_ref[...], w_ref[...].astype(BF16),
                         preferred_element_type=F32).astype(o_ref.dtype)


def _matmul(a, w, out_dtype, tm, tn, name, n=None):
    m, k = a.shape
    n = w.shape[1] if n is None else n
    return pl.pallas_call(
        _mm_kernel,
        grid=(m // tm, n // tn),
        in_specs=[pl.BlockSpec((tm, k), lambda i, j: (i, 0)),
                  pl.BlockSpec((k, tn), lambda i, j: (0, j))],
        out_specs=pl.BlockSpec((tm, tn), lambda i, j: (i, j)),
        out_shape=jax.ShapeDtypeStruct((m, n), out_dtype),
        compiler_params=_cparams(2),
        name=name,
    )(a, w)


def _mm_acc_kernel(a_ref, w_ref, o_ref, acc_ref):
    kk = pl.program_id(2)
    last = pl.num_programs(2) - 1
    part = jnp.dot(a_ref[...], w_ref[...], preferred_element_type=F32)

    @pl.when(kk == 0)
    def _():
        acc_ref[...] = part

    @pl.when((kk > 0) & (kk < last))
    def _():
        acc_ref[...] = acc_ref[...] + part

    @pl.when(kk == last)
    def _():
        o_ref[...] = (acc_ref[...] + part).astype(o_ref.dtype)


def _dot_wt(a, wt):
    return lax.dot_general(a, wt.astype(BF16), (((1,), (1,)), ((), ())), preferred_element_type=F32)


def _mm_wt_kernel(a_ref, wt_ref, o_ref):
    o_ref[...] = _dot_wt(a_ref[...], wt_ref[...]).astype(o_ref.dtype)


def _matmul_wt(a, wt, row0, n, out_dtype, tm, tn, name):
    m, k = a.shape
    return pl.pallas_call(
        _mm_wt_kernel,
        grid=(m // tm, n // tn),
        in_specs=[pl.BlockSpec((tm, k), lambda i, j: (i, 0)),
                  pl.BlockSpec((pl.Element(tn), pl.Element(k)),
                               lambda i, j: (pl.multiple_of(row0 + j * tn, 8), 0))],
        out_specs=pl.BlockSpec((tm, tn), lambda i, j: (i, j)),
        out_shape=jax.ShapeDtypeStruct((m, n), out_dtype),
        compiler_params=_cparams(2),
        name=name,
    )(a, wt)


def _mm_wt_ada_kernel(a_ref, wt_ref, cs_ref, wada_ref, bada_ref, o_ref, mod_ref):
    o_ref[...] = _dot_wt(a_ref[...], wt_ref[...]).astype(o_ref.dtype)
    mod_ref[...] = _col_matvec(cs_ref[...], wada_ref, bada_ref)


def _matmul_wt_ada(a, wt, row0, n, out_dtype, tm, tn, cs, w_ada, b_ada, col0, ncols, name):
    m, k = a.shape
    ni, nj = m // tm, n // tn
    cps = LANES * -(-(ncols // LANES) // (ni * nj))
    assert ncols % cps == 0 and col0 % cps == 0 and ncols // cps <= ni * nj
    nblk = ncols // cps
    blk = lambda i, j: jnp.minimum(i * nj + j, nblk - 1)
    return pl.pallas_call(
        _mm_wt_ada_kernel,
        grid=(ni, nj),
        in_specs=[pl.BlockSpec((tm, k), lambda i, j: (i, 0)),
                  pl.BlockSpec((pl.Element(tn), pl.Element(k)),
                               lambda i, j: (pl.multiple_of(row0 + j * tn, 8), 0)),
                  pl.BlockSpec((D_MODEL, 1), lambda i, j: (0, 0)),
                  pl.BlockSpec((D_MODEL, cps), lambda i, j: (0, col0 // cps + blk(i, j))),
                  pl.BlockSpec((1, cps), lambda i, j: (0, col0 // cps + blk(i, j)))],
        out_specs=[pl.BlockSpec((tm, tn), lambda i, j: (i, j)),
                   pl.BlockSpec((1, cps), lambda i, j: (0, blk(i, j)))],
        out_shape=[jax.ShapeDtypeStruct((m, n), out_dtype), jax.ShapeDtypeStruct((1, ncols), F32)],
        compiler_params=_cparams(2),
        name=name,
    )(a, wt, cs, w_ada, b_ada)


def _matmul_ksplit(a, w, out_dtype, tm, tn, tk, name):
    m, k = a.shape
    n = w.shape[1]
    assert k % tk == 0 and k // tk >= 2
    return pl.pallas_call(
        _mm_acc_kernel,
        grid=(m // tm, n // tn, k // tk),
        in_specs=[pl.BlockSpec((tm, tk), lambda i, j, kk: (i, kk)),
                  pl.BlockSpec((tk, tn), lambda i, j, kk: (kk, j))],
        out_specs=pl.BlockSpec((tm, tn), lambda i, j, kk: (i, j)),
        out_shape=jax.ShapeDtypeStruct((m, n), out_dtype),
        scratch_shapes=[pltpu.VMEM((tm, tn), F32)],
        compiler_params=_cparams(3),
        name=name,
    )(a, w)


def _merge_kernel(oa_ref, ob_ref, wa_ref, wb_ref, ga_ref, gb_ref, o_ref):
    ya = jnp.dot(oa_ref[...], wa_ref[...].astype(BF16), preferred_element_type=F32)
    yb = jnp.dot(ob_ref[...], wb_ref[...].astype(BF16), preferred_element_type=F32)
    m = _sigmoid(ga_ref[...].astype(F32)) * ya + _sigmoid(gb_ref[...].astype(F32)) * yb
    o_ref[...] = m.astype(o_ref.dtype)


def _merge(o_a, o_b, w_up_a, w_up_b, gates, tm, tn):
    L = o_a.shape[0]
    nb = D_MODEL // tn
    return pl.pallas_call(
        _merge_kernel,
        grid=(L // tm, nb),
        in_specs=[pl.BlockSpec((tm, D_RWKV), lambda i, j: (i, 0)),
                  pl.BlockSpec((tm, D_S5), lambda i, j: (i, 0)),
                  pl.BlockSpec((D_RWKV, tn), lambda i, j: (0, j)),
                  pl.BlockSpec((D_S5, tn), lambda i, j: (0, j)),
                  pl.BlockSpec((tm, tn), lambda i, j: (i, j)),
                  pl.BlockSpec((tm, tn), lambda i, j: (i, j + nb))],
        out_specs=pl.BlockSpec((tm, tn), lambda i, j: (i, j)),
        out_shape=jax.ShapeDtypeStruct((L, D_MODEL), BF16),
        compiler_params=_cparams(2),
        name="merge",
    )(o_a, o_b, w_up_a, w_up_b, gates, gates)


def _ffn_up_kernel(x_ref, wg_ref, wu_ref, wd_ref, o_ref, wd_bf_ref):
    x = x_ref[...]
    a = jnp.dot(x, wg_ref[...].astype(BF16), preferred_element_type=F32)
    b = jnp.dot(x, wu_ref[...].astype(BF16), preferred_element_type=F32)
    o_ref[...] = ((a * _sigmoid(a)) * b).astype(o_ref.dtype)
    wd_bf_ref[...] = wd_ref[...].astype(BF16)


def _ffn_up(xf, w_gate, w_up, w_down, tm, tn):
    L = xf.shape[0]
    n = w_gate.shape[1]
    ni, nj = L // tm, n // tn
    slab = w_down.shape[0] // (ni * nj)
    assert slab * ni * nj == w_down.shape[0] and slab % 16 == 0
    wd_spec = lambda: pl.BlockSpec((slab, D_MODEL), lambda i, j: (i * nj + j, 0))
    return pl.pallas_call(
        _ffn_up_kernel,
        grid=(ni, nj),
        in_specs=[pl.BlockSpec((tm, D_MODEL), lambda i, j: (i, 0), pipeline_mode=pl.Buffered(1)),
                  pl.BlockSpec((D_MODEL, tn), lambda i, j: (0, j)),
                  pl.BlockSpec((D_MODEL, tn), lambda i, j: (0, j)),
                  wd_spec()],
        out_specs=[pl.BlockSpec((tm, tn), lambda i, j: (i, j)), wd_spec()],
        out_shape=[jax.ShapeDtypeStruct((L, n), BF16), jax.ShapeDtypeStruct(w_down.shape, BF16)],
        compiler_params=_cparams(2),
        name="ffn_up",
    )(xf, w_gate, w_up, w_down)


def _lora_kernel(p_ref, halo_ref, mu_ref, w0_ref, a0_ref, w2_ref, a2_ref, g2_ref,
                 lw_ref, a_ref, g_ref):
    ps = _token_shift(p_ref[...], halo_ref[...], mu_ref[...], pl.program_id(0) == 0)
    xw = ps[:, :LORA_W]
    xa = ps[:, LORA_W:LORA_W + LORA_A]
    xg = ps[:, LORA_W + LORA_A:]
    z = -(w0_ref[...] + _dot(jnp.tanh(xw), w2_ref[...]))
    softplus = jnp.maximum(z, 0.0) + jnp.log(1.0 + jnp.exp(-jnp.abs(z)))
    lw_ref[...] = -jnp.exp(-softplus - 0.5)
    a_ref[...] = _sigmoid(a0_ref[...] + _dot(xa, a2_ref[...]))
    g_ref[...] = _dot(_sigmoid(xg), g2_ref[...])


def _lora(proj, mu, w0, a0, w2, a2, g2, tm):
    L, n = proj.shape
    full = lambda shape: pl.BlockSpec(shape, lambda i: (0, 0))
    out = pl.BlockSpec((tm, D_RWKV), lambda i: (i, 0))
    return pl.pallas_call(
        _lora_kernel,
        grid=(L // tm,),
        in_specs=[pl.BlockSpec((tm, n), lambda i: (i, 0)),
                  pl.BlockSpec((8, n), lambda i: (jnp.maximum(i * (tm // 8) - 1, 0), 0)),
                  full((1, n)),
                  full((1, D_RWKV)), full((1, D_RWKV)),
                  full((LORA_W, D_RWKV)), full((LORA_A, D_RWKV)), full((LORA_G_PAD, D_RWKV))],
        out_specs=[out, out, out],
        out_shape=[jax.ShapeDtypeStruct((L, D_RWKV), F32)] * 3,
        compiler_params=_cparams(1),
        name="lora",
    )(proj, proj, mu, w0, a0, w2, a2, g2)


def _rwkv_parallel(r, k, v, am, bm, lw, cst, T):
    S = 2 * T
    n = r.shape[0] // T
    lane_lo, tri_t, strict, incl2, eye, same_head = cst
    cr = range(n)

    def chunk(x, c):
        return x[c * T:(c + 1) * T]

    def stack_own(xc):
        return jnp.concatenate([jnp.where(lane_lo, xc, 0.0), jnp.where(lane_lo, 0.0, xc)], axis=0)

    def stack_dup(xc):
        return jnp.concatenate([xc, xc], axis=0)

    c_all = _dot_f32_rhs(tri_t, jnp.concatenate([chunk(lw, c) for c in cr], axis=1))
    yield
    cu = [c_all[:, c * LANES:(c + 1) * LANES] for c in cr]
    c_end = [cu[c][T - 1:T, :] for c in cr]
    e_neg = [jnp.exp(-cu[c]) for c in cr]
    e_end = [jnp.exp(c_end[c] - cu[c]) for c in cr]
    rh = [stack_own(chunk(r, c) * jnp.exp(cu[c])) for c in cr]
    ah = [stack_own(chunk(am, c) * jnp.exp(cu[c] - chunk(lw, c))) for c in cr]
    bh = [stack_dup(chunk(bm, c) * e_neg[c]) for c in cr]
    kh = [stack_dup(chunk(k, c) * e_neg[c]) for c in cr]
    bc = [stack_own(chunk(bm, c) * e_end[c]) for c in cr]
    kc = [stack_own(chunk(k, c) * e_end[c]) for c in cr]
    v_s = [stack_dup(chunk(v, c)) for c in cr]

    big = [_dot_nt(jnp.concatenate([ah[c], rh[c]], axis=0), jnp.concatenate([bh[c], kh[c]], axis=0))
           for c in cr]
    yield
    nj = [jnp.where(strict, big[c][:S, :S], 0.0) for c in cr]
    m_r = [jnp.where(incl2, big[c][S:, :], 0.0) for c in cr]
    av = [_dot(jnp.where(strict, big[c][:S, S:], 0.0), v_s[c]) for c in cr]
    yield

    steps = T.bit_length() - 1
    p = [jnp.where(eye, 1.0, 0.0) + nj[c] for c in cr]
    nj = [_dot(nj[c], nj[c]) for c in cr]
    yield
    for j in range(1, steps):
        if j + 1 < steps:
            res = [_dot(nj[c], jnp.concatenate([p[c], nj[c]], axis=1)) for c in cr]
            p = [p[c] + res[c][:, :S] for c in cr]
            nj = [res[c][:, S:] for c in cr]
        else:
            p = [p[c] + _dot(nj[c], p[c]) for c in cr]
        yield
    x = [_dot(p[c], jnp.concatenate([ah[c], av[c]], axis=1)) for c in cr]
    yield

    wv = [jnp.concatenate([x[c], jnp.concatenate([jnp.zeros_like(v_s[c]), v_s[c]], axis=1)], axis=0)
          for c in cr]
    y = [_dot(m_r[c], wv[c]) for c in cr]
    q = [rh[c] + y[c][:, :LANES] for c in cr]
    o0 = [y[c][:, LANES:] for c in cr]
    gd = [_dot_tn(jnp.concatenate([bc[c], kc[c]], axis=0), wv[c]) for c in cr]
    gd = [jnp.concatenate([gd[c][:, :LANES], jnp.where(same_head, gd[c][:, LANES:], 0.0)], axis=1)
          for c in cr]
    decay = [jnp.broadcast_to(jnp.sum(jnp.where(eye, jnp.exp(c_end[c]), 0.0), axis=1, keepdims=True),
                              (LANES, LANES)) for c in cr]
    return gd, q, o0, decay


def _rwkv_kernel(r_ref, k_ref, v_ref, rh_ref, kh_ref, vh_ref, mur_ref, muk_ref, muv_ref,
                 lw_ref, a_ref, kk_ref, ka_ref, rk_ref, g_ref, lnw_ref, lnb_ref,
                 o_ref, h_ref, gd_ref, q_ref, o0_ref, dec_ref, bonus_ref, *, nblk):
    s = pl.program_id(0)
    last = pl.num_programs(0) - 2
    first_cur = lax.rem(jnp.minimum(s, last), nblk) == 0
    first_prv = lax.rem(jnp.maximum(s - 1, 0), nblk) == 0

    @pl.when(s == 0)
    def _():
        h_ref[...] = jnp.zeros_like(h_ref)
        gd_ref[...] = jnp.zeros_like(gd_ref)
        q_ref[...] = jnp.zeros_like(q_ref)
        o0_ref[...] = jnp.zeros_like(o0_ref)
        dec_ref[...] = jnp.zeros_like(dec_ref)
        bonus_ref[...] = jnp.zeros_like(bonus_ref)

    rows = r_ref.shape[0]
    T = RWKV_T
    S = 2 * T
    n = rows // T
    ii = lax.broadcasted_iota(jnp.int32, (S, S), 0)
    jj = lax.broadcasted_iota(jnp.int32, (S, S), 1)
    same = (ii // T) == (jj // T)
    strict = same & (jj < ii)
    incl = same & (jj <= ii)
    eye = ii == jj
    tri_incl = jnp.where(incl, 1.0, 0.0).astype(BF16)
    ii2 = lax.broadcasted_iota(jnp.int32, (S, 2 * S), 0)
    jj2 = lax.broadcasted_iota(jnp.int32, (S, 2 * S), 1) % S
    incl2 = ((ii2 // T) == (jj2 // T)) & (jj2 <= ii2)
    hi_ = lax.broadcasted_iota(jnp.int32, (2 * LANES, 2 * LANES), 0) // RWKV_HEAD
    hj_ = lax.broadcasted_iota(jnp.int32, (2 * LANES, 2 * LANES), 1) // RWKV_HEAD
    head_sum = jnp.where(hi_ == hj_, 1.0, 0.0).astype(BF16)
    lane_lo = lax.broadcasted_iota(jnp.int32, (T, LANES), 1) < RWKV_HEAD
    it = lax.broadcasted_iota(jnp.int32, (T, T), 0)
    jt = lax.broadcasted_iota(jnp.int32, (T, T), 1)
    tri_t = jnp.where(jt <= it, 1.0, 0.0).astype(BF16)
    same_head = ((lax.broadcasted_iota(jnp.int32, (LANES, LANES), 0) // RWKV_HEAD)
                 == (lax.broadcasted_iota(jnp.int32, (LANES, LANES), 1) // RWKV_HEAD))
    cst = (lane_lo, tri_t, strict, incl2, eye, same_head)

    state = [jnp.where(first_prv, 0.0, h_ref[...])]
    outs = []

    def serial_step(c):
        h = state[0]
        gd = gd_ref[c]
        z = _dot(jnp.concatenate([gd[:, :LANES], q_ref[c]], axis=0), h)
        o_s = z[LANES:] + o0_ref[c]
        outs.append(jnp.where(lane_lo, o_s[:T], o_s[T:]))
        state[0] = dec_ref[c] * h + z[:LANES] + gd[:, LANES:]

    def serial_finish():
        h_ref[...] = state[0]
        o = jnp.concatenate(outs, axis=0) if n > 1 else outs[0]
        inv = 1.0 / RWKV_HEAD
        mean = _head_sums(o, head_sum) * inv
        d = o - mean
        var = _head_sums(d * d, head_sum) * inv
        y = d * lax.rsqrt(var + GN_EPS) * lnw_ref[...] + lnb_ref[...]
        o_ref[...] = ((y + bonus_ref[...]) * g_ref[...]).astype(o_ref.dtype)

    r = _token_shift(r_ref[...], rh_ref[...], mur_ref[...], first_cur)
    k = _token_shift(k_ref[...], kh_ref[...], muk_ref[...], first_cur)
    v = _token_shift(v_ref[...], vh_ref[...], muv_ref[...], first_cur)
    a = a_ref[...]
    kk = k * kk_ref[...]
    k = k * (1.0 + (a - 1.0) * ka_ref[...])
    serial_step(0)
    ss = _head_sums(kk * kk, head_sum)
    kk = kk * lax.rsqrt(jnp.maximum(ss, 1e-24))
    bonus = _head_sums(r * k * rk_ref[...], head_sum) * v

    gen = _rwkv_parallel(r, k, v, -kk, kk * a, lw_ref[...], cst, T)
    done = 1
    while True:
        try:
            next(gen)
        except StopIteration as stop:
            gd, q, o0, decay = stop.value
            break
        if done < n:
            serial_step(done)
        elif done == n:
            serial_finish()
        done += 1
    assert done > n, "fewer parallel stages than chunks per block"

    for c in range(n):
        gd_ref[c] = gd[c]
        q_ref[c] = q[c]
        o0_ref[c] = o0[c]
        dec_ref[c] = decay[c]
    bonus_ref[...] = bonus


def _rwkv(proj, mu, lw, a, g, k_k, k_a, r_k, ln_w, ln_b, rows):
    L = proj.shape[0]
    nb = D_RWKV // LANES
    nblk = L // rows
    nsteps = PAIRS * nblk
    n = rows // RWKV_T

    def cur(s):
        b = jnp.minimum(s, nsteps - 1)
        return lax.div(b, nblk), lax.rem(b, nblk)

    def prv(s):
        b = jnp.maximum(s - 1, 0)
        return lax.div(b, nblk), lax.rem(b, nblk)

    def col(off):
        return pl.BlockSpec((rows, LANES), lambda s: (cur(s)[1], cur(s)[0] + off))

    def halo(off):
        return pl.BlockSpec((8, LANES),
                            lambda s: (jnp.maximum(cur(s)[1] * (rows // 8) - 1, 0), cur(s)[0] + off))

    def vec(off):
        return pl.BlockSpec((1, LANES), lambda s: (0, cur(s)[0] + off))

    def lag_col():
        return pl.BlockSpec((rows, LANES), lambda s: (prv(s)[1], prv(s)[0]))

    def lag_vec():
        return pl.BlockSpec((1, LANES), lambda s: (0, prv(s)[0]))

    return pl.pallas_call(
        functools.partial(_rwkv_kernel, nblk=nblk),
        grid=(nsteps + 1,),
        in_specs=[col(0), col(nb), col(2 * nb), halo(0), halo(nb), halo(2 * nb),
                  vec(0), vec(nb), vec(2 * nb),
                  col(0), col(0), vec(0), vec(0), vec(0),
                  lag_col(), lag_vec(), lag_vec()],
        out_specs=lag_col(),
        out_shape=jax.ShapeDtypeStruct((L, D_RWKV), BF16),
        scratch_shapes=[pltpu.VMEM((LANES, LANES), F32),
                        pltpu.VMEM((n, LANES, 2 * LANES), F32),
                        pltpu.VMEM((n, 2 * RWKV_T, LANES), F32),
                        pltpu.VMEM((n, 2 * RWKV_T, LANES), F32),
                        pltpu.VMEM((n, LANES, LANES), F32),
                        pltpu.VMEM((rows, LANES), F32)],
        compiler_params=_cparams(1),
        name="rwkv",
    )(proj, proj, proj, proj, proj, proj, mu, mu, mu, lw, a, k_k, k_a, r_k, g, ln_w, ln_b)


def _s5_table_kernel(lr_ref, li_ref, ldt_ref, pos_re, pos_im, neg_re, neg_im, z_re, z_im,
                     lbc_re, lbc_im):
    lr = lr_ref[...]
    li = li_ref[...]
    dt = jnp.exp(ldt_ref[...])
    centre = pos_re.shape[0] // 2
    t = (lax.broadcasted_iota(jnp.int32, pos_re.shape, 0) - centre).astype(F32)
    mag = jnp.exp(t * (lr * dt))
    ang = t * (li * dt)
    cs, sn = jnp.cos(ang), jnp.sin(ang)
    pos_re[...] = mag * cs
    pos_im[...] = mag * sn
    inv = 1.0 / mag
    neg_re[...] = inv * cs
    neg_im[...] = -(inv * sn)
    mc = jnp.exp((centre + 1.0) * (lr * dt))
    lbc_re[...] = mc * jnp.cos((centre + 1.0) * (li * dt))
    lbc_im[...] = mc * jnp.sin((centre + 1.0) * (li * dt))
    m1 = jnp.exp(lr * dt)
    lb_re = m1 * jnp.cos(li * dt)
    lb_im = m1 * jnp.sin(li * dt)
    den = lr * lr + li * li
    z_re[...] = ((lb_re - 1.0) * lr + lb_im * li) / den
    z_im[...] = (lb_im * lr - (lb_re - 1.0) * li) / den


def _s5_tables(lr, li, ldt, rows):
    n = lr.shape[1]
    full = pl.BlockSpec((1, n), lambda: (0, 0))
    tab = pl.BlockSpec((rows, n), lambda: (0, 0))
    return pl.pallas_call(
        _s5_table_kernel,
        in_specs=[full, full, full],
        out_specs=[tab, tab, tab, tab, full, full, full, full],
        out_shape=[jax.ShapeDtypeStruct((rows, n), F32)] * 4 + [jax.ShapeDtypeStruct((1, n), F32)] * 4,
        name="s5_tables",
    )(lr, li, ldt)


def _s5_kernel(u_ref, pos_re_ref, pos_im_ref, neg_re_ref, neg_im_ref, z_re_ref, z_im_ref,
               lbc_re_ref, lbc_im_ref, b_re_ref, b_im_ref, c_re_ref, c_im_ref, d_ref, wglu_ref,
               bglu_ref, o_ref, bb_ref, st_ref):
    R = u_ref.shape[0]
    T = pos_re_ref.shape[0]
    NS = S5_GROUPS * S5_STATE
    BS = NS // S5_BLOCKS
    BC = D_S5 // S5_BLOCKS

    @pl.when(pl.program_id(0) == 0)
    def _():
        st_ref[...] = jnp.zeros_like(st_ref)
        for j in range(S5_BLOCKS):
            zr = z_re_ref[:, j * BS:(j + 1) * BS]
            zi = z_im_ref[:, j * BS:(j + 1) * BS]
            br = b_re_ref[j]
            bi = b_im_ref[j]
            bb_ref[j, :, :BS] = (zr * br - zi * bi).astype(BF16)
            bb_ref[j, :, BS:] = (zr * bi + zi * br).astype(BF16)

    u = u_ref[...]
    ub = u.astype(BF16)
    ii = lax.broadcasted_iota(jnp.int32, (T, T), 0)
    jj = lax.broadcasted_iota(jnp.int32, (T, T), 1)
    tri = jnp.where(jj <= ii, 1.0, 0.0).astype(BF16)

    ys = []
    for j in range(S5_BLOCKS):
        sl = slice(j * BS, (j + 1) * BS)
        bu = jnp.dot(ub[:, j * BC:(j + 1) * BC], bb_ref[j], preferred_element_type=F32)
        nr, ni = neg_re_ref[:, sl], neg_im_ref[:, sl]
        pr, pi = pos_re_ref[:, sl], pos_im_ref[:, sl]
        lr_, li_ = lbc_re_ref[:, sl], lbc_im_ref[:, sl]
        s_re = st_ref[0:1, sl]
        s_im = st_ref[1:2, sl]
        xs_re, xs_im = [], []
        for t in range(R // T):
            bu_re, bu_im = bu[t * T:(t + 1) * T, :BS], bu[t * T:(t + 1) * T, BS:]
            zz = jnp.concatenate([nr * bu_re - ni * bu_im, nr * bu_im + ni * bu_re], axis=1)
            acc = jnp.dot(tri, zz.astype(BF16), preferred_element_type=F32)
            a_re = acc[:, :BS] + (lr_ * s_re - li_ * s_im)
            a_im = acc[:, BS:] + (lr_ * s_im + li_ * s_re)
            x_re = pr * a_re - pi * a_im
            x_im = pr * a_im + pi * a_re
            s_re, s_im = x_re[T - 1:T, :], x_im[T - 1:T, :]
            xs_re.append(x_re)
            xs_im.append(x_im)
        st_ref[0:1, sl] = s_re
        st_ref[1:2, sl] = s_im
        cat = lambda xs: jnp.concatenate(xs, axis=0) if len(xs) > 1 else xs[0]
        ys.append(_dot(cat(xs_re), c_re_ref[j]) - _dot(cat(xs_im), c_im_ref[j]))
    y = jnp.concatenate(ys, axis=1) + d_ref[...] * u
    y = 0.5 * y * (1.0 + jnp.tanh(math.sqrt(2.0 / math.pi) * (y + 0.044715 * (y * y * y))))
    zg = _dot(y, wglu_ref[...]) + bglu_ref[...]
    o_ref[...] = (zg[:, :D_S5] * _sigmoid(zg[:, D_S5:])).astype(o_ref.dtype)


def _s5(proj, tables, b_re, b_im, c_re, c_im, d, w_glu, b_glu, rows):
    L = proj.shape[0]
    NS = S5_GROUPS * S5_STATE
    BS = NS // S5_BLOCKS
    BC = D_S5 // S5_BLOCKS
    t = tables[0].shape[0]
    c2 = lambda shape: pl.BlockSpec(shape, lambda i: (0, 0))
    c3 = lambda shape: pl.BlockSpec(shape, lambda i: (0, 0, 0))
    return pl.pallas_call(
        _s5_kernel,
        grid=(L // rows,),
        in_specs=[pl.BlockSpec((rows, D_S5), lambda i: (i, 0)),
                  c2((t, NS)), c2((t, NS)), c2((t, NS)), c2((t, NS)),
                  c2((1, NS)), c2((1, NS)), c2((1, NS)), c2((1, NS)),
                  c3((S5_BLOCKS, BC, BS)), c3((S5_BLOCKS, BC, BS)),
                  c3((S5_BLOCKS, BS, BC)), c3((S5_BLOCKS, BS, BC)),
                  c2((1, D_S5)), c2((D_S5, 2 * D_S5)), c2((1, 2 * D_S5))],
        out_specs=pl.BlockSpec((rows, D_S5), lambda i: (i, 0)),
        out_shape=jax.ShapeDtypeStruct((L, D_S5), BF16),
        scratch_shapes=[pltpu.VMEM((S5_BLOCKS, BC, 2 * BS), BF16), pltpu.VMEM((8, NS), F32)],
        compiler_params=_cparams(1),
        name="s5",
    )(proj, *tables, b_re, b_im, c_re, c_im, d, w_glu, b_glu)


def _block_diag_groups(w):
    g, r, c = w.shape
    gb = g // S5_BLOCKS
    w = w.reshape(S5_BLOCKS, gb, r, c)
    eye = jnp.eye(gb, dtype=w.dtype)
    out = w[:, :, :, None, :] * eye[None, :, None, :, None]
    return out.reshape(S5_BLOCKS, gb * r, gb * c)


def kernel(x, c, w_ada, b_ada, norm_pre_mix, norm_post_mix, norm_pre_ffn, norm_post_ffn, w_in, rwkv_mu, rwkv_w0, rwkv_w2, rwkv_a0, rwkv_a2, rwkv_g2, rwkv_k_k, rwkv_k_a, rwkv_r_k, rwkv_ln_w, rwkv_ln_b, s5_lam_re, s5_lam_im, s5_log_dt, s5_b_re, s5_b_im, s5_c_re, s5_c_im, s5_d, s5_w_glu, s5_b_glu, w_up_rwkv, w_up_s5, w_out, ffn_w_gate, ffn_w_up, ffn_w_down):
    bsz, L, _ = x.shape
    assert bsz == 1 and w_ada.shape[0] == 1
    h = x.reshape(L, D_MODEL)
    tm = min(1024, L)
    tr = min(256, L)
    row = lambda v: v.reshape(1, -1)

    half = 3 * D_MODEL
    mod, cs = _ada(c.reshape(D_MODEL, 1), w_ada[0], row(b_ada[0]), half)

    wit = jnp.swapaxes(w_in[0], 0, 1)
    gpad = LORA_G_PAD - LORA_G
    n_rkv = 3 * D_RWKV
    mu = row(rwkv_mu[0])
    mu_lora = jnp.concatenate([rwkv_mu[0, n_rkv:], jnp.zeros((gpad,), F32)]).reshape(1, -1)
    g2 = jnp.concatenate([rwkv_g2[0], jnp.zeros((gpad, D_RWKV), F32)], axis=0).astype(BF16)

    xm = _prenorm(h, row(norm_pre_mix[0]), mod, 0, 1, tr)
    proj_rkv = _matmul_wt(xm, wit, 0, n_rkv, F32, tm, 512, "proj_rkv")
    proj_l = _matmul_wt(xm, wit, n_rkv, LORA_W + LORA_A + LORA_G_PAD, F32, tm, 256, "proj_lora")
    proj_u = _matmul_wt(xm, wit, RWKV_COLS, D_S5, F32, tm, 512, "proj_u")
    gates, mod2 = _matmul_wt_ada(xm, wit, RWKV_COLS + D_S5, 2 * D_MODEL, BF16, tm, 512,
                                 cs, w_ada[0], row(b_ada[0]), half, half, "proj_g")

    lw, a, g = _lora(proj_l, mu_lora, row(rwkv_w0[0]), row(rwkv_a0[0]), rwkv_w2[0].astype(BF16),
                     rwkv_a2[0].astype(BF16), g2, tr)
    o_a = _rwkv(proj_rkv, mu, lw, a, g, row(rwkv_k_k[0]), row(rwkv_k_a[0]), row(rwkv_r_k[0]),
                row(rwkv_ln_w[0]), row(rwkv_ln_b[0]), min(RWKV_T * RWKV_CHUNKS_PER_STEP, L))

    rep = lambda v: jnp.repeat(v, S5_STATE).reshape(1, -1)
    tables = _s5_tables(row(s5_lam_re[0]), row(s5_lam_im[0]), rep(s5_log_dt[0]), S5_T)
    bt = lambda w: _block_diag_groups(jnp.swapaxes(w, 1, 2))
    o_b = _s5(proj_u, tables, bt(s5_b_re[0]), bt(s5_b_im[0]),
              _block_diag_groups(jnp.swapaxes(s5_c_re[0], 1, 2)).astype(BF16),
              _block_diag_groups(jnp.swapaxes(s5_c_im[0], 1, 2)).astype(BF16),
              row(s5_d[0]), s5_w_glu[0].astype(BF16), row(s5_b_glu[0]), min(S5_ROWS, L))

    merged = _merge(o_a, o_b, w_up_rwkv[0], w_up_s5[0], gates, tm, 512)
    mix = _matmul(merged, w_out[0], BF16, tm, 512, "w_out")

    h1, xf = _mid(h, mix, row(norm_post_mix[0]), row(norm_pre_ffn[0]), mod, mod2, tr)
    act, wd_bf = _ffn_up(xf, ffn_w_gate[0], ffn_w_up[0], ffn_w_down[0], min(2048, L), 256)
    ff = _matmul_ksplit(act, wd_bf, BF16, tm, 1024, D_FF // 2, "ffn_down")
    out = _final(h1, ff, row(norm_post_ffn[0]), mod2, tr)
    return out.reshape(bsz, L, D_MODEL)
```

```python
import functools
import math

import jax
import jax.numpy as jnp
from jax import lax
from jax.experimental import pallas as pl
from jax.experimental.pallas import tpu as pltpu

F32 = jnp.float32
BF16 = jnp.bfloat16

D_MODEL = 4096
RMS_EPS = 1e-6
D_RWKV = 2048
RWKV_HEAD = 64
LORA_W = 128
LORA_A = 128
LORA_G = 480
LORA_G_PAD = 512
GN_EPS = 64e-5
D_S5 = 1024
S5_GROUPS = 64
S5_GROUP_CH = 16
S5_STATE = 64
S5_BLOCKS = 4
D_FF = 11008

LANES = 128
PAIRS = D_RWKV // LANES
RWKV_T = 64
RWKV_CHUNKS_PER_STEP = 8
S5_T = 128
S5_ROWS = 256

RWKV_COLS = 3 * D_RWKV + LORA_W + LORA_A + LORA_G

VMEM_LIMIT = 56 * 1024 * 1024


def _cparams(n_axes, vmem=VMEM_LIMIT):
    return pltpu.CompilerParams(dimension_semantics=("arbitrary",) * n_axes, vmem_limit_bytes=vmem)


def _dot(a, b):
    return jnp.dot(a.astype(BF16), b.astype(BF16), preferred_element_type=F32)


def _dot_nt(a, b):
    return lax.dot_general(a.astype(BF16), b.astype(BF16), (((1,), (1,)), ((), ())),
                           preferred_element_type=F32)


def _dot_tn(a, b):
    return lax.dot_general(a.astype(BF16), b.astype(BF16), (((0,), (0,)), ((), ())),
                           preferred_element_type=F32)


def _split2(x):
    hi = x.astype(BF16)
    lo = (x - hi.astype(F32)).astype(BF16)
    return hi, lo


def _head_sums(x, sel2):
    half = x.shape[0] // 2
    hi, lo = _split2(jnp.concatenate([x[:half], x[half:]], axis=1))
    res = (jnp.dot(hi, sel2, preferred_element_type=F32) + jnp.dot(lo, sel2, preferred_element_type=F32))
    return jnp.concatenate([res[:, :LANES], res[:, LANES:]], axis=0)


def _dot_f32_rhs(sel, x):
    hi, lo = _split2(x)
    return (jnp.dot(sel, hi, preferred_element_type=F32) + jnp.dot(sel, lo, preferred_element_type=F32))


def _rms(x):
    return x * lax.rsqrt(jnp.mean(x * x, axis=-1, keepdims=True) + RMS_EPS)


def _sigmoid(x):
    return 1.0 / (1.0 + jnp.exp(-x))


def _token_shift(p, halo, mu, first):
    last = jnp.where(first, 0.0, halo[7:8, :])
    rolled = pltpu.roll(p, 1, axis=0)
    row = lax.broadcasted_iota(jnp.int32, p.shape, 0)
    prev = jnp.where(row == 0, last, rolled)
    return p + (prev - p) * mu


def _col_matvec(cs, w_ref, b_ref):
    tn = w_ref.shape[1]
    rows = 512
    acc = jnp.zeros((8, tn), F32)
    for k0 in range(0, D_MODEL, rows):
        blk = w_ref[k0:k0 + rows, :] * cs[k0:k0 + rows, :]
        acc = acc + jnp.sum(blk.reshape(rows // 8, 8, tn), axis=0)
    return jnp.sum(acc, axis=0, keepdims=True) + b_ref[...]


def _ada_kernel(c_ref, w_ref, b_ref, o_ref, cs_ref):
    c = c_ref[...]
    cs = c * _sigmoid(c)
    cs_ref[...] = cs
    o_ref[...] = _col_matvec(cs, w_ref, b_ref)


def _ada(c_col, w_ada, b_ada, n):
    tn = 512
    return pl.pallas_call(
        _ada_kernel,
        grid=(n // tn,),
        in_specs=[pl.BlockSpec((D_MODEL, 1), lambda j: (0, 0)),
                  pl.BlockSpec((D_MODEL, tn), lambda j: (0, j)),
                  pl.BlockSpec((1, tn), lambda j: (0, j))],
        out_specs=[pl.BlockSpec((1, tn), lambda j: (0, j)),
                   pl.BlockSpec((D_MODEL, 1), lambda j: (0, 0))],
        out_shape=[jax.ShapeDtypeStruct((1, n), F32), jax.ShapeDtypeStruct((D_MODEL, 1), F32)],
        compiler_params=_cparams(1),
        name="ada",
    )(c_col, w_ada, b_ada)


def _mid_kernel(x_ref, mix_ref, gpost_ref, gate_ref, gpre_ref, sh_ref, sc_ref, h_ref, xf_ref):
    h = x_ref[...] + gate_ref[...] * (_rms(mix_ref[...].astype(F32)) * gpost_ref[...])
    h_ref[...] = h
    y = _rms(h) * gpre_ref[...]
    xf_ref[...] = (y * (1.0 + sc_ref[...]) + sh_ref[...]).astype(xf_ref.dtype)


def _mid(x, mix, gpost, gpre, mod1, mod2, tm):
    L = x.shape[0]
    row = lambda i: (i, 0)
    vec = lambda k: pl.BlockSpec((1, D_MODEL), lambda i: (0, k))
    return pl.pallas_call(
        _mid_kernel,
        grid=(L // tm,),
        in_specs=[pl.BlockSpec((tm, D_MODEL), row), pl.BlockSpec((tm, D_MODEL), row),
                  vec(0), vec(2), vec(0), vec(0), vec(1)],
        out_specs=[pl.BlockSpec((tm, D_MODEL), row), pl.BlockSpec((tm, D_MODEL), row)],
        out_shape=[jax.ShapeDtypeStruct((L, D_MODEL), F32), jax.ShapeDtypeStruct((L, D_MODEL), BF16)],
        compiler_params=_cparams(1),
        name="mid",
    )(x, mix, gpost, mod1, gpre, mod2, mod2)


def _final_kernel(h_ref, ff_ref, gpost_ref, gate_ref, o_ref):
    o_ref[...] = h_ref[...] + gate_ref[...] * (_rms(ff_ref[...].astype(F32)) * gpost_ref[...])


def _final(h, ff, gpost, mod, tm):
    L = h.shape[0]
    row = lambda i: (i, 0)
    return pl.pallas_call(
        _final_kernel,
        grid=(L // tm,),
        in_specs=[pl.BlockSpec((tm, D_MODEL), row), pl.BlockSpec((tm, D_MODEL), row),
                  pl.BlockSpec((1, D_MODEL), lambda i: (0, 0)),
                  pl.BlockSpec((1, D_MODEL), lambda i: (0, 2))],
        out_specs=pl.BlockSpec((tm, D_MODEL), row),
        out_shape=jax.ShapeDtypeStruct((L, D_MODEL), F32),
        compiler_params=_cparams(1),
        name="final",
    )(h, ff, gpost, mod)


def _mm_kernel(a_ref, w_ref, o_ref):
    o_ref[...] = jnp.dot(a_ref[...], w_ref[...].astype(BF16),
                         preferred_element_type=F32).astype(o_ref.dtype)


def _matmul(a, w, out_dtype, tm, tn, name, n=None):
    m, k = a.shape
    n = w.shape[1] if n is None else n
    return pl.pallas_call(
        _mm_kernel,
        grid=(m // tm, n // tn),
        in_specs=[pl.BlockSpec((tm, k), lambda i, j: (i, 0)),
                  pl.BlockSpec((k, tn), lambda i, j: (0, j))],
        out_specs=pl.BlockSpec((tm, tn), lambda i, j: (i, j)),
        out_shape=jax.ShapeDtypeStruct((m, n), out_dtype),
        compiler_params=_cparams(2),
        name=name,
    )(a, w)


def _mm_acc_kernel(a_ref, w_ref, o_ref, acc_ref):
    kk = pl.program_id(2)
    last = pl.num_programs(2) - 1
    part = jnp.dot(a_ref[...], w_ref[...], preferred_element_type=F32)

    @pl.when(kk == 0)
    def _():
        acc_ref[...] = part

    @pl.when((kk > 0) & (kk < last))
    def _():
        acc_ref[...] = acc_ref[...] + part

    @pl.when(kk == last)
    def _():
        o_ref[...] = (acc_ref[...] + part).astype(o_ref.dtype)


def _dot_wt(a, wt):
    return lax.dot_general(a, wt.astype(BF16), (((1,), (1,)), ((), ())), preferred_element_type=F32)


def _mm_wt_kernel(a_ref, wt_ref, o_ref):
    o_ref[...] = _dot_wt(a_ref[...], wt_ref[...]).astype(o_ref.dtype)


def _matmul_wt(a, wt, row0, n, out_dtype, tm, tn, name):
    m, k = a.shape
    return pl.pallas_call(
        _mm_wt_kernel,
        grid=(m // tm, n // tn),
        in_specs=[pl.BlockSpec((tm, k), lambda i, j: (i, 0)),
                  pl.BlockSpec((pl.Element(tn), pl.Element(k)),
                               lambda i, j: (pl.multiple_of(row0 + j * tn, 8), 0))],
        out_specs=pl.BlockSpec((tm, tn), lambda i, j: (i, j)),
        out_shape=jax.ShapeDtypeStruct((m, n), out_dtype),
        compiler_params=_cparams(2),
        name=name,
    )(a, wt)


def _mm_wt_ada_kernel(a_ref, wt_ref, cs_ref, wada_ref, bada_ref, o_ref, mod_ref):
    o_ref[...] = _dot_wt(a_ref[...], wt_ref[...]).astype(o_ref.dtype)
    mod_ref[...] = _col_matvec(cs_ref[...], wada_ref, bada_ref)


def _matmul_wt_ada(a, wt, row0, n, out_dtype, tm, tn, cs, w_ada, b_ada, col0, ncols, name):
    m, k = a.shape
    ni, nj = m // tm, n // tn
    cps = LANES * -(-(ncols // LANES) // (ni * nj))
    assert ncols % cps == 0 and col0 % cps == 0 and ncols // cps <= ni * nj
    nblk = ncols // cps
    blk = lambda i, j: jnp.minimum(i * nj + j, nblk - 1)
    return pl.pallas_call(
        _mm_wt_ada_kernel,
        grid=(ni, nj),
        in_specs=[pl.BlockSpec((tm, k), lambda i, j: (i, 0)),
                  pl.BlockSpec((pl.Element(tn), pl.Element(k)),
                               lambda i, j: (pl.multiple_of(row0 + j * tn, 8), 0)),
                  pl.BlockSpec((D_MODEL, 1), lambda i, j: (0, 0)),
                  pl.BlockSpec((D_MODEL, cps), lambda i, j: (0, col0 // cps + blk(i, j))),
                  pl.BlockSpec((1, cps), lambda i, j: (0, col0 // cps + blk(i, j)))],
        out_specs=[pl.BlockSpec((tm, tn), lambda i, j: (i, j)),
                   pl.BlockSpec((1, cps), lambda i, j: (0, blk(i, j)))],
        out_shape=[jax.ShapeDtypeStruct((m, n), out_dtype), jax.ShapeDtypeStruct((1, ncols), F32)],
        compiler_params=_cparams(2),
        name=name,
    )(a, wt, cs, w_ada, b_ada)


def _matmul_ksplit(a, w, out_dtype, tm, tn, tk, name):
    m, k = a.shape
    n = w.shape[1]
    assert k % tk == 0 and k // tk >= 2
    return pl.pallas_call(
        _mm_acc_kernel,
        grid=(m // tm, n // tn, k // tk),
        in_specs=[pl.BlockSpec((tm, tk), lambda i, j, kk: (i, kk)),
                  pl.BlockSpec((tk, tn), lambda i, j, kk: (kk, j))],
        out_specs=pl.BlockSpec((tm, tn), lambda i, j, kk: (i, j)),
        out_shape=jax.ShapeDtypeStruct((m, n), out_dtype),
        scratch_shapes=[pltpu.VMEM((tm, tn), F32)],
        compiler_params=_cparams(3),
        name=name,
    )(a, w)


def _merge_kernel(oa_ref, ob_ref, wa_ref, wb_ref, ga_ref, gb_ref, o_ref):
    ya = jnp.dot(oa_ref[...], wa_ref[...].astype(BF16), preferred_element_type=F32)
    yb = jnp.dot(ob_ref[...], wb_ref[...].astype(BF16), preferred_element_type=F32)
    m = _sigmoid(ga_ref[...].astype(F32)) * ya + _sigmoid(gb_ref[...].astype(F32)) * yb
    o_ref[...] = m.astype(o_ref.dtype)


def _merge(o_a, o_b, w_up_a, w_up_b, gates, tm, tn):
    L = o_a.shape[0]
    nb = D_MODEL // tn
    return pl.pallas_call(
        _merge_kernel,
        grid=(L // tm, nb),
        in_specs=[pl.BlockSpec((tm, D_RWKV), lambda i, j: (i, 0)),
                  pl.BlockSpec((tm, D_S5), lambda i, j: (i, 0)),
                  pl.BlockSpec((D_RWKV, tn), lambda i, j: (0, j)),
                  pl.BlockSpec((D_S5, tn), lambda i, j: (0, j)),
                  pl.BlockSpec((tm, tn), lambda i, j: (i, j)),
                  pl.BlockSpec((tm, tn), lambda i, j: (i, j + nb))],
        out_specs=pl.BlockSpec((tm, tn), lambda i, j: (i, j)),
        out_shape=jax.ShapeDtypeStruct((L, D_MODEL), BF16),
        compiler_params=_cparams(2),
        name="merge",
    )(o_a, o_b, w_up_a, w_up_b, gates, gates)


def _ffn_up_kernel(x_ref, wg_ref, wu_ref, wd_ref, o_ref, wd_bf_ref):
    x = x_ref[...]
    a = jnp.dot(x, wg_ref[...].astype(BF16), preferred_element_type=F32)
    b = jnp.dot(x, wu_ref[...].astype(BF16), preferred_element_type=F32)
    o_ref[...] = ((a * _sigmoid(a)) * b).astype(o_ref.dtype)
    wd_bf_ref[...] = wd_ref[...].astype(BF16)


def _ffn_up(xf, w_gate, w_up, w_down, tm, tn):
    L = xf.shape[0]
    n = w_gate.shape[1]
    ni, nj = L // tm, n // tn
    slab = w_down.shape[0] // (ni * nj)
    assert slab * ni * nj == w_down.shape[0] and slab % 16 == 0
    wd_spec = lambda: pl.BlockSpec((slab, D_MODEL), lambda i, j: (i * nj + j, 0))
    return pl.pallas_call(
        _ffn_up_kernel,
        grid=(ni, nj),
        in_specs=[pl.BlockSpec((tm, D_MODEL), lambda i, j: (i, 0), pipeline_mode=pl.Buffered(1)),
                  pl.BlockSpec((D_MODEL, tn), lambda i, j: (0, j)),
                  pl.BlockSpec((D_MODEL, tn), lambda i, j: (0, j)),
                  wd_spec()],
        out_specs=[pl.BlockSpec((tm, tn), lambda i, j: (i, j)), wd_spec()],
        out_shape=[jax.ShapeDtypeStruct((L, n), BF16), jax.ShapeDtypeStruct(w_down.shape, BF16)],
        compiler_params=_cparams(2),
        name="ffn_up",
    )(xf, w_gate, w_up, w_down)


def _head_kernel(x_ref, gpre_ref, sh_ref, sc_ref, wl_ref, mu_ref, w0_ref, a0_ref, w2_ref, a2_ref,
                 g2_ref, xm_ref, lw_ref, a_ref, g_ref, wl_bf_ref, carry_ref):
    first = pl.program_id(0) == 0

    @pl.when(first)
    def _():
        wl_bf_ref[...] = wl_ref[...].astype(BF16)
        carry_ref[...] = jnp.zeros_like(carry_ref)

    y = _rms(x_ref[...]) * gpre_ref[...]
    xm = (y * (1.0 + sc_ref[...]) + sh_ref[...]).astype(BF16)
    xm_ref[...] = xm
    p = lax.dot_general(xm, wl_bf_ref[...], (((1,), (1,)), ((), ())), preferred_element_type=F32)
    ps = _token_shift(p, carry_ref[...], mu_ref[...], first)
    carry_ref[...] = p[p.shape[0] - 8:, :]
    xw = ps[:, :LORA_W]
    xa = ps[:, LORA_W:LORA_W + LORA_A]
    xg = ps[:, LORA_W + LORA_A:]
    z = -(w0_ref[...] + _dot(jnp.tanh(xw), w2_ref[...]))
    softplus = jnp.maximum(z, 0.0) + jnp.log(1.0 + jnp.exp(-jnp.abs(z)))
    lw_ref[...] = -jnp.exp(-softplus - 0.5)
    a_ref[...] = _sigmoid(a0_ref[...] + _dot(xa, a2_ref[...]))
    g_ref[...] = _dot(_sigmoid(xg), g2_ref[...])


def _head(x, gpre, mod, wt, row_lora, mu, w0, a0, w2, a2, g2, tm):
    L = x.shape[0]
    nl = mu.shape[1]
    full = lambda shape: pl.BlockSpec(shape, lambda i: (0, 0))
    rows = lambda width: pl.BlockSpec((tm, width), lambda i: (i, 0))
    return pl.pallas_call(
        _head_kernel,
        grid=(L // tm,),
        in_specs=[rows(D_MODEL), full((1, D_MODEL)),
                  pl.BlockSpec((1, D_MODEL), lambda i: (0, 0)), pl.BlockSpec((1, D_MODEL), lambda i: (0, 1)),
                  pl.BlockSpec((pl.Element(nl), pl.Element(D_MODEL)), lambda i: (row_lora, 0),
                               pipeline_mode=pl.Buffered(1)),
                  full((1, nl)), full((1, D_RWKV)), full((1, D_RWKV)),
                  full((LORA_W, D_RWKV)), full((LORA_A, D_RWKV)), full((LORA_G_PAD, D_RWKV))],
        out_specs=[rows(D_MODEL), rows(D_RWKV), rows(D_RWKV), rows(D_RWKV)],
        out_shape=[jax.ShapeDtypeStruct((L, D_MODEL), BF16)] + [jax.ShapeDtypeStruct((L, D_RWKV), F32)] * 3,
        scratch_shapes=[pltpu.VMEM((nl, D_MODEL), BF16), pltpu.VMEM((8, nl), F32)],
        compiler_params=_cparams(1),
        name="head",
    )(x, gpre, mod, mod, wt, mu, w0, a0, w2, a2, g2)


def _rwkv_parallel(r, k, v, am, bm, lw, cst, T):
    S = 2 * T
    n = r.shape[0] // T
    lane_lo, tri_t, strict, incl2, eye, same_head = cst
    cr = range(n)

    def chunk(x, c):
        return x[c * T:(c + 1) * T]

    def stack_own(xc):
        return jnp.concatenate([jnp.where(lane_lo, xc, 0.0), jnp.where(lane_lo, 0.0, xc)], axis=0)

    def stack_dup(xc):
        return jnp.concatenate([xc, xc], axis=0)

    c_all = _dot_f32_rhs(tri_t, jnp.concatenate([chunk(lw, c) for c in cr], axis=1))
    yield
    cu = [c_all[:, c * LANES:(c + 1) * LANES] for c in cr]
    c_end = [cu[c][T - 1:T, :] for c in cr]
    e_neg = [jnp.exp(-cu[c]) for c in cr]
    e_end = [jnp.exp(c_end[c] - cu[c]) for c in cr]
    rh = [stack_own(chunk(r, c) * jnp.exp(cu[c])) for c in cr]
    ah = [stack_own(chunk(am, c) * jnp.exp(cu[c] - chunk(lw, c))) for c in cr]
    bh = [stack_dup(chunk(bm, c) * e_neg[c]) for c in cr]
    kh = [stack_dup(chunk(k, c) * e_neg[c]) for c in cr]
    bc = [stack_own(chunk(bm, c) * e_end[c]) for c in cr]
    kc = [stack_own(chunk(k, c) * e_end[c]) for c in cr]
    v_s = [stack_dup(chunk(v, c)) for c in cr]

    big = [_dot_nt(jnp.concatenate([ah[c], rh[c]], axis=0), jnp.concatenate([bh[c], kh[c]], axis=0))
           for c in cr]
    yield
    nj = [jnp.where(strict, big[c][:S, :S], 0.0) for c in cr]
    m_r = [jnp.where(incl2, big[c][S:, :], 0.0) for c in cr]
    av = [_dot(jnp.where(strict, big[c][:S, S:], 0.0), v_s[c]) for c in cr]
    yield

    steps = T.bit_length() - 1
    p = [jnp.where(eye, 1.0, 0.0) + nj[c] for c in cr]
    nj = [_dot(nj[c], nj[c]) for c in cr]
    yield
    for j in range(1, steps):
        if j + 1 < steps:
            res = [_dot(nj[c], jnp.concatenate([p[c], nj[c]], axis=1)) for c in cr]
            p = [p[c] + res[c][:, :S] for c in cr]
            nj = [res[c][:, S:] for c in cr]
        else:
            p = [p[c] + _dot(nj[c], p[c]) for c in cr]
        yield
    x = [_dot(p[c], jnp.concatenate([ah[c], av[c]], axis=1)) for c in cr]
    yield

    wv = [jnp.concatenate([x[c], jnp.concatenate([jnp.zeros_like(v_s[c]), v_s[c]], axis=1)], axis=0)
          for c in cr]
    y = [_dot(m_r[c], wv[c]) for c in cr]
    q = [rh[c] + y[c][:, :LANES] for c in cr]
    o0 = [y[c][:, LANES:] for c in cr]
    gd = [_dot_tn(jnp.concatenate([bc[c], kc[c]], axis=0), wv[c]) for c in cr]
    gd = [jnp.concatenate([gd[c][:, :LANES], jnp.where(same_head, gd[c][:, LANES:], 0.0)], axis=1)
          for c in cr]
    decay = [jnp.broadcast_to(jnp.sum(jnp.where(eye, jnp.exp(c_end[c]), 0.0), axis=1, keepdims=True),
                              (LANES, LANES)) for c in cr]
    return gd, q, o0, decay


def _rwkv_kernel(r_ref, k_ref, v_ref, rh_ref, kh_ref, vh_ref, mur_ref, muk_ref, muv_ref,
                 lw_ref, a_ref, kk_ref, ka_ref, rk_ref, g_ref, lnw_ref, lnb_ref,
                 o_ref, h_ref, gd_ref, q_ref, o0_ref, dec_ref, bonus_ref, *, nblk):
    s = pl.program_id(0)
    last = pl.num_programs(0) - 2
    first_cur = lax.rem(jnp.minimum(s, last), nblk) == 0
    first_prv = lax.rem(jnp.maximum(s - 1, 0), nblk) == 0

    @pl.when(s == 0)
    def _():
        h_ref[...] = jnp.zeros_like(h_ref)
        gd_ref[...] = jnp.zeros_like(gd_ref)
        q_ref[...] = jnp.zeros_like(q_ref)
        o0_ref[...] = jnp.zeros_like(o0_ref)
        dec_ref[...] = jnp.zeros_like(dec_ref)
        bonus_ref[...] = jnp.zeros_like(bonus_ref)

    rows = r_ref.shape[0]
    T = RWKV_T
    S = 2 * T
    n = rows // T
    ii = lax.broadcasted_iota(jnp.int32, (S, S), 0)
    jj = lax.broadcasted_iota(jnp.int32, (S, S), 1)
    same = (ii // T) == (jj // T)
    strict = same & (jj < ii)
    incl = same & (jj <= ii)
    eye = ii == jj
    tri_incl = jnp.where(incl, 1.0, 0.0).astype(BF16)
    ii2 = lax.broadcasted_iota(jnp.int32, (S, 2 * S), 0)
    jj2 = lax.broadcasted_iota(jnp.int32, (S, 2 * S), 1) % S
    incl2 = ((ii2 // T) == (jj2 // T)) & (jj2 <= ii2)
    hi_ = lax.broadcasted_iota(jnp.int32, (2 * LANES, 2 * LANES), 0) // RWKV_HEAD
    hj_ = lax.broadcasted_iota(jnp.int32, (2 * LANES, 2 * LANES), 1) // RWKV_HEAD
    head_sum = jnp.where(hi_ == hj_, 1.0, 0.0).astype(BF16)
    lane_lo = lax.broadcasted_iota(jnp.int32, (T, LANES), 1) < RWKV_HEAD
    it = lax.broadcasted_iota(jnp.int32, (T, T), 0)
    jt = lax.broadcasted_iota(jnp.int32, (T, T), 1)
    tri_t = jnp.where(jt <= it, 1.0, 0.0).astype(BF16)
    same_head = ((lax.broadcasted_iota(jnp.int32, (LANES, LANES), 0) // RWKV_HEAD)
                 == (lax.broadcasted_iota(jnp.int32, (LANES, LANES), 1) // RWKV_HEAD))
    cst = (lane_lo, tri_t, strict, incl2, eye, same_head)

    state = [jnp.where(first_prv, 0.0, h_ref[...])]
    outs = []

    def serial_step(c):
        h = state[0]
        gd = gd_ref[c]
        z = _dot(jnp.concatenate([gd[:, :LANES], q_ref[c]], axis=0), h)
        o_s = z[LANES:] + o0_ref[c]
        outs.append(jnp.where(lane_lo, o_s[:T], o_s[T:]))
        state[0] = dec_ref[c] * h + z[:LANES] + gd[:, LANES:]

    def serial_finish():
        h_ref[...] = state[0]
        o = jnp.concatenate(outs, axis=0) if n > 1 else outs[0]
        inv = 1.0 / RWKV_HEAD
        mean = _head_sums(o, head_sum) * inv
        d = o - mean
        var = _head_sums(d * d, head_sum) * inv
        y = d * lax.rsqrt(var + GN_EPS) * lnw_ref[...] + lnb_ref[...]
        o_ref[...] = ((y + bonus_ref[...]) * g_ref[...]).astype(o_ref.dtype)

    r = _token_shift(r_ref[...], rh_ref[...], mur_ref[...], first_cur)
    k = _token_shift(k_ref[...], kh_ref[...], muk_ref[...], first_cur)
    v = _token_shift(v_ref[...], vh_ref[...], muv_ref[...], first_cur)
    a = a_ref[...]
    kk = k * kk_ref[...]
    k = k * (1.0 + (a - 1.0) * ka_ref[...])
    serial_step(0)
    ss = _head_sums(kk * kk, head_sum)
    kk = kk * lax.rsqrt(jnp.maximum(ss, 1e-24))
    bonus = _head_sums(r * k * rk_ref[...], head_sum) * v

    gen = _rwkv_parallel(r, k, v, -kk, kk * a, lw_ref[...], cst, T)
    done = 1
    while True:
        try:
            next(gen)
        except StopIteration as stop:
            gd, q, o0, decay = stop.value
            break
        if done < n:
            serial_step(done)
        elif done == n:
            serial_finish()
        done += 1
    assert done > n, "fewer parallel stages than chunks per block"

    for c in range(n):
        gd_ref[c] = gd[c]
        q_ref[c] = q[c]
        o0_ref[c] = o0[c]
        dec_ref[c] = decay[c]
    bonus_ref[...] = bonus


def _rwkv(proj, mu, lw, a, g, k_k, k_a, r_k, ln_w, ln_b, rows):
    L = proj.shape[0]
    nb = D_RWKV // LANES
    nblk = L // rows
    nsteps = PAIRS * nblk
    n = rows // RWKV_T

    def cur(s):
        b = jnp.minimum(s, nsteps - 1)
        return lax.div(b, nblk), lax.rem(b, nblk)

    def prv(s):
        b = jnp.maximum(s - 1, 0)
        return lax.div(b, nblk), lax.rem(b, nblk)

    def col(off):
        return pl.BlockSpec((rows, LANES), lambda s: (cur(s)[1], cur(s)[0] + off))

    def halo(off):
        return pl.BlockSpec((8, LANES),
                            lambda s: (jnp.maximum(cur(s)[1] * (rows // 8) - 1, 0), cur(s)[0] + off))

    def vec(off):
        return pl.BlockSpec((1, LANES), lambda s: (0, cur(s)[0] + off))

    def lag_col():
        return pl.BlockSpec((rows, LANES), lambda s: (prv(s)[1], prv(s)[0]))

    def lag_vec():
        return pl.BlockSpec((1, LANES), lambda s: (0, prv(s)[0]))

    return pl.pallas_call(
        functools.partial(_rwkv_kernel, nblk=nblk),
        grid=(nsteps + 1,),
        in_specs=[col(0), col(nb), col(2 * nb), halo(0), halo(nb), halo(2 * nb),
                  vec(0), vec(nb), vec(2 * nb),
                  col(0), col(0), vec(0), vec(0), vec(0),
                  lag_col(), lag_vec(), lag_vec()],
        out_specs=lag_col(),
        out_shape=jax.ShapeDtypeStruct((L, D_RWKV), BF16),
        scratch_shapes=[pltpu.VMEM((LANES, LANES), F32),
                        pltpu.VMEM((n, LANES, 2 * LANES), F32),
                        pltpu.VMEM((n, 2 * RWKV_T, LANES), F32),
                        pltpu.VMEM((n, 2 * RWKV_T, LANES), F32),
                        pltpu.VMEM((n, LANES, LANES), F32),
                        pltpu.VMEM((rows, LANES), F32)],
        compiler_params=_cparams(1),
        name="rwkv",
    )(proj, proj, proj, proj, proj, proj, mu, mu, mu, lw, a, k_k, k_a, r_k, g, ln_w, ln_b)


def _s5_table_kernel(lr_ref, li_ref, ldt_ref, pos_re, pos_im, neg_re, neg_im, z_re, z_im,
                     lbc_re, lbc_im):
    lr = lr_ref[...]
    li = li_ref[...]
    dt = jnp.exp(ldt_ref[...])
    centre = pos_re.shape[0] // 2
    t = (lax.broadcasted_iota(jnp.int32, pos_re.shape, 0) - centre).astype(F32)
    mag = jnp.exp(t * (lr * dt))
    ang = t * (li * dt)
    cs, sn = jnp.cos(ang), jnp.sin(ang)
    pos_re[...] = mag * cs
    pos_im[...] = mag * sn
    inv = 1.0 / mag
    neg_re[...] = inv * cs
    neg_im[...] = -(inv * sn)
    mc = jnp.exp((centre + 1.0) * (lr * dt))
    lbc_re[...] = mc * jnp.cos((centre + 1.0) * (li * dt))
    lbc_im[...] = mc * jnp.sin((centre + 1.0) * (li * dt))
    m1 = jnp.exp(lr * dt)
    lb_re = m1 * jnp.cos(li * dt)
    lb_im = m1 * jnp.sin(li * dt)
    den = lr * lr + li * li
    z_re[...] = ((lb_re - 1.0) * lr + lb_im * li) / den
    z_im[...] = (lb_im * lr - (lb_re - 1.0) * li) / den


def _s5_tables(lr, li, ldt, rows):
    n = lr.shape[1]
    full = pl.BlockSpec((1, n), lambda: (0, 0))
    tab = pl.BlockSpec((rows, n), lambda: (0, 0))
    return pl.pallas_call(
        _s5_table_kernel,
        in_specs=[full, full, full],
        out_specs=[tab, tab, tab, tab, full, full, full, full],
        out_shape=[jax.ShapeDtypeStruct((rows, n), F32)] * 4 + [jax.ShapeDtypeStruct((1, n), F32)] * 4,
        name="s5_tables",
    )(lr, li, ldt)


def _s5_kernel(u_ref, pos_re_ref, pos_im_ref, neg_re_ref, neg_im_ref, z_re_ref, z_im_ref,
               lbc_re_ref, lbc_im_ref, b_re_ref, b_im_ref, c_re_ref, c_im_ref, d_ref, wglu_ref,
               bglu_ref, o_ref, bb_ref, st_ref):
    R = u_ref.shape[0]
    T = pos_re_ref.shape[0]
    NS = S5_GROUPS * S5_STATE
    BS = NS // S5_BLOCKS
    BC = D_S5 // S5_BLOCKS

    @pl.when(pl.program_id(0) == 0)
    def _():
        st_ref[...] = jnp.zeros_like(st_ref)
        for j in range(S5_BLOCKS):
            zr = z_re_ref[:, j * BS:(j + 1) * BS]
            zi = z_im_ref[:, j * BS:(j + 1) * BS]
            br = b_re_ref[j]
            bi = b_im_ref[j]
            bb_ref[j, :, :BS] = (zr * br - zi * bi).astype(BF16)
            bb_ref[j, :, BS:] = (zr * bi + zi * br).astype(BF16)

    u = u_ref[...]
    ub = u.astype(BF16)
    ii = lax.broadcasted_iota(jnp.int32, (T, T), 0)
    jj = lax.broadcasted_iota(jnp.int32, (T, T), 1)
    tri = jnp.where(jj <= ii, 1.0, 0.0).astype(BF16)

    ys = []
    for j in range(S5_BLOCKS):
        sl = slice(j * BS, (j + 1) * BS)
        bu = jnp.dot(ub[:, j * BC:(j + 1) * BC], bb_ref[j], preferred_element_type=F32)
        nr, ni = neg_re_ref[:, sl], neg_im_ref[:, sl]
        pr, pi = pos_re_ref[:, sl], pos_im_ref[:, sl]
        lr_, li_ = lbc_re_ref[:, sl], lbc_im_ref[:, sl]
        s_re = st_ref[0:1, sl]
        s_im = st_ref[1:2, sl]
        xs_re, xs_im = [], []
        for t in range(R // T):
            bu_re, bu_im = bu[t * T:(t + 1) * T, :BS], bu[t * T:(t + 1) * T, BS:]
            zz = jnp.concatenate([nr * bu_re - ni * bu_im, nr * bu_im + ni * bu_re], axis=1)
            acc = jnp.dot(tri, zz.astype(BF16), preferred_element_type=F32)
            a_re = acc[:, :BS] + (lr_ * s_re - li_ * s_im)
            a_im = acc[:, BS:] + (lr_ * s_im + li_ * s_re)
            x_re = pr * a_re - pi * a_im
            x_im = pr * a_im + pi * a_re
            s_re, s_im = x_re[T - 1:T, :], x_im[T - 1:T, :]
            xs_re.append(x_re)
            xs_im.append(x_im)
        st_ref[0:1, sl] = s_re
        st_ref[1:2, sl] = s_im
        cat = lambda xs: jnp.concatenate(xs, axis=0) if len(xs) > 1 else xs[0]
        ys.append(_dot(cat(xs_re), c_re_ref[j]) - _dot(cat(xs_im), c_im_ref[j]))
    y = jnp.concatenate(ys, axis=1) + d_ref[...] * u
    y = 0.5 * y * (1.0 + jnp.tanh(math.sqrt(2.0 / math.pi) * (y + 0.044715 * (y * y * y))))
    zg = _dot(y, wglu_ref[...]) + bglu_ref[...]
    o_ref[...] = (zg[:, :D_S5] * _sigmoid(zg[:, D_S5:])).astype(o_ref.dtype)


def _s5(proj, tables, b_re, b_im, c_re, c_im, d, w_glu, b_glu, rows):
    L = proj.shape[0]
    NS = S5_GROUPS * S5_STATE
    BS = NS // S5_BLOCKS
    BC = D_S5 // S5_BLOCKS
    t = tables[0].shape[0]
    c2 = lambda shape: pl.BlockSpec(shape, lambda i: (0, 0))
    c3 = lambda shape: pl.BlockSpec(shape, lambda i: (0, 0, 0))
    return pl.pallas_call(
        _s5_kernel,
        grid=(L // rows,),
        in_specs=[pl.BlockSpec((rows, D_S5), lambda i: (i, 0)),
                  c2((t, NS)), c2((t, NS)), c2((t, NS)), c2((t, NS)),
                  c2((1, NS)), c2((1, NS)), c2((1, NS)), c2((1, NS)),
                  c3((S5_BLOCKS, BC, BS)), c3((S5_BLOCKS, BC, BS)),
                  c3((S5_BLOCKS, BS, BC)), c3((S5_BLOCKS, BS, BC)),
                  c2((1, D_S5)), c2((D_S5, 2 * D_S5)), c2((1, 2 * D_S5))],
        out_specs=pl.BlockSpec((rows, D_S5), lambda i: (i, 0)),
        out_shape=jax.ShapeDtypeStruct((L, D_S5), BF16),
        scratch_shapes=[pltpu.VMEM((S5_BLOCKS, BC, 2 * BS), BF16), pltpu.VMEM((8, NS), F32)],
        compiler_params=_cparams(1),
        name="s5",
    )(proj, *tables, b_re, b_im, c_re, c_im, d, w_glu, b_glu)


def _block_diag_groups(w):
    g, r, c = w.shape
    gb = g // S5_BLOCKS
    w = w.reshape(S5_BLOCKS, gb, r, c)
    eye = jnp.eye(gb, dtype=w.dtype)
    out = w[:, :, :, None, :] * eye[None, :, None, :, None]
    return out.reshape(S5_BLOCKS, gb * r, gb * c)


def kernel(x, c, w_ada, b_ada, norm_pre_mix, norm_post_mix, norm_pre_ffn, norm_post_ffn, w_in, rwkv_mu, rwkv_w0, rwkv_w2, rwkv_a0, rwkv_a2, rwkv_g2, rwkv_k_k, rwkv_k_a, rwkv_r_k, rwkv_ln_w, rwkv_ln_b, s5_lam_re, s5_lam_im, s5_log_dt, s5_b_re, s5_b_im, s5_c_re, s5_c_im, s5_d, s5_w_glu, s5_b_glu, w_up_rwkv, w_up_s5, w_out, ffn_w_gate, ffn_w_up, ffn_w_down):
    bsz, L, _ = x.shape
    assert bsz == 1 and w_ada.shape[0] == 1
    h = x.reshape(L, D_MODEL)
    tm = min(1024, L)
    tr = min(256, L)
    row = lambda v: v.reshape(1, -1)

    half = 3 * D_MODEL
    mod, cs = _ada(c.reshape(D_MODEL, 1), w_ada[0], row(b_ada[0]), half)

    wit = jnp.swapaxes(w_in[0], 0, 1)
    gpad = LORA_G_PAD - LORA_G
    n_rkv = 3 * D_RWKV
    mu = row(rwkv_mu[0])
    mu_lora = jnp.concatenate([rwkv_mu[0, n_rkv:], jnp.zeros((gpad,), F32)]).reshape(1, -1)
    g2 = jnp.concatenate([rwkv_g2[0], jnp.zeros((gpad, D_RWKV), F32)], axis=0).astype(BF16)

    xm, lw, a, g = _head(h, row(norm_pre_mix[0]), mod, wit, n_rkv, mu_lora, row(rwkv_w0[0]),
                         row(rwkv_a0[0]), rwkv_w2[0].astype(BF16), rwkv_a2[0].astype(BF16), g2, tr)
    proj_rkv = _matmul_wt(xm, wit, 0, n_rkv, F32, tm, 512, "proj_rkv")
    proj_u = _matmul_wt(xm, wit, RWKV_COLS, D_S5, F32, tm, 512, "proj_u")
    gates, mod2 = _matmul_wt_ada(xm, wit, RWKV_COLS + D_S5, 2 * D_MODEL, BF16, tm, 512,
                                 cs, w_ada[0], row(b_ada[0]), half, half, "proj_g")
    o_a = _rwkv(proj_rkv, mu, lw, a, g, row(rwkv_k_k[0]), row(rwkv_k_a[0]), row(rwkv_r_k[0]),
                row(rwkv_ln_w[0]), row(rwkv_ln_b[0]), min(RWKV_T * RWKV_CHUNKS_PER_STEP, L))

    rep = lambda v: jnp.repeat(v, S5_STATE).reshape(1, -1)
    tables = _s5_tables(row(s5_lam_re[0]), row(s5_lam_im[0]), rep(s5_log_dt[0]), S5_T)
    bt = lambda w: _block_diag_groups(jnp.swapaxes(w, 1, 2))
    o_b = _s5(proj_u, tables, bt(s5_b_re[0]), bt(s5_b_im[0]),
              _block_diag_groups(jnp.swapaxes(s5_c_re[0], 1, 2)).astype(BF16),
              _block_diag_groups(jnp.swapaxes(s5_c_im[0], 1, 2)).astype(BF16),
              row(s5_d[0]), s5_w_glu[0].astype(BF16), row(s5_b_glu[0]), min(S5_ROWS, L))

    merged = _merge(o_a, o_b, w_up_rwkv[0], w_up_s5[0], gates, tm, 512)
    mix = _matmul(merged, w_out[0], BF16, tm, 512, "w_out")

    h1, xf = _mid(h, mix, row(norm_post_mix[0]), row(norm_pre_ffn[0]), mod, mod2, tr)
    act, wd_bf = _ffn_up(xf, ffn_w_gate[0], ffn_w_up[0], ffn_w_down[0], min(2048, L), 256)
    ff = _matmul_ksplit(act, wd_bf, BF16, tm, 1024, D_FF // 2, "ffn_down")
    out = _final(h1, ff, row(norm_post_ffn[0]), mod2, tr)
    return out.reshape(bsz, L, D_MODEL)
```

```python
import functools
import math

import jax
import jax.numpy as jnp
from jax import lax
from jax.experimental import pallas as pl
from jax.experimental.pallas import tpu as pltpu

F32 = jnp.float32
BF16 = jnp.bfloat16

D_MODEL = 4096
RMS_EPS = 1e-6
D_RWKV = 2048
RWKV_HEAD = 64
LORA_W = 128
LORA_A = 128
LORA_G = 480
LORA_G_PAD = 512
GN_EPS = 64e-5
D_S5 = 1024
S5_GROUPS = 64
S5_GROUP_CH = 16
S5_STATE = 64
S5_BLOCKS = 4
D_FF = 11008

LANES = 128
PAIRS = D_RWKV // LANES
RWKV_T = 64
RWKV_CHUNKS_PER_STEP = 8
S5_T = 128
S5_ROWS = 256

RWKV_COLS = 3 * D_RWKV + LORA_W + LORA_A + LORA_G

VMEM_LIMIT = 56 * 1024 * 1024


def _cparams(n_axes, vmem=VMEM_LIMIT):
    return pltpu.CompilerParams(dimension_semantics=("arbitrary",) * n_axes, vmem_limit_bytes=vmem)


def _dot(a, b):
    return jnp.dot(a.astype(BF16), b.astype(BF16), preferred_element_type=F32)


def _dot_nt(a, b):
    return lax.dot_general(a.astype(BF16), b.astype(BF16), (((1,), (1,)), ((), ())),
                           preferred_element_type=F32)


def _dot_tn(a, b):
    return lax.dot_general(a.astype(BF16), b.astype(BF16), (((0,), (0,)), ((), ())),
                           preferred_element_type=F32)


def _split2(x):
    hi = x.astype(BF16)
    lo = (x - hi.astype(F32)).astype(BF16)
    return hi, lo


def _head_sums(x, sel2):
    half = x.shape[0] // 2
    hi, lo = _split2(jnp.concatenate([x[:half], x[half:]], axis=1))
    res = (jnp.dot(hi, sel2, preferred_element_type=F32) + jnp.dot(lo, sel2, preferred_element_type=F32))
    return jnp.concatenate([res[:, :LANES], res[:, LANES:]], axis=0)


def _dot_f32_rhs(sel, x):
    hi, lo = _split2(x)
    return (jnp.dot(sel, hi, preferred_element_type=F32) + jnp.dot(sel, lo, preferred_element_type=F32))


def _rms(x):
    return x * lax.rsqrt(jnp.mean(x * x, axis=-1, keepdims=True) + RMS_EPS)


def _sigmoid(x):
    return 1.0 / (1.0 + jnp.exp(-x))


def _token_shift(p, halo, mu, first):
    last = jnp.where(first, 0.0, halo[7:8, :])
    rolled = pltpu.roll(p, 1, axis=0)
    row = lax.broadcasted_iota(jnp.int32, p.shape, 0)
    prev = jnp.where(row == 0, last, rolled)
    return p + (prev - p) * mu


def _col_matvec(cs, w_ref, b_ref):
    tn = w_ref.shape[1]
    rows = 512
    acc = jnp.zeros((8, tn), F32)
    for k0 in range(0, D_MODEL, rows):
        blk = w_ref[k0:k0 + rows, :] * cs[k0:k0 + rows, :]
        acc = acc + jnp.sum(blk.reshape(rows // 8, 8, tn), axis=0)
    return jnp.sum(acc, axis=0, keepdims=True) + b_ref[...]


def _ada_kernel(c_ref, w_ref, b_ref, o_ref, cs_ref):
    c = c_ref[...]
    cs = c * _sigmoid(c)
    cs_ref[...] = cs
    o_ref[...] = _col_matvec(cs, w_ref, b_ref)


def _ada(c_col, w_ada, b_ada, n):
    tn = 512
    return pl.pallas_call(
        _ada_kernel,
        grid=(n // tn,),
        in_specs=[pl.BlockSpec((D_MODEL, 1), lambda j: (0, 0)),
                  pl.BlockSpec((D_MODEL, tn), lambda j: (0, j)),
                  pl.BlockSpec((1, tn), lambda j: (0, j))],
        out_specs=[pl.BlockSpec((1, tn), lambda j: (0, j)),
                   pl.BlockSpec((D_MODEL, 1), lambda j: (0, 0))],
        out_shape=[jax.ShapeDtypeStruct((1, n), F32), jax.ShapeDtypeStruct((D_MODEL, 1), F32)],
        compiler_params=_cparams(1),
        name="ada",
    )(c_col, w_ada, b_ada)


def _mid_kernel(x_ref, mix_ref, gpost_ref, gate_ref, gpre_ref, sh_ref, sc_ref, h_ref, xf_ref):
    h = x_ref[...] + gate_ref[...] * (_rms(mix_ref[...].astype(F32)) * gpost_ref[...])
    h_ref[...] = h
    y = _rms(h) * gpre_ref[...]
    xf_ref[...] = (y * (1.0 + sc_ref[...]) + sh_ref[...]).astype(xf_ref.dtype)


def _mid(x, mix, gpost, gpre, mod1, mod2, tm):
    L = x.shape[0]
    row = lambda i: (i, 0)
    vec = lambda k: pl.BlockSpec((1, D_MODEL), lambda i: (0, k))
    return pl.pallas_call(
        _mid_kernel,
        grid=(L // tm,),
        in_specs=[pl.BlockSpec((tm, D_MODEL), row), pl.BlockSpec((tm, D_MODEL), row),
                  vec(0), vec(2), vec(0), vec(0), vec(1)],
        out_specs=[pl.BlockSpec((tm, D_MODEL), row), pl.BlockSpec((tm, D_MODEL), row)],
        out_shape=[jax.ShapeDtypeStruct((L, D_MODEL), F32), jax.ShapeDtypeStruct((L, D_MODEL), BF16)],
        compiler_params=_cparams(1),
        name="mid",
    )(x, mix, gpost, mod1, gpre, mod2, mod2)


def _final_kernel(h_ref, ff_ref, gpost_ref, gate_ref, o_ref):
    o_ref[...] = h_ref[...] + gate_ref[...] * (_rms(ff_ref[...].astype(F32)) * gpost_ref[...])


def _final(h, ff, gpost, mod, tm):
    L = h.shape[0]
    row = lambda i: (i, 0)
    return pl.pallas_call(
        _final_kernel,
        grid=(L // tm,),
        in_specs=[pl.BlockSpec((tm, D_MODEL), row), pl.BlockSpec((tm, D_MODEL), row),
                  pl.BlockSpec((1, D_MODEL), lambda i: (0, 0)),
                  pl.BlockSpec((1, D_MODEL), lambda i: (0, 2))],
        out_specs=pl.BlockSpec((tm, D_MODEL), row),
        out_shape=jax.ShapeDtypeStruct((L, D_MODEL), F32),
        compiler_params=_cparams(1),
        name="final",
    )(h, ff, gpost, mod)


def _mm_kernel(a_ref, w_ref, o_ref):
    o_ref[...] = jnp.dot(a_ref[...], w_ref[...].astype(BF16),
                         preferred_element_type=F32).astype(o_ref.dtype)


def _matmul(a, w, out_dtype, tm, tn, name, n=None):
    m, k = a.shape
    n = w.shape[1] if n is None else n
    return pl.pallas_call(
        _mm_kernel,
        grid=(m // tm, n // tn),
        in_specs=[pl.BlockSpec((tm, k), lambda i, j: (i, 0)),
                  pl.BlockSpec((k, tn), lambda i, j: (0, j))],
        out_specs=pl.BlockSpec((tm, tn), lambda i, j: (i, j)),
        out_shape=jax.ShapeDtypeStruct((m, n), out_dtype),
        compiler_params=_cparams(2),
        name=name,
    )(a, w)


def _mm_acc_kernel(a_ref, w_ref, o_ref, acc_ref):
    kk = pl.program_id(2)
    last = pl.num_programs(2) - 1
    part = jnp.dot(a_ref[...], w_ref[...], preferred_element_type=F32)

    @pl.when(kk == 0)
    def _():
        acc_ref[...] = part

    @pl.when((kk > 0) & (kk < last))
    def _():
        acc_ref[...] = acc_ref[...] + part

    @pl.when(kk == last)
    def _():
        o_ref[...] = (acc_ref[...] + part).astype(o_ref.dtype)


def _dot_wt(a, wt):
    return lax.dot_general(a, wt.astype(BF16), (((1,), (1,)), ((), ())), preferred_element_type=F32)


def _mm_wt_kernel(a_ref, wt_ref, o_ref):
    o_ref[...] = _dot_wt(a_ref[...], wt_ref[...]).astype(o_ref.dtype)


def _matmul_wt(a, wt, row0, n, out_dtype, tm, tn, name):
    m, k = a.shape
    return pl.pallas_call(
        _mm_wt_kernel,
        grid=(m // tm, n // tn),
        in_specs=[pl.BlockSpec((tm, k), lambda i, j: (i, 0)),
                  pl.BlockSpec((pl.Element(tn), pl.Element(k)),
                               lambda i, j: (pl.multiple_of(row0 + j * tn, 8), 0))],
        out_specs=pl.BlockSpec((tm, tn), lambda i, j: (i, j)),
        out_shape=jax.ShapeDtypeStruct((m, n), out_dtype),
        compiler_params=_cparams(2),
        name=name,
    )(a, wt)


def _mm_wt_ada_kernel(a_ref, wt_ref, cs_ref, wada_ref, bada_ref, o_ref, mod_ref):
    o_ref[...] = _dot_wt(a_ref[...], wt_ref[...]).astype(o_ref.dtype)
    mod_ref[...] = _col_matvec(cs_ref[...], wada_ref, bada_ref)


def _matmul_wt_ada(a, wt, row0, n, out_dtype, tm, tn, cs, w_ada, b_ada, col0, ncols, name):
    m, k = a.shape
    ni, nj = m // tm, n // tn
    cps = LANES * -(-(ncols // LANES) // (ni * nj))
    assert ncols % cps == 0 and col0 % cps == 0 and ncols // cps <= ni * nj
    nblk = ncols // cps
    blk = lambda i, j: jnp.minimum(i * nj + j, nblk - 1)
    return pl.pallas_call(
        _mm_wt_ada_kernel,
        grid=(ni, nj),
        in_specs=[pl.BlockSpec((tm, k), lambda i, j: (i, 0)),
                  pl.BlockSpec((pl.Element(tn), pl.Element(k)),
                               lambda i, j: (pl.multiple_of(row0 + j * tn, 8), 0)),
                  pl.BlockSpec((D_MODEL, 1), lambda i, j: (0, 0)),
                  pl.BlockSpec((D_MODEL, cps), lambda i, j: (0, col0 // cps + blk(i, j))),
                  pl.BlockSpec((1, cps), lambda i, j: (0, col0 // cps + blk(i, j)))],
        out_specs=[pl.BlockSpec((tm, tn), lambda i, j: (i, j)),
                   pl.BlockSpec((1, cps), lambda i, j: (0, blk(i, j)))],
        out_shape=[jax.ShapeDtypeStruct((m, n), out_dtype), jax.ShapeDtypeStruct((1, ncols), F32)],
        compiler_params=_cparams(2),
        name=name,
    )(a, wt, cs, w_ada, b_ada)


def _matmul_ksplit(a, w, out_dtype, tm, tn, tk, name):
    m, k = a.shape
    n = w.shape[1]
    assert k % tk == 0 and k // tk >= 2
    return pl.pallas_call(
        _mm_acc_kernel,
        grid=(m // tm, n // tn, k // tk),
        in_specs=[pl.BlockSpec((tm, tk), lambda i, j, kk: (i, kk)),
                  pl.BlockSpec((tk, tn), lambda i, j, kk: (kk, j))],
        out_specs=pl.BlockSpec((tm, tn), lambda i, j, kk: (i, j)),
        out_shape=jax.ShapeDtypeStruct((m, n), out_dtype),
        scratch_shapes=[pltpu.VMEM((tm, tn), F32)],
        compiler_params=_cparams(3),
        name=name,
    )(a, w)


def _merge_kernel(oa_ref, ob_ref, wa_ref, wb_ref, ga_ref, gb_ref, o_ref):
    ya = jnp.dot(oa_ref[...], wa_ref[...].astype(BF16), preferred_element_type=F32)
    yb = jnp.dot(ob_ref[...], wb_ref[...].astype(BF16), preferred_element_type=F32)
    m = _sigmoid(ga_ref[...].astype(F32)) * ya + _sigmoid(gb_ref[...].astype(F32)) * yb
    o_ref[...] = m.astype(o_ref.dtype)


def _merge(o_a, o_b, w_up_a, w_up_b, gates, tm, tn):
    L = o_a.shape[0]
    nb = D_MODEL // tn
    return pl.pallas_call(
        _merge_kernel,
        grid=(L // tm, nb),
        in_specs=[pl.BlockSpec((tm, D_RWKV), lambda i, j: (i, 0)),
                  pl.BlockSpec((tm, D_S5), lambda i, j: (i, 0)),
                  pl.BlockSpec((D_RWKV, tn), lambda i, j: (0, j)),
                  pl.BlockSpec((D_S5, tn), lambda i, j: (0, j)),
                  pl.BlockSpec((tm, tn), lambda i, j: (i, j)),
                  pl.BlockSpec((tm, tn), lambda i, j: (i, j + nb))],
        out_specs=pl.BlockSpec((tm, tn), lambda i, j: (i, j)),
        out_shape=jax.ShapeDtypeStruct((L, D_MODEL), BF16),
        compiler_params=_cparams(2),
        name="merge",
    )(o_a, o_b, w_up_a, w_up_b, gates, gates)


def _ffn_up_kernel(x_ref, wg_ref, wu_ref, wd_ref, o_ref, wd_bf_ref):
    x = x_ref[...]
    a = jnp.dot(x, wg_ref[...].astype(BF16), preferred_element_type=F32)
    b = jnp.dot(x, wu_ref[...].astype(BF16), preferred_element_type=F32)
    o_ref[...] = ((a * _sigmoid(a)) * b).astype(o_ref.dtype)
    wd_bf_ref[...] = wd_ref[...].astype(BF16)


def _ffn_up(xf, w_gate, w_up, w_down, tm, tn):
    L = xf.shape[0]
    n = w_gate.shape[1]
    ni, nj = L // tm, n // tn
    slab = w_down.shape[0] // (ni * nj)
    assert slab * ni * nj == w_down.shape[0] and slab % 16 == 0
    wd_spec = lambda: pl.BlockSpec((slab, D_MODEL), lambda i, j: (i * nj + j, 0))
    return pl.pallas_call(
        _ffn_up_kernel,
        grid=(ni, nj),
        in_specs=[pl.BlockSpec((tm, D_MODEL), lambda i, j: (i, 0), pipeline_mode=pl.Buffered(1)),
                  pl.BlockSpec((D_MODEL, tn), lambda i, j: (0, j)),
                  pl.BlockSpec((D_MODEL, tn), lambda i, j: (0, j)),
                  wd_spec()],
        out_specs=[pl.BlockSpec((tm, tn), lambda i, j: (i, j)), wd_spec()],
        out_shape=[jax.ShapeDtypeStruct((L, n), BF16), jax.ShapeDtypeStruct(w_down.shape, BF16)],
        compiler_params=_cparams(2),
        name="ffn_up",
    )(xf, w_gate, w_up, w_down)


def _head_kernel(x_ref, gpre_ref, sh_ref, sc_ref, wl_ref, mu_ref, w0_ref, a0_ref, w2_ref, a2_ref,
                 g2_ref, xm_ref, lw_ref, a_ref, g_ref, wl_bf_ref, carry_ref):
    first = pl.program_id(0) == 0

    @pl.when(first)
    def _():
        wl_bf_ref[...] = wl_ref[...].astype(BF16)
        carry_ref[...] = jnp.zeros_like(carry_ref)

    y = _rms(x_ref[...]) * gpre_ref[...]
    xm = (y * (1.0 + sc_ref[...]) + sh_ref[...]).astype(BF16)
    xm_ref[...] = xm
    p = lax.dot_general(xm, wl_bf_ref[...], (((1,), (1,)), ((), ())), preferred_element_type=F32)
    ps = _token_shift(p, carry_ref[...], mu_ref[...], first)
    carry_ref[...] = p[p.shape[0] - 8:, :]
    xw = ps[:, :LORA_W]
    xa = ps[:, LORA_W:LORA_W + LORA_A]
    xg = ps[:, LORA_W + LORA_A:]
    u = w0_ref[...] + _dot(jnp.tanh(xw), w2_ref[...])
    lw_ref[...] = -math.exp(-0.5) * _sigmoid(u)
    a_ref[...] = _sigmoid(a0_ref[...] + _dot(xa, a2_ref[...]))
    g_ref[...] = _dot(_sigmoid(xg), g2_ref[...])


def _head(x, gpre, mod, wt, row_lora, mu, w0, a0, w2, a2, g2, tm):
    L = x.shape[0]
    nl = mu.shape[1]
    full = lambda shape: pl.BlockSpec(shape, lambda i: (0, 0))
    rows = lambda width: pl.BlockSpec((tm, width), lambda i: (i, 0))
    return pl.pallas_call(
        _head_kernel,
        grid=(L // tm,),
        in_specs=[rows(D_MODEL), full((1, D_MODEL)),
                  pl.BlockSpec((1, D_MODEL), lambda i: (0, 0)), pl.BlockSpec((1, D_MODEL), lambda i: (0, 1)),
                  pl.BlockSpec((pl.Element(nl), pl.Element(D_MODEL)), lambda i: (row_lora, 0),
                               pipeline_mode=pl.Buffered(1)),
                  full((1, nl)), full((1, D_RWKV)), full((1, D_RWKV)),
                  full((LORA_W, D_RWKV)), full((LORA_A, D_RWKV)), full((LORA_G_PAD, D_RWKV))],
        out_specs=[rows(D_MODEL), rows(D_RWKV), rows(D_RWKV), rows(D_RWKV)],
        out_shape=[jax.ShapeDtypeStruct((L, D_MODEL), BF16)] + [jax.ShapeDtypeStruct((L, D_RWKV), F32)] * 3,
        scratch_shapes=[pltpu.VMEM((nl, D_MODEL), BF16), pltpu.VMEM((8, nl), F32)],
        compiler_params=_cparams(1),
        name="head",
    )(x, gpre, mod, mod, wt, mu, w0, a0, w2, a2, g2)


def _rwkv_prep(r, k, v, am, bm, lw, cst, T):
    n = r.shape[0] // T
    lane_lo, tri_t = cst[0], cst[1]

    def chunk(x, c):
        return x[c * T:(c + 1) * T]

    def stack_own(xc):
        return jnp.concatenate([jnp.where(lane_lo, xc, 0.0), jnp.where(lane_lo, 0.0, xc)], axis=0)

    c_all = _dot_f32_rhs(tri_t, jnp.concatenate([chunk(lw, c) for c in range(n)], axis=1))
    yield
    lhs, rhs, bk, vv, dec = [], [], [], [], []
    for c in range(n):
        cu = c_all[:, c * LANES:(c + 1) * LANES]
        c_end = cu[T - 1:T, :]
        e_neg = jnp.exp(-cu)
        e_end = jnp.exp(c_end - cu)
        rc, kc_, ac, bc_ = chunk(r, c), chunk(k, c), chunk(am, c), chunk(bm, c)
        lhs.append(jnp.concatenate([stack_own(ac * jnp.exp(cu - chunk(lw, c))),
                                    stack_own(rc * jnp.exp(cu))], axis=0).astype(BF16))
        rhs.append(jnp.concatenate([bc_ * e_neg, kc_ * e_neg], axis=0).astype(BF16))
        bk.append(jnp.concatenate([stack_own(bc_ * e_end), stack_own(kc_ * e_end)], axis=0).astype(BF16))
        vv.append(chunk(v, c).astype(BF16))
        dec.append(jnp.broadcast_to(jnp.exp(c_end), (8, LANES)))
        if c % 2 == 1:
            yield
    return lhs, rhs, bk, vv, dec


def _rwkv_core(lhs, rhs, bk, vv, dec, cst, T):
    S = 2 * T
    n = len(lhs)
    _, _, strict, incl2, eye, same_head = cst
    cr = range(n)
    dup = lambda x: jnp.concatenate([x, x], axis=0)

    big = [lax.dot_general(lhs[c], jnp.concatenate([dup(rhs[c][:T]), dup(rhs[c][T:])], axis=0),
                           (((1,), (1,)), ((), ())), preferred_element_type=F32) for c in cr]
    yield
    ah = [lhs[c][:S] for c in cr]
    rh = [lhs[c][S:].astype(F32) for c in cr]
    v_s = [dup(vv[c]) for c in cr]
    nj = [jnp.where(strict, big[c][:S, :S], 0.0) for c in cr]
    m_r = [jnp.where(incl2, big[c][S:, :], 0.0) for c in cr]
    av = [_dot(jnp.where(strict, big[c][:S, S:], 0.0), v_s[c]) for c in cr]
    yield

    steps = T.bit_length() - 1
    p = [jnp.where(eye, 1.0, 0.0) + nj[c] for c in cr]
    nj = [_dot(nj[c], nj[c]) for c in cr]
    yield
    for j in range(1, steps):
        if j + 1 < steps:
            res = [_dot(nj[c], jnp.concatenate([p[c], nj[c]], axis=1)) for c in cr]
            p = [p[c] + res[c][:, :S] for c in cr]
            nj = [res[c][:, S:] for c in cr]
        else:
            p = [p[c] + _dot(nj[c], p[c]) for c in cr]
        yield
    x = [_dot(p[c], jnp.concatenate([ah[c].astype(F32), av[c]], axis=1)) for c in cr]
    yield

    wv = [jnp.concatenate([x[c].astype(BF16),
                           jnp.concatenate([jnp.zeros_like(v_s[c]), v_s[c]], axis=1)], axis=0)
          for c in cr]
    y = [_dot(m_r[c], wv[c]) for c in cr]
    q = [rh[c] + y[c][:, :LANES] for c in cr]
    o0 = [y[c][:, LANES:] for c in cr]
    gd = [_dot_tn(bk[c], wv[c]) for c in cr]
    gd = [jnp.concatenate([gd[c][:, :LANES], jnp.where(same_head, gd[c][:, LANES:], 0.0)], axis=1)
          for c in cr]
    decay = [jnp.broadcast_to(jnp.sum(jnp.where(eye, dec[c][0:1, :], 0.0), axis=1, keepdims=True),
                              (LANES, LANES)) for c in cr]
    return gd, q, o0, decay


def _rwkv_kernel(r_ref, k_ref, v_ref, rh_ref, kh_ref, vh_ref, mur_ref, muk_ref, muv_ref,
                 lw_ref, a_ref, kk_ref, ka_ref, rk_ref, g_ref, lnw_ref, lnb_ref,
                 o_ref, h_ref, gd_ref, q_ref, o0_ref, dec_ref, bonus_ref,
                 lhs_ref, rhs_ref, bk_ref, vv_ref, decr_ref, *, nblk):
    s = pl.program_id(0)
    last = pl.num_programs(0) - 3
    first_cur = lax.rem(jnp.minimum(s, last), nblk) == 0
    first_out = lax.rem(jnp.maximum(s - 2, 0), nblk) == 0

    @pl.when(s == 0)
    def _():
        for ref in (h_ref, gd_ref, q_ref, o0_ref, dec_ref, bonus_ref,
                    lhs_ref, rhs_ref, bk_ref, vv_ref, decr_ref):
            ref[...] = jnp.zeros_like(ref)

    rows = r_ref.shape[0]
    T = RWKV_T
    S = 2 * T
    n = rows // T
    ii = lax.broadcasted_iota(jnp.int32, (S, S), 0)
    jj = lax.broadcasted_iota(jnp.int32, (S, S), 1)
    strict = ((ii // T) == (jj // T)) & (jj < ii)
    eye = ii == jj
    ii2 = lax.broadcasted_iota(jnp.int32, (S, 2 * S), 0)
    jj2 = lax.broadcasted_iota(jnp.int32, (S, 2 * S), 1) % S
    incl2 = ((ii2 // T) == (jj2 // T)) & (jj2 <= ii2)
    hi_ = lax.broadcasted_iota(jnp.int32, (2 * LANES, 2 * LANES), 0) // RWKV_HEAD
    hj_ = lax.broadcasted_iota(jnp.int32, (2 * LANES, 2 * LANES), 1) // RWKV_HEAD
    head_sum = jnp.where(hi_ == hj_, 1.0, 0.0).astype(BF16)
    lane_lo = lax.broadcasted_iota(jnp.int32, (T, LANES), 1) < RWKV_HEAD
    it = lax.broadcasted_iota(jnp.int32, (T, T), 0)
    jt = lax.broadcasted_iota(jnp.int32, (T, T), 1)
    tri_t = jnp.where(jt <= it, 1.0, 0.0).astype(BF16)
    same_head = ((lax.broadcasted_iota(jnp.int32, (LANES, LANES), 0) // RWKV_HEAD)
                 == (lax.broadcasted_iota(jnp.int32, (LANES, LANES), 1) // RWKV_HEAD))
    cst = (lane_lo, tri_t, strict, incl2, eye, same_head)
    slot = lax.rem(s, 2)

    state = [jnp.where(first_out, 0.0, h_ref[...])]
    outs = []

    def serial_step(c):
        h = state[0]
        gd = gd_ref[c]
        z = _dot(jnp.concatenate([gd[:, :LANES], q_ref[c]], axis=0), h)
        o_s = z[LANES:] + o0_ref[c]
        outs.append(jnp.where(lane_lo, o_s[:T], o_s[T:]))
        state[0] = dec_ref[c] * h + z[:LANES] + gd[:, LANES:]

    def serial_finish():
        h_ref[...] = state[0]
        o = jnp.concatenate(outs, axis=0) if n > 1 else outs[0]
        inv = 1.0 / RWKV_HEAD
        mean = _head_sums(o, head_sum) * inv
        d = o - mean
        var = _head_sums(d * d, head_sum) * inv
        y = d * lax.rsqrt(var + GN_EPS) * lnw_ref[...] + lnb_ref[...]
        o_ref[...] = ((y + bonus_ref[slot]) * g_ref[...]).astype(o_ref.dtype)

    core = _rwkv_core([lhs_ref[c] for c in range(n)], [rhs_ref[c] for c in range(n)],
                      [bk_ref[c] for c in range(n)], [vv_ref[c] for c in range(n)],
                      [decr_ref[c] for c in range(n)], cst, T)

    def prepare():
        r = _token_shift(r_ref[...], rh_ref[...], mur_ref[...], first_cur)
        k = _token_shift(k_ref[...], kh_ref[...], muk_ref[...], first_cur)
        v = _token_shift(v_ref[...], vh_ref[...], muv_ref[...], first_cur)
        a = a_ref[...]
        kk = k * kk_ref[...]
        k = k * (1.0 + (a - 1.0) * ka_ref[...])
        yield
        ss = _head_sums(kk * kk, head_sum)
        bonus = _head_sums(r * k * rk_ref[...], head_sum) * v
        yield
        kk = kk * lax.rsqrt(jnp.maximum(ss, 1e-24))
        operands = yield from _rwkv_prep(r, k, v, -kk, kk * a, lw_ref[...], cst, T)
        return operands + (bonus,)

    serial_step(0)
    prep = prepare()

    results = {}
    gens = {"core": core, "prep": prep}
    done = 1
    while gens:
        for name in ("core", "prep"):
            if name not in gens:
                continue
            try:
                next(gens[name])
            except StopIteration as stop:
                results[name] = stop.value
                del gens[name]
            if name == "core":
                if done < n:
                    serial_step(done)
                elif done == n:
                    serial_finish()
                done += 1
    assert done > n, "fewer matmul stages than chunks per block"

    gd, q, o0, decay = results["core"]
    lhs, rhs, bk, vv, dec, bonus = results["prep"]
    for c in range(n):
        gd_ref[c] = gd[c]
        q_ref[c] = q[c]
        o0_ref[c] = o0[c]
        dec_ref[c] = decay[c]
        lhs_ref[c] = lhs[c]
        rhs_ref[c] = rhs[c]
        bk_ref[c] = bk[c]
        vv_ref[c] = vv[c]
        decr_ref[c] = dec[c]
    bonus_ref[slot] = bonus


def _rwkv(proj, mu, lw, a, g, k_k, k_a, r_k, ln_w, ln_b, rows):
    L = proj.shape[0]
    nb = D_RWKV // LANES
    nblk = L // rows
    nsteps = PAIRS * nblk
    n = rows // RWKV_T

    def cur(s):
        b = jnp.minimum(s, nsteps - 1)
        return lax.div(b, nblk), lax.rem(b, nblk)

    def prv(s):
        b = jnp.maximum(s - 2, 0)
        return lax.div(b, nblk), lax.rem(b, nblk)

    def col(off):
        return pl.BlockSpec((rows, LANES), lambda s: (cur(s)[1], cur(s)[0] + off))

    def halo(off):
        return pl.BlockSpec((8, LANES),
                            lambda s: (jnp.maximum(cur(s)[1] * (rows // 8) - 1, 0), cur(s)[0] + off))

    def vec(off):
        return pl.BlockSpec((1, LANES), lambda s: (0, cur(s)[0] + off))

    def lag_col():
        return pl.BlockSpec((rows, LANES), lambda s: (prv(s)[1], prv(s)[0]))

    def lag_vec():
        return pl.BlockSpec((1, LANES), lambda s: (0, prv(s)[0]))

    return pl.pallas_call(
        functools.partial(_rwkv_kernel, nblk=nblk),
        grid=(nsteps + 2,),
        in_specs=[col(0), col(nb), col(2 * nb), halo(0), halo(nb), halo(2 * nb),
                  vec(0), vec(nb), vec(2 * nb),
                  col(0), col(0), vec(0), vec(0), vec(0),
                  lag_col(), lag_vec(), lag_vec()],
        out_specs=lag_col(),
        out_shape=jax.ShapeDtypeStruct((L, D_RWKV), BF16),
        scratch_shapes=[pltpu.VMEM((LANES, LANES), F32),
                        pltpu.VMEM((n, LANES, 2 * LANES), F32),
                        pltpu.VMEM((n, 2 * RWKV_T, LANES), F32),
                        pltpu.VMEM((n, 2 * RWKV_T, LANES), F32),
                        pltpu.VMEM((n, LANES, LANES), F32),
                        pltpu.VMEM((2, rows, LANES), F32),
                        pltpu.VMEM((n, 4 * RWKV_T, LANES), BF16),
                        pltpu.VMEM((n, 2 * RWKV_T, LANES), BF16),
                        pltpu.VMEM((n, 4 * RWKV_T, LANES), BF16),
                        pltpu.VMEM((n, RWKV_T, LANES), BF16),
                        pltpu.VMEM((n, 8, LANES), F32)],
        compiler_params=_cparams(1),
        name="rwkv",
    )(proj, proj, proj, proj, proj, proj, mu, mu, mu, lw, a, k_k, k_a, r_k, g, ln_w, ln_b)


def _s5_table_kernel(lr_ref, li_ref, ldt_ref, pos_re, pos_im, neg_re, neg_im, z_re, z_im,
                     lbc_re, lbc_im):
    lr = lr_ref[...]
    li = li_ref[...]
    dt = jnp.exp(ldt_ref[...])
    centre = pos_re.shape[0] // 2
    t = (lax.broadcasted_iota(jnp.int32, pos_re.shape, 0) - centre).astype(F32)
    mag = jnp.exp(t * (lr * dt))
    ang = t * (li * dt)
    cs, sn = jnp.cos(ang), jnp.sin(ang)
    pos_re[...] = mag * cs
    pos_im[...] = mag * sn
    inv = 1.0 / mag
    neg_re[...] = inv * cs
    neg_im[...] = -(inv * sn)
    mc = jnp.exp((centre + 1.0) * (lr * dt))
    lbc_re[...] = mc * jnp.cos((centre + 1.0) * (li * dt))
    lbc_im[...] = mc * jnp.sin((centre + 1.0) * (li * dt))
    m1 = jnp.exp(lr * dt)
    lb_re = m1 * jnp.cos(li * dt)
    lb_im = m1 * jnp.sin(li * dt)
    den = lr * lr + li * li
    z_re[...] = ((lb_re - 1.0) * lr + lb_im * li) / den
    z_im[...] = (lb_im * lr - (lb_re - 1.0) * li) / den


def _s5_tables(lr, li, ldt, rows):
    n = lr.shape[1]
    full = pl.BlockSpec((1, n), lambda: (0, 0))
    tab = pl.BlockSpec((rows, n), lambda: (0, 0))
    return pl.pallas_call(
        _s5_table_kernel,
        in_specs=[full, full, full],
        out_specs=[tab, tab, tab, tab, full, full, full, full],
        out_shape=[jax.ShapeDtypeStruct((rows, n), F32)] * 4 + [jax.ShapeDtypeStruct((1, n), F32)] * 4,
        name="s5_tables",
    )(lr, li, ldt)


def _s5_kernel(u_ref, pos_re_ref, pos_im_ref, neg_re_ref, neg_im_ref, z_re_ref, z_im_ref,
               lbc_re_ref, lbc_im_ref, b_re_ref, b_im_ref, c_re_ref, c_im_ref, d_ref, wglu_ref,
               bglu_ref, o_ref, bb_ref, st_ref):
    R = u_ref.shape[0]
    T = pos_re_ref.shape[0]
    NS = S5_GROUPS * S5_STATE
    BS = NS // S5_BLOCKS
    BC = D_S5 // S5_BLOCKS

    @pl.when(pl.program_id(0) == 0)
    def _():
        st_ref[...] = jnp.zeros_like(st_ref)
        for j in range(S5_BLOCKS):
            zr = z_re_ref[:, j * BS:(j + 1) * BS]
            zi = z_im_ref[:, j * BS:(j + 1) * BS]
            br = b_re_ref[j]
            bi = b_im_ref[j]
            bb_ref[j, :, :BS] = (zr * br - zi * bi).astype(BF16)
            bb_ref[j, :, BS:] = (zr * bi + zi * br).astype(BF16)

    u = u_ref[...]
    ub = u.astype(BF16)
    ii = lax.broadcasted_iota(jnp.int32, (T, T), 0)
    jj = lax.broadcasted_iota(jnp.int32, (T, T), 1)
    tri = jnp.where(jj <= ii, 1.0, 0.0).astype(BF16)

    ys = []
    for j in range(S5_BLOCKS):
        sl = slice(j * BS, (j + 1) * BS)
        bu = jnp.dot(ub[:, j * BC:(j + 1) * BC], bb_ref[j], preferred_element_type=F32)
        nr, ni = neg_re_ref[:, sl], neg_im_ref[:, sl]
        pr, pi = pos_re_ref[:, sl], pos_im_ref[:, sl]
        lr_, li_ = lbc_re_ref[:, sl], lbc_im_ref[:, sl]
        s_re = st_ref[0:1, sl]
        s_im = st_ref[1:2, sl]
        xs_re, xs_im = [], []
        for t in range(R // T):
            bu_re, bu_im = bu[t * T:(t + 1) * T, :BS], bu[t * T:(t + 1) * T, BS:]
            zz = jnp.concatenate([nr * bu_re - ni * bu_im, nr * bu_im + ni * bu_re], axis=1)
            acc = jnp.dot(tri, zz.astype(BF16), preferred_element_type=F32)
            a_re = acc[:, :BS] + (lr_ * s_re - li_ * s_im)
            a_im = acc[:, BS:] + (lr_ * s_im + li_ * s_re)
            x_re = pr * a_re - pi * a_im
            x_im = pr * a_im + pi * a_re
            s_re, s_im = x_re[T - 1:T, :], x_im[T - 1:T, :]
            xs_re.append(x_re)
            xs_im.append(x_im)
        st_ref[0:1, sl] = s_re
        st_ref[1:2, sl] = s_im
        cat = lambda xs: jnp.concatenate(xs, axis=0) if len(xs) > 1 else xs[0]
        ys.append(_dot(cat(xs_re), c_re_ref[j]) - _dot(cat(xs_im), c_im_ref[j]))
    y = jnp.concatenate(ys, axis=1) + d_ref[...] * u
    y = 0.5 * y * (1.0 + jnp.tanh(math.sqrt(2.0 / math.pi) * (y + 0.044715 * (y * y * y))))
    zg = _dot(y, wglu_ref[...]) + bglu_ref[...]
    o_ref[...] = (zg[:, :D_S5] * _sigmoid(zg[:, D_S5:])).astype(o_ref.dtype)


def _s5(proj, tables, b_re, b_im, c_re, c_im, d, w_glu, b_glu, rows):
    L = proj.shape[0]
    NS = S5_GROUPS * S5_STATE
    BS = NS // S5_BLOCKS
    BC = D_S5 // S5_BLOCKS
    t = tables[0].shape[0]
    c2 = lambda shape: pl.BlockSpec(shape, lambda i: (0, 0))
    c3 = lambda shape: pl.BlockSpec(shape, lambda i: (0, 0, 0))
    return pl.pallas_call(
        _s5_kernel,
        grid=(L // rows,),
        in_specs=[pl.BlockSpec((rows, D_S5), lambda i: (i, 0)),
                  c2((t, NS)), c2((t, NS)), c2((t, NS)), c2((t, NS)),
                  c2((1, NS)), c2((1, NS)), c2((1, NS)), c2((1, NS)),
                  c3((S5_BLOCKS, BC, BS)), c3((S5_BLOCKS, BC, BS)),
                  c3((S5_BLOCKS, BS, BC)), c3((S5_BLOCKS, BS, BC)),
                  c2((1, D_S5)), c2((D_S5, 2 * D_S5)), c2((1, 2 * D_S5))],
        out_specs=pl.BlockSpec((rows, D_S5), lambda i: (i, 0)),
        out_shape=jax.ShapeDtypeStruct((L, D_S5), BF16),
        scratch_shapes=[pltpu.VMEM((S5_BLOCKS, BC, 2 * BS), BF16), pltpu.VMEM((8, NS), F32)],
        compiler_params=_cparams(1),
        name="s5",
    )(proj, *tables, b_re, b_im, c_re, c_im, d, w_glu, b_glu)


def _block_diag_groups(w):
    g, r, c = w.shape
    gb = g // S5_BLOCKS
    w = w.reshape(S5_BLOCKS, gb, r, c)
    eye = jnp.eye(gb, dtype=w.dtype)
    out = w[:, :, :, None, :] * eye[None, :, None, :, None]
    return out.reshape(S5_BLOCKS, gb * r, gb * c)


def kernel(x, c, w_ada, b_ada, norm_pre_mix, norm_post_mix, norm_pre_ffn, norm_post_ffn, w_in, rwkv_mu, rwkv_w0, rwkv_w2, rwkv_a0, rwkv_a2, rwkv_g2, rwkv_k_k, rwkv_k_a, rwkv_r_k, rwkv_ln_w, rwkv_ln_b, s5_lam_re, s5_lam_im, s5_log_dt, s5_b_re, s5_b_im, s5_c_re, s5_c_im, s5_d, s5_w_glu, s5_b_glu, w_up_rwkv, w_up_s5, w_out, ffn_w_gate, ffn_w_up, ffn_w_down):
    bsz, L, _ = x.shape
    assert bsz == 1 and w_ada.shape[0] == 1
    h = x.reshape(L, D_MODEL)
    tm = min(1024, L)
    tr = min(256, L)
    row = lambda v: v.reshape(1, -1)

    half = 3 * D_MODEL
    mod, cs = _ada(c.reshape(D_MODEL, 1), w_ada[0], row(b_ada[0]), half)

    wit = jnp.swapaxes(w_in[0], 0, 1)
    gpad = LORA_G_PAD - LORA_G
    n_rkv = 3 * D_RWKV
    mu = row(rwkv_mu[0])
    mu_lora = jnp.concatenate([rwkv_mu[0, n_rkv:], jnp.zeros((gpad,), F32)]).reshape(1, -1)
    g2 = jnp.concatenate([rwkv_g2[0], jnp.zeros((gpad, D_RWKV), F32)], axis=0).astype(BF16)

    xm, lw, a, g = _head(h, row(norm_pre_mix[0]), mod, wit, n_rkv, mu_lora, row(rwkv_w0[0]),
                         row(rwkv_a0[0]), rwkv_w2[0].astype(BF16), rwkv_a2[0].astype(BF16), g2, tr)
    proj_rkv = _matmul_wt(xm, wit, 0, n_rkv, F32, tm, 512, "proj_rkv")
    proj_u = _matmul_wt(xm, wit, RWKV_COLS, D_S5, F32, tm, 512, "proj_u")
    gates, mod2 = _matmul_wt_ada(xm, wit, RWKV_COLS + D_S5, 2 * D_MODEL, BF16, tm, 512,
                                 cs, w_ada[0], row(b_ada[0]), half, half, "proj_g")
    o_a = _rwkv(proj_rkv, mu, lw, a, g, row(rwkv_k_k[0]), row(rwkv_k_a[0]), row(rwkv_r_k[0]),
                row(rwkv_ln_w[0]), row(rwkv_ln_b[0]), min(RWKV_T * RWKV_CHUNKS_PER_STEP, L))

    rep = lambda v: jnp.repeat(v, S5_STATE).reshape(1, -1)
    tables = _s5_tables(row(s5_lam_re[0]), row(s5_lam_im[0]), rep(s5_log_dt[0]), S5_T)
    bt = lambda w: _block_diag_groups(jnp.swapaxes(w, 1, 2))
    o_b = _s5(proj_u, tables, bt(s5_b_re[0]), bt(s5_b_im[0]),
              _block_diag_groups(jnp.swapaxes(s5_c_re[0], 1, 2)).astype(BF16),
              _block_diag_groups(jnp.swapaxes(s5_c_im[0], 1, 2)).astype(BF16),
              row(s5_d[0]), s5_w_glu[0].astype(BF16), row(s5_b_glu[0]), min(S5_ROWS, L))

    merged = _merge(o_a, o_b, w_up_rwkv[0], w_up_s5[0], gates, tm, 512)
    mix = _matmul(merged, w_out[0], BF16, tm, 512, "w_out")

    h1, xf = _mid(h, mix, row(norm_post_mix[0]), row(norm_pre_ffn[0]), mod, mod2, tr)
    act, wd_bf = _ffn_up(xf, ffn_w_gate[0], ffn_w_up[0], ffn_w_down[0], min(2048, L), 256)
    ff = _matmul_ksplit(act, wd_bf, BF16, tm, 1024, D_FF // 2, "ffn_down")
    out = _final(h1, ff, row(norm_post_ffn[0]), mod2, tr)
    return out.reshape(bsz, L, D_MODEL)
```

```python
import functools
import math

import jax
import jax.numpy as jnp
from jax import lax
from jax.experimental import pallas as pl
from jax.experimental.pallas import tpu as pltpu

F32 = jnp.float32
BF16 = jnp.bfloat16

D_MODEL = 4096
RMS_EPS = 1e-6
D_RWKV = 2048
RWKV_HEAD = 64
LORA_W = 128
LORA_A = 128
LORA_G = 480
LORA_G_PAD = 512
GN_EPS = 64e-5
D_S5 = 1024
S5_GROUPS = 64
S5_GROUP_CH = 16
S5_STATE = 64
S5_BLOCKS = 4
D_FF = 11008

LANES = 128
PAIRS = D_RWKV // LANES
RWKV_T = 64
RWKV_CHUNKS_PER_STEP = 8
S5_T = 128
S5_ROWS = 512

RWKV_COLS = 3 * D_RWKV + LORA_W + LORA_A + LORA_G

VMEM_LIMIT = 56 * 1024 * 1024


def _cparams(n_axes, vmem=VMEM_LIMIT):
    return pltpu.CompilerParams(dimension_semantics=("arbitrary",) * n_axes, vmem_limit_bytes=vmem)


def _dot(a, b):
    return jnp.dot(a.astype(BF16), b.astype(BF16), preferred_element_type=F32)


def _dot_tn(a, b):
    return lax.dot_general(a.astype(BF16), b.astype(BF16), (((0,), (0,)), ((), ())),
                           preferred_element_type=F32)


def _split2(x):
    hi = x.astype(BF16)
    lo = (x - hi.astype(F32)).astype(BF16)
    return hi, lo


def _head_sums(x, sel2):
    half = x.shape[0] // 2
    hi, lo = _split2(jnp.concatenate([x[:half], x[half:]], axis=1))
    res = (jnp.dot(hi, sel2, preferred_element_type=F32) + jnp.dot(lo, sel2, preferred_element_type=F32))
    return jnp.concatenate([res[:, :LANES], res[:, LANES:]], axis=0)


def _dot_f32_rhs(sel, x):
    hi, lo = _split2(x)
    return (jnp.dot(sel, hi, preferred_element_type=F32) + jnp.dot(sel, lo, preferred_element_type=F32))


def _rms(x):
    return x * lax.rsqrt(jnp.mean(x * x, axis=-1, keepdims=True) + RMS_EPS)


def _sigmoid(x):
    return 1.0 / (1.0 + jnp.exp(-x))


def _token_shift(p, halo, mu, first):
    last = jnp.where(first, 0.0, halo[7:8, :])
    rolled = pltpu.roll(p, 1, axis=0)
    row = lax.broadcasted_iota(jnp.int32, p.shape, 0)
    prev = jnp.where(row == 0, last, rolled)
    return p + (prev - p) * mu


def _col_matvec(cs, w_ref, b_ref):
    tn = w_ref.shape[1]
    rows = 512
    acc = jnp.zeros((8, tn), F32)
    for k0 in range(0, D_MODEL, rows):
        blk = w_ref[k0:k0 + rows, :] * cs[k0:k0 + rows, :]
        acc = acc + jnp.sum(blk.reshape(rows // 8, 8, tn), axis=0)
    return jnp.sum(acc, axis=0, keepdims=True) + b_ref[...]


def _ada_kernel(c_ref, w_ref, b_ref, o_ref, cs_ref):
    c = c_ref[...]
    cs = c * _sigmoid(c)
    cs_ref[...] = cs
    o_ref[...] = _col_matvec(cs, w_ref, b_ref)


def _ada(c_col, w_ada, b_ada, n):
    tn = 512
    return pl.pallas_call(
        _ada_kernel,
        grid=(n // tn,),
        in_specs=[pl.BlockSpec((D_MODEL, 1), lambda j: (0, 0)),
                  pl.BlockSpec((D_MODEL, tn), lambda j: (0, j)),
                  pl.BlockSpec((1, tn), lambda j: (0, j))],
        out_specs=[pl.BlockSpec((1, tn), lambda j: (0, j)),
                   pl.BlockSpec((D_MODEL, 1), lambda j: (0, 0))],
        out_shape=[jax.ShapeDtypeStruct((1, n), F32), jax.ShapeDtypeStruct((D_MODEL, 1), F32)],
        compiler_params=_cparams(1),
        name="ada",
    )(c_col, w_ada, b_ada)


def _mid_kernel(x_ref, mix_ref, gpost_ref, gate_ref, gpre_ref, sh_ref, sc_ref, h_ref, xf_ref):
    h = x_ref[...] + gate_ref[...] * (_rms(mix_ref[...].astype(F32)) * gpost_ref[...])
    h_ref[...] = h
    y = _rms(h) * gpre_ref[...]
    xf_ref[...] = (y * (1.0 + sc_ref[...]) + sh_ref[...]).astype(xf_ref.dtype)


def _mid(x, mix, gpost, gpre, mod1, mod2, tm):
    L = x.shape[0]
    row = lambda i: (i, 0)
    vec = lambda k: pl.BlockSpec((1, D_MODEL), lambda i: (0, k))
    return pl.pallas_call(
        _mid_kernel,
        grid=(L // tm,),
        in_specs=[pl.BlockSpec((tm, D_MODEL), row), pl.BlockSpec((tm, D_MODEL), row),
                  vec(0), vec(2), vec(0), vec(0), vec(1)],
        out_specs=[pl.BlockSpec((tm, D_MODEL), row), pl.BlockSpec((tm, D_MODEL), row)],
        out_shape=[jax.ShapeDtypeStruct((L, D_MODEL), F32), jax.ShapeDtypeStruct((L, D_MODEL), BF16)],
        compiler_params=_cparams(1),
        name="mid",
    )(x, mix, gpost, mod1, gpre, mod2, mod2)


def _final_kernel(h_ref, ff_ref, gpost_ref, gate_ref, o_ref):
    o_ref[...] = h_ref[...] + gate_ref[...] * (_rms(ff_ref[...].astype(F32)) * gpost_ref[...])


def _final(h, ff, gpost, mod, tm):
    L = h.shape[0]
    row = lambda i: (i, 0)
    return pl.pallas_call(
        _final_kernel,
        grid=(L // tm,),
        in_specs=[pl.BlockSpec((tm, D_MODEL), row), pl.BlockSpec((tm, D_MODEL), row),
                  pl.BlockSpec((1, D_MODEL), lambda i: (0, 0)),
                  pl.BlockSpec((1, D_MODEL), lambda i: (0, 2))],
        out_specs=pl.BlockSpec((tm, D_MODEL), row),
        out_shape=jax.ShapeDtypeStruct((L, D_MODEL), F32),
        compiler_params=_cparams(1),
        name="final",
    )(h, ff, gpost, mod)


def _mm_kernel(a_ref, w_ref, o_ref):
    o_ref[...] = jnp.dot(a_ref[...], w_ref[...].astype(BF16),
                         preferred_element_type=F32).astype(o_ref.dtype)


def _matmul(a, w, out_dtype, tm, tn, name):
    m, k = a.shape
    n = w.shape[1]
    return pl.pallas_call(
        _mm_kernel,
        grid=(m // tm, n // tn),
        in_specs=[pl.BlockSpec((tm, k), lambda i, j: (i, 0)),
                  pl.BlockSpec((k, tn), lambda i, j: (0, j))],
        out_specs=pl.BlockSpec((tm, tn), lambda i, j: (i, j)),
        out_shape=jax.ShapeDtypeStruct((m, n), out_dtype),
        compiler_params=_cparams(2),
        name=name,
    )(a, w)


def _mm_acc_kernel(a_ref, w_ref, o_ref, acc_ref):
    kk = pl.program_id(2)
    last = pl.num_programs(2) - 1
    part = jnp.dot(a_ref[...], w_ref[...], preferred_element_type=F32)

    @pl.when(kk == 0)
    def _():
        acc_ref[...] = part

    @pl.when((kk > 0) & (kk < last))
    def _():
        acc_ref[...] = acc_ref[...] + part

    @pl.when(kk == last)
    def _():
        o_ref[...] = (acc_ref[...] + part).astype(o_ref.dtype)


def _dot_wt(a, wt):
    return lax.dot_general(a, wt.astype(BF16), (((1,), (1,)), ((), ())), preferred_element_type=F32)


def _mm_wt_kernel(a_ref, wt_ref, o_ref):
    o_ref[...] = _dot_wt(a_ref[...], wt_ref[...]).astype(o_ref.dtype)


def _matmul_wt(a, wt, row0, n, out_dtype, tm, tn, name):
    m, k = a.shape
    return pl.pallas_call(
        _mm_wt_kernel,
        grid=(m // tm, n // tn),
        in_specs=[pl.BlockSpec((tm, k), lambda i, j: (i, 0)),
                  pl.BlockSpec((pl.Element(tn), pl.Element(k)),
                               lambda i, j: (pl.multiple_of(row0 + j * tn, 8), 0))],
        out_specs=pl.BlockSpec((tm, tn), lambda i, j: (i, j)),
        out_shape=jax.ShapeDtypeStruct((m, n), out_dtype),
        compiler_params=_cparams(2),
        name=name,
    )(a, wt)


def _mm_wt_ada_kernel(a_ref, wt_ref, cs_ref, wada_ref, bada_ref, o_ref, mod_ref):
    o_ref[...] = _dot_wt(a_ref[...], wt_ref[...]).astype(o_ref.dtype)
    mod_ref[...] = _col_matvec(cs_ref[...], wada_ref, bada_ref)


def _matmul_wt_ada(a, wt, row0, n, out_dtype, tm, tn, cs, w_ada, b_ada, col0, ncols, name):
    m, k = a.shape
    ni, nj = m // tm, n // tn
    cps = LANES * -(-(ncols // LANES) // (ni * nj))
    assert ncols % cps == 0 and col0 % cps == 0 and ncols // cps <= ni * nj
    nblk = ncols // cps
    blk = lambda i, j: jnp.minimum(i * nj + j, nblk - 1)
    return pl.pallas_call(
        _mm_wt_ada_kernel,
        grid=(ni, nj),
        in_specs=[pl.BlockSpec((tm, k), lambda i, j: (i, 0)),
                  pl.BlockSpec((pl.Element(tn), pl.Element(k)),
                               lambda i, j: (pl.multiple_of(row0 + j * tn, 8), 0)),
                  pl.BlockSpec((D_MODEL, 1), lambda i, j: (0, 0)),
                  pl.BlockSpec((D_MODEL, cps), lambda i, j: (0, col0 // cps + blk(i, j))),
                  pl.BlockSpec((1, cps), lambda i, j: (0, col0 // cps + blk(i, j)))],
        out_specs=[pl.BlockSpec((tm, tn), lambda i, j: (i, j)),
                   pl.BlockSpec((1, cps), lambda i, j: (0, blk(i, j)))],
        out_shape=[jax.ShapeDtypeStruct((m, n), out_dtype), jax.ShapeDtypeStruct((1, ncols), F32)],
        compiler_params=_cparams(2),
        name=name,
    )(a, wt, cs, w_ada, b_ada)


def _matmul_ksplit(a, w, out_dtype, tm, tn, tk, name):
    m, k = a.shape
    n = w.shape[1]
    assert k % tk == 0 and k // tk >= 2
    return pl.pallas_call(
        _mm_acc_kernel,
        grid=(m // tm, n // tn, k // tk),
        in_specs=[pl.BlockSpec((tm, tk), lambda i, j, kk: (i, kk)),
                  pl.BlockSpec((tk, tn), lambda i, j, kk: (kk, j))],
        out_specs=pl.BlockSpec((tm, tn), lambda i, j, kk: (i, j)),
        out_shape=jax.ShapeDtypeStruct((m, n), out_dtype),
        scratch_shapes=[pltpu.VMEM((tm, tn), F32)],
        compiler_params=_cparams(3),
        name=name,
    )(a, w)


def _merge_kernel(oa_ref, ob_ref, wa_ref, wb_ref, ga_ref, gb_ref, o_ref):
    ya = jnp.dot(oa_ref[...], wa_ref[...].astype(BF16), preferred_element_type=F32)
    yb = jnp.dot(ob_ref[...], wb_ref[...].astype(BF16), preferred_element_type=F32)
    m = _sigmoid(ga_ref[...].astype(F32)) * ya + _sigmoid(gb_ref[...].astype(F32)) * yb
    o_ref[...] = m.astype(o_ref.dtype)


def _merge(o_a, o_b, w_up_a, w_up_b, gates, tm, tn):
    L = o_a.shape[0]
    nb = D_MODEL // tn
    return pl.pallas_call(
        _merge_kernel,
        grid=(L // tm, nb),
        in_specs=[pl.BlockSpec((tm, D_RWKV), lambda i, j: (i, 0)),
                  pl.BlockSpec((tm, D_S5), lambda i, j: (i, 0)),
                  pl.BlockSpec((D_RWKV, tn), lambda i, j: (0, j)),
                  pl.BlockSpec((D_S5, tn), lambda i, j: (0, j)),
                  pl.BlockSpec((tm, tn), lambda i, j: (i, j)),
                  pl.BlockSpec((tm, tn), lambda i, j: (i, j + nb))],
        out_specs=pl.BlockSpec((tm, tn), lambda i, j: (i, j)),
        out_shape=jax.ShapeDtypeStruct((L, D_MODEL), BF16),
        compiler_params=_cparams(2),
        name="merge",
    )(o_a, o_b, w_up_a, w_up_b, gates, gates)


def _ffn_up_kernel(x_ref, wg_ref, wu_ref, wd_ref, o_ref, wd_bf_ref):
    x = x_ref[...]
    a = jnp.dot(x, wg_ref[...].astype(BF16), preferred_element_type=F32)
    b = jnp.dot(x, wu_ref[...].astype(BF16), preferred_element_type=F32)
    o_ref[...] = ((a * _sigmoid(a)) * b).astype(o_ref.dtype)
    wd_bf_ref[...] = wd_ref[...].astype(BF16)


def _ffn_up(xf, w_gate, w_up, w_down, tm, tn):
    L = xf.shape[0]
    n = w_gate.shape[1]
    ni, nj = L // tm, n // tn
    slab = w_down.shape[0] // (ni * nj)
    assert slab * ni * nj == w_down.shape[0] and slab % 16 == 0
    wd_spec = lambda: pl.BlockSpec((slab, D_MODEL), lambda i, j: (i * nj + j, 0))
    return pl.pallas_call(
        _ffn_up_kernel,
        grid=(ni, nj),
        in_specs=[pl.BlockSpec((tm, D_MODEL), lambda i, j: (i, 0), pipeline_mode=pl.Buffered(1)),
                  pl.BlockSpec((D_MODEL, tn), lambda i, j: (0, j)),
                  pl.BlockSpec((D_MODEL, tn), lambda i, j: (0, j)),
                  wd_spec()],
        out_specs=[pl.BlockSpec((tm, tn), lambda i, j: (i, j)), wd_spec()],
        out_shape=[jax.ShapeDtypeStruct((L, n), BF16), jax.ShapeDtypeStruct(w_down.shape, BF16)],
        compiler_params=_cparams(2),
        name="ffn_up",
    )(xf, w_gate, w_up, w_down)


def _head_kernel(x_ref, gpre_ref, sh_ref, sc_ref, wl_ref, mu_ref, w0_ref, a0_ref, w2_ref, a2_ref,
                 g2_ref, xm_ref, lw_ref, a_ref, g_ref, wl_bf_ref, carry_ref):
    first = pl.program_id(0) == 0

    @pl.when(first)
    def _():
        wl_bf_ref[...] = wl_ref[...].astype(BF16)
        carry_ref[...] = jnp.zeros_like(carry_ref)

    y = _rms(x_ref[...]) * gpre_ref[...]
    xm = (y * (1.0 + sc_ref[...]) + sh_ref[...]).astype(BF16)
    xm_ref[...] = xm
    p = lax.dot_general(xm, wl_bf_ref[...], (((1,), (1,)), ((), ())), preferred_element_type=F32)
    ps = _token_shift(p, carry_ref[...], mu_ref[...], first)
    carry_ref[...] = p[p.shape[0] - 8:, :]
    xw = ps[:, :LORA_W]
    xa = ps[:, LORA_W:LORA_W + LORA_A]
    xg = ps[:, LORA_W + LORA_A:]
    u = w0_ref[...] + _dot(jnp.tanh(xw), w2_ref[...])
    lw_ref[...] = -math.exp(-0.5) * _sigmoid(u)
    a_ref[...] = _sigmoid(a0_ref[...] + _dot(xa, a2_ref[...]))
    g_ref[...] = _dot(_sigmoid(xg), g2_ref[...])


def _head(x, gpre, mod, wt, row_lora, mu, w0, a0, w2, a2, g2, tm):
    L = x.shape[0]
    nl = mu.shape[1]
    full = lambda shape: pl.BlockSpec(shape, lambda i: (0, 0))
    rows = lambda width: pl.BlockSpec((tm, width), lambda i: (i, 0))
    return pl.pallas_call(
        _head_kernel,
        grid=(L // tm,),
        in_specs=[rows(D_MODEL), full((1, D_MODEL)),
                  pl.BlockSpec((1, D_MODEL), lambda i: (0, 0)), pl.BlockSpec((1, D_MODEL), lambda i: (0, 1)),
                  pl.BlockSpec((pl.Element(nl), pl.Element(D_MODEL)), lambda i: (row_lora, 0),
                               pipeline_mode=pl.Buffered(1)),
                  full((1, nl)), full((1, D_RWKV)), full((1, D_RWKV)),
                  full((LORA_W, D_RWKV)), full((LORA_A, D_RWKV)), full((LORA_G_PAD, D_RWKV))],
        out_specs=[rows(D_MODEL), rows(D_RWKV), rows(D_RWKV), rows(D_RWKV)],
        out_shape=[jax.ShapeDtypeStruct((L, D_MODEL), BF16)] + [jax.ShapeDtypeStruct((L, D_RWKV), F32)] * 3,
        scratch_shapes=[pltpu.VMEM((nl, D_MODEL), BF16), pltpu.VMEM((8, nl), F32)],
        compiler_params=_cparams(1),
        name="head",
    )(x, gpre, mod, mod, wt, mu, w0, a0, w2, a2, g2)


def _rwkv_prep(r, k, v, am, bm, lw, cst, T):
    n = r.shape[0] // T
    lane_lo, tri_t = cst[0], cst[1]

    def chunk(x, c):
        return x[c * T:(c + 1) * T]

    def stack_own(xc):
        return jnp.concatenate([jnp.where(lane_lo, xc, 0.0), jnp.where(lane_lo, 0.0, xc)], axis=0)

    c_all = _dot_f32_rhs(tri_t, jnp.concatenate([chunk(lw, c) for c in range(n)], axis=1))
    yield
    lhs, rhs, bk, vv, dec = [], [], [], [], []
    for c in range(n):
        cu = c_all[:, c * LANES:(c + 1) * LANES]
        c_end = cu[T - 1:T, :]
        e_neg = jnp.exp(-cu)
        e_end = jnp.exp(c_end - cu)
        rc, kc_, ac, bc_ = chunk(r, c), chunk(k, c), chunk(am, c), chunk(bm, c)
        lhs.append(jnp.concatenate([stack_own(ac * jnp.exp(cu - chunk(lw, c))),
                                    stack_own(rc * jnp.exp(cu))], axis=0).astype(BF16))
        rhs.append(jnp.concatenate([bc_ * e_neg, kc_ * e_neg], axis=0).astype(BF16))
        bk.append(jnp.concatenate([stack_own(bc_ * e_end), stack_own(kc_ * e_end)], axis=0).astype(BF16))
        vv.append(chunk(v, c).astype(BF16))
        dec.append(jnp.broadcast_to(jnp.exp(c_end), (8, LANES)))
        if c % 2 == 1:
            yield
    return lhs, rhs, bk, vv, dec


def _rwkv_core(lhs, rhs, bk, vv, dec, cst, T):
    S = 2 * T
    n = len(lhs)
    _, _, strict, incl2, eye, same_head = cst
    cr = range(n)
    dup = lambda x: jnp.concatenate([x, x], axis=0)

    big = [lax.dot_general(lhs[c], jnp.concatenate([dup(rhs[c][:T]), dup(rhs[c][T:])], axis=0),
                           (((1,), (1,)), ((), ())), preferred_element_type=F32) for c in cr]
    yield
    ah = [lhs[c][:S] for c in cr]
    rh = [lhs[c][S:].astype(F32) for c in cr]
    v_s = [dup(vv[c]) for c in cr]
    nj = [jnp.where(strict, big[c][:S, :S], 0.0) for c in cr]
    m_r = [jnp.where(incl2, big[c][S:, :], 0.0) for c in cr]
    av = [_dot(jnp.where(strict, big[c][:S, S:], 0.0), v_s[c]) for c in cr]
    yield

    steps = T.bit_length() - 1
    p = [jnp.where(eye, 1.0, 0.0) + nj[c] for c in cr]
    nj = [_dot(nj[c], nj[c]) for c in cr]
    yield
    for j in range(1, steps):
        if j + 1 < steps:
            res = [_dot(nj[c], jnp.concatenate([p[c], nj[c]], axis=1)) for c in cr]
            p = [p[c] + res[c][:, :S] for c in cr]
            nj = [res[c][:, S:] for c in cr]
        else:
            p = [p[c] + _dot(nj[c], p[c]) for c in cr]
        yield
    x = [_dot(p[c], jnp.concatenate([ah[c].astype(F32), av[c]], axis=1)) for c in cr]
    yield

    wv = [jnp.concatenate([x[c].astype(BF16),
                           jnp.concatenate([jnp.zeros_like(v_s[c]), v_s[c]], axis=1)], axis=0)
          for c in cr]
    y = [_dot(m_r[c], wv[c]) for c in cr]
    q = [rh[c] + y[c][:, :LANES] for c in cr]
    o0 = [y[c][:, LANES:] for c in cr]
    gd = [_dot_tn(bk[c], wv[c]) for c in cr]
    gd = [jnp.concatenate([gd[c][:, :LANES], jnp.where(same_head, gd[c][:, LANES:], 0.0)], axis=1)
          for c in cr]
    decay = [jnp.broadcast_to(jnp.sum(jnp.where(eye, dec[c][0:1, :], 0.0), axis=1, keepdims=True),
                              (LANES, LANES)) for c in cr]
    return gd, q, o0, decay


def _rwkv_kernel(r_ref, k_ref, v_ref, rh_ref, kh_ref, vh_ref, mur_ref, muk_ref, muv_ref,
                 lw_ref, a_ref, kk_ref, ka_ref, rk_ref, g_ref, lnw_ref, lnb_ref,
                 o_ref, h_ref, gd_ref, q_ref, o0_ref, dec_ref, bonus_ref,
                 lhs_ref, rhs_ref, bk_ref, vv_ref, decr_ref, *, nblk):
    s = pl.program_id(0)
    last = pl.num_programs(0) - 3
    first_cur = lax.rem(jnp.minimum(s, last), nblk) == 0
    first_out = lax.rem(jnp.maximum(s - 2, 0), nblk) == 0

    @pl.when(s == 0)
    def _():
        for ref in (h_ref, gd_ref, q_ref, o0_ref, dec_ref, bonus_ref,
                    lhs_ref, rhs_ref, bk_ref, vv_ref, decr_ref):
            ref[...] = jnp.zeros_like(ref)

    rows = r_ref.shape[0]
    T = RWKV_T
    S = 2 * T
    n = rows // T
    ii = lax.broadcasted_iota(jnp.int32, (S, S), 0)
    jj = lax.broadcasted_iota(jnp.int32, (S, S), 1)
    strict = ((ii // T) == (jj // T)) & (jj < ii)
    eye = ii == jj
    ii2 = lax.broadcasted_iota(jnp.int32, (S, 2 * S), 0)
    jj2 = lax.broadcasted_iota(jnp.int32, (S, 2 * S), 1) % S
    incl2 = ((ii2 // T) == (jj2 // T)) & (jj2 <= ii2)
    hi_ = lax.broadcasted_iota(jnp.int32, (2 * LANES, 2 * LANES), 0) // RWKV_HEAD
    hj_ = lax.broadcasted_iota(jnp.int32, (2 * LANES, 2 * LANES), 1) // RWKV_HEAD
    head_sum = jnp.where(hi_ == hj_, 1.0, 0.0).astype(BF16)
    lane_lo = lax.broadcasted_iota(jnp.int32, (T, LANES), 1) < RWKV_HEAD
    it = lax.broadcasted_iota(jnp.int32, (T, T), 0)
    jt = lax.broadcasted_iota(jnp.int32, (T, T), 1)
    tri_t = jnp.where(jt <= it, 1.0, 0.0).astype(BF16)
    same_head = ((lax.broadcasted_iota(jnp.int32, (LANES, LANES), 0) // RWKV_HEAD)
                 == (lax.broadcasted_iota(jnp.int32, (LANES, LANES), 1) // RWKV_HEAD))
    cst = (lane_lo, tri_t, strict, incl2, eye, same_head)
    slot = lax.rem(s, 2)

    state = [jnp.where(first_out, 0.0, h_ref[...])]
    outs = []

    def serial_step(c):
        h = state[0]
        gd = gd_ref[c]
        z = _dot(jnp.concatenate([gd[:, :LANES], q_ref[c]], axis=0), h)
        o_s = z[LANES:] + o0_ref[c]
        outs.append(jnp.where(lane_lo, o_s[:T], o_s[T:]))
        state[0] = dec_ref[c] * h + z[:LANES] + gd[:, LANES:]

    def serial_finish():
        h_ref[...] = state[0]
        o = jnp.concatenate(outs, axis=0) if n > 1 else outs[0]
        inv = 1.0 / RWKV_HEAD
        mean = _head_sums(o, head_sum) * inv
        d = o - mean
        var = _head_sums(d * d, head_sum) * inv
        y = d * lax.rsqrt(var + GN_EPS) * lnw_ref[...] + lnb_ref[...]
        o_ref[...] = ((y + bonus_ref[slot]) * g_ref[...]).astype(o_ref.dtype)

    core = _rwkv_core([lhs_ref[c] for c in range(n)], [rhs_ref[c] for c in range(n)],
                      [bk_ref[c] for c in range(n)], [vv_ref[c] for c in range(n)],
                      [decr_ref[c] for c in range(n)], cst, T)

    def prepare():
        r = _token_shift(r_ref[...], rh_ref[...], mur_ref[...], first_cur)
        k = _token_shift(k_ref[...], kh_ref[...], muk_ref[...], first_cur)
        v = _token_shift(v_ref[...], vh_ref[...], muv_ref[...], first_cur)
        a = a_ref[...]
        kk = k * kk_ref[...]
        k = k * (1.0 + (a - 1.0) * ka_ref[...])
        yield
        ss = _head_sums(kk * kk, head_sum)
        bonus = _head_sums(r * k * rk_ref[...], head_sum) * v
        yield
        kk = kk * lax.rsqrt(jnp.maximum(ss, 1e-24))
        operands = yield from _rwkv_prep(r, k, v, -kk, kk * a, lw_ref[...], cst, T)
        return operands + (bonus,)

    serial_step(0)
    prep = prepare()

    results = {}
    gens = {"core": core, "prep": prep}
    done = 1
    while gens:
        for name in ("core", "prep"):
            if name not in gens:
                continue
            try:
                next(gens[name])
            except StopIteration as stop:
                results[name] = stop.value
                del gens[name]
            if name == "core":
                if done < n:
                    serial_step(done)
                elif done == n:
                    serial_finish()
                done += 1
    assert done > n, "fewer matmul stages than chunks per block"

    gd, q, o0, decay = results["core"]
    lhs, rhs, bk, vv, dec, bonus = results["prep"]
    for c in range(n):
        gd_ref[c] = gd[c]
        q_ref[c] = q[c]
        o0_ref[c] = o0[c]
        dec_ref[c] = decay[c]
        lhs_ref[c] = lhs[c]
        rhs_ref[c] = rhs[c]
        bk_ref[c] = bk[c]
        vv_ref[c] = vv[c]
        decr_ref[c] = dec[c]
    bonus_ref[slot] = bonus


def _rwkv(proj, mu, lw, a, g, k_k, k_a, r_k, ln_w, ln_b, rows):
    L = proj.shape[0]
    nb = D_RWKV // LANES
    nblk = L // rows
    nsteps = PAIRS * nblk
    n = rows // RWKV_T

    def cur(s):
        b = jnp.minimum(s, nsteps - 1)
        return lax.div(b, nblk), lax.rem(b, nblk)

    def prv(s):
        b = jnp.maximum(s - 2, 0)
        return lax.div(b, nblk), lax.rem(b, nblk)

    def col(off):
        return pl.BlockSpec((rows, LANES), lambda s: (cur(s)[1], cur(s)[0] + off))

    def halo(off):
        return pl.BlockSpec((8, LANES),
                            lambda s: (jnp.maximum(cur(s)[1] * (rows // 8) - 1, 0), cur(s)[0] + off))

    def vec(off):
        return pl.BlockSpec((1, LANES), lambda s: (0, cur(s)[0] + off))

    def lag_col():
        return pl.BlockSpec((rows, LANES), lambda s: (prv(s)[1], prv(s)[0]))

    def lag_vec():
        return pl.BlockSpec((1, LANES), lambda s: (0, prv(s)[0]))

    return pl.pallas_call(
        functools.partial(_rwkv_kernel, nblk=nblk),
        grid=(nsteps + 2,),
        in_specs=[col(0), col(nb), col(2 * nb), halo(0), halo(nb), halo(2 * nb),
                  vec(0), vec(nb), vec(2 * nb),
                  col(0), col(0), vec(0), vec(0), vec(0),
                  lag_col(), lag_vec(), lag_vec()],
        out_specs=lag_col(),
        out_shape=jax.ShapeDtypeStruct((L, D_RWKV), BF16),
        scratch_shapes=[pltpu.VMEM((LANES, LANES), F32),
                        pltpu.VMEM((n, LANES, 2 * LANES), F32),
                        pltpu.VMEM((n, 2 * RWKV_T, LANES), F32),
                        pltpu.VMEM((n, 2 * RWKV_T, LANES), F32),
                        pltpu.VMEM((n, LANES, LANES), F32),
                        pltpu.VMEM((2, rows, LANES), F32),
                        pltpu.VMEM((n, 4 * RWKV_T, LANES), BF16),
                        pltpu.VMEM((n, 2 * RWKV_T, LANES), BF16),
                        pltpu.VMEM((n, 4 * RWKV_T, LANES), BF16),
                        pltpu.VMEM((n, RWKV_T, LANES), BF16),
                        pltpu.VMEM((n, 8, LANES), F32)],
        compiler_params=_cparams(1),
        name="rwkv",
    )(proj, proj, proj, proj, proj, proj, mu, mu, mu, lw, a, k_k, k_a, r_k, g, ln_w, ln_b)


def _s5_table_kernel(lr_ref, li_ref, ldt_ref, pos_re, pos_im, neg_re, neg_im, z_re, z_im,
                     lbc_re, lbc_im):
    lr = lr_ref[...]
    li = li_ref[...]
    dt = jnp.exp(ldt_ref[...])
    centre = pos_re.shape[0] // 2
    t = (lax.broadcasted_iota(jnp.int32, pos_re.shape, 0) - centre).astype(F32)
    mag = jnp.exp(t * (lr * dt))
    ang = t * (li * dt)
    cs, sn = jnp.cos(ang), jnp.sin(ang)
    pos_re[...] = mag * cs
    pos_im[...] = mag * sn
    inv = 1.0 / mag
    neg_re[...] = inv * cs
    neg_im[...] = -(inv * sn)
    mc = jnp.exp((centre + 1.0) * (lr * dt))
    lbc_re[...] = mc * jnp.cos((centre + 1.0) * (li * dt))
    lbc_im[...] = mc * jnp.sin((centre + 1.0) * (li * dt))
    m1 = jnp.exp(lr * dt)
    lb_re = m1 * jnp.cos(li * dt)
    lb_im = m1 * jnp.sin(li * dt)
    den = lr * lr + li * li
    z_re[...] = ((lb_re - 1.0) * lr + lb_im * li) / den
    z_im[...] = (lb_im * lr - (lb_re - 1.0) * li) / den


def _s5_tables(lr, li, ldt, rows):
    n = lr.shape[1]
    full = pl.BlockSpec((1, n), lambda: (0, 0))
    tab = pl.BlockSpec((rows, n), lambda: (0, 0))
    return pl.pallas_call(
        _s5_table_kernel,
        in_specs=[full, full, full],
        out_specs=[tab, tab, tab, tab, full, full, full, full],
        out_shape=[jax.ShapeDtypeStruct((rows, n), F32)] * 4 + [jax.ShapeDtypeStruct((1, n), F32)] * 4,
        name="s5_tables",
    )(lr, li, ldt)


def _s5_kernel(u_ref, pos_re_ref, pos_im_ref, neg_re_ref, neg_im_ref, z_re_ref, z_im_ref,
               lbc_re_ref, lbc_im_ref, b_re_ref, b_im_ref, c_re_ref, c_im_ref, d_ref, wglu_ref,
               bglu_ref, o_ref, bb_ref, st_ref):
    R = u_ref.shape[0]
    T = pos_re_ref.shape[0]
    NS = S5_GROUPS * S5_STATE
    BS = NS // S5_BLOCKS
    BC = D_S5 // S5_BLOCKS

    @pl.when(pl.program_id(0) == 0)
    def _():
        st_ref[...] = jnp.zeros_like(st_ref)
        for j in range(S5_BLOCKS):
            zr = z_re_ref[:, j * BS:(j + 1) * BS]
            zi = z_im_ref[:, j * BS:(j + 1) * BS]
            br = b_re_ref[j]
            bi = b_im_ref[j]
            bb_ref[j, :, :BS] = (zr * br - zi * bi).astype(BF16)
            bb_ref[j, :, BS:] = (zr * bi + zi * br).astype(BF16)

    u = u_ref[...]
    ub = u.astype(BF16)
    ii = lax.broadcasted_iota(jnp.int32, (T, T), 0)
    jj = lax.broadcasted_iota(jnp.int32, (T, T), 1)
    tri = jnp.where(jj <= ii, 1.0, 0.0).astype(BF16)

    ys = []
    for j in range(S5_BLOCKS):
        sl = slice(j * BS, (j + 1) * BS)
        bu = jnp.dot(ub[:, j * BC:(j + 1) * BC], bb_ref[j], preferred_element_type=F32)
        nr, ni = neg_re_ref[:, sl], neg_im_ref[:, sl]
        pr, pi = pos_re_ref[:, sl], pos_im_ref[:, sl]
        lr_, li_ = lbc_re_ref[:, sl], lbc_im_ref[:, sl]
        s_re = st_ref[0:1, sl]
        s_im = st_ref[1:2, sl]
        xs_re, xs_im = [], []
        for t in range(R // T):
            bu_re, bu_im = bu[t * T:(t + 1) * T, :BS], bu[t * T:(t + 1) * T, BS:]
            zz = jnp.concatenate([nr * bu_re - ni * bu_im, nr * bu_im + ni * bu_re], axis=1)
            acc = jnp.dot(tri, zz.astype(BF16), preferred_element_type=F32)
            a_re = acc[:, :BS] + (lr_ * s_re - li_ * s_im)
            a_im = acc[:, BS:] + (lr_ * s_im + li_ * s_re)
            x_re = pr * a_re - pi * a_im
            x_im = pr * a_im + pi * a_re
            s_re, s_im = x_re[T - 1:T, :], x_im[T - 1:T, :]
            xs_re.append(x_re)
            xs_im.append(x_im)
        st_ref[0:1, sl] = s_re
        st_ref[1:2, sl] = s_im
        cat = lambda xs: jnp.concatenate(xs, axis=0) if len(xs) > 1 else xs[0]
        ys.append(_dot(cat(xs_re), c_re_ref[j]) - _dot(cat(xs_im), c_im_ref[j]))
    y = jnp.concatenate(ys, axis=1) + d_ref[...] * u
    y = 0.5 * y * (1.0 + jnp.tanh(math.sqrt(2.0 / math.pi) * (y + 0.044715 * (y * y * y))))
    zg = _dot(y, wglu_ref[...]) + bglu_ref[...]
    o_ref[...] = (zg[:, :D_S5] * _sigmoid(zg[:, D_S5:])).astype(o_ref.dtype)


def _s5(proj, tables, b_re, b_im, c_re, c_im, d, w_glu, b_glu, rows):
    L = proj.shape[0]
    NS = S5_GROUPS * S5_STATE
    BS = NS // S5_BLOCKS
    BC = D_S5 // S5_BLOCKS
    t = tables[0].shape[0]
    c2 = lambda shape: pl.BlockSpec(shape, lambda i: (0, 0))
    c3 = lambda shape: pl.BlockSpec(shape, lambda i: (0, 0, 0))
    return pl.pallas_call(
        _s5_kernel,
        grid=(L // rows,),
        in_specs=[pl.BlockSpec((rows, D_S5), lambda i: (i, 0)),
                  c2((t, NS)), c2((t, NS)), c2((t, NS)), c2((t, NS)),
                  c2((1, NS)), c2((1, NS)), c2((1, NS)), c2((1, NS)),
                  c3((S5_BLOCKS, BC, BS)), c3((S5_BLOCKS, BC, BS)),
                  c3((S5_BLOCKS, BS, BC)), c3((S5_BLOCKS, BS, BC)),
                  c2((1, D_S5)), c2((D_S5, 2 * D_S5)), c2((1, 2 * D_S5))],
        out_specs=pl.BlockSpec((rows, D_S5), lambda i: (i, 0)),
        out_shape=jax.ShapeDtypeStruct((L, D_S5), BF16),
        scratch_shapes=[pltpu.VMEM((S5_BLOCKS, BC, 2 * BS), BF16), pltpu.VMEM((8, NS), F32)],
        compiler_params=_cparams(1),
        name="s5",
    )(proj, *tables, b_re, b_im, c_re, c_im, d, w_glu, b_glu)


def _block_diag_groups(w):
    g, r, c = w.shape
    gb = g // S5_BLOCKS
    w = w.reshape(S5_BLOCKS, gb, r, c)
    eye = jnp.eye(gb, dtype=w.dtype)
    out = w[:, :, :, None, :] * eye[None, :, None, :, None]
    return out.reshape(S5_BLOCKS, gb * r, gb * c)


def kernel(x, c, w_ada, b_ada, norm_pre_mix, norm_post_mix, norm_pre_ffn, norm_post_ffn, w_in, rwkv_mu, rwkv_w0, rwkv_w2, rwkv_a0, rwkv_a2, rwkv_g2, rwkv_k_k, rwkv_k_a, rwkv_r_k, rwkv_ln_w, rwkv_ln_b, s5_lam_re, s5_lam_im, s5_log_dt, s5_b_re, s5_b_im, s5_c_re, s5_c_im, s5_d, s5_w_glu, s5_b_glu, w_up_rwkv, w_up_s5, w_out, ffn_w_gate, ffn_w_up, ffn_w_down):
    bsz, L, _ = x.shape
    assert bsz == 1 and w_ada.shape[0] == 1
    h = x.reshape(L, D_MODEL)
    tm = min(1024, L)
    tr = min(256, L)
    row = lambda v: v.reshape(1, -1)

    half = 3 * D_MODEL
    mod, cs = _ada(c.reshape(D_MODEL, 1), w_ada[0], row(b_ada[0]), half)

    wit = jnp.swapaxes(w_in[0], 0, 1)
    gpad = LORA_G_PAD - LORA_G
    n_rkv = 3 * D_RWKV
    mu = row(rwkv_mu[0])
    mu_lora = jnp.concatenate([rwkv_mu[0, n_rkv:], jnp.zeros((gpad,), F32)]).reshape(1, -1)
    g2 = jnp.concatenate([rwkv_g2[0], jnp.zeros((gpad, D_RWKV), F32)], axis=0).astype(BF16)

    xm, lw, a, g = _head(h, row(norm_pre_mix[0]), mod, wit, n_rkv, mu_lora, row(rwkv_w0[0]),
                         row(rwkv_a0[0]), rwkv_w2[0].astype(BF16), rwkv_a2[0].astype(BF16), g2, tr)
    proj_rkv = _matmul_wt(xm, wit, 0, n_rkv, F32, tm, 512, "proj_rkv")
    proj_u = _matmul_wt(xm, wit, RWKV_COLS, D_S5, F32, tm, 512, "proj_u")
    gates, mod2 = _matmul_wt_ada(xm, wit, RWKV_COLS + D_S5, 2 * D_MODEL, BF16, tm, 512,
                                 cs, w_ada[0], row(b_ada[0]), half, half, "proj_g")
    o_a = _rwkv(proj_rkv, mu, lw, a, g, row(rwkv_k_k[0]), row(rwkv_k_a[0]), row(rwkv_r_k[0]),
                row(rwkv_ln_w[0]), row(rwkv_ln_b[0]), min(RWKV_T * RWKV_CHUNKS_PER_STEP, L))

    rep = lambda v: jnp.repeat(v, S5_STATE).reshape(1, -1)
    tables = _s5_tables(row(s5_lam_re[0]), row(s5_lam_im[0]), rep(s5_log_dt[0]), S5_T)
    bt = lambda w: _block_diag_groups(jnp.swapaxes(w, 1, 2))
    o_b = _s5(proj_u, tables, bt(s5_b_re[0]), bt(s5_b_im[0]),
              _block_diag_groups(jnp.swapaxes(s5_c_re[0], 1, 2)).astype(BF16),
              _block_diag_groups(jnp.swapaxes(s5_c_im[0], 1, 2)).astype(BF16),
              row(s5_d[0]), s5_w_glu[0].astype(BF16), row(s5_b_glu[0]), min(S5_ROWS, L))

    merged = _merge(o_a, o_b, w_up_rwkv[0], w_up_s5[0], gates, tm, 512)
    mix = _matmul(merged, w_out[0], BF16, tm, 512, "w_out")

    h1, xf = _mid(h, mix, row(norm_post_mix[0]), row(norm_pre_ffn[0]), mod, mod2, tr)
    act, wd_bf = _ffn_up(xf, ffn_w_gate[0], ffn_w_up[0], ffn_w_down[0], min(2048, L), 256)
    ff = _matmul_ksplit(act, wd_bf, BF16, tm, 1024, D_FF // 2, "ffn_down")
    out = _final(h1, ff, row(norm_post_ffn[0]), mod2, tr)
    return out.reshape(bsz, L, D_MODEL)
```

```python
import functools
import math

import jax
import jax.numpy as jnp
from jax import lax
from jax.experimental import pallas as pl
from jax.experimental.pallas import tpu as pltpu

F32 = jnp.float32
BF16 = jnp.bfloat16

D_MODEL = 4096
RMS_EPS = 1e-6
D_RWKV = 2048
RWKV_HEAD = 64
LORA_W = 128
LORA_A = 128
LORA_G = 480
LORA_G_PAD = 512
GN_EPS = 64e-5
D_S5 = 1024
S5_GROUPS = 64
S5_GROUP_CH = 16
S5_STATE = 64
S5_BLOCKS = 4
D_FF = 11008

LANES = 128
PAIRS = D_RWKV // LANES
RWKV_T = 64
RWKV_CHUNKS_PER_STEP = 16
RWKV_SERIAL_PER_STAGE = 2
S5_T = 128
S5_ROWS = 512

RWKV_COLS = 3 * D_RWKV + LORA_W + LORA_A + LORA_G

VMEM_LIMIT = 56 * 1024 * 1024


def _cparams(n_axes, vmem=VMEM_LIMIT):
    return pltpu.CompilerParams(dimension_semantics=("arbitrary",) * n_axes, vmem_limit_bytes=vmem)


def _dot(a, b):
    return jnp.dot(a.astype(BF16), b.astype(BF16), preferred_element_type=F32)


def _dot_tn(a, b):
    return lax.dot_general(a.astype(BF16), b.astype(BF16), (((0,), (0,)), ((), ())),
                           preferred_element_type=F32)


def _split2(x):
    hi = x.astype(BF16)
    lo = (x - hi.astype(F32)).astype(BF16)
    return hi, lo


def _head_sums(x, sel2):
    half = x.shape[0] // 2
    hi, lo = _split2(jnp.concatenate([x[:half], x[half:]], axis=1))
    res = (jnp.dot(hi, sel2, preferred_element_type=F32) + jnp.dot(lo, sel2, preferred_element_type=F32))
    return jnp.concatenate([res[:, :LANES], res[:, LANES:]], axis=0)


def _dot_f32_rhs(sel, x):
    hi, lo = _split2(x)
    return (jnp.dot(sel, hi, preferred_element_type=F32) + jnp.dot(sel, lo, preferred_element_type=F32))


def _rms(x):
    return x * lax.rsqrt(jnp.mean(x * x, axis=-1, keepdims=True) + RMS_EPS)


def _sigmoid(x):
    return 1.0 / (1.0 + jnp.exp(-x))


def _token_shift(p, halo, mu, first):
    last = jnp.where(first, 0.0, halo[7:8, :])
    rolled = pltpu.roll(p, 1, axis=0)
    row = lax.broadcasted_iota(jnp.int32, p.shape, 0)
    prev = jnp.where(row == 0, last, rolled)
    return p + (prev - p) * mu


def _col_matvec(cs, w_ref, b_ref):
    tn = w_ref.shape[1]
    rows = 512
    acc = jnp.zeros((8, tn), F32)
    for k0 in range(0, D_MODEL, rows):
        blk = w_ref[k0:k0 + rows, :] * cs[k0:k0 + rows, :]
        acc = acc + jnp.sum(blk.reshape(rows // 8, 8, tn), axis=0)
    return jnp.sum(acc, axis=0, keepdims=True) + b_ref[...]


def _ada_kernel(c_ref, w_ref, b_ref, o_ref, cs_ref):
    c = c_ref[...]
    cs = c * _sigmoid(c)
    cs_ref[...] = cs
    o_ref[...] = _col_matvec(cs, w_ref, b_ref)


def _ada(c_col, w_ada, b_ada, n):
    tn = 512
    return pl.pallas_call(
        _ada_kernel,
        grid=(n // tn,),
        in_specs=[pl.BlockSpec((D_MODEL, 1), lambda j: (0, 0)),
                  pl.BlockSpec((D_MODEL, tn), lambda j: (0, j)),
                  pl.BlockSpec((1, tn), lambda j: (0, j))],
        out_specs=[pl.BlockSpec((1, tn), lambda j: (0, j)),
                   pl.BlockSpec((D_MODEL, 1), lambda j: (0, 0))],
        out_shape=[jax.ShapeDtypeStruct((1, n), F32), jax.ShapeDtypeStruct((D_MODEL, 1), F32)],
        compiler_params=_cparams(1),
        name="ada",
    )(c_col, w_ada, b_ada)


def _mid_kernel(x_ref, mix_ref, gpost_ref, gate_ref, gpre_ref, sh_ref, sc_ref, h_ref, xf_ref):
    h = x_ref[...] + gate_ref[...] * (_rms(mix_ref[...].astype(F32)) * gpost_ref[...])
    h_ref[...] = h
    y = _rms(h) * gpre_ref[...]
    xf_ref[...] = (y * (1.0 + sc_ref[...]) + sh_ref[...]).astype(xf_ref.dtype)


def _mid(x, mix, gpost, gpre, mod1, mod2, tm):
    L = x.shape[0]
    row = lambda i: (i, 0)
    vec = lambda k: pl.BlockSpec((1, D_MODEL), lambda i: (0, k))
    return pl.pallas_call(
        _mid_kernel,
        grid=(L // tm,),
        in_specs=[pl.BlockSpec((tm, D_MODEL), row), pl.BlockSpec((tm, D_MODEL), row),
                  vec(0), vec(2), vec(0), vec(0), vec(1)],
        out_specs=[pl.BlockSpec((tm, D_MODEL), row), pl.BlockSpec((tm, D_MODEL), row)],
        out_shape=[jax.ShapeDtypeStruct((L, D_MODEL), F32), jax.ShapeDtypeStruct((L, D_MODEL), BF16)],
        compiler_params=_cparams(1),
        name="mid",
    )(x, mix, gpost, mod1, gpre, mod2, mod2)


def _final_kernel(h_ref, ff_ref, gpost_ref, gate_ref, o_ref):
    o_ref[...] = h_ref[...] + gate_ref[...] * (_rms(ff_ref[...].astype(F32)) * gpost_ref[...])


def _final(h, ff, gpost, mod, tm):
    L = h.shape[0]
    row = lambda i: (i, 0)
    return pl.pallas_call(
        _final_kernel,
        grid=(L // tm,),
        in_specs=[pl.BlockSpec((tm, D_MODEL), row), pl.BlockSpec((tm, D_MODEL), row),
                  pl.BlockSpec((1, D_MODEL), lambda i: (0, 0)),
                  pl.BlockSpec((1, D_MODEL), lambda i: (0, 2))],
        out_specs=pl.BlockSpec((tm, D_MODEL), row),
        out_shape=jax.ShapeDtypeStruct((L, D_MODEL), F32),
        compiler_params=_cparams(1),
        name="final",
    )(h, ff, gpost, mod)


def _mm_kernel(a_ref, w_ref, o_ref):
    o_ref[...] = jnp.dot(a_ref[...], w_ref[...].astype(BF16),
                         preferred_element_type=F32).astype(o_ref.dtype)


def _matmul(a, w, out_dtype, tm, tn, name):
    m, k = a.shape
    n = w.shape[1]
    return pl.pallas_call(
        _mm_kernel,
        grid=(m // tm, n // tn),
        in_specs=[pl.BlockSpec((tm, k), lambda i, j: (i, 0)),
                  pl.BlockSpec((k, tn), lambda i, j: (0, j))],
        out_specs=pl.BlockSpec((tm, tn), lambda i, j: (i, j)),
        out_shape=jax.ShapeDtypeStruct((m, n), out_dtype),
        compiler_params=_cparams(2),
        name=name,
    )(a, w)


def _mm_acc_kernel(a_ref, w_ref, o_ref, acc_ref):
    kk = pl.program_id(2)
    last = pl.num_programs(2) - 1
    part = jnp.dot(a_ref[...], w_ref[...], preferred_element_type=F32)

    @pl.when(kk == 0)
    def _():
        acc_ref[...] = part

    @pl.when((kk > 0) & (kk < last))
    def _():
        acc_ref[...] = acc_ref[...] + part

    @pl.when(kk == last)
    def _():
        o_ref[...] = (acc_ref[...] + part).astype(o_ref.dtype)


def _dot_wt(a, wt):
    return lax.dot_general(a, wt.astype(BF16), (((1,), (1,)), ((), ())), preferred_element_type=F32)


def _mm_wt_kernel(a_ref, wt_ref, o_ref):
    o_ref[...] = _dot_wt(a_ref[...], wt_ref[...]).astype(o_ref.dtype)


def _matmul_wt(a, wt, row0, n, out_dtype, tm, tn, name):
    m, k = a.shape
    return pl.pallas_call(
        _mm_wt_kernel,
        grid=(m // tm, n // tn),
        in_specs=[pl.BlockSpec((tm, k), lambda i, j: (i, 0)),
                  pl.BlockSpec((pl.Element(tn), pl.Element(k)),
                               lambda i, j: (pl.multiple_of(row0 + j * tn, 8), 0))],
        out_specs=pl.BlockSpec((tm, tn), lambda i, j: (i, j)),
        out_shape=jax.ShapeDtypeStruct((m, n), out_dtype),
        compiler_params=_cparams(2),
        name=name,
    )(a, wt)


def _mm_wt_ada_kernel(a_ref, wt_ref, cs_ref, wada_ref, bada_ref, o_ref, mod_ref):
    o_ref[...] = _dot_wt(a_ref[...], wt_ref[...]).astype(o_ref.dtype)
    mod_ref[...] = _col_matvec(cs_ref[...], wada_ref, bada_ref)


def _matmul_wt_ada(a, wt, row0, n, out_dtype, tm, tn, cs, w_ada, b_ada, col0, ncols, name):
    m, k = a.shape
    ni, nj = m // tm, n // tn
    cps = LANES * -(-(ncols // LANES) // (ni * nj))
    assert ncols % cps == 0 and col0 % cps == 0 and ncols // cps <= ni * nj
    nblk = ncols // cps
    blk = lambda i, j: jnp.minimum(i * nj + j, nblk - 1)
    return pl.pallas_call(
        _mm_wt_ada_kernel,
        grid=(ni, nj),
        in_specs=[pl.BlockSpec((tm, k), lambda i, j: (i, 0)),
                  pl.BlockSpec((pl.Element(tn), pl.Element(k)),
                               lambda i, j: (pl.multiple_of(row0 + j * tn, 8), 0)),
                  pl.BlockSpec((D_MODEL, 1), lambda i, j: (0, 0)),
                  pl.BlockSpec((D_MODEL, cps), lambda i, j: (0, col0 // cps + blk(i, j))),
                  pl.BlockSpec((1, cps), lambda i, j: (0, col0 // cps + blk(i, j)))],
        out_specs=[pl.BlockSpec((tm, tn), lambda i, j: (i, j)),
                   pl.BlockSpec((1, cps), lambda i, j: (0, blk(i, j)))],
        out_shape=[jax.ShapeDtypeStruct((m, n), out_dtype), jax.ShapeDtypeStruct((1, ncols), F32)],
        compiler_params=_cparams(2),
        name=name,
    )(a, wt, cs, w_ada, b_ada)


def _matmul_ksplit(a, w, out_dtype, tm, tn, tk, name):
    m, k = a.shape
    n = w.shape[1]
    assert k % tk == 0 and k // tk >= 2
    return pl.pallas_call(
        _mm_acc_kernel,
        grid=(m // tm, n // tn, k // tk),
        in_specs=[pl.BlockSpec((tm, tk), lambda i, j, kk: (i, kk)),
                  pl.BlockSpec((tk, tn), lambda i, j, kk: (kk, j))],
        out_specs=pl.BlockSpec((tm, tn), lambda i, j, kk: (i, j)),
        out_shape=jax.ShapeDtypeStruct((m, n), out_dtype),
        scratch_shapes=[pltpu.VMEM((tm, tn), F32)],
        compiler_params=_cparams(3),
        name=name,
    )(a, w)


def _merge_kernel(oa_ref, ob_ref, wa_ref, wb_ref, ga_ref, gb_ref, o_ref):
    ya = jnp.dot(oa_ref[...], wa_ref[...].astype(BF16), preferred_element_type=F32)
    yb = jnp.dot(ob_ref[...], wb_ref[...].astype(BF16), preferred_element_type=F32)
    m = _sigmoid(ga_ref[...].astype(F32)) * ya + _sigmoid(gb_ref[...].astype(F32)) * yb
    o_ref[...] = m.astype(o_ref.dtype)


def _merge(o_a, o_b, w_up_a, w_up_b, gates, tm, tn):
    L = o_a.shape[0]
    nb = D_MODEL // tn
    return pl.pallas_call(
        _merge_kernel,
        grid=(L // tm, nb),
        in_specs=[pl.BlockSpec((tm, D_RWKV), lambda i, j: (i, 0)),
                  pl.BlockSpec((tm, D_S5), lambda i, j: (i, 0)),
                  pl.BlockSpec((D_RWKV, tn), lambda i, j: (0, j)),
                  pl.BlockSpec((D_S5, tn), lambda i, j: (0, j)),
                  pl.BlockSpec((tm, tn), lambda i, j: (i, j)),
                  pl.BlockSpec((tm, tn), lambda i, j: (i, j + nb))],
        out_specs=pl.BlockSpec((tm, tn), lambda i, j: (i, j)),
        out_shape=jax.ShapeDtypeStruct((L, D_MODEL), BF16),
        compiler_params=_cparams(2),
        name="merge",
    )(o_a, o_b, w_up_a, w_up_b, gates, gates)


def _ffn_up_kernel(x_ref, wg_ref, wu_ref, wd_ref, o_ref, wd_bf_ref):
    x = x_ref[...]
    a = jnp.dot(x, wg_ref[...].astype(BF16), preferred_element_type=F32)
    b = jnp.dot(x, wu_ref[...].astype(BF16), preferred_element_type=F32)
    o_ref[...] = ((a * _sigmoid(a)) * b).astype(o_ref.dtype)
    wd_bf_ref[...] = wd_ref[...].astype(BF16)


def _ffn_up(xf, w_gate, w_up, w_down, tm, tn):
    L = xf.shape[0]
    n = w_gate.shape[1]
    ni, nj = L // tm, n // tn
    slab = w_down.shape[0] // (ni * nj)
    assert slab * ni * nj == w_down.shape[0] and slab % 16 == 0
    wd_spec = lambda: pl.BlockSpec((slab, D_MODEL), lambda i, j: (i * nj + j, 0))
    return pl.pallas_call(
        _ffn_up_kernel,
        grid=(ni, nj),
        in_specs=[pl.BlockSpec((tm, D_MODEL), lambda i, j: (i, 0), pipeline_mode=pl.Buffered(1)),
                  pl.BlockSpec((D_MODEL, tn), lambda i, j: (0, j)),
                  pl.BlockSpec((D_MODEL, tn), lambda i, j: (0, j)),
                  wd_spec()],
        out_specs=[pl.BlockSpec((tm, tn), lambda i, j: (i, j)), wd_spec()],
        out_shape=[jax.ShapeDtypeStruct((L, n), BF16), jax.ShapeDtypeStruct(w_down.shape, BF16)],
        compiler_params=_cparams(2),
        name="ffn_up",
    )(xf, w_gate, w_up, w_down)


def _head_kernel(x_ref, gpre_ref, sh_ref, sc_ref, wl_ref, mu_ref, w0_ref, a0_ref, w2_ref, a2_ref,
                 g2_ref, xm_ref, lw_ref, a_ref, g_ref, wl_bf_ref, carry_ref):
    first = pl.program_id(0) == 0

    @pl.when(first)
    def _():
        wl_bf_ref[...] = wl_ref[...].astype(BF16)
        carry_ref[...] = jnp.zeros_like(carry_ref)

    y = _rms(x_ref[...]) * gpre_ref[...]
    xm = (y * (1.0 + sc_ref[...]) + sh_ref[...]).astype(BF16)
    xm_ref[...] = xm
    p = lax.dot_general(xm, wl_bf_ref[...], (((1,), (1,)), ((), ())), preferred_element_type=F32)
    ps = _token_shift(p, carry_ref[...], mu_ref[...], first)
    carry_ref[...] = p[p.shape[0] - 8:, :]
    xw = ps[:, :LORA_W]
    xa = ps[:, LORA_W:LORA_W + LORA_A]
    xg = ps[:, LORA_W + LORA_A:]
    u = w0_ref[...] + _dot(jnp.tanh(xw), w2_ref[...])
    lw_ref[...] = -math.exp(-0.5) * _sigmoid(u)
    a_ref[...] = _sigmoid(a0_ref[...] + _dot(xa, a2_ref[...]))
    g_ref[...] = _dot(_sigmoid(xg), g2_ref[...])


def _head(x, gpre, mod, wt, row_lora, mu, w0, a0, w2, a2, g2, tm):
    L = x.shape[0]
    nl = mu.shape[1]
    full = lambda shape: pl.BlockSpec(shape, lambda i: (0, 0))
    rows = lambda width: pl.BlockSpec((tm, width), lambda i: (i, 0))
    return pl.pallas_call(
        _head_kernel,
        grid=(L // tm,),
        in_specs=[rows(D_MODEL), full((1, D_MODEL)),
                  pl.BlockSpec((1, D_MODEL), lambda i: (0, 0)), pl.BlockSpec((1, D_MODEL), lambda i: (0, 1)),
                  pl.BlockSpec((pl.Element(nl), pl.Element(D_MODEL)), lambda i: (row_lora, 0),
                               pipeline_mode=pl.Buffered(1)),
                  full((1, nl)), full((1, D_RWKV)), full((1, D_RWKV)),
                  full((LORA_W, D_RWKV)), full((LORA_A, D_RWKV)), full((LORA_G_PAD, D_RWKV))],
        out_specs=[rows(D_MODEL), rows(D_RWKV), rows(D_RWKV), rows(D_RWKV)],
        out_shape=[jax.ShapeDtypeStruct((L, D_MODEL), BF16)] + [jax.ShapeDtypeStruct((L, D_RWKV), F32)] * 3,
        scratch_shapes=[pltpu.VMEM((nl, D_MODEL), BF16), pltpu.VMEM((8, nl), F32)],
        compiler_params=_cparams(1),
        name="head",
    )(x, gpre, mod, mod, wt, mu, w0, a0, w2, a2, g2)


def _rwkv_prep(r, k, v, am, bm, lw, cst, T):
    n = r.shape[0] // T
    lane_lo, tri_t = cst[0], cst[1]

    def chunk(x, c):
        return x[c * T:(c + 1) * T]

    def stack_own(xc):
        return jnp.concatenate([jnp.where(lane_lo, xc, 0.0), jnp.where(lane_lo, 0.0, xc)], axis=0)

    c_all = _dot_f32_rhs(tri_t, jnp.concatenate([chunk(lw, c) for c in range(n)], axis=1))
    yield
    lhs, rhs, bk, vv, dec = [], [], [], [], []
    for c in range(n):
        cu = c_all[:, c * LANES:(c + 1) * LANES]
        c_end = cu[T - 1:T, :]
        e_neg = jnp.exp(-cu)
        e_end = jnp.exp(c_end - cu)
        rc, kc_, ac, bc_ = chunk(r, c), chunk(k, c), chunk(am, c), chunk(bm, c)
        lhs.append(jnp.concatenate([stack_own(ac * jnp.exp(cu - chunk(lw, c))),
                                    stack_own(rc * jnp.exp(cu))], axis=0).astype(BF16))
        rhs.append(jnp.concatenate([bc_ * e_neg, kc_ * e_neg], axis=0).astype(BF16))
        bk.append(jnp.concatenate([stack_own(bc_ * e_end), stack_own(kc_ * e_end)], axis=0).astype(BF16))
        vv.append(chunk(v, c).astype(BF16))
        dec.append(jnp.broadcast_to(jnp.exp(c_end), (8, LANES)))
        if c % 2 == 1:
            yield
    return lhs, rhs, bk, vv, dec


def _rwkv_core(lhs, rhs, bk, vv, dec, cst, T):
    S = 2 * T
    n = len(lhs)
    _, _, strict, incl2, eye, same_head = cst
    cr = range(n)
    dup = lambda x: jnp.concatenate([x, x], axis=0)

    big = [lax.dot_general(lhs[c], jnp.concatenate([dup(rhs[c][:T]), dup(rhs[c][T:])], axis=0),
                           (((1,), (1,)), ((), ())), preferred_element_type=F32) for c in cr]
    yield
    ah = [lhs[c][:S] for c in cr]
    rh = [lhs[c][S:].astype(F32) for c in cr]
    v_s = [dup(vv[c]) for c in cr]
    nj = [jnp.where(strict, big[c][:S, :S], 0.0) for c in cr]
    m_r = [jnp.where(incl2, big[c][S:, :], 0.0) for c in cr]
    av = [_dot(jnp.where(strict, big[c][:S, S:], 0.0), v_s[c]) for c in cr]
    yield

    steps = T.bit_length() - 1
    p = [jnp.where(eye, 1.0, 0.0) + nj[c] for c in cr]
    nj = [_dot(nj[c], nj[c]) for c in cr]
    yield
    for j in range(1, steps):
        if j + 1 < steps:
            res = [_dot(nj[c], jnp.concatenate([p[c], nj[c]], axis=1)) for c in cr]
            p = [p[c] + res[c][:, :S] for c in cr]
            nj = [res[c][:, S:] for c in cr]
        else:
            p = [p[c] + _dot(nj[c], p[c]) for c in cr]
        yield
    x = [_dot(p[c], jnp.concatenate([ah[c].astype(F32), av[c]], axis=1)) for c in cr]
    yield

    wv = [jnp.concatenate([x[c].astype(BF16),
                           jnp.concatenate([jnp.zeros_like(v_s[c]), v_s[c]], axis=1)], axis=0)
          for c in cr]
    y = [_dot(m_r[c], wv[c]) for c in cr]
    q = [rh[c] + y[c][:, :LANES] for c in cr]
    o0 = [y[c][:, LANES:] for c in cr]
    gd = [_dot_tn(bk[c], wv[c]) for c in cr]
    gd = [jnp.concatenate([gd[c][:, :LANES], jnp.where(same_head, gd[c][:, LANES:], 0.0)], axis=1)
          for c in cr]
    decay = [jnp.broadcast_to(jnp.sum(jnp.where(eye, dec[c][0:1, :], 0.0), axis=1, keepdims=True),
                              (LANES, LANES)) for c in cr]
    return gd, q, o0, decay


def _rwkv_kernel(r_ref, k_ref, v_ref, rh_ref, kh_ref, vh_ref, mur_ref, muk_ref, muv_ref,
                 lw_ref, a_ref, kk_ref, ka_ref, rk_ref, g_ref, lnw_ref, lnb_ref,
                 o_ref, h_ref, gd_ref, q_ref, o0_ref, dec_ref, bonus_ref,
                 lhs_ref, rhs_ref, bk_ref, vv_ref, decr_ref, *, nblk):
    s = pl.program_id(0)
    last = pl.num_programs(0) - 3
    first_cur = lax.rem(jnp.minimum(s, last), nblk) == 0
    first_out = lax.rem(jnp.maximum(s - 2, 0), nblk) == 0

    @pl.when(s == 0)
    def _():
        for ref in (h_ref, gd_ref, q_ref, o0_ref, dec_ref, bonus_ref,
                    lhs_ref, rhs_ref, bk_ref, vv_ref, decr_ref):
            ref[...] = jnp.zeros_like(ref)

    rows = r_ref.shape[0]
    T = RWKV_T
    S = 2 * T
    n = rows // T
    ii = lax.broadcasted_iota(jnp.int32, (S, S), 0)
    jj = lax.broadcasted_iota(jnp.int32, (S, S), 1)
    strict = ((ii // T) == (jj // T)) & (jj < ii)
    eye = ii == jj
    ii2 = lax.broadcasted_iota(jnp.int32, (S, 2 * S), 0)
    jj2 = lax.broadcasted_iota(jnp.int32, (S, 2 * S), 1) % S
    incl2 = ((ii2 // T) == (jj2 // T)) & (jj2 <= ii2)
    hi_ = lax.broadcasted_iota(jnp.int32, (2 * LANES, 2 * LANES), 0) // RWKV_HEAD
    hj_ = lax.broadcasted_iota(jnp.int32, (2 * LANES, 2 * LANES), 1) // RWKV_HEAD
    head_sum = jnp.where(hi_ == hj_, 1.0, 0.0).astype(BF16)
    lane_lo = lax.broadcasted_iota(jnp.int32, (T, LANES), 1) < RWKV_HEAD
    it = lax.broadcasted_iota(jnp.int32, (T, T), 0)
    jt = lax.broadcasted_iota(jnp.int32, (T, T), 1)
    tri_t = jnp.where(jt <= it, 1.0, 0.0).astype(BF16)
    same_head = ((lax.broadcasted_iota(jnp.int32, (LANES, LANES), 0) // RWKV_HEAD)
                 == (lax.broadcasted_iota(jnp.int32, (LANES, LANES), 1) // RWKV_HEAD))
    cst = (lane_lo, tri_t, strict, incl2, eye, same_head)
    slot = lax.rem(s, 2)

    state = [jnp.where(first_out, 0.0, h_ref[...])]
    outs = []

    def serial_step(c):
        h = state[0]
        gd = gd_ref[c]
        z = _dot(jnp.concatenate([gd[:, :LANES], q_ref[c]], axis=0), h)
        o_s = z[LANES:] + o0_ref[c]
        outs.append(jnp.where(lane_lo, o_s[:T], o_s[T:]))
        state[0] = dec_ref[c] * h + z[:LANES] + gd[:, LANES:]

    def serial_finish():
        h_ref[...] = state[0]
        o = jnp.concatenate(outs, axis=0) if n > 1 else outs[0]
        inv = 1.0 / RWKV_HEAD
        mean = _head_sums(o, head_sum) * inv
        d = o - mean
        var = _head_sums(d * d, head_sum) * inv
        y = d * lax.rsqrt(var + GN_EPS) * lnw_ref[...] + lnb_ref[...]
        o_ref[...] = ((y + bonus_ref[slot]) * g_ref[...]).astype(o_ref.dtype)

    core = _rwkv_core([lhs_ref[c] for c in range(n)], [rhs_ref[c] for c in range(n)],
                      [bk_ref[c] for c in range(n)], [vv_ref[c] for c in range(n)],
                      [decr_ref[c] for c in range(n)], cst, T)

    def prepare():
        r = _token_shift(r_ref[...], rh_ref[...], mur_ref[...], first_cur)
        k = _token_shift(k_ref[...], kh_ref[...], muk_ref[...], first_cur)
        v = _token_shift(v_ref[...], vh_ref[...], muv_ref[...], first_cur)
        a = a_ref[...]
        kk = k * kk_ref[...]
        k = k * (1.0 + (a - 1.0) * ka_ref[...])
        yield
        ss = _head_sums(kk * kk, head_sum)
        bonus = _head_sums(r * k * rk_ref[...], head_sum) * v
        yield
        kk = kk * lax.rsqrt(jnp.maximum(ss, 1e-24))
        operands = yield from _rwkv_prep(r, k, v, -kk, kk * a, lw_ref[...], cst, T)
        return operands + (bonus,)

    serial_step(0)
    prep = prepare()

    results = {}
    gens = {"core": core, "prep": prep}
    done = 1
    while gens:
        for name in ("core", "prep"):
            if name not in gens:
                continue
            try:
                next(gens[name])
            except StopIteration as stop:
                results[name] = stop.value
                del gens[name]
            if name == "core":
                for _ in range(RWKV_SERIAL_PER_STAGE):
                    if done < n:
                        serial_step(done)
                    elif done == n:
                        serial_finish()
                    done += 1
    assert done > n, "fewer matmul stages than chunks per block"

    gd, q, o0, decay = results["core"]
    lhs, rhs, bk, vv, dec, bonus = results["prep"]
    for c in range(n):
        gd_ref[c] = gd[c]
        q_ref[c] = q[c]
        o0_ref[c] = o0[c]
        dec_ref[c] = decay[c]
        lhs_ref[c] = lhs[c]
        rhs_ref[c] = rhs[c]
        bk_ref[c] = bk[c]
        vv_ref[c] = vv[c]
        decr_ref[c] = dec[c]
    bonus_ref[slot] = bonus


def _rwkv(proj, mu, lw, a, g, k_k, k_a, r_k, ln_w, ln_b, rows):
    L = proj.shape[0]
    nb = D_RWKV // LANES
    nblk = L // rows
    nsteps = PAIRS * nblk
    n = rows // RWKV_T

    def cur(s):
        b = jnp.minimum(s, nsteps - 1)
        return lax.div(b, nblk), lax.rem(b, nblk)

    def prv(s):
        b = jnp.maximum(s - 2, 0)
        return lax.div(b, nblk), lax.rem(b, nblk)

    def col(off):
        return pl.BlockSpec((rows, LANES), lambda s: (cur(s)[1], cur(s)[0] + off))

    def halo(off):
        return pl.BlockSpec((8, LANES),
                            lambda s: (jnp.maximum(cur(s)[1] * (rows // 8) - 1, 0), cur(s)[0] + off))

    def vec(off):
        return pl.BlockSpec((1, LANES), lambda s: (0, cur(s)[0] + off))

    def lag_col():
        return pl.BlockSpec((rows, LANES), lambda s: (prv(s)[1], prv(s)[0]))

    def lag_vec():
        return pl.BlockSpec((1, LANES), lambda s: (0, prv(s)[0]))

    return pl.pallas_call(
        functools.partial(_rwkv_kernel, nblk=nblk),
        grid=(nsteps + 2,),
        in_specs=[col(0), col(nb), col(2 * nb), halo(0), halo(nb), halo(2 * nb),
                  vec(0), vec(nb), vec(2 * nb),
                  col(0), col(0), vec(0), vec(0), vec(0),
                  lag_col(), lag_vec(), lag_vec()],
        out_specs=lag_col(),
        out_shape=jax.ShapeDtypeStruct((L, D_RWKV), BF16),
        scratch_shapes=[pltpu.VMEM((LANES, LANES), F32),
                        pltpu.VMEM((n, LANES, 2 * LANES), F32),
                        pltpu.VMEM((n, 2 * RWKV_T, LANES), F32),
                        pltpu.VMEM((n, 2 * RWKV_T, LANES), F32),
                        pltpu.VMEM((n, LANES, LANES), F32),
                        pltpu.VMEM((2, rows, LANES), F32),
                        pltpu.VMEM((n, 4 * RWKV_T, LANES), BF16),
                        pltpu.VMEM((n, 2 * RWKV_T, LANES), BF16),
                        pltpu.VMEM((n, 4 * RWKV_T, LANES), BF16),
                        pltpu.VMEM((n, RWKV_T, LANES), BF16),
                        pltpu.VMEM((n, 8, LANES), F32)],
        compiler_params=_cparams(1),
        name="rwkv",
    )(proj, proj, proj, proj, proj, proj, mu, mu, mu, lw, a, k_k, k_a, r_k, g, ln_w, ln_b)


def _s5_table_kernel(lr_ref, li_ref, ldt_ref, pos_re, pos_im, neg_re, neg_im, z_re, z_im,
                     lbc_re, lbc_im):
    lr = lr_ref[...]
    li = li_ref[...]
    dt = jnp.exp(ldt_ref[...])
    centre = pos_re.shape[0] // 2
    t = (lax.broadcasted_iota(jnp.int32, pos_re.shape, 0) - centre).astype(F32)
    mag = jnp.exp(t * (lr * dt))
    ang = t * (li * dt)
    cs, sn = jnp.cos(ang), jnp.sin(ang)
    pos_re[...] = mag * cs
    pos_im[...] = mag * sn
    inv = 1.0 / mag
    neg_re[...] = inv * cs
    neg_im[...] = -(inv * sn)
    mc = jnp.exp((centre + 1.0) * (lr * dt))
    lbc_re[...] = mc * jnp.cos((centre + 1.0) * (li * dt))
    lbc_im[...] = mc * jnp.sin((centre + 1.0) * (li * dt))
    m1 = jnp.exp(lr * dt)
    lb_re = m1 * jnp.cos(li * dt)
    lb_im = m1 * jnp.sin(li * dt)
    den = lr * lr + li * li
    z_re[...] = ((lb_re - 1.0) * lr + lb_im * li) / den
    z_im[...] = (lb_im * lr - (lb_re - 1.0) * li) / den


def _s5_tables(lr, li, ldt, rows):
    n = lr.shape[1]
    full = pl.BlockSpec((1, n), lambda: (0, 0))
    tab = pl.BlockSpec((rows, n), lambda: (0, 0))
    return pl.pallas_call(
        _s5_table_kernel,
        in_specs=[full, full, full],
        out_specs=[tab, tab, tab, tab, full, full, full, full],
        out_shape=[jax.ShapeDtypeStruct((rows, n), F32)] * 4 + [jax.ShapeDtypeStruct((1, n), F32)] * 4,
        name="s5_tables",
    )(lr, li, ldt)


def _s5_kernel(u_ref, pos_re_ref, pos_im_ref, neg_re_ref, neg_im_ref, z_re_ref, z_im_ref,
               lbc_re_ref, lbc_im_ref, b_re_ref, b_im_ref, c_re_ref, c_im_ref, d_ref, wglu_ref,
               bglu_ref, o_ref, bb_ref, st_ref):
    R = u_ref.shape[0]
    T = pos_re_ref.shape[0]
    NS = S5_GROUPS * S5_STATE
    BS = NS // S5_BLOCKS
    BC = D_S5 // S5_BLOCKS

    @pl.when(pl.program_id(0) == 0)
    def _():
        st_ref[...] = jnp.zeros_like(st_ref)
        for j in range(S5_BLOCKS):
            zr = z_re_ref[:, j * BS:(j + 1) * BS]
            zi = z_im_ref[:, j * BS:(j + 1) * BS]
            br = b_re_ref[j]
            bi = b_im_ref[j]
            bb_ref[j, :, :BS] = (zr * br - zi * bi).astype(BF16)
            bb_ref[j, :, BS:] = (zr * bi + zi * br).astype(BF16)

    u = u_ref[...]
    ub = u.astype(BF16)
    ii = lax.broadcasted_iota(jnp.int32, (T, T), 0)
    jj = lax.broadcasted_iota(jnp.int32, (T, T), 1)
    tri = jnp.where(jj <= ii, 1.0, 0.0).astype(BF16)

    ys = []
    for j in range(S5_BLOCKS):
        sl = slice(j * BS, (j + 1) * BS)
        bu = jnp.dot(ub[:, j * BC:(j + 1) * BC], bb_ref[j], preferred_element_type=F32)
        nr, ni = neg_re_ref[:, sl], neg_im_ref[:, sl]
        pr, pi = pos_re_ref[:, sl], pos_im_ref[:, sl]
        lr_, li_ = lbc_re_ref[:, sl], lbc_im_ref[:, sl]
        s_re = st_ref[0:1, sl]
        s_im = st_ref[1:2, sl]
        xs_re, xs_im = [], []
        for t in range(R // T):
            bu_re, bu_im = bu[t * T:(t + 1) * T, :BS], bu[t * T:(t + 1) * T, BS:]
            zz = jnp.concatenate([nr * bu_re - ni * bu_im, nr * bu_im + ni * bu_re], axis=1)
            acc = jnp.dot(tri, zz.astype(BF16), preferred_element_type=F32)
            a_re = acc[:, :BS] + (lr_ * s_re - li_ * s_im)
            a_im = acc[:, BS:] + (lr_ * s_im + li_ * s_re)
            x_re = pr * a_re - pi * a_im
            x_im = pr * a_im + pi * a_re
            s_re, s_im = x_re[T - 1:T, :], x_im[T - 1:T, :]
            xs_re.append(x_re)
            xs_im.append(x_im)
        st_ref[0:1, sl] = s_re
        st_ref[1:2, sl] = s_im
        cat = lambda xs: jnp.concatenate(xs, axis=0) if len(xs) > 1 else xs[0]
        ys.append(_dot(cat(xs_re), c_re_ref[j]) - _dot(cat(xs_im), c_im_ref[j]))
    y = jnp.concatenate(ys, axis=1) + d_ref[...] * u
    y = 0.5 * y * (1.0 + jnp.tanh(math.sqrt(2.0 / math.pi) * (y + 0.044715 * (y * y * y))))
    zg = _dot(y, wglu_ref[...]) + bglu_ref[...]
    o_ref[...] = (zg[:, :D_S5] * _sigmoid(zg[:, D_S5:])).astype(o_ref.dtype)


def _s5(proj, tables, b_re, b_im, c_re, c_im, d, w_glu, b_glu, rows):
    L = proj.shape[0]
    NS = S5_GROUPS * S5_STATE
    BS = NS // S5_BLOCKS
    BC = D_S5 // S5_BLOCKS
    t = tables[0].shape[0]
    c2 = lambda shape: pl.BlockSpec(shape, lambda i: (0, 0))
    c3 = lambda shape: pl.BlockSpec(shape, lambda i: (0, 0, 0))
    return pl.pallas_call(
        _s5_kernel,
        grid=(L // rows,),
        in_specs=[pl.BlockSpec((rows, D_S5), lambda i: (i, 0)),
                  c2((t, NS)), c2((t, NS)), c2((t, NS)), c2((t, NS)),
                  c2((1, NS)), c2((1, NS)), c2((1, NS)), c2((1, NS)),
                  c3((S5_BLOCKS, BC, BS)), c3((S5_BLOCKS, BC, BS)),
                  c3((S5_BLOCKS, BS, BC)), c3((S5_BLOCKS, BS, BC)),
                  c2((1, D_S5)), c2((D_S5, 2 * D_S5)), c2((1, 2 * D_S5))],
        out_specs=pl.BlockSpec((rows, D_S5), lambda i: (i, 0)),
        out_shape=jax.ShapeDtypeStruct((L, D_S5), BF16),
        scratch_shapes=[pltpu.VMEM((S5_BLOCKS, BC, 2 * BS), BF16), pltpu.VMEM((8, NS), F32)],
        compiler_params=_cparams(1),
        name="s5",
    )(proj, *tables, b_re, b_im, c_re, c_im, d, w_glu, b_glu)


def _block_diag_groups(w):
    g, r, c = w.shape
    gb = g // S5_BLOCKS
    w = w.reshape(S5_BLOCKS, gb, r, c)
    eye = jnp.eye(gb, dtype=w.dtype)
    out = w[:, :, :, None, :] * eye[None, :, None, :, None]
    return out.reshape(S5_BLOCKS, gb * r, gb * c)


def kernel(x, c, w_ada, b_ada, norm_pre_mix, norm_post_mix, norm_pre_ffn, norm_post_ffn, w_in, rwkv_mu, rwkv_w0, rwkv_w2, rwkv_a0, rwkv_a2, rwkv_g2, rwkv_k_k, rwkv_k_a, rwkv_r_k, rwkv_ln_w, rwkv_ln_b, s5_lam_re, s5_lam_im, s5_log_dt, s5_b_re, s5_b_im, s5_c_re, s5_c_im, s5_d, s5_w_glu, s5_b_glu, w_up_rwkv, w_up_s5, w_out, ffn_w_gate, ffn_w_up, ffn_w_down):
    bsz, L, _ = x.shape
    assert bsz == 1 and w_ada.shape[0] == 1
    h = x.reshape(L, D_MODEL)
    tm = min(1024, L)
    tr = min(256, L)
    row = lambda v: v.reshape(1, -1)

    half = 3 * D_MODEL
    mod, cs = _ada(c.reshape(D_MODEL, 1), w_ada[0], row(b_ada[0]), half)

    wit = jnp.swapaxes(w_in[0], 0, 1)
    gpad = LORA_G_PAD - LORA_G
    n_rkv = 3 * D_RWKV
    mu = row(rwkv_mu[0])
    mu_lora = jnp.concatenate([rwkv_mu[0, n_rkv:], jnp.zeros((gpad,), F32)]).reshape(1, -1)
    g2 = jnp.concatenate([rwkv_g2[0], jnp.zeros((gpad, D_RWKV), F32)], axis=0).astype(BF16)

    xm, lw, a, g = _head(h, row(norm_pre_mix[0]), mod, wit, n_rkv, mu_lora, row(rwkv_w0[0]),
                         row(rwkv_a0[0]), rwkv_w2[0].astype(BF16), rwkv_a2[0].astype(BF16), g2, tr)
    proj_rkv = _matmul_wt(xm, wit, 0, n_rkv, F32, tm, 512, "proj_rkv")
    proj_u = _matmul_wt(xm, wit, RWKV_COLS, D_S5, F32, tm, 512, "proj_u")
    gates, mod2 = _matmul_wt_ada(xm, wit, RWKV_COLS + D_S5, 2 * D_MODEL, BF16, tm, 512,
                                 cs, w_ada[0], row(b_ada[0]), half, half, "proj_g")
    o_a = _rwkv(proj_rkv, mu, lw, a, g, row(rwkv_k_k[0]), row(rwkv_k_a[0]), row(rwkv_r_k[0]),
                row(rwkv_ln_w[0]), row(rwkv_ln_b[0]), min(RWKV_T * RWKV_CHUNKS_PER_STEP, L))

    rep = lambda v: jnp.repeat(v, S5_STATE).reshape(1, -1)
    tables = _s5_tables(row(s5_lam_re[0]), row(s5_lam_im[0]), rep(s5_log_dt[0]), S5_T)
    bt = lambda w: _block_diag_groups(jnp.swapaxes(w, 1, 2))
    o_b = _s5(proj_u, tables, bt(s5_b_re[0]), bt(s5_b_im[0]),
              _block_diag_groups(jnp.swapaxes(s5_c_re[0], 1, 2)).astype(BF16),
              _block_diag_groups(jnp.swapaxes(s5_c_im[0], 1, 2)).astype(BF16),
              row(s5_d[0]), s5_w_glu[0].astype(BF16), row(s5_b_glu[0]), min(S5_ROWS, L))

    merged = _merge(o_a, o_b, w_up_rwkv[0], w_up_s5[0], gates, tm, 512)
    mix = _matmul(merged, w_out[0], BF16, tm, 512, "w_out")

    h1, xf = _mid(h, mix, row(norm_post_mix[0]), row(norm_pre_ffn[0]), mod, mod2, tr)
    act, wd_bf = _ffn_up(xf, ffn_w_gate[0], ffn_w_up[0], ffn_w_down[0], min(2048, L), 256)
    ff = _matmul_ksplit(act, wd_bf, BF16, tm, 1024, D_FF // 2, "ffn_down")
    out = _final(h1, ff, row(norm_post_ffn[0]), mod2, tr)
    return out.reshape(bsz, L, D_MODEL)
```

```python
import functools
import math

import jax
import jax.numpy as jnp
from jax import lax
from jax.experimental import pallas as pl
from jax.experimental.pallas import tpu as pltpu

F32 = jnp.float32
BF16 = jnp.bfloat16

D_MODEL = 4096
RMS_EPS = 1e-6
D_RWKV = 2048
RWKV_HEAD = 64
LORA_W = 128
LORA_A = 128
LORA_G = 480
LORA_G_PAD = 512
GN_EPS = 64e-5
D_S5 = 1024
S5_GROUPS = 64
S5_GROUP_CH = 16
S5_STATE = 64
S5_BLOCKS = 4
D_FF = 11008

LANES = 128
PAIRS = D_RWKV // LANES
RWKV_T = 64
RWKV_CHUNKS_PER_STEP = 16
RWKV_SERIAL_PER_STAGE = 2
S5_T = 128
S5_ROWS = 512

RWKV_COLS = 3 * D_RWKV + LORA_W + LORA_A + LORA_G

VMEM_LIMIT = 56 * 1024 * 1024


def _cparams(n_axes, vmem=VMEM_LIMIT):
    return pltpu.CompilerParams(dimension_semantics=("arbitrary",) * n_axes, vmem_limit_bytes=vmem)


def _dot(a, b):
    return jnp.dot(a.astype(BF16), b.astype(BF16), preferred_element_type=F32)


def _dot_tn(a, b):
    return lax.dot_general(a.astype(BF16), b.astype(BF16), (((0,), (0,)), ((), ())),
                           preferred_element_type=F32)


def _split2(x):
    hi = x.astype(BF16)
    lo = (x - hi.astype(F32)).astype(BF16)
    return hi, lo


def _head_sums(x, sel2):
    half = x.shape[0] // 2
    hi, lo = _split2(jnp.concatenate([x[:half], x[half:]], axis=1))
    res = (jnp.dot(hi, sel2, preferred_element_type=F32) + jnp.dot(lo, sel2, preferred_element_type=F32))
    return jnp.concatenate([res[:, :LANES], res[:, LANES:]], axis=0)


def _dot_f32_rhs(sel, x):
    hi, lo = _split2(x)
    return (jnp.dot(sel, hi, preferred_element_type=F32) + jnp.dot(sel, lo, preferred_element_type=F32))


def _rms(x):
    return x * lax.rsqrt(jnp.mean(x * x, axis=-1, keepdims=True) + RMS_EPS)


def _sigmoid(x):
    return 1.0 / (1.0 + jnp.exp(-x))


def _token_shift(p, halo, mu, first):
    last = jnp.where(first, 0.0, halo[7:8, :])
    rolled = pltpu.roll(p, 1, axis=0)
    row = lax.broadcasted_iota(jnp.int32, p.shape, 0)
    prev = jnp.where(row == 0, last, rolled)
    return p + (prev - p) * mu


def _col_matvec(cs, w_ref, b_ref):
    tn = w_ref.shape[1]
    rows = 512
    acc = jnp.zeros((8, tn), F32)
    for k0 in range(0, D_MODEL, rows):
        blk = w_ref[k0:k0 + rows, :] * cs[k0:k0 + rows, :]
        acc = acc + jnp.sum(blk.reshape(rows // 8, 8, tn), axis=0)
    return jnp.sum(acc, axis=0, keepdims=True) + b_ref[...]


def _ada_kernel(c_ref, w_ref, b_ref, o_ref, cs_ref):
    c = c_ref[...]
    cs = c * _sigmoid(c)
    cs_ref[...] = cs
    o_ref[...] = _col_matvec(cs, w_ref, b_ref)


def _ada(c_col, w_ada, b_ada, n):
    tn = 512
    return pl.pallas_call(
        _ada_kernel,
        grid=(n // tn,),
        in_specs=[pl.BlockSpec((D_MODEL, 1), lambda j: (0, 0)),
                  pl.BlockSpec((D_MODEL, tn), lambda j: (0, j)),
                  pl.BlockSpec((1, tn), lambda j: (0, j))],
        out_specs=[pl.BlockSpec((1, tn), lambda j: (0, j)),
                   pl.BlockSpec((D_MODEL, 1), lambda j: (0, 0))],
        out_shape=[jax.ShapeDtypeStruct((1, n), F32), jax.ShapeDtypeStruct((D_MODEL, 1), F32)],
        compiler_params=_cparams(1),
        name="ada",
    )(c_col, w_ada, b_ada)


def _mid_kernel(x_ref, mix_ref, gpost_ref, gate_ref, gpre_ref, sh_ref, sc_ref, h_ref, xf_ref):
    h = x_ref[...] + gate_ref[...] * (_rms(mix_ref[...].astype(F32)) * gpost_ref[...])
    h_ref[...] = h
    y = _rms(h) * gpre_ref[...]
    xf_ref[...] = (y * (1.0 + sc_ref[...]) + sh_ref[...]).astype(xf_ref.dtype)


def _mid(x, mix, gpost, gpre, mod1, mod2, tm):
    L = x.shape[0]
    row = lambda i: (i, 0)
    vec = lambda k: pl.BlockSpec((1, D_MODEL), lambda i: (0, k))
    return pl.pallas_call(
        _mid_kernel,
        grid=(L // tm,),
        in_specs=[pl.BlockSpec((tm, D_MODEL), row), pl.BlockSpec((tm, D_MODEL), row),
                  vec(0), vec(2), vec(0), vec(0), vec(1)],
        out_specs=[pl.BlockSpec((tm, D_MODEL), row), pl.BlockSpec((tm, D_MODEL), row)],
        out_shape=[jax.ShapeDtypeStruct((L, D_MODEL), F32), jax.ShapeDtypeStruct((L, D_MODEL), BF16)],
        compiler_params=_cparams(1),
        name="mid",
    )(x, mix, gpost, mod1, gpre, mod2, mod2)


def _final_kernel(h_ref, ff_ref, gpost_ref, gate_ref, o_ref):
    o_ref[...] = h_ref[...] + gate_ref[...] * (_rms(ff_ref[...].astype(F32)) * gpost_ref[...])


def _final(h, ff, gpost, mod, tm):
    L = h.shape[0]
    row = lambda i: (i, 0)
    return pl.pallas_call(
        _final_kernel,
        grid=(L // tm,),
        in_specs=[pl.BlockSpec((tm, D_MODEL), row), pl.BlockSpec((tm, D_MODEL), row),
                  pl.BlockSpec((1, D_MODEL), lambda i: (0, 0)),
                  pl.BlockSpec((1, D_MODEL), lambda i: (0, 2))],
        out_specs=pl.BlockSpec((tm, D_MODEL), row),
        out_shape=jax.ShapeDtypeStruct((L, D_MODEL), F32),
        compiler_params=_cparams(1),
        name="final",
    )(h, ff, gpost, mod)


def _mm_kernel(a_ref, w_ref, o_ref):
    o_ref[...] = jnp.dot(a_ref[...], w_ref[...].astype(BF16),
                         preferred_element_type=F32).astype(o_ref.dtype)


def _matmul(a, w, out_dtype, tm, tn, name):
    m, k = a.shape
    n = w.shape[1]
    return pl.pallas_call(
        _mm_kernel,
        grid=(m // tm, n // tn),
        in_specs=[pl.BlockSpec((tm, k), lambda i, j: (i, 0)),
                  pl.BlockSpec((k, tn), lambda i, j: (0, j))],
        out_specs=pl.BlockSpec((tm, tn), lambda i, j: (i, j)),
        out_shape=jax.ShapeDtypeStruct((m, n), out_dtype),
        compiler_params=_cparams(2),
        name=name,
    )(a, w)


def _mm_acc_kernel(a_ref, w_ref, o_ref, acc_ref):
    kk = pl.program_id(2)
    last = pl.num_programs(2) - 1
    part = jnp.dot(a_ref[...], w_ref[...], preferred_element_type=F32)

    @pl.when(kk == 0)
    def _():
        acc_ref[...] = part

    @pl.when((kk > 0) & (kk < last))
    def _():
        acc_ref[...] = acc_ref[...] + part

    @pl.when(kk == last)
    def _():
        o_ref[...] = (acc_ref[...] + part).astype(o_ref.dtype)


def _dot_wt(a, wt):
    return lax.dot_general(a, wt.astype(BF16), (((1,), (1,)), ((), ())), preferred_element_type=F32)


def _mm_wt_kernel(a_ref, wt_ref, o_ref):
    o_ref[...] = _dot_wt(a_ref[...], wt_ref[...]).astype(o_ref.dtype)


def _matmul_wt(a, wt, row0, n, out_dtype, tm, tn, name):
    m, k = a.shape
    return pl.pallas_call(
        _mm_wt_kernel,
        grid=(m // tm, n // tn),
        in_specs=[pl.BlockSpec((tm, k), lambda i, j: (i, 0)),
                  pl.BlockSpec((pl.Element(tn), pl.Element(k)),
                               lambda i, j: (pl.multiple_of(row0 + j * tn, 8), 0))],
        out_specs=pl.BlockSpec((tm, tn), lambda i, j: (i, j)),
        out_shape=jax.ShapeDtypeStruct((m, n), out_dtype),
        compiler_params=_cparams(2),
        name=name,
    )(a, wt)


def _mm_wt_ada_kernel(a_ref, wt_ref, cs_ref, wada_ref, bada_ref, o_ref, mod_ref):
    o_ref[...] = _dot_wt(a_ref[...], wt_ref[...]).astype(o_ref.dtype)
    mod_ref[...] = _col_matvec(cs_ref[...], wada_ref, bada_ref)


def _matmul_wt_ada(a, wt, row0, n, out_dtype, tm, tn, cs, w_ada, b_ada, col0, ncols, name):
    m, k = a.shape
    ni, nj = m // tm, n // tn
    cps = LANES * -(-(ncols // LANES) // (ni * nj))
    assert ncols % cps == 0 and col0 % cps == 0 and ncols // cps <= ni * nj
    nblk = ncols // cps
    blk = lambda i, j: jnp.minimum(i * nj + j, nblk - 1)
    return pl.pallas_call(
        _mm_wt_ada_kernel,
        grid=(ni, nj),
        in_specs=[pl.BlockSpec((tm, k), lambda i, j: (i, 0)),
                  pl.BlockSpec((pl.Element(tn), pl.Element(k)),
                               lambda i, j: (pl.multiple_of(row0 + j * tn, 8), 0)),
                  pl.BlockSpec((D_MODEL, 1), lambda i, j: (0, 0)),
                  pl.BlockSpec((D_MODEL, cps), lambda i, j: (0, col0 // cps + blk(i, j))),
                  pl.BlockSpec((1, cps), lambda i, j: (0, col0 // cps + blk(i, j)))],
        out_specs=[pl.BlockSpec((tm, tn), lambda i, j: (i, j)),
                   pl.BlockSpec((1, cps), lambda i, j: (0, blk(i, j)))],
        out_shape=[jax.ShapeDtypeStruct((m, n), out_dtype), jax.ShapeDtypeStruct((1, ncols), F32)],
        compiler_params=_cparams(2),
        name=name,
    )(a, wt, cs, w_ada, b_ada)


def _matmul_ksplit(a, w, out_dtype, tm, tn, tk, name):
    m, k = a.shape
    n = w.shape[1]
    assert k % tk == 0 and k // tk >= 2
    return pl.pallas_call(
        _mm_acc_kernel,
        grid=(m // tm, n // tn, k // tk),
        in_specs=[pl.BlockSpec((tm, tk), lambda i, j, kk: (i, kk)),
                  pl.BlockSpec((tk, tn), lambda i, j, kk: (kk, j))],
        out_specs=pl.BlockSpec((tm, tn), lambda i, j, kk: (i, j)),
        out_shape=jax.ShapeDtypeStruct((m, n), out_dtype),
        scratch_shapes=[pltpu.VMEM((tm, tn), F32)],
        compiler_params=_cparams(3),
        name=name,
    )(a, w)


def _merge_kernel(oa_ref, ob_ref, wa_ref, wb_ref, ga_ref, gb_ref, o_ref):
    ya = jnp.dot(oa_ref[...], wa_ref[...].astype(BF16), preferred_element_type=F32)
    yb = jnp.dot(ob_ref[...], wb_ref[...].astype(BF16), preferred_element_type=F32)
    m = _sigmoid(ga_ref[...].astype(F32)) * ya + _sigmoid(gb_ref[...].astype(F32)) * yb
    o_ref[...] = m.astype(o_ref.dtype)


def _merge(o_a, o_b, w_up_a, w_up_b, gates, tm, tn):
    L = o_a.shape[0]
    nb = D_MODEL // tn
    return pl.pallas_call(
        _merge_kernel,
        grid=(L // tm, nb),
        in_specs=[pl.BlockSpec((tm, D_RWKV), lambda i, j: (i, 0)),
                  pl.BlockSpec((tm, D_S5), lambda i, j: (i, 0)),
                  pl.BlockSpec((D_RWKV, tn), lambda i, j: (0, j)),
                  pl.BlockSpec((D_S5, tn), lambda i, j: (0, j)),
                  pl.BlockSpec((tm, tn), lambda i, j: (i, j)),
                  pl.BlockSpec((tm, tn), lambda i, j: (i, j + nb))],
        out_specs=pl.BlockSpec((tm, tn), lambda i, j: (i, j)),
        out_shape=jax.ShapeDtypeStruct((L, D_MODEL), BF16),
        compiler_params=_cparams(2),
        name="merge",
    )(o_a, o_b, w_up_a, w_up_b, gates, gates)


def _ffn_up_kernel(x_ref, wg_ref, wu_ref, wd_ref, o_ref, wd_bf_ref):
    x = x_ref[...]
    a = jnp.dot(x, wg_ref[...].astype(BF16), preferred_element_type=F32)
    b = jnp.dot(x, wu_ref[...].astype(BF16), preferred_element_type=F32)
    o_ref[...] = ((a * _sigmoid(a)) * b).astype(o_ref.dtype)
    wd_bf_ref[...] = wd_ref[...].astype(BF16)


def _ffn_up(xf, w_gate, w_up, w_down, tm, tn):
    L = xf.shape[0]
    n = w_gate.shape[1]
    ni, nj = L // tm, n // tn
    slab = w_down.shape[0] // (ni * nj)
    assert slab * ni * nj == w_down.shape[0] and slab % 16 == 0
    wd_spec = lambda: pl.BlockSpec((slab, D_MODEL), lambda i, j: (i * nj + j, 0))
    return pl.pallas_call(
        _ffn_up_kernel,
        grid=(ni, nj),
        in_specs=[pl.BlockSpec((tm, D_MODEL), lambda i, j: (i, 0), pipeline_mode=pl.Buffered(1)),
                  pl.BlockSpec((D_MODEL, tn), lambda i, j: (0, j)),
                  pl.BlockSpec((D_MODEL, tn), lambda i, j: (0, j)),
                  wd_spec()],
        out_specs=[pl.BlockSpec((tm, tn), lambda i, j: (i, j)), wd_spec()],
        out_shape=[jax.ShapeDtypeStruct((L, n), BF16), jax.ShapeDtypeStruct(w_down.shape, BF16)],
        compiler_params=_cparams(2),
        name="ffn_up",
    )(xf, w_gate, w_up, w_down)


def _head_kernel(x_ref, gpre_ref, sh_ref, sc_ref, wl_ref, mu_ref, w0_ref, a0_ref, w2_ref, a2_ref,
                 g2_ref, xm_ref, lw_ref, a_ref, g_ref, wl_bf_ref, carry_ref):
    first = pl.program_id(0) == 0

    @pl.when(first)
    def _():
        wl_bf_ref[...] = wl_ref[...].astype(BF16)
        carry_ref[...] = jnp.zeros_like(carry_ref)

    y = _rms(x_ref[...]) * gpre_ref[...]
    xm = (y * (1.0 + sc_ref[...]) + sh_ref[...]).astype(BF16)
    xm_ref[...] = xm
    p = lax.dot_general(xm, wl_bf_ref[...], (((1,), (1,)), ((), ())), preferred_element_type=F32)
    ps = _token_shift(p, carry_ref[...], mu_ref[...], first)
    carry_ref[...] = p[p.shape[0] - 8:, :]
    xw = ps[:, :LORA_W]
    xa = ps[:, LORA_W:LORA_W + LORA_A]
    xg = ps[:, LORA_W + LORA_A:]
    u = w0_ref[...] + _dot(jnp.tanh(xw), w2_ref[...])
    lw_ref[...] = -math.exp(-0.5) * _sigmoid(u)
    a_ref[...] = _sigmoid(a0_ref[...] + _dot(xa, a2_ref[...]))
    g_ref[...] = _dot(_sigmoid(xg), g2_ref[...])


def _head(x, gpre, mod, wt, row_lora, mu, w0, a0, w2, a2, g2, tm):
    L = x.shape[0]
    nl = mu.shape[1]
    full = lambda shape: pl.BlockSpec(shape, lambda i: (0, 0))
    rows = lambda width: pl.BlockSpec((tm, width), lambda i: (i, 0))
    return pl.pallas_call(
        _head_kernel,
        grid=(L // tm,),
        in_specs=[rows(D_MODEL), full((1, D_MODEL)),
                  pl.BlockSpec((1, D_MODEL), lambda i: (0, 0)), pl.BlockSpec((1, D_MODEL), lambda i: (0, 1)),
                  pl.BlockSpec((pl.Element(nl), pl.Element(D_MODEL)), lambda i: (row_lora, 0),
                               pipeline_mode=pl.Buffered(1)),
                  full((1, nl)), full((1, D_RWKV)), full((1, D_RWKV)),
                  full((LORA_W, D_RWKV)), full((LORA_A, D_RWKV)), full((LORA_G_PAD, D_RWKV))],
        out_specs=[rows(D_MODEL), rows(D_RWKV), rows(D_RWKV), rows(D_RWKV)],
        out_shape=[jax.ShapeDtypeStruct((L, D_MODEL), BF16)] + [jax.ShapeDtypeStruct((L, D_RWKV), F32)] * 3,
        scratch_shapes=[pltpu.VMEM((nl, D_MODEL), BF16), pltpu.VMEM((8, nl), F32)],
        compiler_params=_cparams(1),
        name="head",
    )(x, gpre, mod, mod, wt, mu, w0, a0, w2, a2, g2)


def _rwkv_prep(r, k, v, am, bm, lw, cst, T):
    n = r.shape[0] // T
    lane_lo, tri_t = cst[0], cst[1]

    def chunk(x, c):
        return x[c * T:(c + 1) * T]

    def stack_own(xc):
        return jnp.concatenate([jnp.where(lane_lo, xc, 0.0), jnp.where(lane_lo, 0.0, xc)], axis=0)

    c_all = _dot_f32_rhs(tri_t, jnp.concatenate([chunk(lw, c) for c in range(n)], axis=1))
    yield
    lhs, rhs, bk, vv, dec = [], [], [], [], []
    for c in range(n):
        cu = c_all[:, c * LANES:(c + 1) * LANES]
        c_end = cu[T - 1:T, :]
        e_neg = jnp.exp(-cu)
        e_end = jnp.exp(c_end - cu)
        rc, kc_, ac, bc_ = chunk(r, c), chunk(k, c), chunk(am, c), chunk(bm, c)
        lhs.append(jnp.concatenate([stack_own(ac * jnp.exp(cu - chunk(lw, c))),
                                    stack_own(rc * jnp.exp(cu))], axis=0).astype(BF16))
        rhs.append(jnp.concatenate([bc_ * e_neg, kc_ * e_neg], axis=0).astype(BF16))
        bk.append(jnp.concatenate([stack_own(bc_ * e_end), stack_own(kc_ * e_end)], axis=0).astype(BF16))
        vv.append(chunk(v, c).astype(BF16))
        dec.append(jnp.broadcast_to(jnp.exp(c_end), (8, LANES)))
        if c % 2 == 1:
            yield
    return lhs, rhs, bk, vv, dec


def _rwkv_core(lhs, rhs, bk, vv, dec, cst, T):
    S = 2 * T
    n = len(lhs)
    _, _, strict, incl2, eye, same_head = cst
    cr = range(n)
    dup = lambda x: jnp.concatenate([x, x], axis=0)

    big = [lax.dot_general(lhs[c], jnp.concatenate([dup(rhs[c][:T]), dup(rhs[c][T:])], axis=0),
                           (((1,), (1,)), ((), ())), preferred_element_type=F32) for c in cr]
    yield
    ah = [lhs[c][:S] for c in cr]
    rh = [lhs[c][S:].astype(F32) for c in cr]
    v_s = [dup(vv[c]) for c in cr]
    nj = [jnp.where(strict, big[c][:S, :S], 0.0) for c in cr]
    m_r = [jnp.where(incl2, big[c][S:, :], 0.0) for c in cr]
    av = [_dot(jnp.where(strict, big[c][:S, S:], 0.0), v_s[c]) for c in cr]
    yield

    steps = T.bit_length() - 1
    p = [jnp.where(eye, 1.0, 0.0) + nj[c] for c in cr]
    nj = [_dot(nj[c], nj[c]) for c in cr]
    yield
    for j in range(1, steps):
        if j + 1 < steps:
            res = [_dot(nj[c], jnp.concatenate([p[c], nj[c]], axis=1)) for c in cr]
            p = [p[c] + res[c][:, :S] for c in cr]
            nj = [res[c][:, S:] for c in cr]
        else:
            p = [p[c] + _dot(nj[c], p[c]) for c in cr]
        yield
    x = [_dot(p[c], jnp.concatenate([ah[c].astype(F32), av[c]], axis=1)) for c in cr]
    yield

    wv = [jnp.concatenate([x[c].astype(BF16),
                           jnp.concatenate([jnp.zeros_like(v_s[c]), v_s[c]], axis=1)], axis=0)
          for c in cr]
    y = [_dot(m_r[c], wv[c]) for c in cr]
    q = [rh[c] + y[c][:, :LANES] for c in cr]
    o0 = [y[c][:, LANES:] for c in cr]
    gd = [_dot_tn(bk[c], wv[c]) for c in cr]
    gd = [jnp.concatenate([gd[c][:, :LANES], jnp.where(same_head, gd[c][:, LANES:], 0.0)], axis=1)
          for c in cr]
    decay = [jnp.broadcast_to(jnp.sum(jnp.where(eye, dec[c][0:1, :], 0.0), axis=1, keepdims=True),
                              (LANES, LANES)) for c in cr]
    return gd, q, o0, decay


def _rwkv_kernel(r_ref, k_ref, v_ref, rh_ref, kh_ref, vh_ref, mur_ref, muk_ref, muv_ref,
                 lw_ref, a_ref, kk_ref, ka_ref, rk_ref, g_ref, lnw_ref, lnb_ref,
                 o_ref, h_ref, gd_ref, q_ref, o0_ref, dec_ref, bonus_ref,
                 lhs_ref, rhs_ref, bk_ref, vv_ref, decr_ref, *, nblk):
    s = pl.program_id(0)
    last = pl.num_programs(0) - 3
    first_cur = lax.rem(jnp.minimum(s, last), nblk) == 0
    first_out = lax.rem(jnp.maximum(s - 2, 0), nblk) == 0

    @pl.when(s == 0)
    def _():
        for ref in (h_ref, gd_ref, q_ref, o0_ref, dec_ref, bonus_ref,
                    lhs_ref, rhs_ref, bk_ref, vv_ref, decr_ref):
            ref[...] = jnp.zeros_like(ref)

    rows = r_ref.shape[0]
    T = RWKV_T
    S = 2 * T
    n = rows // T
    ii = lax.broadcasted_iota(jnp.int32, (S, S), 0)
    jj = lax.broadcasted_iota(jnp.int32, (S, S), 1)
    strict = ((ii // T) == (jj // T)) & (jj < ii)
    eye = ii == jj
    ii2 = lax.broadcasted_iota(jnp.int32, (S, 2 * S), 0)
    jj2 = lax.broadcasted_iota(jnp.int32, (S, 2 * S), 1) % S
    incl2 = ((ii2 // T) == (jj2 // T)) & (jj2 <= ii2)
    hi_ = lax.broadcasted_iota(jnp.int32, (2 * LANES, 2 * LANES), 0) // RWKV_HEAD
    hj_ = lax.broadcasted_iota(jnp.int32, (2 * LANES, 2 * LANES), 1) // RWKV_HEAD
    head_sum = jnp.where(hi_ == hj_, 1.0, 0.0).astype(BF16)
    lane_lo = lax.broadcasted_iota(jnp.int32, (T, LANES), 1) < RWKV_HEAD
    it = lax.broadcasted_iota(jnp.int32, (T, T), 0)
    jt = lax.broadcasted_iota(jnp.int32, (T, T), 1)
    tri_t = jnp.where(jt <= it, 1.0, 0.0).astype(BF16)
    same_head = ((lax.broadcasted_iota(jnp.int32, (LANES, LANES), 0) // RWKV_HEAD)
                 == (lax.broadcasted_iota(jnp.int32, (LANES, LANES), 1) // RWKV_HEAD))
    cst = (lane_lo, tri_t, strict, incl2, eye, same_head)
    slot = lax.rem(s, 2)

    state = [jnp.where(first_out, 0.0, h_ref[...])]
    outs = []

    def serial_step(c):
        h = state[0]
        gd = gd_ref[c]
        z = _dot(jnp.concatenate([gd[:, :LANES], q_ref[c]], axis=0), h)
        o_s = z[LANES:] + o0_ref[c]
        outs.append(jnp.where(lane_lo, o_s[:T], o_s[T:]))
        state[0] = dec_ref[c] * h + z[:LANES] + gd[:, LANES:]

    def serial_finish():
        h_ref[...] = state[0]
        o = jnp.concatenate(outs, axis=0) if n > 1 else outs[0]
        inv = 1.0 / RWKV_HEAD
        mean = _head_sums(o, head_sum) * inv
        d = o - mean
        var = _head_sums(d * d, head_sum) * inv
        y = d * lax.rsqrt(var + GN_EPS) * lnw_ref[...] + lnb_ref[...]
        o_ref[...] = ((y + bonus_ref[slot]) * g_ref[...]).astype(o_ref.dtype)

    core = _rwkv_core([lhs_ref[c] for c in range(n)], [rhs_ref[c] for c in range(n)],
                      [bk_ref[c] for c in range(n)], [vv_ref[c] for c in range(n)],
                      [decr_ref[c] for c in range(n)], cst, T)

    def prepare():
        r = _token_shift(r_ref[...], rh_ref[...], mur_ref[...], first_cur)
        k = _token_shift(k_ref[...], kh_ref[...], muk_ref[...], first_cur)
        v = _token_shift(v_ref[...], vh_ref[...], muv_ref[...], first_cur)
        a = a_ref[...]
        kk = k * kk_ref[...]
        k = k * (1.0 + (a - 1.0) * ka_ref[...])
        yield
        ss = _head_sums(kk * kk, head_sum)
        bonus = _head_sums(r * k * rk_ref[...], head_sum) * v
        yield
        kk = kk * lax.rsqrt(jnp.maximum(ss, 1e-24))
        operands = yield from _rwkv_prep(r, k, v, -kk, kk * a, lw_ref[...], cst, T)
        return operands + (bonus,)

    serial_step(0)
    prep = prepare()

    results = {}
    gens = {"core": core, "prep": prep}
    done = 1
    while gens:
        for name in ("core", "prep"):
            if name not in gens:
                continue
            try:
                next(gens[name])
            except StopIteration as stop:
                results[name] = stop.value
                del gens[name]
            if name == "core":
                for _ in range(RWKV_SERIAL_PER_STAGE):
                    if done < n:
                        serial_step(done)
                    elif done == n:
                        serial_finish()
                    done += 1
    assert done > n, "fewer matmul stages than chunks per block"

    gd, q, o0, decay = results["core"]
    lhs, rhs, bk, vv, dec, bonus = results["prep"]
    for c in range(n):
        gd_ref[c] = gd[c]
        q_ref[c] = q[c]
        o0_ref[c] = o0[c]
        dec_ref[c] = decay[c]
        lhs_ref[c] = lhs[c]
        rhs_ref[c] = rhs[c]
        bk_ref[c] = bk[c]
        vv_ref[c] = vv[c]
        decr_ref[c] = dec[c]
    bonus_ref[slot] = bonus


def _rwkv(proj, mu, lw, a, g, k_k, k_a, r_k, ln_w, ln_b, rows):
    L = proj.shape[0]
    nb = D_RWKV // LANES
    nblk = L // rows
    nsteps = PAIRS * nblk
    n = rows // RWKV_T

    def cur(s):
        b = jnp.minimum(s, nsteps - 1)
        return lax.div(b, nblk), lax.rem(b, nblk)

    def prv(s):
        b = jnp.maximum(s - 2, 0)
        return lax.div(b, nblk), lax.rem(b, nblk)

    def col(off):
        return pl.BlockSpec((rows, LANES), lambda s: (cur(s)[1], cur(s)[0] + off))

    def halo(off):
        return pl.BlockSpec((8, LANES),
                            lambda s: (jnp.maximum(cur(s)[1] * (rows // 8) - 1, 0), cur(s)[0] + off))

    def vec(off):
        return pl.BlockSpec((1, LANES), lambda s: (0, cur(s)[0] + off))

    def lag_col():
        return pl.BlockSpec((rows, LANES), lambda s: (prv(s)[1], prv(s)[0]))

    def lag_vec():
        return pl.BlockSpec((1, LANES), lambda s: (0, prv(s)[0]))

    return pl.pallas_call(
        functools.partial(_rwkv_kernel, nblk=nblk),
        grid=(nsteps + 2,),
        in_specs=[col(0), col(nb), col(2 * nb), halo(0), halo(nb), halo(2 * nb),
                  vec(0), vec(nb), vec(2 * nb),
                  col(0), col(0), vec(0), vec(0), vec(0),
                  lag_col(), lag_vec(), lag_vec()],
        out_specs=lag_col(),
        out_shape=jax.ShapeDtypeStruct((L, D_RWKV), BF16),
        scratch_shapes=[pltpu.VMEM((LANES, LANES), F32),
                        pltpu.VMEM((n, LANES, 2 * LANES), F32),
                        pltpu.VMEM((n, 2 * RWKV_T, LANES), F32),
                        pltpu.VMEM((n, 2 * RWKV_T, LANES), F32),
                        pltpu.VMEM((n, LANES, LANES), F32),
                        pltpu.VMEM((2, rows, LANES), F32),
                        pltpu.VMEM((n, 4 * RWKV_T, LANES), BF16),
                        pltpu.VMEM((n, 2 * RWKV_T, LANES), BF16),
                        pltpu.VMEM((n, 4 * RWKV_T, LANES), BF16),
                        pltpu.VMEM((n, RWKV_T, LANES), BF16),
                        pltpu.VMEM((n, 8, LANES), F32)],
        compiler_params=_cparams(1),
        name="rwkv",
    )(proj, proj, proj, proj, proj, proj, mu, mu, mu, lw, a, k_k, k_a, r_k, g, ln_w, ln_b)


def _s5_table_kernel(lr_ref, li_ref, ldt_ref, pos_re, pos_im, neg_re, neg_im, z_re, z_im,
                     lbc_re, lbc_im):
    lr = lr_ref[...]
    li = li_ref[...]
    dt = jnp.exp(ldt_ref[...])
    centre = pos_re.shape[0] // 2
    t = (lax.broadcasted_iota(jnp.int32, pos_re.shape, 0) - centre).astype(F32)
    mag = jnp.exp(t * (lr * dt))
    ang = t * (li * dt)
    cs, sn = jnp.cos(ang), jnp.sin(ang)
    pos_re[...] = mag * cs
    pos_im[...] = mag * sn
    inv = 1.0 / mag
    neg_re[...] = inv * cs
    neg_im[...] = -(inv * sn)
    mc = jnp.exp((centre + 1.0) * (lr * dt))
    lbc_re[...] = mc * jnp.cos((centre + 1.0) * (li * dt))
    lbc_im[...] = mc * jnp.sin((centre + 1.0) * (li * dt))
    m1 = jnp.exp(lr * dt)
    lb_re = m1 * jnp.cos(li * dt)
    lb_im = m1 * jnp.sin(li * dt)
    den = lr * lr + li * li
    z_re[...] = ((lb_re - 1.0) * lr + lb_im * li) / den
    z_im[...] = (lb_im * lr - (lb_re - 1.0) * li) / den


def _s5_tables(lr, li, ldt, rows):
    n = lr.shape[1]
    full = pl.BlockSpec((1, n), lambda: (0, 0))
    tab = pl.BlockSpec((rows, n), lambda: (0, 0))
    return pl.pallas_call(
        _s5_table_kernel,
        in_specs=[full, full, full],
        out_specs=[tab, tab, tab, tab, full, full, full, full],
        out_shape=[jax.ShapeDtypeStruct((rows, n), F32)] * 4 + [jax.ShapeDtypeStruct((1, n), F32)] * 4,
        name="s5_tables",
    )(lr, li, ldt)


def _s5_kernel(u_ref, pos_re_ref, pos_im_ref, neg_re_ref, neg_im_ref, z_re_ref, z_im_ref,
               lbc_re_ref, lbc_im_ref, b_re_ref, b_im_ref, c_re_ref, c_im_ref, d_ref, wglu_ref,
               bglu_ref, o_ref, bb_ref, st_ref, cc_re_ref, cc_im_ref):
    R = u_ref.shape[0]
    T = pos_re_ref.shape[0]
    NS = S5_GROUPS * S5_STATE
    BS = NS // S5_BLOCKS
    BC = D_S5 // S5_BLOCKS

    @pl.when(pl.program_id(0) == 0)
    def _():
        st_ref[...] = jnp.zeros_like(st_ref)
        ng = S5_GROUPS // S5_BLOCKS
        rep_b = (lax.broadcasted_iota(jnp.int32, (S5_STATE, BS), 1) % S5_STATE
                 == lax.broadcasted_iota(jnp.int32, (S5_STATE, BS), 0))
        rep_b = jnp.where(rep_b, 1.0, 0.0).astype(BF16)
        own_b = (lax.broadcasted_iota(jnp.int32, (BC, BS), 0) // S5_GROUP_CH
                 == lax.broadcasted_iota(jnp.int32, (BC, BS), 1) // S5_STATE)
        rep_c = (lax.broadcasted_iota(jnp.int32, (S5_GROUP_CH, BC), 1) % S5_GROUP_CH
                 == lax.broadcasted_iota(jnp.int32, (S5_GROUP_CH, BC), 0))
        rep_c = jnp.where(rep_c, 1.0, 0.0).astype(BF16)
        own_c = (lax.broadcasted_iota(jnp.int32, (BS, BC), 0) // S5_STATE
                 == lax.broadcasted_iota(jnp.int32, (BS, BC), 1) // S5_GROUP_CH)
        assert BC == ng * S5_GROUP_CH and BS == ng * S5_STATE

        def spread_b(w):
            hi, lo = _split2(w)
            full = (jnp.dot(hi, rep_b, preferred_element_type=F32)
                    + jnp.dot(lo, rep_b, preferred_element_type=F32))
            return jnp.where(own_b, full, 0.0)

        for j in range(S5_BLOCKS):
            zr = z_re_ref[:, j * BS:(j + 1) * BS]
            zi = z_im_ref[:, j * BS:(j + 1) * BS]
            br = spread_b(b_re_ref[j])
            bi = spread_b(b_im_ref[j])
            bb_ref[j, :, :BS] = (zr * br - zi * bi).astype(BF16)
            bb_ref[j, :, BS:] = (zr * bi + zi * br).astype(BF16)
            for src, dst in ((c_re_ref, cc_re_ref), (c_im_ref, cc_im_ref)):
                full = jnp.dot(src[j].astype(BF16), rep_c, preferred_element_type=F32)
                dst[j] = jnp.where(own_c, full, 0.0).astype(BF16)

    u = u_ref[...]
    ub = u.astype(BF16)
    ii = lax.broadcasted_iota(jnp.int32, (T, T), 0)
    jj = lax.broadcasted_iota(jnp.int32, (T, T), 1)
    tri = jnp.where(jj <= ii, 1.0, 0.0).astype(BF16)

    ys = []
    for j in range(S5_BLOCKS):
        sl = slice(j * BS, (j + 1) * BS)
        bu = jnp.dot(ub[:, j * BC:(j + 1) * BC], bb_ref[j], preferred_element_type=F32)
        nr, ni = neg_re_ref[:, sl], neg_im_ref[:, sl]
        pr, pi = pos_re_ref[:, sl], pos_im_ref[:, sl]
        lr_, li_ = lbc_re_ref[:, sl], lbc_im_ref[:, sl]
        s_re = st_ref[0:1, sl]
        s_im = st_ref[1:2, sl]
        xs_re, xs_im = [], []
        for t in range(R // T):
            bu_re, bu_im = bu[t * T:(t + 1) * T, :BS], bu[t * T:(t + 1) * T, BS:]
            zz = jnp.concatenate([nr * bu_re - ni * bu_im, nr * bu_im + ni * bu_re], axis=1)
            acc = jnp.dot(tri, zz.astype(BF16), preferred_element_type=F32)
            a_re = acc[:, :BS] + (lr_ * s_re - li_ * s_im)
            a_im = acc[:, BS:] + (lr_ * s_im + li_ * s_re)
            x_re = pr * a_re - pi * a_im
            x_im = pr * a_im + pi * a_re
            s_re, s_im = x_re[T - 1:T, :], x_im[T - 1:T, :]
            xs_re.append(x_re)
            xs_im.append(x_im)
        st_ref[0:1, sl] = s_re
        st_ref[1:2, sl] = s_im
        cat = lambda xs: jnp.concatenate(xs, axis=0) if len(xs) > 1 else xs[0]
        ys.append(_dot(cat(xs_re), cc_re_ref[j]) - _dot(cat(xs_im), cc_im_ref[j]))
    y = jnp.concatenate(ys, axis=1) + d_ref[...] * u
    y = 0.5 * y * (1.0 + jnp.tanh(math.sqrt(2.0 / math.pi) * (y + 0.044715 * (y * y * y))))
    zg = _dot(y, wglu_ref[...]) + bglu_ref[...]
    o_ref[...] = (zg[:, :D_S5] * _sigmoid(zg[:, D_S5:])).astype(o_ref.dtype)


def _s5(proj, tables, b_re, b_im, c_re, c_im, d, w_glu, b_glu, rows):
    L = proj.shape[0]
    NS = S5_GROUPS * S5_STATE
    BS = NS // S5_BLOCKS
    BC = D_S5 // S5_BLOCKS
    t = tables[0].shape[0]
    c2 = lambda shape: pl.BlockSpec(shape, lambda i: (0, 0))
    c3 = lambda shape: pl.BlockSpec(shape, lambda i: (0, 0, 0))
    return pl.pallas_call(
        _s5_kernel,
        grid=(L // rows,),
        in_specs=[pl.BlockSpec((rows, D_S5), lambda i: (i, 0)),
                  c2((t, NS)), c2((t, NS)), c2((t, NS)), c2((t, NS)),
                  c2((1, NS)), c2((1, NS)), c2((1, NS)), c2((1, NS)),
                  c3((S5_BLOCKS, BC, S5_STATE)), c3((S5_BLOCKS, BC, S5_STATE)),
                  c3((S5_BLOCKS, BS, S5_GROUP_CH)), c3((S5_BLOCKS, BS, S5_GROUP_CH)),
                  c2((1, D_S5)), c2((D_S5, 2 * D_S5)), c2((1, 2 * D_S5))],
        out_specs=pl.BlockSpec((rows, D_S5), lambda i: (i, 0)),
        out_shape=jax.ShapeDtypeStruct((L, D_S5), BF16),
        scratch_shapes=[pltpu.VMEM((S5_BLOCKS, BC, 2 * BS), BF16), pltpu.VMEM((8, NS), F32),
                        pltpu.VMEM((S5_BLOCKS, BS, BC), BF16), pltpu.VMEM((S5_BLOCKS, BS, BC), BF16)],
        compiler_params=_cparams(1),
        name="s5",
    )(proj, *tables, b_re, b_im, c_re, c_im, d, w_glu, b_glu)


def kernel(x, c, w_ada, b_ada, norm_pre_mix, norm_post_mix, norm_pre_ffn, norm_post_ffn, w_in, rwkv_mu, rwkv_w0, rwkv_w2, rwkv_a0, rwkv_a2, rwkv_g2, rwkv_k_k, rwkv_k_a, rwkv_r_k, rwkv_ln_w, rwkv_ln_b, s5_lam_re, s5_lam_im, s5_log_dt, s5_b_re, s5_b_im, s5_c_re, s5_c_im, s5_d, s5_w_glu, s5_b_glu, w_up_rwkv, w_up_s5, w_out, ffn_w_gate, ffn_w_up, ffn_w_down):
    bsz, L, _ = x.shape
    assert bsz == 1 and w_ada.shape[0] == 1
    h = x.reshape(L, D_MODEL)
    tm = min(1024, L)
    tr = min(256, L)
    row = lambda v: v.reshape(1, -1)

    half = 3 * D_MODEL
    mod, cs = _ada(c.reshape(D_MODEL, 1), w_ada[0], row(b_ada[0]), half)

    wit = jnp.swapaxes(w_in[0], 0, 1)
    gpad = LORA_G_PAD - LORA_G
    n_rkv = 3 * D_RWKV
    mu = row(rwkv_mu[0])
    mu_lora = jnp.concatenate([rwkv_mu[0, n_rkv:], jnp.zeros((gpad,), F32)]).reshape(1, -1)
    g2 = jnp.concatenate([rwkv_g2[0], jnp.zeros((gpad, D_RWKV), F32)], axis=0).astype(BF16)

    xm, lw, a, g = _head(h, row(norm_pre_mix[0]), mod, wit, n_rkv, mu_lora, row(rwkv_w0[0]),
                         row(rwkv_a0[0]), rwkv_w2[0].astype(BF16), rwkv_a2[0].astype(BF16), g2, tr)
    proj_rkv = _matmul_wt(xm, wit, 0, n_rkv, F32, tm, 512, "proj_rkv")
    proj_u = _matmul_wt(xm, wit, RWKV_COLS, D_S5, F32, tm, 512, "proj_u")
    gates, mod2 = _matmul_wt_ada(xm, wit, RWKV_COLS + D_S5, 2 * D_MODEL, BF16, tm, 512,
                                 cs, w_ada[0], row(b_ada[0]), half, half, "proj_g")
    o_a = _rwkv(proj_rkv, mu, lw, a, g, row(rwkv_k_k[0]), row(rwkv_k_a[0]), row(rwkv_r_k[0]),
                row(rwkv_ln_w[0]), row(rwkv_ln_b[0]), min(RWKV_T * RWKV_CHUNKS_PER_STEP, L))

    rep = lambda v: jnp.repeat(v, S5_STATE).reshape(1, -1)
    tables = _s5_tables(row(s5_lam_re[0]), row(s5_lam_im[0]), rep(s5_log_dt[0]), S5_T)
    bt = lambda w: jnp.swapaxes(w, 1, 2).reshape(S5_BLOCKS, -1, S5_STATE)
    ct = lambda w: jnp.swapaxes(w, 1, 2).reshape(S5_BLOCKS, -1, S5_GROUP_CH)
    o_b = _s5(proj_u, tables, bt(s5_b_re[0]), bt(s5_b_im[0]), ct(s5_c_re[0]), ct(s5_c_im[0]),
              row(s5_d[0]), s5_w_glu[0].astype(BF16), row(s5_b_glu[0]), min(S5_ROWS, L))

    merged = _merge(o_a, o_b, w_up_rwkv[0], w_up_s5[0], gates, tm, 512)
    mix = _matmul(merged, w_out[0], BF16, tm, 512, "w_out")

    h1, xf = _mid(h, mix, row(norm_post_mix[0]), row(norm_pre_ffn[0]), mod, mod2, tr)
    act, wd_bf = _ffn_up(xf, ffn_w_gate[0], ffn_w_up[0], ffn_w_down[0], min(2048, L), 256)
    ff = _matmul_ksplit(act, wd_bf, BF16, tm, 1024, D_FF // 2, "ffn_down")
    out = _final(h1, ff, row(norm_post_ffn[0]), mod2, tr)
    return out.reshape(bsz, L, D_MODEL)
```

```python
import functools
import math

import jax
import jax.numpy as jnp
from jax import lax
from jax.experimental import pallas as pl
from jax.experimental.pallas import tpu as pltpu

F32 = jnp.float32
BF16 = jnp.bfloat16

D_MODEL = 4096
RMS_EPS = 1e-6
D_RWKV = 2048
RWKV_HEAD = 64
LORA_W = 128
LORA_A = 128
LORA_G = 480
LORA_G_PAD = 512
GN_EPS = 64e-5
D_S5 = 1024
S5_GROUPS = 64
S5_GROUP_CH = 16
S5_STATE = 64
S5_BLOCKS = 4
D_FF = 11008

LANES = 128
PAIRS = D_RWKV // LANES
RWKV_T = 64
RWKV_CHUNKS_PER_STEP = 16
RWKV_SERIAL_PER_STAGE = 2
S5_T = 128
S5_ROWS = 512

RWKV_COLS = 3 * D_RWKV + LORA_W + LORA_A + LORA_G

VMEM_LIMIT = 56 * 1024 * 1024


def _cparams(n_axes, vmem=VMEM_LIMIT):
    return pltpu.CompilerParams(dimension_semantics=("arbitrary",) * n_axes, vmem_limit_bytes=vmem)


def _dot(a, b):
    return jnp.dot(a.astype(BF16), b.astype(BF16), preferred_element_type=F32)


def _dot_tn(a, b):
    return lax.dot_general(a.astype(BF16), b.astype(BF16), (((0,), (0,)), ((), ())),
                           preferred_element_type=F32)


def _split2(x):
    hi = x.astype(BF16)
    lo = (x - hi.astype(F32)).astype(BF16)
    return hi, lo


def _head_sums(x, sel2, two_terms=True):
    half = x.shape[0] // 2
    hi, lo = _split2(jnp.concatenate([x[:half], x[half:]], axis=1))
    res = jnp.dot(hi, sel2, preferred_element_type=F32)
    if two_terms:
        res = res + jnp.dot(lo, sel2, preferred_element_type=F32)
    return jnp.concatenate([res[:, :LANES], res[:, LANES:]], axis=0)


def _dot_f32_rhs(sel, x):
    hi, lo = _split2(x)
    return (jnp.dot(sel, hi, preferred_element_type=F32) + jnp.dot(sel, lo, preferred_element_type=F32))


def _rms(x):
    return x * lax.rsqrt(jnp.mean(x * x, axis=-1, keepdims=True) + RMS_EPS)


def _sigmoid(x):
    return 1.0 / (1.0 + jnp.exp(-x))


def _token_shift(p, halo, mu, first):
    last = jnp.where(first, 0.0, halo[7:8, :])
    rolled = pltpu.roll(p, 1, axis=0)
    row = lax.broadcasted_iota(jnp.int32, p.shape, 0)
    prev = jnp.where(row == 0, last, rolled)
    return p + (prev - p) * mu


def _col_matvec(cs, w_ref, b_ref):
    tn = w_ref.shape[1]
    rows = 512
    acc = jnp.zeros((8, tn), F32)
    for k0 in range(0, D_MODEL, rows):
        blk = w_ref[k0:k0 + rows, :] * cs[k0:k0 + rows, :]
        acc = acc + jnp.sum(blk.reshape(rows // 8, 8, tn), axis=0)
    return jnp.sum(acc, axis=0, keepdims=True) + b_ref[...]


def _ada_kernel(c_ref, w_ref, b_ref, o_ref, cs_ref):
    c = c_ref[...]
    cs = c * _sigmoid(c)
    cs_ref[...] = cs
    o_ref[...] = _col_matvec(cs, w_ref, b_ref)


def _ada(c_col, w_ada, b_ada, n):
    tn = 512
    return pl.pallas_call(
        _ada_kernel,
        grid=(n // tn,),
        in_specs=[pl.BlockSpec((D_MODEL, 1), lambda j: (0, 0)),
                  pl.BlockSpec((D_MODEL, tn), lambda j: (0, j)),
                  pl.BlockSpec((1, tn), lambda j: (0, j))],
        out_specs=[pl.BlockSpec((1, tn), lambda j: (0, j)),
                   pl.BlockSpec((D_MODEL, 1), lambda j: (0, 0))],
        out_shape=[jax.ShapeDtypeStruct((1, n), F32), jax.ShapeDtypeStruct((D_MODEL, 1), F32)],
        compiler_params=_cparams(1),
        name="ada",
    )(c_col, w_ada, b_ada)


def _mid_kernel(x_ref, mix_ref, gpost_ref, gate_ref, gpre_ref, sh_ref, sc_ref, h_ref, xf_ref):
    h = x_ref[...] + gate_ref[...] * (_rms(mix_ref[...].astype(F32)) * gpost_ref[...])
    h_ref[...] = h
    y = _rms(h) * gpre_ref[...]
    xf_ref[...] = (y * (1.0 + sc_ref[...]) + sh_ref[...]).astype(xf_ref.dtype)


def _mid(x, mix, gpost, gpre, mod1, mod2, tm):
    L = x.shape[0]
    row = lambda i: (i, 0)
    vec = lambda k: pl.BlockSpec((1, D_MODEL), lambda i: (0, k))
    return pl.pallas_call(
        _mid_kernel,
        grid=(L // tm,),
        in_specs=[pl.BlockSpec((tm, D_MODEL), row), pl.BlockSpec((tm, D_MODEL), row),
                  vec(0), vec(2), vec(0), vec(0), vec(1)],
        out_specs=[pl.BlockSpec((tm, D_MODEL), row), pl.BlockSpec((tm, D_MODEL), row)],
        out_shape=[jax.ShapeDtypeStruct((L, D_MODEL), F32), jax.ShapeDtypeStruct((L, D_MODEL), BF16)],
        compiler_params=_cparams(1),
        name="mid",
    )(x, mix, gpost, mod1, gpre, mod2, mod2)


def _final_kernel(h_ref, ff_ref, gpost_ref, gate_ref, o_ref):
    o_ref[...] = h_ref[...] + gate_ref[...] * (_rms(ff_ref[...].astype(F32)) * gpost_ref[...])


def _final(h, ff, gpost, mod, tm):
    L = h.shape[0]
    row = lambda i: (i, 0)
    return pl.pallas_call(
        _final_kernel,
        grid=(L // tm,),
        in_specs=[pl.BlockSpec((tm, D_MODEL), row), pl.BlockSpec((tm, D_MODEL), row),
                  pl.BlockSpec((1, D_MODEL), lambda i: (0, 0)),
                  pl.BlockSpec((1, D_MODEL), lambda i: (0, 2))],
        out_specs=pl.BlockSpec((tm, D_MODEL), row),
        out_shape=jax.ShapeDtypeStruct((L, D_MODEL), F32),
        compiler_params=_cparams(1),
        name="final",
    )(h, ff, gpost, mod)


def _mm_kernel(a_ref, w_ref, o_ref):
    o_ref[...] = jnp.dot(a_ref[...], w_ref[...].astype(BF16),
                         preferred_element_type=F32).astype(o_ref.dtype)


def _matmul(a, w, out_dtype, tm, tn, name):
    m, k = a.shape
    n = w.shape[1]
    return pl.pallas_call(
        _mm_kernel,
        grid=(m // tm, n // tn),
        in_specs=[pl.BlockSpec((tm, k), lambda i, j: (i, 0)),
                  pl.BlockSpec((k, tn), lambda i, j: (0, j))],
        out_specs=pl.BlockSpec((tm, tn), lambda i, j: (i, j)),
        out_shape=jax.ShapeDtypeStruct((m, n), out_dtype),
        compiler_params=_cparams(2),
        name=name,
    )(a, w)


def _mm_acc_kernel(a_ref, w_ref, o_ref, acc_ref):
    kk = pl.program_id(2)
    last = pl.num_programs(2) - 1
    part = jnp.dot(a_ref[...], w_ref[...], preferred_element_type=F32)

    @pl.when(kk == 0)
    def _():
        acc_ref[...] = part

    @pl.when((kk > 0) & (kk < last))
    def _():
        acc_ref[...] = acc_ref[...] + part

    @pl.when(kk == last)
    def _():
        o_ref[...] = (acc_ref[...] + part).astype(o_ref.dtype)


def _dot_wt(a, wt):
    return lax.dot_general(a, wt.astype(BF16), (((1,), (1,)), ((), ())), preferred_element_type=F32)


def _mm_wt_kernel(a_ref, wt_ref, o_ref):
    o_ref[...] = _dot_wt(a_ref[...], wt_ref[...]).astype(o_ref.dtype)


def _matmul_wt(a, wt, row0, n, out_dtype, tm, tn, name):
    m, k = a.shape
    return pl.pallas_call(
        _mm_wt_kernel,
        grid=(m // tm, n // tn),
        in_specs=[pl.BlockSpec((tm, k), lambda i, j: (i, 0)),
                  pl.BlockSpec((pl.Element(tn), pl.Element(k)),
                               lambda i, j: (pl.multiple_of(row0 + j * tn, 8), 0))],
        out_specs=pl.BlockSpec((tm, tn), lambda i, j: (i, j)),
        out_shape=jax.ShapeDtypeStruct((m, n), out_dtype),
        compiler_params=_cparams(2),
        name=name,
    )(a, wt)


def _mm_wt_ada_kernel(a_ref, wt_ref, cs_ref, wada_ref, bada_ref, o_ref, mod_ref):
    o_ref[...] = _dot_wt(a_ref[...], wt_ref[...]).astype(o_ref.dtype)
    mod_ref[...] = _col_matvec(cs_ref[...], wada_ref, bada_ref)


def _matmul_wt_ada(a, wt, row0, n, out_dtype, tm, tn, cs, w_ada, b_ada, col0, ncols, name):
    m, k = a.shape
    ni, nj = m // tm, n // tn
    cps = LANES * -(-(ncols // LANES) // (ni * nj))
    assert ncols % cps == 0 and col0 % cps == 0 and ncols // cps <= ni * nj
    nblk = ncols // cps
    blk = lambda i, j: jnp.minimum(i * nj + j, nblk - 1)
    return pl.pallas_call(
        _mm_wt_ada_kernel,
        grid=(ni, nj),
        in_specs=[pl.BlockSpec((tm, k), lambda i, j: (i, 0)),
                  pl.BlockSpec((pl.Element(tn), pl.Element(k)),
                               lambda i, j: (pl.multiple_of(row0 + j * tn, 8), 0)),
                  pl.BlockSpec((D_MODEL, 1), lambda i, j: (0, 0)),
                  pl.BlockSpec((D_MODEL, cps), lambda i, j: (0, col0 // cps + blk(i, j))),
                  pl.BlockSpec((1, cps), lambda i, j: (0, col0 // cps + blk(i, j)))],
        out_specs=[pl.BlockSpec((tm, tn), lambda i, j: (i, j)),
                   pl.BlockSpec((1, cps), lambda i, j: (0, blk(i, j)))],
        out_shape=[jax.ShapeDtypeStruct((m, n), out_dtype), jax.ShapeDtypeStruct((1, ncols), F32)],
        compiler_params=_cparams(2),
        name=name,
    )(a, wt, cs, w_ada, b_ada)


def _matmul_ksplit(a, w, out_dtype, tm, tn, tk, name):
    m, k = a.shape
    n = w.shape[1]
    assert k % tk == 0 and k // tk >= 2
    return pl.pallas_call(
        _mm_acc_kernel,
        grid=(m // tm, n // tn, k // tk),
        in_specs=[pl.BlockSpec((tm, tk), lambda i, j, kk: (i, kk)),
                  pl.BlockSpec((tk, tn), lambda i, j, kk: (kk, j))],
        out_specs=pl.BlockSpec((tm, tn), lambda i, j, kk: (i, j)),
        out_shape=jax.ShapeDtypeStruct((m, n), out_dtype),
        scratch_shapes=[pltpu.VMEM((tm, tn), F32)],
        compiler_params=_cparams(3),
        name=name,
    )(a, w)


def _merge_kernel(oa_ref, ob_ref, wa_ref, wb_ref, ga_ref, gb_ref, o_ref):
    ya = jnp.dot(oa_ref[...], wa_ref[...].astype(BF16), preferred_element_type=F32)
    yb = jnp.dot(ob_ref[...], wb_ref[...].astype(BF16), preferred_element_type=F32)
    m = _sigmoid(ga_ref[...].astype(F32)) * ya + _sigmoid(gb_ref[...].astype(F32)) * yb
    o_ref[...] = m.astype(o_ref.dtype)


def _merge(o_a, o_b, w_up_a, w_up_b, gates, tm, tn):
    L = o_a.shape[0]
    nb = D_MODEL // tn
    return pl.pallas_call(
        _merge_kernel,
        grid=(L // tm, nb),
        in_specs=[pl.BlockSpec((tm, D_RWKV), lambda i, j: (i, 0)),
                  pl.BlockSpec((tm, D_S5), lambda i, j: (i, 0)),
                  pl.BlockSpec((D_RWKV, tn), lambda i, j: (0, j)),
                  pl.BlockSpec((D_S5, tn), lambda i, j: (0, j)),
                  pl.BlockSpec((tm, tn), lambda i, j: (i, j)),
                  pl.BlockSpec((tm, tn), lambda i, j: (i, j + nb))],
        out_specs=pl.BlockSpec((tm, tn), lambda i, j: (i, j)),
        out_shape=jax.ShapeDtypeStruct((L, D_MODEL), BF16),
        compiler_params=_cparams(2),
        name="merge",
    )(o_a, o_b, w_up_a, w_up_b, gates, gates)


def _ffn_up_kernel(x_ref, wg_ref, wu_ref, wd_ref, o_ref, wd_bf_ref):
    x = x_ref[...]
    a = jnp.dot(x, wg_ref[...].astype(BF16), preferred_element_type=F32)
    b = jnp.dot(x, wu_ref[...].astype(BF16), preferred_element_type=F32)
    o_ref[...] = ((a * _sigmoid(a)) * b).astype(o_ref.dtype)
    wd_bf_ref[...] = wd_ref[...].astype(BF16)


def _ffn_up(xf, w_gate, w_up, w_down, tm, tn):
    L = xf.shape[0]
    n = w_gate.shape[1]
    ni, nj = L // tm, n // tn
    slab = w_down.shape[0] // (ni * nj)
    assert slab * ni * nj == w_down.shape[0] and slab % 16 == 0
    wd_spec = lambda: pl.BlockSpec((slab, D_MODEL), lambda i, j: (i * nj + j, 0))
    return pl.pallas_call(
        _ffn_up_kernel,
        grid=(ni, nj),
        in_specs=[pl.BlockSpec((tm, D_MODEL), lambda i, j: (i, 0), pipeline_mode=pl.Buffered(1)),
                  pl.BlockSpec((D_MODEL, tn), lambda i, j: (0, j)),
                  pl.BlockSpec((D_MODEL, tn), lambda i, j: (0, j)),
                  wd_spec()],
        out_specs=[pl.BlockSpec((tm, tn), lambda i, j: (i, j)), wd_spec()],
        out_shape=[jax.ShapeDtypeStruct((L, n), BF16), jax.ShapeDtypeStruct(w_down.shape, BF16)],
        compiler_params=_cparams(2),
        name="ffn_up",
    )(xf, w_gate, w_up, w_down)


def _head_kernel(x_ref, gpre_ref, sh_ref, sc_ref, wl_ref, mu_ref, w0_ref, a0_ref, w2_ref, a2_ref,
                 g2_ref, xm_ref, lw_ref, a_ref, g_ref, wl_bf_ref, carry_ref):
    first = pl.program_id(0) == 0

    @pl.when(first)
    def _():
        wl_bf_ref[...] = wl_ref[...].astype(BF16)
        carry_ref[...] = jnp.zeros_like(carry_ref)

    y = _rms(x_ref[...]) * gpre_ref[...]
    xm = (y * (1.0 + sc_ref[...]) + sh_ref[...]).astype(BF16)
    xm_ref[...] = xm
    p = lax.dot_general(xm, wl_bf_ref[...], (((1,), (1,)), ((), ())), preferred_element_type=F32)
    ps = _token_shift(p, carry_ref[...], mu_ref[...], first)
    carry_ref[...] = p[p.shape[0] - 8:, :]
    xw = ps[:, :LORA_W]
    xa = ps[:, LORA_W:LORA_W + LORA_A]
    xg = ps[:, LORA_W + LORA_A:]
    u = w0_ref[...] + _dot(jnp.tanh(xw), w2_ref[...])
    lw_ref[...] = -math.exp(-0.5) * _sigmoid(u)
    a_ref[...] = _sigmoid(a0_ref[...] + _dot(xa, a2_ref[...]))
    g_ref[...] = _dot(_sigmoid(xg), g2_ref[...])


def _head(x, gpre, mod, wt, row_lora, mu, w0, a0, w2, a2, g2, tm):
    L = x.shape[0]
    nl = mu.shape[1]
    full = lambda shape: pl.BlockSpec(shape, lambda i: (0, 0))
    rows = lambda width: pl.BlockSpec((tm, width), lambda i: (i, 0))
    return pl.pallas_call(
        _head_kernel,
        grid=(L // tm,),
        in_specs=[rows(D_MODEL), full((1, D_MODEL)),
                  pl.BlockSpec((1, D_MODEL), lambda i: (0, 0)), pl.BlockSpec((1, D_MODEL), lambda i: (0, 1)),
                  pl.BlockSpec((pl.Element(nl), pl.Element(D_MODEL)), lambda i: (row_lora, 0),
                               pipeline_mode=pl.Buffered(1)),
                  full((1, nl)), full((1, D_RWKV)), full((1, D_RWKV)),
                  full((LORA_W, D_RWKV)), full((LORA_A, D_RWKV)), full((LORA_G_PAD, D_RWKV))],
        out_specs=[rows(D_MODEL), rows(D_RWKV), rows(D_RWKV), rows(D_RWKV)],
        out_shape=[jax.ShapeDtypeStruct((L, D_MODEL), BF16)] + [jax.ShapeDtypeStruct((L, D_RWKV), F32)] * 3,
        scratch_shapes=[pltpu.VMEM((nl, D_MODEL), BF16), pltpu.VMEM((8, nl), F32)],
        compiler_params=_cparams(1),
        name="head",
    )(x, gpre, mod, mod, wt, mu, w0, a0, w2, a2, g2)


def _rwkv_prep(r, k, v, am, bm, lw, cst, T):
    n = r.shape[0] // T
    lane_lo, tri_t = cst[0], cst[1]

    def chunk(x, c):
        return x[c * T:(c + 1) * T]

    def stack_own(xc):
        return jnp.concatenate([jnp.where(lane_lo, xc, 0.0), jnp.where(lane_lo, 0.0, xc)], axis=0)

    c_all = _dot_f32_rhs(tri_t, jnp.concatenate([chunk(lw, c) for c in range(n)], axis=1))
    yield
    lhs, rhs, bk, vv, dec = [], [], [], [], []
    for c in range(n):
        cu = c_all[:, c * LANES:(c + 1) * LANES]
        c_end = cu[T - 1:T, :]
        e_neg = jnp.exp(-cu)
        e_end = jnp.exp(c_end - cu)
        rc, kc_, ac, bc_ = chunk(r, c), chunk(k, c), chunk(am, c), chunk(bm, c)
        lhs.append(jnp.concatenate([stack_own(ac * jnp.exp(cu - chunk(lw, c))),
                                    stack_own(rc * jnp.exp(cu))], axis=0).astype(BF16))
        rhs.append(jnp.concatenate([bc_ * e_neg, kc_ * e_neg], axis=0).astype(BF16))
        bk.append(jnp.concatenate([stack_own(bc_ * e_end), stack_own(kc_ * e_end)], axis=0).astype(BF16))
        vv.append(chunk(v, c).astype(BF16))
        dec.append(jnp.broadcast_to(jnp.exp(c_end), (8, LANES)))
        if c % 2 == 1:
            yield
    return lhs, rhs, bk, vv, dec


def _rwkv_core(lhs, rhs, bk, vv, dec, cst, T):
    S = 2 * T
    n = len(lhs)
    _, _, strict, incl2, eye, same_head = cst
    cr = range(n)
    dup = lambda x: jnp.concatenate([x, x], axis=0)

    big = [lax.dot_general(lhs[c], jnp.concatenate([dup(rhs[c][:T]), dup(rhs[c][T:])], axis=0),
                           (((1,), (1,)), ((), ())), preferred_element_type=F32) for c in cr]
    yield
    ah = [lhs[c][:S] for c in cr]
    rh = [lhs[c][S:].astype(F32) for c in cr]
    v_s = [dup(vv[c]) for c in cr]
    nj = [jnp.where(strict, big[c][:S, :S], 0.0) for c in cr]
    m_r = [jnp.where(incl2, big[c][S:, :], 0.0) for c in cr]
    av = [_dot(jnp.where(strict, big[c][:S, S:], 0.0), v_s[c]) for c in cr]
    yield

    steps = T.bit_length() - 1
    p = [jnp.where(eye, 1.0, 0.0) + nj[c] for c in cr]
    nj = [_dot(nj[c], nj[c]) for c in cr]
    yield
    for j in range(1, steps):
        if j + 1 < steps:
            res = [_dot(nj[c], jnp.concatenate([p[c], nj[c]], axis=1)) for c in cr]
            p = [p[c] + res[c][:, :S] for c in cr]
            nj = [res[c][:, S:] for c in cr]
        else:
            p = [p[c] + _dot(nj[c], p[c]) for c in cr]
        yield
    x = [_dot(p[c], jnp.concatenate([ah[c].astype(F32), av[c]], axis=1)) for c in cr]
    yield

    wv = [jnp.concatenate([x[c].astype(BF16),
                           jnp.concatenate([jnp.zeros_like(v_s[c]), v_s[c]], axis=1)], axis=0)
          for c in cr]
    y = [_dot(m_r[c], wv[c]) for c in cr]
    q = [rh[c] + y[c][:, :LANES] for c in cr]
    o0 = [y[c][:, LANES:] for c in cr]
    gd = [_dot_tn(bk[c], wv[c]) for c in cr]
    gd = [jnp.concatenate([gd[c][:, :LANES], jnp.where(same_head, gd[c][:, LANES:], 0.0)], axis=1)
          for c in cr]
    decay = [jnp.broadcast_to(jnp.sum(jnp.where(eye, dec[c][0:1, :], 0.0), axis=1, keepdims=True),
                              (LANES, LANES)) for c in cr]
    return gd, q, o0, decay


def _rwkv_kernel(r_ref, k_ref, v_ref, rh_ref, kh_ref, vh_ref, mur_ref, muk_ref, muv_ref,
                 lw_ref, a_ref, kk_ref, ka_ref, rk_ref, g_ref, lnw_ref, lnb_ref,
                 o_ref, h_ref, gd_ref, q_ref, o0_ref, dec_ref, bonus_ref,
                 lhs_ref, rhs_ref, bk_ref, vv_ref, decr_ref, *, nblk):
    s = pl.program_id(0)
    last = pl.num_programs(0) - 3
    first_cur = lax.rem(jnp.minimum(s, last), nblk) == 0
    first_out = lax.rem(jnp.maximum(s - 2, 0), nblk) == 0

    @pl.when(s == 0)
    def _():
        for ref in (h_ref, gd_ref, q_ref, o0_ref, dec_ref, bonus_ref,
                    lhs_ref, rhs_ref, bk_ref, vv_ref, decr_ref):
            ref[...] = jnp.zeros_like(ref)

    rows = r_ref.shape[0]
    T = RWKV_T
    S = 2 * T
    n = rows // T
    ii = lax.broadcasted_iota(jnp.int32, (S, S), 0)
    jj = lax.broadcasted_iota(jnp.int32, (S, S), 1)
    strict = ((ii // T) == (jj // T)) & (jj < ii)
    eye = ii == jj
    ii2 = lax.broadcasted_iota(jnp.int32, (S, 2 * S), 0)
    jj2 = lax.broadcasted_iota(jnp.int32, (S, 2 * S), 1) % S
    incl2 = ((ii2 // T) == (jj2 // T)) & (jj2 <= ii2)
    hi_ = lax.broadcasted_iota(jnp.int32, (2 * LANES, 2 * LANES), 0) // RWKV_HEAD
    hj_ = lax.broadcasted_iota(jnp.int32, (2 * LANES, 2 * LANES), 1) // RWKV_HEAD
    head_sum = jnp.where(hi_ == hj_, 1.0, 0.0).astype(BF16)
    lane_lo = lax.broadcasted_iota(jnp.int32, (T, LANES), 1) < RWKV_HEAD
    it = lax.broadcasted_iota(jnp.int32, (T, T), 0)
    jt = lax.broadcasted_iota(jnp.int32, (T, T), 1)
    tri_t = jnp.where(jt <= it, 1.0, 0.0).astype(BF16)
    same_head = ((lax.broadcasted_iota(jnp.int32, (LANES, LANES), 0) // RWKV_HEAD)
                 == (lax.broadcasted_iota(jnp.int32, (LANES, LANES), 1) // RWKV_HEAD))
    cst = (lane_lo, tri_t, strict, incl2, eye, same_head)
    slot = lax.rem(s, 2)

    state = [jnp.where(first_out, 0.0, h_ref[...])]
    outs = []

    def serial_step(c):
        h = state[0]
        gd = gd_ref[c]
        z = _dot(jnp.concatenate([gd[:, :LANES], q_ref[c]], axis=0), h)
        o_s = z[LANES:] + o0_ref[c]
        outs.append(jnp.where(lane_lo, o_s[:T], o_s[T:]))
        state[0] = dec_ref[c] * h + z[:LANES] + gd[:, LANES:]

    def serial_finish():
        h_ref[...] = state[0]
        o = jnp.concatenate(outs, axis=0) if n > 1 else outs[0]
        inv = 1.0 / RWKV_HEAD
        mean = _head_sums(o, head_sum) * inv
        d = o - mean
        var = _head_sums(d * d, head_sum, two_terms=False) * inv
        y = d * lax.rsqrt(var + GN_EPS) * lnw_ref[...] + lnb_ref[...]
        o_ref[...] = ((y + bonus_ref[slot]) * g_ref[...]).astype(o_ref.dtype)

    core = _rwkv_core([lhs_ref[c] for c in range(n)], [rhs_ref[c] for c in range(n)],
                      [bk_ref[c] for c in range(n)], [vv_ref[c] for c in range(n)],
                      [decr_ref[c] for c in range(n)], cst, T)

    def prepare():
        r = _token_shift(r_ref[...], rh_ref[...], mur_ref[...], first_cur)
        k = _token_shift(k_ref[...], kh_ref[...], muk_ref[...], first_cur)
        v = _token_shift(v_ref[...], vh_ref[...], muv_ref[...], first_cur)
        a = a_ref[...]
        kk = k * kk_ref[...]
        k = k * (1.0 + (a - 1.0) * ka_ref[...])
        yield
        ss = _head_sums(kk * kk, head_sum)
        bonus = _head_sums(r * k * rk_ref[...], head_sum, two_terms=False) * v
        yield
        kk = kk * lax.rsqrt(jnp.maximum(ss, 1e-24))
        operands = yield from _rwkv_prep(r, k, v, -kk, kk * a, lw_ref[...], cst, T)
        return operands + (bonus,)

    serial_step(0)
    prep = prepare()

    results = {}
    gens = {"core": core, "prep": prep}
    done = 1
    while gens:
        for name in ("core", "prep"):
            if name not in gens:
                continue
            try:
                next(gens[name])
            except StopIteration as stop:
                results[name] = stop.value
                del gens[name]
            if name == "core":
                for _ in range(RWKV_SERIAL_PER_STAGE):
                    if done < n:
                        serial_step(done)
                    elif done == n:
                        serial_finish()
                    done += 1
    assert done > n, "fewer matmul stages than chunks per block"

    gd, q, o0, decay = results["core"]
    lhs, rhs, bk, vv, dec, bonus = results["prep"]
    for c in range(n):
        gd_ref[c] = gd[c]
        q_ref[c] = q[c]
        o0_ref[c] = o0[c]
        dec_ref[c] = decay[c]
        lhs_ref[c] = lhs[c]
        rhs_ref[c] = rhs[c]
        bk_ref[c] = bk[c]
        vv_ref[c] = vv[c]
        decr_ref[c] = dec[c]
    bonus_ref[slot] = bonus


def _rwkv(proj, mu, lw, a, g, k_k, k_a, r_k, ln_w, ln_b, rows):
    L = proj.shape[0]
    nb = D_RWKV // LANES
    nblk = L // rows
    nsteps = PAIRS * nblk
    n = rows // RWKV_T

    def cur(s):
        b = jnp.minimum(s, nsteps - 1)
        return lax.div(b, nblk), lax.rem(b, nblk)

    def prv(s):
        b = jnp.maximum(s - 2, 0)
        return lax.div(b, nblk), lax.rem(b, nblk)

    def col(off):
        return pl.BlockSpec((rows, LANES), lambda s: (cur(s)[1], cur(s)[0] + off))

    def halo(off):
        return pl.BlockSpec((8, LANES),
                            lambda s: (jnp.maximum(cur(s)[1] * (rows // 8) - 1, 0), cur(s)[0] + off))

    def vec(off):
        return pl.BlockSpec((1, LANES), lambda s: (0, cur(s)[0] + off))

    def lag_col():
        return pl.BlockSpec((rows, LANES), lambda s: (prv(s)[1], prv(s)[0]))

    def lag_vec():
        return pl.BlockSpec((1, LANES), lambda s: (0, prv(s)[0]))

    return pl.pallas_call(
        functools.partial(_rwkv_kernel, nblk=nblk),
        grid=(nsteps + 2,),
        in_specs=[col(0), col(nb), col(2 * nb), halo(0), halo(nb), halo(2 * nb),
                  vec(0), vec(nb), vec(2 * nb),
                  col(0), col(0), vec(0), vec(0), vec(0),
                  lag_col(), lag_vec(), lag_vec()],
        out_specs=lag_col(),
        out_shape=jax.ShapeDtypeStruct((L, D_RWKV), BF16),
        scratch_shapes=[pltpu.VMEM((LANES, LANES), F32),
                        pltpu.VMEM((n, LANES, 2 * LANES), F32),
                        pltpu.VMEM((n, 2 * RWKV_T, LANES), F32),
                        pltpu.VMEM((n, 2 * RWKV_T, LANES), F32),
                        pltpu.VMEM((n, LANES, LANES), F32),
                        pltpu.VMEM((2, rows, LANES), F32),
                        pltpu.VMEM((n, 4 * RWKV_T, LANES), BF16),
                        pltpu.VMEM((n, 2 * RWKV_T, LANES), BF16),
                        pltpu.VMEM((n, 4 * RWKV_T, LANES), BF16),
                        pltpu.VMEM((n, RWKV_T, LANES), BF16),
                        pltpu.VMEM((n, 8, LANES), F32)],
        compiler_params=_cparams(1),
        name="rwkv",
    )(proj, proj, proj, proj, proj, proj, mu, mu, mu, lw, a, k_k, k_a, r_k, g, ln_w, ln_b)


def _s5_table_kernel(lr_ref, li_ref, ldt_ref, pos_re, pos_im, neg_re, neg_im, z_re, z_im,
                     lbc_re, lbc_im):
    lr = lr_ref[...]
    li = li_ref[...]
    dt = jnp.exp(ldt_ref[...])
    centre = pos_re.shape[0] // 2
    t = (lax.broadcasted_iota(jnp.int32, pos_re.shape, 0) - centre).astype(F32)
    mag = jnp.exp(t * (lr * dt))
    ang = t * (li * dt)
    cs, sn = jnp.cos(ang), jnp.sin(ang)
    pos_re[...] = mag * cs
    pos_im[...] = mag * sn
    inv = 1.0 / mag
    neg_re[...] = inv * cs
    neg_im[...] = -(inv * sn)
    mc = jnp.exp((centre + 1.0) * (lr * dt))
    lbc_re[...] = mc * jnp.cos((centre + 1.0) * (li * dt))
    lbc_im[...] = mc * jnp.sin((centre + 1.0) * (li * dt))
    m1 = jnp.exp(lr * dt)
    lb_re = m1 * jnp.cos(li * dt)
    lb_im = m1 * jnp.sin(li * dt)
    den = lr * lr + li * li
    z_re[...] = ((lb_re - 1.0) * lr + lb_im * li) / den
    z_im[...] = (lb_im * lr - (lb_re - 1.0) * li) / den


def _s5_tables(lr, li, ldt, rows):
    n = lr.shape[1]
    full = pl.BlockSpec((1, n), lambda: (0, 0))
    tab = pl.BlockSpec((rows, n), lambda: (0, 0))
    return pl.pallas_call(
        _s5_table_kernel,
        in_specs=[full, full, full],
        out_specs=[tab, tab, tab, tab, full, full, full, full],
        out_shape=[jax.ShapeDtypeStruct((rows, n), F32)] * 4 + [jax.ShapeDtypeStruct((1, n), F32)] * 4,
        name="s5_tables",
    )(lr, li, ldt)


def _s5_kernel(u_ref, pos_re_ref, pos_im_ref, neg_re_ref, neg_im_ref, z_re_ref, z_im_ref,
               lbc_re_ref, lbc_im_ref, b_re_ref, b_im_ref, c_re_ref, c_im_ref, d_ref, wglu_ref,
               bglu_ref, o_ref, bb_ref, st_ref, cc_re_ref, cc_im_ref):
    R = u_ref.shape[0]
    T = pos_re_ref.shape[0]
    NS = S5_GROUPS * S5_STATE
    BS = NS // S5_BLOCKS
    BC = D_S5 // S5_BLOCKS

    @pl.when(pl.program_id(0) == 0)
    def _():
        st_ref[...] = jnp.zeros_like(st_ref)
        ng = S5_GROUPS // S5_BLOCKS
        rep_b = (lax.broadcasted_iota(jnp.int32, (S5_STATE, BS), 1) % S5_STATE
                 == lax.broadcasted_iota(jnp.int32, (S5_STATE, BS), 0))
        rep_b = jnp.where(rep_b, 1.0, 0.0).astype(BF16)
        own_b = (lax.broadcasted_iota(jnp.int32, (BC, BS), 0) // S5_GROUP_CH
                 == lax.broadcasted_iota(jnp.int32, (BC, BS), 1) // S5_STATE)
        rep_c = (lax.broadcasted_iota(jnp.int32, (S5_GROUP_CH, BC), 1) % S5_GROUP_CH
                 == lax.broadcasted_iota(jnp.int32, (S5_GROUP_CH, BC), 0))
        rep_c = jnp.where(rep_c, 1.0, 0.0).astype(BF16)
        own_c = (lax.broadcasted_iota(jnp.int32, (BS, BC), 0) // S5_STATE
                 == lax.broadcasted_iota(jnp.int32, (BS, BC), 1) // S5_GROUP_CH)
        assert BC == ng * S5_GROUP_CH and BS == ng * S5_STATE

        def spread_b(w):
            hi, lo = _split2(w)
            full = (jnp.dot(hi, rep_b, preferred_element_type=F32)
                    + jnp.dot(lo, rep_b, preferred_element_type=F32))
            return jnp.where(own_b, full, 0.0)

        for j in range(S5_BLOCKS):
            zr = z_re_ref[:, j * BS:(j + 1) * BS]
            zi = z_im_ref[:, j * BS:(j + 1) * BS]
            br = spread_b(b_re_ref[j])
            bi = spread_b(b_im_ref[j])
            bb_ref[j, :, :BS] = (zr * br - zi * bi).astype(BF16)
            bb_ref[j, :, BS:] = (zr * bi + zi * br).astype(BF16)
            for src, dst in ((c_re_ref, cc_re_ref), (c_im_ref, cc_im_ref)):
                full = jnp.dot(src[j].astype(BF16), rep_c, preferred_element_type=F32)
                dst[j] = jnp.where(own_c, full, 0.0).astype(BF16)

    u = u_ref[...]
    ub = u.astype(BF16)
    ii = lax.broadcasted_iota(jnp.int32, (T, T), 0)
    jj = lax.broadcasted_iota(jnp.int32, (T, T), 1)
    tri = jnp.where(jj <= ii, 1.0, 0.0).astype(BF16)

    ys = []
    for j in range(S5_BLOCKS):
        sl = slice(j * BS, (j + 1) * BS)
        bu = jnp.dot(ub[:, j * BC:(j + 1) * BC], bb_ref[j], preferred_element_type=F32)
        nr, ni = neg_re_ref[:, sl], neg_im_ref[:, sl]
        pr, pi = pos_re_ref[:, sl], pos_im_ref[:, sl]
        lr_, li_ = lbc_re_ref[:, sl], lbc_im_ref[:, sl]
        s_re = st_ref[0:1, sl]
        s_im = st_ref[1:2, sl]
        xs_re, xs_im = [], []
        for t in range(R // T):
            bu_re, bu_im = bu[t * T:(t + 1) * T, :BS], bu[t * T:(t + 1) * T, BS:]
            zz = jnp.concatenate([nr * bu_re - ni * bu_im, nr * bu_im + ni * bu_re], axis=1)
            acc = jnp.dot(tri, zz.astype(BF16), preferred_element_type=F32)
            a_re = acc[:, :BS] + (lr_ * s_re - li_ * s_im)
            a_im = acc[:, BS:] + (lr_ * s_im + li_ * s_re)
            x_re = pr * a_re - pi * a_im
            x_im = pr * a_im + pi * a_re
            s_re, s_im = x_re[T - 1:T, :], x_im[T - 1:T, :]
            xs_re.append(x_re)
            xs_im.append(x_im)
        st_ref[0:1, sl] = s_re
        st_ref[1:2, sl] = s_im
        cat = lambda xs: jnp.concatenate(xs, axis=0) if len(xs) > 1 else xs[0]
        ys.append(_dot(cat(xs_re), cc_re_ref[j]) - _dot(cat(xs_im), cc_im_ref[j]))
    y = jnp.concatenate(ys, axis=1) + d_ref[...] * u
    y = 0.5 * y * (1.0 + jnp.tanh(math.sqrt(2.0 / math.pi) * (y + 0.044715 * (y * y * y))))
    zg = _dot(y, wglu_ref[...]) + bglu_ref[...]
    o_ref[...] = (zg[:, :D_S5] * _sigmoid(zg[:, D_S5:])).astype(o_ref.dtype)


def _s5(proj, tables, b_re, b_im, c_re, c_im, d, w_glu, b_glu, rows):
    L = proj.shape[0]
    NS = S5_GROUPS * S5_STATE
    BS = NS // S5_BLOCKS
    BC = D_S5 // S5_BLOCKS
    t = tables[0].shape[0]
    c2 = lambda shape: pl.BlockSpec(shape, lambda i: (0, 0))
    c3 = lambda shape: pl.BlockSpec(shape, lambda i: (0, 0, 0))
    return pl.pallas_call(
        _s5_kernel,
        grid=(L // rows,),
        in_specs=[pl.BlockSpec((rows, D_S5), lambda i: (i, 0)),
                  c2((t, NS)), c2((t, NS)), c2((t, NS)), c2((t, NS)),
                  c2((1, NS)), c2((1, NS)), c2((1, NS)), c2((1, NS)),
                  c3((S5_BLOCKS, BC, S5_STATE)), c3((S5_BLOCKS, BC, S5_STATE)),
                  c3((S5_BLOCKS, BS, S5_GROUP_CH)), c3((S5_BLOCKS, BS, S5_GROUP_CH)),
                  c2((1, D_S5)), c2((D_S5, 2 * D_S5)), c2((1, 2 * D_S5))],
        out_specs=pl.BlockSpec((rows, D_S5), lambda i: (i, 0)),
        out_shape=jax.ShapeDtypeStruct((L, D_S5), BF16),
        scratch_shapes=[pltpu.VMEM((S5_BLOCKS, BC, 2 * BS), BF16), pltpu.VMEM((8, NS), F32),
                        pltpu.VMEM((S5_BLOCKS, BS, BC), BF16), pltpu.VMEM((S5_BLOCKS, BS, BC), BF16)],
        compiler_params=_cparams(1),
        name="s5",
    )(proj, *tables, b_re, b_im, c_re, c_im, d, w_glu, b_glu)


def kernel(x, c, w_ada, b_ada, norm_pre_mix, norm_post_mix, norm_pre_ffn, norm_post_ffn, w_in, rwkv_mu, rwkv_w0, rwkv_w2, rwkv_a0, rwkv_a2, rwkv_g2, rwkv_k_k, rwkv_k_a, rwkv_r_k, rwkv_ln_w, rwkv_ln_b, s5_lam_re, s5_lam_im, s5_log_dt, s5_b_re, s5_b_im, s5_c_re, s5_c_im, s5_d, s5_w_glu, s5_b_glu, w_up_rwkv, w_up_s5, w_out, ffn_w_gate, ffn_w_up, ffn_w_down):
    bsz, L, _ = x.shape
    assert bsz == 1 and w_ada.shape[0] == 1
    h = x.reshape(L, D_MODEL)
    tm = min(1024, L)
    tr = min(256, L)
    row = lambda v: v.reshape(1, -1)

    half = 3 * D_MODEL
    mod, cs = _ada(c.reshape(D_MODEL, 1), w_ada[0], row(b_ada[0]), half)

    wit = jnp.swapaxes(w_in[0], 0, 1)
    gpad = LORA_G_PAD - LORA_G
    n_rkv = 3 * D_RWKV
    mu = row(rwkv_mu[0])
    mu_lora = jnp.concatenate([rwkv_mu[0, n_rkv:], jnp.zeros((gpad,), F32)]).reshape(1, -1)
    g2 = jnp.concatenate([rwkv_g2[0], jnp.zeros((gpad, D_RWKV), F32)], axis=0).astype(BF16)

    xm, lw, a, g = _head(h, row(norm_pre_mix[0]), mod, wit, n_rkv, mu_lora, row(rwkv_w0[0]),
                         row(rwkv_a0[0]), rwkv_w2[0].astype(BF16), rwkv_a2[0].astype(BF16), g2, tr)
    proj_rkv = _matmul_wt(xm, wit, 0, n_rkv, F32, tm, 512, "proj_rkv")
    proj_u = _matmul_wt(xm, wit, RWKV_COLS, D_S5, F32, tm, 512, "proj_u")
    gates, mod2 = _matmul_wt_ada(xm, wit, RWKV_COLS + D_S5, 2 * D_MODEL, BF16, tm, 512,
                                 cs, w_ada[0], row(b_ada[0]), half, half, "proj_g")
    o_a = _rwkv(proj_rkv, mu, lw, a, g, row(rwkv_k_k[0]), row(rwkv_k_a[0]), row(rwkv_r_k[0]),
                row(rwkv_ln_w[0]), row(rwkv_ln_b[0]), min(RWKV_T * RWKV_CHUNKS_PER_STEP, L))

    rep = lambda v: jnp.repeat(v, S5_STATE).reshape(1, -1)
    tables = _s5_tables(row(s5_lam_re[0]), row(s5_lam_im[0]), rep(s5_log_dt[0]), S5_T)
    bt = lambda w: jnp.swapaxes(w, 1, 2).reshape(S5_BLOCKS, -1, S5_STATE)
    ct = lambda w: jnp.swapaxes(w, 1, 2).reshape(S5_BLOCKS, -1, S5_GROUP_CH)
    o_b = _s5(proj_u, tables, bt(s5_b_re[0]), bt(s5_b_im[0]), ct(s5_c_re[0]), ct(s5_c_im[0]),
              row(s5_d[0]), s5_w_glu[0].astype(BF16), row(s5_b_glu[0]), min(S5_ROWS, L))

    merged = _merge(o_a, o_b, w_up_rwkv[0], w_up_s5[0], gates, tm, 512)
    mix = _matmul(merged, w_out[0], BF16, tm, 512, "w_out")

    h1, xf = _mid(h, mix, row(norm_post_mix[0]), row(norm_pre_ffn[0]), mod, mod2, tr)
    act, wd_bf = _ffn_up(xf, ffn_w_gate[0], ffn_w_up[0], ffn_w_down[0], min(2048, L), 256)
    ff = _matmul_ksplit(act, wd_bf, BF16, tm, 1024, D_FF // 2, "ffn_down")
    out = _final(h1, ff, row(norm_post_ffn[0]), mod2, tr)
    return out.reshape(bsz, L, D_MODEL)
```
